```python
import jax, jax.numpy as jnp
from jax import lax
import numpy as np

D_MODEL = 2048
BATCH = 2
SEQ = 4096
DEPTH = 1
DEC_BATCH = 32
DEC_SEQ = 4
PAST_LEN = 16384
PAGE_SIZE = 128

HEAD_DIM = 128
ROT_DIM = HEAD_DIM // 4
A_HEADS = D_MODEL // (2 * HEAD_DIM)
A_KV_HEADS = 2
IDX_HEADS = 8
IDX_DIM = 64
IDX_ROT_DIM = IDX_DIM // 4
A_TOPK = 256
B_HEADS = D_MODEL // (2 * HEAD_DIM)
B_KV_HEADS = 2
CMP_LEN = 32
CMP_STRIDE = 16
CMP_HIDDEN = 128
SLC_BLOCK = 64
N_SEL = 16
WINDOW = 512
Q_BLOCK = 128
ROPE_THETA = 500000.0
EPS = 1e-6
NEG = -1e30
A_WIDTH = A_HEADS * HEAD_DIM
B_WIDTH = B_HEADS * HEAD_DIM
IN_SIZES = (A_WIDTH, 2 * A_KV_HEADS * HEAD_DIM, IDX_HEADS * IDX_DIM, IDX_DIM, IDX_HEADS, A_WIDTH,
            B_WIDTH, 3 * 2 * B_KV_HEADS * HEAD_DIM, 3 * B_HEADS, B_WIDTH, 2 * D_MODEL)
IN_COLS = sum(IN_SIZES)

kernel_name = "dsa_nsa_gated_parallel_decoder_step"


def _rms_norm(x, g):
    xf = x.astype(jnp.float32)
    y = xf * lax.rsqrt(jnp.mean(xf * xf, axis=-1, keepdims=True) + EPS)
    return y.astype(x.dtype) * g


def _rope(x, pos, rot):
    half = rot // 2
    inv = ROPE_THETA ** (-jnp.arange(half, dtype=jnp.float32) / half)
    ang = pos.astype(jnp.float32)[..., None, None] * inv
    cos = jnp.cos(ang).astype(x.dtype)
    sin = jnp.sin(ang).astype(x.dtype)
    x1 = x[..., :half]
    x2 = x[..., half:rot]
    return jnp.concatenate([x1 * cos - x2 * sin, x1 * sin + x2 * cos, x[..., rot:]], axis=-1)


def _norm_rope_kv(kv, g, pos):
    k = _rope(_rms_norm(kv[..., 0, :], g), pos, ROT_DIM)
    return jnp.stack([k, kv[..., 1, :]], axis=-2)


def _group(q, kv_heads):
    return q.reshape(q.shape[:-2] + (kv_heads, q.shape[-2] // kv_heads, q.shape[-1]))


def _masked_softmax(s, mask):
    s = jnp.where(mask, s.astype(jnp.float32), NEG)
    e = jnp.exp(s - jnp.max(s, axis=-1, keepdims=True)) * mask
    return e / jnp.maximum(jnp.sum(e, axis=-1, keepdims=True), 1e-30)


def _attend_shared(q, k, v, mask):
    s = jnp.einsum("...qkgd,...skd->...kgqs", q, k) * HEAD_DIM ** -0.5
    p = _masked_softmax(s, mask[..., None, None, :, :])
    o = jnp.einsum("...kgqs,...skd->...qkgd", p.astype(v.dtype), v)
    return o, p


def _attend_dsa(q, kv, mask):
    s = jnp.einsum("bqkgd,bqskd->bkgqs", q, kv[..., 0, :]) * HEAD_DIM ** -0.5
    p = _masked_softmax(s, mask[:, None, None])
    return jnp.einsum("bkgqs,bqskd->bqkgd", p.astype(kv.dtype), kv[..., 1, :])


def _attend_slc(q, kv, mask):
    s = jnp.einsum("bqkgd,bqksd->bkgqs", q, kv[..., 0, :]) * HEAD_DIM ** -0.5
    p = _masked_softmax(s, jnp.transpose(mask, (0, 2, 1, 3))[:, :, None])
    return jnp.einsum("bkgqs,bqksd->bqkgd", p.astype(kv.dtype), kv[..., 1, :])


def _gather_rows(pool, page_table, new_rows, pos, head=None):
    db = pos.shape[0]
    flat = pos.reshape(db, -1)
    past = jnp.minimum(flat, PAST_LEN - 1)
    phys = jnp.take_along_axis(page_table, past // PAGE_SIZE, axis=1)
    off = past % PAGE_SIZE
    new_idx = jnp.clip(flat - PAST_LEN, 0, new_rows.shape[1] - 1)
    bi = jnp.arange(db)[:, None]
    if head is None:
        old = pool[phys, off]
        new = new_rows[bi, new_idx]
    else:
        hf = jnp.broadcast_to(head, pos.shape).reshape(db, -1)
        old = pool[phys, off, hf]
        new = new_rows[bi, new_idx, hf]
    is_new = (flat >= PAST_LEN).reshape(flat.shape + (1,) * (old.ndim - 2))
    out = jnp.where(is_new, new, old)
    return out.reshape(pos.shape + old.shape[2:])


def _project(x, pos, norm_gain, w_in, q_norm_a, k_norm_a, q_norm_b, k_norm_slc, k_norm_win):
    b, t, _ = x.shape
    h = _rms_norm(x, norm_gain)
    z = h @ w_in
    cuts = [int(c) for c in np.cumsum(IN_SIZES)[:-1]]
    qa, kva, iq, ik, iw, za, qb, kvb, gb, zb, gm = jnp.split(z, cuts, axis=-1)
    qb = _rms_norm(qb.reshape(b, t, B_HEADS, HEAD_DIM), q_norm_b)
    kvb = kvb.reshape(b, t, 3, B_KV_HEADS, 2, HEAD_DIM)
    return {
        "qa": _rope(_rms_norm(qa.reshape(b, t, A_HEADS, HEAD_DIM), q_norm_a), pos, ROT_DIM),
        "a_kv": _norm_rope_kv(kva.reshape(b, t, A_KV_HEADS, 2, HEAD_DIM), k_norm_a, pos),
        "iq": _rope(iq.reshape(b, t, IDX_HEADS, IDX_DIM), pos, IDX_ROT_DIM),
        "ik": _rope(ik[:, :, None, :], pos, IDX_ROT_DIM)[:, :, 0, :],
        "iw": iw * IDX_HEADS ** -0.5,
        "za": za,
        "qb": qb,
        "qb_rot": _rope(qb, pos, ROT_DIM),
        "cmp_kv": kvb[:, :, 0],
        "slc_kv": _norm_rope_kv(kvb[:, :, 1], k_norm_slc, pos),
        "win_kv": _norm_rope_kv(kvb[:, :, 2], k_norm_win, pos),
        "g_nsa": jax.nn.sigmoid(gb.reshape(b, t, 3, B_HEADS)),
        "zb": zb,
        "gates": jax.nn.sigmoid(gm.reshape(b, t, 2, D_MODEL)),
    }


def _chunk_terms(chunks, w1_cmp):
    first = jnp.einsum("bnpkcd,cpdh->bnkch", chunks, w1_cmp[:, :CMP_STRIDE])
    second = jnp.einsum("bnpkcd,cpdh->bnkch", chunks, w1_cmp[:, CMP_STRIDE:])
    return first, second


def _compress(first, second, pe_cmp, w1_cmp, w2_cmp, k_norm_cmp):
    bias = jnp.einsum("cpd,cpdh->ch", pe_cmp, w1_cmp)
    hid = jax.nn.silu(first[:, :-1] + second[:, 1:] + bias)
    out = jnp.einsum("bnkch,chd->bnkcd", hid, w2_cmp)
    return _rms_norm(out[..., 0, :], k_norm_cmp), out[..., 1, :]


def _dsa_attention(qa, iq, iw, qpos, ik_all, gather_kv, k_top):
    n_keys = ik_all.shape[1]
    sc = jax.nn.relu(jnp.einsum("bqhd,bsd->bqhs", iq, ik_all) * IDX_DIM ** -0.5)
    score = jnp.einsum("bqhs,bqh->bqs", sc, iw).astype(jnp.float32)
    kpos = jnp.arange(n_keys)
    score = jnp.where(kpos <= qpos[..., None], score, NEG)
    _, sel = lax.top_k(score, k_top)
    kv = gather_kv(sel)
    o = _attend_dsa(_group(qa, A_KV_HEADS), kv, sel <= qpos[..., None])
    return o.reshape(qa.shape)


def _nsa_sparse(qc, qs, qpos, kc, vc, n_blocks, n_sel, gather_blocks):
    nc = kc.shape[1]
    end = CMP_STRIDE * jnp.arange(nc) + CMP_LEN - 1
    o_cmp, p_cmp = _attend_shared(_group(qc, B_KV_HEADS), kc, vc, end <= qpos[..., None])
    i = jnp.arange(nc)[:, None]
    j = jnp.arange(n_blocks)[None, :]
    overlap = ((CMP_STRIDE * i < SLC_BLOCK * (j + 1)) & (CMP_STRIDE * i + CMP_LEN > SLC_BLOCK * j)).astype(jnp.float32)
    imp = jnp.einsum("bkgqc,cj->bkqj", p_cmp, overlap)
    blk = jnp.arange(n_blocks)
    qp = qpos[:, None, :, None]
    imp = jnp.where(blk == qp // SLC_BLOCK, -NEG, jnp.where(blk * SLC_BLOCK > qp, NEG, imp))
    _, sel = lax.top_k(imp, n_sel)
    sel = jnp.transpose(sel, (0, 2, 1, 3))
    kv_sel, pos_sel = gather_blocks(sel)
    o_slc = _attend_slc(_group(qs, B_KV_HEADS), kv_sel, pos_sel <= qpos[..., None, None])
    return o_cmp.reshape(qc.shape), o_slc.reshape(qs.shape)


def _window_attention(q, qpos, kv, kpos):
    d = qpos[..., :, None] - kpos[..., None, :]
    mask = (d >= 0) & (d < WINDOW) & (kpos[..., None, :] >= 0)
    o, _ = _attend_shared(_group(q, B_KV_HEADS), kv[..., 0, :], kv[..., 1, :], mask)
    return o.reshape(q.shape)


def _combine(x, p, o_a, o_cmp, o_slc, o_win, w_proj_a, w_proj_b, w_out):
    b, t, _ = x.shape
    y_a = (o_a.reshape(b, t, A_WIDTH) * jax.nn.silu(p["za"])) @ w_proj_a
    g = p["g_nsa"][..., None]
    o_b = g[:, :, 0] * o_cmp + g[:, :, 1] * o_slc + g[:, :, 2] * o_win
    y_b = (o_b.reshape(b, t, B_WIDTH) * jax.nn.silu(p["zb"])) @ w_proj_b
    merged = p["gates"][:, :, 0] * y_a + p["gates"][:, :, 1] * y_b
    return x + merged @ w_out


def setup_inputs(seed: int = 0) -> dict:
    key = jax.random.key(seed)
    ks = jax.random.split(key, 24)

    def nrm(k, shape, scale=1.0):
        return jax.random.normal(k, shape, jnp.float32) * scale

    def gain(k, n):
        return 1.0 + 0.1 * jax.random.normal(k, (n,), jnp.float32)

    n_pages = PAST_LEN // PAGE_SIZE
    n_used = DEC_BATCH * n_pages
    n_pool = n_used + max(1, n_used // 4)
    w_buf = min(WINDOW, PAST_LEN)
    page_table = jax.random.permutation(ks[0], n_pool)[:n_used].reshape(DEC_BATCH, n_pages).astype(jnp.int32)
    return {
        "x_prompt": nrm(ks[1], (BATCH, SEQ, D_MODEL)),
        "x_sample": nrm(ks[2], (DEC_BATCH, DEC_SEQ, D_MODEL)),
        "cache_a_kv": nrm(ks[3], (n_pool, PAGE_SIZE, A_KV_HEADS, 2, HEAD_DIM)),
        "cache_a_idx": nrm(ks[4], (n_pool, PAGE_SIZE, IDX_DIM)),
        "cache_cmp_kv": nrm(ks[5], (n_pool, PAGE_SIZE, B_KV_HEADS, 2, HEAD_DIM)),
        "cache_slc_kv": nrm(ks[6], (n_pool, PAGE_SIZE, B_KV_HEADS, 2, HEAD_DIM)),
        "state_win_kv": nrm(ks[7], (DEC_BATCH, w_buf, B_KV_HEADS, 2, HEAD_DIM)),
        "page_table": page_table,
        "norm_gain": gain(ks[8], D_MODEL),
        "w_in": nrm(ks[9], (D_MODEL, IN_COLS), D_MODEL ** -0.5),
        "q_norm_a": gain(ks[10], HEAD_DIM),
        "k_norm_a": gain(ks[11], HEAD_DIM),
        "q_norm_b": gain(ks[12], HEAD_DIM),
        "k_norm_cmp": gain(ks[13], HEAD_DIM),
        "k_norm_slc": gain(ks[14], HEAD_DIM),
        "k_norm_win": gain(ks[15], HEAD_DIM),
        "pe_cmp": nrm(ks[16], (2, CMP_LEN, HEAD_DIM), 0.5),
        "w1_cmp": nrm(ks[17], (2, CMP_LEN, HEAD_DIM, CMP_HIDDEN), (CMP_LEN * HEAD_DIM) ** -0.5),
        "w2_cmp": nrm(ks[18], (2, CMP_HIDDEN, HEAD_DIM), CMP_HIDDEN ** -0.5),
        "w_proj_a": nrm(ks[19], (A_WIDTH, D_MODEL), A_WIDTH ** -0.5),
        "w_proj_b": nrm(ks[20], (B_WIDTH, D_MODEL), B_WIDTH ** -0.5),
        "w_out": nrm(ks[21], (D_MODEL, D_MODEL), D_MODEL ** -0.5),
    }


def reference(x_prompt, x_sample, cache_a_kv, cache_a_idx, cache_cmp_kv, cache_slc_kv, state_win_kv,
              page_table, norm_gain, w_in, q_norm_a, k_norm_a, q_norm_b, k_norm_cmp, k_norm_slc,
              k_norm_win, pe_cmp, w1_cmp, w2_cmp, w_proj_a, w_proj_b, w_out):
    proj_w = (norm_gain, w_in, q_norm_a, k_norm_a, q_norm_b, k_norm_slc, k_norm_win)
    cmp_w = (pe_cmp, w1_cmp, w2_cmp, k_norm_cmp)
    out_w = (w_proj_a, w_proj_b, w_out)
    kv_heads = jnp.arange(B_KV_HEADS)[None, None, :, None]
    block_off = jnp.arange(SLC_BLOCK)

    b, t, _ = x_prompt.shape
    pos_p = jnp.arange(t, dtype=jnp.int32)[None]
    pp = _project(x_prompt, pos_p, *proj_w)
    chunks_p = pp["cmp_kv"].reshape(b, t // CMP_STRIDE, CMP_STRIDE, B_KV_HEADS, 2, HEAD_DIM)
    f_p, s_p = _chunk_terms(chunks_p, w1_cmp)
    kc_p, vc_p = _compress(f_p, s_p, *cmp_w)
    n_blk_p = t // SLC_BLOCK
    slc_blocks = pp["slc_kv"].reshape(b, n_blk_p, SLC_BLOCK, B_KV_HEADS, 2, HEAD_DIM)
    bidx = jnp.arange(b)

    def gather_kv_p(sel):
        return pp["a_kv"][bidx[:, None, None], sel]

    def gather_blocks_p(sel):
        rows = slc_blocks[bidx[:, None, None, None], sel, :, kv_heads]
        pos = sel[..., None] * SLC_BLOCK + block_off
        return rows.reshape(sel.shape[:3] + (-1, 2, HEAD_DIM)), pos.reshape(sel.shape[:3] + (-1,))

    nqb = t // Q_BLOCK

    def to_blocks(a):
        return jnp.moveaxis(a.reshape((b, nqb, Q_BLOCK) + a.shape[2:]), 1, 0)

    def from_blocks(a):
        return jnp.moveaxis(a, 0, 1).reshape((b, t) + a.shape[3:])

    k_top_p = min(A_TOPK, t // 4)
    n_sel_p = min(N_SEL, n_blk_p)

    def query_block(args):
        i, qa, iq, iw, qb, qb_rot = args
        qpos = (i * Q_BLOCK + jnp.arange(Q_BLOCK, dtype=jnp.int32))[None]
        o_a = _dsa_attention(qa, iq, iw, qpos, pp["ik"], gather_kv_p, k_top_p)
        o_cmp, o_slc = _nsa_sparse(qb, qb_rot, qpos, kc_p, vc_p, n_blk_p, n_sel_p, gather_blocks_p)
        return o_a, o_cmp, o_slc

    o_a_p, o_cmp_p, o_slc_p = lax.map(
        query_block,
        (jnp.arange(nqb, dtype=jnp.int32), to_blocks(pp["qa"]), to_blocks(pp["iq"]), to_blocks(pp["iw"]),
         to_blocks(pp["qb"]), to_blocks(pp["qb_rot"])))
    o_a_p = from_blocks(o_a_p)
    o_cmp_p = from_blocks(o_cmp_p)
    o_slc_p = from_blocks(o_slc_p)

    wb = WINDOW // Q_BLOCK
    padded = jnp.pad(pp["win_kv"], ((0, 0), (WINDOW, 0), (0, 0), (0, 0), (0, 0)))
    padded = padded.reshape((b, nqb + wb, Q_BLOCK) + padded.shape[2:])
    band = jnp.concatenate([padded[:, j:j + nqb] for j in range(wb + 1)], axis=2)
    kpos_band = ((jnp.arange(nqb)[:, None] - wb) * Q_BLOCK + jnp.arange((wb + 1) * Q_BLOCK)[None])[None]
    qpos_band = jnp.arange(t, dtype=jnp.int32).reshape(1, nqb, Q_BLOCK)
    o_win_p = _window_attention(pp["qb_rot"].reshape(b, nqb, Q_BLOCK, B_HEADS, HEAD_DIM), qpos_band, band,
                                kpos_band).reshape(b, t, B_HEADS, HEAD_DIM)
    y_prompt = _combine(x_prompt, pp, o_a_p, o_cmp_p, o_slc_p, o_win_p, *out_w)
    p_win_kv = pp["win_kv"][:, t - min(WINDOW, t):]

    db, ds, _ = x_sample.shape
    total = PAST_LEN + ds
    pos_s = (PAST_LEN + jnp.arange(ds, dtype=jnp.int32))[None]
    ps = _project(x_sample, pos_s, *proj_w)

    ik_all = jnp.concatenate([cache_a_idx[page_table].reshape(db, PAST_LEN, IDX_DIM), ps["ik"]], axis=1)

    def gather_kv_s(sel):
        return _gather_rows(cache_a_kv, page_table, ps["a_kv"], sel)

    o_a_s = _dsa_attention(ps["qa"], ps["iq"], ps["iw"], pos_s, ik_all, gather_kv_s, min(A_TOPK, total // 4))

    past_chunks = cache_cmp_kv[page_table].reshape(db, PAST_LEN // CMP_STRIDE, CMP_STRIDE, B_KV_HEADS, 2, HEAD_DIM)
    pad = (-ds) % CMP_STRIDE
    new_chunks = jnp.pad(ps["cmp_kv"], ((0, 0), (0, pad), (0, 0), (0, 0), (0, 0))).reshape(
        db, -1, CMP_STRIDE, B_KV_HEADS, 2, HEAD_DIM)
    f_past, s_past = _chunk_terms(past_chunks, w1_cmp)
    f_new, s_new = _chunk_terms(new_chunks, w1_cmp)
    kc_s, vc_s = _compress(jnp.concatenate([f_past, f_new], axis=1), jnp.concatenate([s_past, s_new], axis=1), *cmp_w)
    n_blk_s = -(-total // SLC_BLOCK)

    def gather_blocks_s(sel):
        pos = (sel[..., None] * SLC_BLOCK + block_off).reshape(sel.shape[:3] + (-1,))
        return _gather_rows(cache_slc_kv, page_table, ps["slc_kv"], pos, kv_heads), pos

    o_cmp_s, o_slc_s = _nsa_sparse(ps["qb"], ps["qb_rot"], pos_s, kc_s, vc_s, n_blk_s, min(N_SEL, n_blk_s),
                                   gather_blocks_s)

    w_buf = state_win_kv.shape[1]
    keys_s = jnp.concatenate([state_win_kv, ps["win_kv"]], axis=1)
    kpos_s = (PAST_LEN - w_buf + jnp.arange(w_buf + ds, dtype=jnp.int32))[None]
    o_win_s = _window_attention(ps["qb_rot"], pos_s, keys_s, kpos_s)
    y_sample = _combine(x_sample, ps, o_a_s, o_cmp_s, o_slc_s, o_win_s, *out_w)
    s_win_kv = keys_s[:, keys_s.shape[1] - w_buf:]

    return (y_prompt, y_sample,
            pp["a_kv"], pp["ik"], pp["cmp_kv"], pp["slc_kv"], p_win_kv,
            ps["a_kv"], ps["ik"], ps["cmp_kv"], ps["slc_kv"], s_win_kv)
```

```python
import functools

import jax
import jax.numpy as jnp
from jax import lax
from jax.experimental import pallas as pl
from jax.experimental.pallas import tpu as pltpu

D_MODEL = 2048
HEAD_DIM = 128
ROT_DIM = HEAD_DIM // 4
N_HEADS = D_MODEL // (2 * HEAD_DIM)
KV_HEADS = 2
GROUP = N_HEADS // KV_HEADS
IDX_HEADS = 8
IDX_DIM = 64
IDX_ROT_DIM = IDX_DIM // 4
A_TOPK = 256
CMP_LEN = 32
CMP_STRIDE = 16
CMP_HIDDEN = 128
SLC_BLOCK = 64
N_SEL = 16
WINDOW = 512
PAGE_SIZE = 128
ROPE_THETA = 500000.0
EPS = 1e-6
NEG = -1e30
WIDTH = N_HEADS * HEAD_DIM
KV_ROW = KV_HEADS * 2 * HEAD_DIM
SCALE = HEAD_DIM ** -0.5

LANES = 128
SUBLANES = 8
VMEM_LIMIT = 48 * 1024 * 1024
PAGES_PER_STEP = 16
SAMPLE_Q = 8

F32 = jnp.float32
BF16 = jnp.bfloat16
NT_DIMS = (((1,), (1,)), ((), ()))


def _params(n_axes):
    return pltpu.CompilerParams(dimension_semantics=("arbitrary",) * n_axes,
                                vmem_limit_bytes=VMEM_LIMIT)


def _dot(a, b):
    return jnp.dot(a, b, preferred_element_type=F32)


def _dot_nt(a, b):
    return lax.dot_general(a, b, NT_DIMS, preferred_element_type=F32)


def _rope_tables(pos, rot, period, width=LANES):
    half = rot // 2
    inv = ROPE_THETA ** (-jnp.arange(half, dtype=F32) / half)
    ang = pos.astype(F32)[:, None] * inv
    cos, sin = jnp.cos(ang), jnp.sin(ang)
    n = pos.shape[0]
    zeros = lambda w: jnp.zeros((n, w), F32)
    ones = lambda w: jnp.ones((n, w), F32)
    c = jnp.concatenate([cos, cos, ones(period - rot)], axis=1)
    s1 = jnp.concatenate([-sin, zeros(period - half)], axis=1)
    s2 = jnp.concatenate([zeros(half), sin, zeros(period - rot)], axis=1)
    return c, s1, s2


def _apply_rope(y, c, s1, s2, half):
    return y * c + pltpu.roll(y, LANES - half, 1) * s1 + pltpu.roll(y, half, 1) * s2


def _head_norm(z, g):
    return z * lax.rsqrt(jnp.mean(z * z, axis=-1, keepdims=True) + EPS) * g


def _rmsnorm_kernel(x_ref, g_ref, o_ref):
    x = x_ref[...]
    y = x * lax.rsqrt(jnp.mean(x * x, axis=-1, keepdims=True) + EPS)
    o_ref[...] = (y * g_ref[...]).astype(o_ref.dtype)


def _rmsnorm(x, gain, tr):
    n, d = x.shape
    return pl.pallas_call(
        _rmsnorm_kernel,
        grid=(n // tr,),
        in_specs=[pl.BlockSpec((tr, d), lambda i: (i, 0)), pl.BlockSpec((1, d), lambda i: (0, 0))],
        out_specs=pl.BlockSpec((tr, d), lambda i: (i, 0)),
        out_shape=jax.ShapeDtypeStruct((n, d), BF16),
        compiler_params=_params(1),
        name="rmsnorm",
    )(x, gain.reshape(1, d))


def _proj_q_kernel(h_ref, w_ref, g_ref, c_ref, s1_ref, s2_ref, rot_ref, *plain_ref):
    z = _dot(h_ref[...], w_ref[...])
    g = g_ref[...]
    c, s1, s2 = c_ref[...], s1_ref[...], s2_ref[...]
    for hd in range(N_HEADS):
        sl = slice(hd * HEAD_DIM, (hd + 1) * HEAD_DIM)
        y = _head_norm(z[:, sl], g)
        if plain_ref:
            plain_ref[0][:, sl] = y.astype(BF16)
        rot_ref[:, sl] = _apply_rope(y, c, s1, s2, ROT_DIM // 2).astype(BF16)


def _proj_q(h, w, gain, tabs, tr, n_tab_blocks, want_plain):
    n = h.shape[0]
    row = lambda i: (i, 0)
    fix = lambda i: (0, 0)
    tab = lambda i: (i % n_tab_blocks, 0)
    out_shape = [jax.ShapeDtypeStruct((n, WIDTH), BF16)] * (2 if want_plain else 1)
    out_specs = [pl.BlockSpec((tr, WIDTH), row)] * (2 if want_plain else 1)
    return pl.pallas_call(
        _proj_q_kernel,
        grid=(n // tr,),
        in_specs=[pl.BlockSpec((tr, D_MODEL), row), pl.BlockSpec((D_MODEL, WIDTH), fix),
                  pl.BlockSpec((1, HEAD_DIM), fix)] + [pl.BlockSpec((tr, LANES), tab)] * 3,
        out_specs=out_specs,
        out_shape=out_shape,
        compiler_params=_params(1),
        name="proj_q",
    )(h, w, gain.reshape(1, HEAD_DIM), *tabs)


def _proj_kv_kernel(h_ref, w_ref, ga_ref, gs_ref, gw_ref, c_ref, s1_ref, s2_ref,
                    akv_ref, cmp_ref, slc_ref, win_ref, akv_b_ref, slc_b_ref, win_b_ref):
    z = _dot(h_ref[...], w_ref[...])
    c, s1, s2 = c_ref[...], s1_ref[...], s2_ref[...]

    def emit(base, gain_ref, f32_ref, bf_ref):
        for j in range(KV_HEADS * 2):
            sl = slice(j * HEAD_DIM, (j + 1) * HEAD_DIM)
            v = z[:, base + j * HEAD_DIM: base + (j + 1) * HEAD_DIM]
            if gain_ref is not None and j % 2 == 0:
                v = _apply_rope(_head_norm(v, gain_ref[...]), c, s1, s2, ROT_DIM // 2)
            f32_ref[:, sl] = v
            if bf_ref is not None:
                bf_ref[:, sl] = v.astype(BF16)

    emit(0 * KV_ROW, ga_ref, akv_ref, akv_b_ref)
    emit(1 * KV_ROW, None, cmp_ref, None)
    emit(2 * KV_ROW, gs_ref, slc_ref, slc_b_ref)
    emit(3 * KV_ROW, gw_ref, win_ref, win_b_ref)


def _proj_kv(h, w, g_a, g_slc, g_win, tabs, tr, n_tab_blocks):
    n = h.shape[0]
    row = lambda i: (i, 0)
    fix = lambda i: (0, 0)
    tab = lambda i: (i % n_tab_blocks, 0)
    f32_out = jax.ShapeDtypeStruct((n, KV_ROW), F32)
    bf_out = jax.ShapeDtypeStruct((n, KV_ROW), BF16)
    return pl.pallas_call(
        _proj_kv_kernel,
        grid=(n // tr,),
        in_specs=[pl.BlockSpec((tr, D_MODEL), row), pl.BlockSpec((D_MODEL, 4 * KV_ROW), fix)]
                 + [pl.BlockSpec((1, HEAD_DIM), fix)] * 3 + [pl.BlockSpec((tr, LANES), tab)] * 3,
        out_specs=[pl.BlockSpec((tr, KV_ROW), row)] * 7,
        out_shape=[f32_out] * 4 + [bf_out] * 3,
        compiler_params=_params(1),
        name="proj_kv",
    )(h, w, g_a.reshape(1, HEAD_DIM), g_slc.reshape(1, HEAD_DIM), g_win.reshape(1, HEAD_DIM), *tabs)


IDX_COLS = IDX_HEADS * IDX_DIM
MISC_IK = IDX_DIM
MISC_IW = MISC_IK + IDX_HEADS
MISC_G = MISC_IW + 3 * N_HEADS


def _proj_idx_kernel(h_ref, w_ref, c_ref, s1_ref, s2_ref, cm_ref, s1m_ref, s2m_ref, iq_ref, misc_ref):
    z = _dot(h_ref[...], w_ref[...])
    half = IDX_ROT_DIM // 2
    c, s1, s2 = c_ref[...], s1_ref[...], s2_ref[...]
    for j in range(IDX_COLS // LANES):
        sl = slice(j * LANES, (j + 1) * LANES)
        iq_ref[:, sl] = _apply_rope(z[:, sl], c, s1, s2, half).astype(BF16)
    m = z[:, IDX_COLS:IDX_COLS + LANES]
    roped = _apply_rope(m, cm_ref[...], s1m_ref[...], s2m_ref[...], half)
    lane = lax.broadcasted_iota(jnp.int32, m.shape, 1)
    misc_ref[...] = jnp.where(lane < MISC_IK, roped,
                              jnp.where(lane < MISC_IW, m * (IDX_HEADS ** -0.5),
                                        jnp.where(lane < MISC_G, jax.nn.sigmoid(m), 0.0)))


def _proj_idx(h, w, tabs_iq, tabs_misc, tr, n_tab_blocks):
    n = h.shape[0]
    row = lambda i: (i, 0)
    fix = lambda i: (0, 0)
    tab = lambda i: (i % n_tab_blocks, 0)
    return pl.pallas_call(
        _proj_idx_kernel,
        grid=(n // tr,),
        in_specs=[pl.BlockSpec((tr, D_MODEL), row), pl.BlockSpec((D_MODEL, IDX_COLS + LANES), fix)]
                 + [pl.BlockSpec((tr, LANES), tab)] * 6,
        out_specs=[pl.BlockSpec((tr, IDX_COLS), row), pl.BlockSpec((tr, LANES), row)],
        out_shape=[jax.ShapeDtypeStruct((n, IDX_COLS), BF16), jax.ShapeDtypeStruct((n, LANES), F32)],
        compiler_params=_params(1),
        name="proj_idx",
    )(h, w, *tabs_iq, *tabs_misc)


def _proj_z_kernel(h_ref, w_ref, za_ref, zb_ref):
    z = _dot(h_ref[...], w_ref[...])
    s = z * jax.nn.sigmoid(z)
    za_ref[...] = s[:, :WIDTH]
    zb_ref[...] = s[:, WIDTH:]


def _proj_z(h, w, tr):
    n = h.shape[0]
    row = lambda i: (i, 0)
    out = jax.ShapeDtypeStruct((n, WIDTH), F32)
    return pl.pallas_call(
        _proj_z_kernel,
        grid=(n // tr,),
        in_specs=[pl.BlockSpec((tr, D_MODEL), row), pl.BlockSpec((D_MODEL, 2 * WIDTH), lambda i: (0, 0))],
        out_specs=[pl.BlockSpec((tr, WIDTH), row)] * 2,
        out_shape=[out, out],
        compiler_params=_params(1),
        name="proj_z",
    )(h, w)


def _key_to_float(u):
    int_min = jnp.int32(-2 ** 31)
    bits = jnp.where(u < 0, u ^ int_min, ~u)
    return lax.bitcast_convert_type(bits, F32)


def _lane_fold(w):
    acc = w[:, :LANES]
    for j in range(1, w.shape[1] // LANES):
        acc = acc + w[:, j * LANES:(j + 1) * LANES]
    return acc


def _topk_threshold(load, nck, k, idx_bits, rows, cw):
    kf = jnp.float32(k)

    def count(pred_value):
        def body(c, acc):
            return acc + _lane_fold(pred_value(c, load(c)))
        acc = lax.fori_loop(0, nck, body, jnp.zeros((rows, LANES), F32))
        return jnp.sum(acc, axis=1, keepdims=True)

    def value_bit(it, u):
        cand = u | jnp.left_shift(jnp.int32(1), 31 - it)
        candf = _key_to_float(cand)
        cnt = count(lambda c, x: jnp.where(x >= candf, 1.0, 0.0))
        return jnp.where(cnt >= kf, cand, u)

    u = lax.fori_loop(0, 32, value_bit, jnp.zeros((rows, 1), jnp.int32))
    thr = jnp.where((u >= 0) & (u < 2 ** 23), -jnp.inf, _key_to_float(u))
    need = kf - count(lambda c, x: jnp.where(x > thr, 1.0, 0.0))

    lane = lax.broadcasted_iota(jnp.int32, (1, cw), 1)

    def index_bit(it, v):
        cand = v | jnp.left_shift(jnp.int32(1), idx_bits - 1 - it)
        below = count(lambda c, x: jnp.where(x == thr, jnp.where(c * cw + lane < cand, 1.0, 0.0), 0.0))
        return jnp.where(below < need, cand, v)

    v = lax.fori_loop(0, idx_bits, index_bit, jnp.zeros((rows, 1), jnp.int32))
    return thr, jnp.where(need >= 1.0, v, -1)


def _topk_mask(x, idx, thr, vmax):
    gt = jnp.where(x > thr, 1.0, 0.0)
    eq = jnp.where(x == thr, jnp.where(idx <= vmax, 1.0, 0.0), 0.0)
    return jnp.maximum(gt, eq)


def _softmax_init(rows):
    return (jnp.full((rows, 1), NEG, F32), jnp.zeros((rows, 1), F32), jnp.zeros((rows, HEAD_DIM), F32))


def _softmax_step(carry, s, maskf, v):
    m, l, acc = carry
    sm = jnp.where(maskf > 0.0, s, NEG)
    m_new = jnp.maximum(m, jnp.max(sm, axis=1, keepdims=True))
    alpha = jnp.exp(m - m_new)
    e = jnp.exp(sm - m_new) * maskf
    l = alpha * l + jnp.sum(e, axis=1, keepdims=True)
    acc = alpha * acc + _dot(e.astype(BF16), v)
    return m_new, l, acc


def _softmax_done(carry):
    _, l, acc = carry
    return acc / jnp.maximum(l, 1e-30)


def _dsa_prompt_kernel(iq_ref, misc_ref, ik2_ref, qa_ref, akv_ref, sza_ref, o_ref, sc_ref, mk_ref,
                       *, tq, tk, k_top, idx_bits):
    i = pl.program_id(1)
    nck = ((i + 1) * tq + tk - 1) // tk
    qpos = i * tq + lax.broadcasted_iota(jnp.int32, (tq, 1), 0)
    lane = lax.broadcasted_iota(jnp.int32, (1, LANES), 1)
    kiota = lax.broadcasted_iota(jnp.int32, (1, tk), 1)
    misc = misc_ref[...]

    iqs, iws = [], []
    for hd in range(IDX_HEADS):
        pair = iq_ref[:, (hd // 2) * LANES:(hd // 2 + 1) * LANES]
        keep = (lane < IDX_DIM) if hd % 2 == 0 else (lane >= IDX_DIM)
        iqs.append(jnp.where(keep, pair, jnp.zeros_like(pair)))
        iws.append(misc[:, MISC_IK + hd:MISC_IK + hd + 1] * (IDX_DIM ** -0.5))

    def score_chunk(c, carry):
        k0 = pl.multiple_of(c * tk, tk)
        ikc = ik2_ref[pl.ds(k0, tk), :]
        acc = jnp.zeros((tq, tk), F32)
        for hd in range(IDX_HEADS):
            acc = acc + jnp.maximum(_dot_nt(iqs[hd], ikc), 0.0) * iws[hd]
        sc_ref[c] = jnp.where(k0 + kiota <= qpos, acc, NEG)
        return carry

    lax.fori_loop(0, nck, score_chunk, 0)

    thr, vmax = _topk_threshold(lambda c: sc_ref[c], nck, k_top, idx_bits, tq, tk)

    def mask_chunk(c, carry):
        idx = c * tk + kiota
        mk_ref[c] = _topk_mask(sc_ref[c], idx, thr, vmax) * jnp.where(idx <= qpos, 1.0, 0.0)
        return carry

    lax.fori_loop(0, nck, mask_chunk, 0)

    for hd in range(N_HEADS):
        kvh = hd // GROUP
        sl = slice(hd * HEAD_DIM, (hd + 1) * HEAD_DIM)
        q = qa_ref[:, sl]

        def attend(c, carry, q=q, kvh=kvh):
            k0 = pl.multiple_of(c * tk, tk)
            kc = akv_ref[pl.ds(k0, tk), kvh * 2 * HEAD_DIM:(kvh * 2 + 1) * HEAD_DIM]
            vc = akv_ref[pl.ds(k0, tk), (kvh * 2 + 1) * HEAD_DIM:(kvh * 2 + 2) * HEAD_DIM]
            return _softmax_step(carry, _dot_nt(q, kc) * SCALE, mk_ref[c], vc)

        o = _softmax_done(lax.fori_loop(0, nck, attend, _softmax_init(tq)))
        o_ref[:, sl] = (o * sza_ref[:, sl]).astype(BF16)


def _dsa_prompt(iq, misc, ik2, qa, akv_b, sza, b, t, tq, tk):
    nqb = t // tq
    k_top = min(A_TOPK, t // 4)
    idx_bits = max(1, (t - 1).bit_length())
    row = lambda bi, i: (bi * nqb + i, 0)
    whole = lambda bi, i: (bi, 0, 0)
    kern = functools.partial(_dsa_prompt_kernel, tq=tq, tk=tk, k_top=k_top, idx_bits=idx_bits)
    return pl.pallas_call(
        kern,
        grid=(b, nqb),
        in_specs=[pl.BlockSpec((tq, IDX_COLS), row), pl.BlockSpec((tq, LANES), row),
                  pl.BlockSpec((None, t, LANES), whole), pl.BlockSpec((tq, WIDTH), row),
                  pl.BlockSpec((None, t, KV_ROW), whole), pl.BlockSpec((tq, WIDTH), row)],
        out_specs=pl.BlockSpec((tq, WIDTH), row),
        out_shape=jax.ShapeDtypeStruct((b * t, WIDTH), BF16),
        scratch_shapes=[pltpu.VMEM((t // tk, tq, tk), F32), pltpu.VMEM((t // tk, tq, tk), F32)],
        compiler_params=_params(2),
        name="dsa_prompt",
    )(iq, misc, ik2, qa, akv_b, sza)


def _block_rules(imp, blk, qpos):
    cur = qpos // SLC_BLOCK
    return jnp.where(blk == cur, -NEG, jnp.where(blk * SLC_BLOCK > qpos, NEG, imp))


def _split_dot(p, m):
    hi = p.astype(BF16)
    lo = (p - hi.astype(F32)).astype(BF16)
    return _dot(hi, m) + _dot(lo, m)


def _nsa_prompt_kernel(qb_ref, qr_ref, kc_ref, vc_ref, slc_ref, win_ref, ov_ref, misc_ref, szb_ref,
                       o_ref, sc_ref, *, tq, tk, n_cmp, n_blk, n_sel):
    i = pl.program_id(1)
    nck = ((i + 1) * tq + tk - 1) // tk
    qpos = i * tq + lax.broadcasted_iota(jnp.int32, (tq, 1), 0)
    kiota = lax.broadcasted_iota(jnp.int32, (1, tk), 1)
    misc = misc_ref[...]
    ncp = kc_ref.shape[0]
    cidx = lax.broadcasted_iota(jnp.int32, (1, ncp), 1)
    cmask = jnp.where((CMP_STRIDE * cidx + CMP_LEN - 1 <= qpos) & (cidx < n_cmp), 1.0, 0.0)
    blk = lax.broadcasted_iota(jnp.int32, (1, LANES), 1)

    o_cmp = []
    for kvh in range(KV_HEADS):
        kc = kc_ref[:, kvh * HEAD_DIM:(kvh + 1) * HEAD_DIM]
        vc = vc_ref[:, kvh * HEAD_DIM:(kvh + 1) * HEAD_DIM]
        psum = jnp.zeros((tq, ncp), F32)
        for g in range(GROUP):
            hd = kvh * GROUP + g
            s = _dot_nt(qb_ref[:, hd * HEAD_DIM:(hd + 1) * HEAD_DIM], kc) * SCALE
            sm = jnp.where(cmask > 0.0, s, NEG)
            e = jnp.exp(sm - jnp.max(sm, axis=1, keepdims=True)) * cmask
            p = e / jnp.maximum(jnp.sum(e, axis=1, keepdims=True), 1e-30)
            o_cmp.append(_dot(p.astype(BF16), vc))
            psum = psum + p
        imp = _split_dot(psum, ov_ref[...])
        imp = _block_rules(imp, blk, qpos)
        sc_ref[kvh] = jnp.where(blk < n_blk, imp, -jnp.inf)

    rows = KV_HEADS * tq
    load = lambda c: sc_ref[...].reshape(rows, LANES)
    thr, vmax = _topk_threshold(load, 1, n_sel, 7, rows, LANES)
    bmask = _topk_mask(load(0), blk, thr, vmax).astype(BF16)

    riota = lax.broadcasted_iota(jnp.int32, (LANES, 1), 0)
    g_nsa = [misc[:, MISC_IW + j:MISC_IW + j + 1] for j in range(3 * N_HEADS)]
    nwb = (WINDOW + tq) // LANES

    for hd in range(N_HEADS):
        kvh = hd // GROUP
        sl = slice(hd * HEAD_DIM, (hd + 1) * HEAD_DIM)
        q = qr_ref[:, sl]
        bm = bmask[kvh * tq:(kvh + 1) * tq]

        def attend(c, carry, q=q, kvh=kvh, bm=bm):
            k0 = pl.multiple_of(c * tk, tk)
            kc = slc_ref[pl.ds(k0, tk), kvh * 2 * HEAD_DIM:(kvh * 2 + 1) * HEAD_DIM]
            vc = slc_ref[pl.ds(k0, tk), (kvh * 2 + 1) * HEAD_DIM:(kvh * 2 + 2) * HEAD_DIM]
            kpos = k0 + kiota
            expand = jnp.where(riota == kpos // SLC_BLOCK, 1.0, 0.0).astype(BF16)
            mk = _dot(bm, expand) * jnp.where(kpos <= qpos, 1.0, 0.0)
            return _softmax_step(carry, _dot_nt(q, kc) * SCALE, mk, vc)

        o_slc = _softmax_done(lax.fori_loop(0, nck, attend, _softmax_init(tq)))

        carry = _softmax_init(tq)
        for j in range(nwb):
            kb = i * tq - WINDOW + j * LANES
            k0 = pl.multiple_of(jnp.maximum(kb, 0), LANES)
            kc = win_ref[pl.ds(k0, LANES), kvh * 2 * HEAD_DIM:(kvh * 2 + 1) * HEAD_DIM]
            vc = win_ref[pl.ds(k0, LANES), (kvh * 2 + 1) * HEAD_DIM:(kvh * 2 + 2) * HEAD_DIM]
            d = qpos - (kb + blk)
            mk = jnp.where((d >= 0) & (d < WINDOW) & (kb >= 0), 1.0, 0.0)
            carry = _softmax_step(carry, _dot_nt(q, kc) * SCALE, mk, vc)
        o_win = _softmax_done(carry)

        o_b = g_nsa[hd] * o_cmp[hd] + g_nsa[N_HEADS + hd] * o_slc + g_nsa[2 * N_HEADS + hd] * o_win
        o_ref[:, sl] = (o_b * szb_ref[:, sl]).astype(BF16)


def _overlap_matrix(n_cmp_pad, n_cols, col_block):
    i = jnp.arange(n_cmp_pad)[:, None]
    j = col_block[None, :]
    ov = (CMP_STRIDE * i < SLC_BLOCK * (j + 1)) & (CMP_STRIDE * i + CMP_LEN > SLC_BLOCK * j) & (j >= 0)
    return ov.astype(BF16)


def _nsa_prompt(qb, qr, kc, vc, slc_b, win_b, misc, szb, b, t, tq, tk):
    nqb = t // tq
    n_cmp = t // CMP_STRIDE - 1
    n_blk = t // SLC_BLOCK
    assert n_blk <= LANES
    ncp = kc.shape[1]
    ov = _overlap_matrix(ncp, LANES, jnp.where(jnp.arange(LANES) < n_blk, jnp.arange(LANES), -1))
    row = lambda bi, i: (bi * nqb + i, 0)
    whole = lambda bi, i: (bi, 0, 0)
    kern = functools.partial(_nsa_prompt_kernel, tq=tq, tk=tk, n_cmp=n_cmp, n_blk=n_blk,
                             n_sel=min(N_SEL, n_blk))
    return pl.pallas_call(
        kern,
        grid=(b, nqb),
        in_specs=[pl.BlockSpec((tq, WIDTH), row), pl.BlockSpec((tq, WIDTH), row),
                  pl.BlockSpec((None, ncp, KV_HEADS * HEAD_DIM), whole),
                  pl.BlockSpec((None, ncp, KV_HEADS * HEAD_DIM), whole),
                  pl.BlockSpec((None, t, KV_ROW), whole), pl.BlockSpec((None, t, KV_ROW), whole),
                  pl.BlockSpec((ncp, LANES), lambda bi, i: (0, 0)),
                  pl.BlockSpec((tq, LANES), row), pl.BlockSpec((tq, WIDTH), row)],
        out_specs=pl.BlockSpec((tq, WIDTH), row),
        out_shape=jax.ShapeDtypeStruct((b * t, WIDTH), BF16),
        scratch_shapes=[pltpu.VMEM((KV_HEADS, tq, LANES), F32)],
        compiler_params=_params(2),
        name="nsa_prompt",
    )(qb, qr, kc, vc, slc_b, win_b, ov, misc, szb)


CHUNK_FLAT = CMP_STRIDE * KV_ROW
FS_COLS = KV_HEADS * 2 * 2 * CMP_HIDDEN


def _chunk_terms_body(x, w_ref, o_ref):
    for kc in range(KV_HEADS * 2):
        xk = jnp.concatenate(
            [x[:, p * KV_ROW + kc * HEAD_DIM: p * KV_ROW + (kc + 1) * HEAD_DIM] for p in range(CMP_STRIDE)],
            axis=1).astype(BF16)
        o_ref[:, kc * 2 * CMP_HIDDEN:(kc + 1) * 2 * CMP_HIDDEN] = _dot(xk, w_ref[kc % 2])


def _chunk_terms_dense_kernel(x_ref, w_ref, o_ref):
    _chunk_terms_body(x_ref[...], w_ref, o_ref)


def _chunk_terms_dense(x, w, tr):
    n = x.shape[0]
    return pl.pallas_call(
        _chunk_terms_dense_kernel,
        grid=(n // tr,),
        in_specs=[pl.BlockSpec((tr, CHUNK_FLAT), lambda i: (i, 0)),
                  pl.BlockSpec((2, CMP_STRIDE * HEAD_DIM, 2 * CMP_HIDDEN), lambda i: (0, 0, 0))],
        out_specs=pl.BlockSpec((tr, FS_COLS), lambda i: (i, 0)),
        out_shape=jax.ShapeDtypeStruct((n, FS_COLS), F32),
        compiler_params=_params(1),
        name="chunk_terms_dense",
    )(x, w)


def _chunk_terms_paged_kernel(pt_ref, *refs, pg):
    pages, w_ref, o_ref = refs[:pg], refs[pg], refs[pg + 1]
    x = jnp.concatenate([p[...] for p in pages], axis=0)
    _chunk_terms_body(x, w_ref, o_ref)


def _page_specs(block, pg, cols=0):
    def spec(j):
        return pl.BlockSpec(block, lambda bi, g, pt: (pt[bi, g * pg + j], 0, cols))
    return [spec(j) for j in range(pg)]


def _chunk_terms_paged(pool, page_table, w, pg):
    db, n_pages = page_table.shape
    cpp = PAGE_SIZE // CMP_STRIDE
    pool_v = pool.reshape(pool.shape[0], cpp, CHUNK_FLAT)
    kern = functools.partial(_chunk_terms_paged_kernel, pg=pg)
    grid_spec = pltpu.PrefetchScalarGridSpec(
        num_scalar_prefetch=1,
        grid=(db, n_pages // pg),
        in_specs=_page_specs((None, cpp, CHUNK_FLAT), pg)
                 + [pl.BlockSpec((2, CMP_STRIDE * HEAD_DIM, 2 * CMP_HIDDEN), lambda bi, g, pt: (0, 0, 0))],
        out_specs=pl.BlockSpec((None, pg * cpp, FS_COLS), lambda bi, g, pt: (bi, g, 0)),
    )
    return pl.pallas_call(
        kern, grid_spec=grid_spec,
        out_shape=jax.ShapeDtypeStruct((db, n_pages * cpp, FS_COLS), F32),
        compiler_params=_params(2),
        name="chunk_terms_paged",
    )(page_table, *([pool_v] * pg), w)


def _compress_mlp_kernel(fs_ref, new_ref, pe_ref, w1_ref, w2_ref, g_ref, kc_ref, vc_ref):
    n = fs_ref.shape[0]
    last = lax.broadcasted_iota(jnp.int32, (n, 1), 0) == n - 1
    for kc in range(KV_HEADS * 2):
        kvh, c = kc // 2, kc % 2
        base = kc * 2 * CMP_HIDDEN
        first = fs_ref[:, base:base + CMP_HIDDEN]
        second = fs_ref[:, base + CMP_HIDDEN:base + 2 * CMP_HIDDEN]
        shifted = jnp.where(last, new_ref[0:1, base + CMP_HIDDEN:base + 2 * CMP_HIDDEN],
                            pltpu.roll(second, n - 1, 0))
        pe = jnp.broadcast_to(pe_ref[c:c + 1, :], (SUBLANES, pe_ref.shape[1])).astype(BF16)
        bias = _dot(pe, w1_ref[c])[0:1, :]
        pre = first + shifted + bias
        hid = pre * jax.nn.sigmoid(pre)
        out = _dot(hid.astype(BF16), w2_ref[c])
        sl = slice(kvh * HEAD_DIM, (kvh + 1) * HEAD_DIM)
        if c == 0:
            kc_ref[:, sl] = _head_norm(out, g_ref[...]).astype(BF16)
        else:
            vc_ref[:, sl] = out.astype(BF16)


def _compress_mlp(fs, fs_new, pe, w1r, w2, g_cmp):
    nb, n, _ = fs.shape
    whole = lambda bi: (bi, 0, 0)
    fix2 = lambda bi: (0, 0)
    fix3 = lambda bi: (0, 0, 0)
    out = jax.ShapeDtypeStruct((nb, n, KV_HEADS * HEAD_DIM), BF16)
    return pl.pallas_call(
        _compress_mlp_kernel,
        grid=(nb,),
        in_specs=[pl.BlockSpec((None, n, FS_COLS), whole), pl.BlockSpec((None, SUBLANES, FS_COLS), whole),
                  pl.BlockSpec((2, CMP_LEN * HEAD_DIM), fix2),
                  pl.BlockSpec((2, CMP_LEN * HEAD_DIM, CMP_HIDDEN), fix3),
                  pl.BlockSpec((2, CMP_HIDDEN, HEAD_DIM), fix3), pl.BlockSpec((1, HEAD_DIM), fix2)],
        out_specs=[pl.BlockSpec((None, n, KV_HEADS * HEAD_DIM), whole)] * 2,
        out_shape=[out, out],
        compiler_params=_params(1),
        name="compress_mlp",
    )(fs, fs_new, pe, w1r, w2, g_cmp.reshape(1, HEAD_DIM))


def _idx_scores_kernel(pt_ref, *refs, pg):
    pages = refs[:pg]
    iq_ref, iw_ref, new_ref, o_ref, onew_ref = refs[pg:pg + 5]
    g = pl.program_id(1)
    iq = iq_ref[...]
    iw = iw_ref[...]

    def scores(keys):
        s = jnp.maximum(_dot_nt(iq, keys), 0.0)
        n = s.shape[1]
        w = s * jnp.concatenate([iw] * (n // LANES), axis=1)
        acc = w[0:SAMPLE_Q]
        for hd in range(1, IDX_HEADS):
            acc = acc + w[hd * SAMPLE_Q:(hd + 1) * SAMPLE_Q]
        return acc

    keys = jnp.concatenate([p[...] for p in pages], axis=0).astype(BF16)
    o_ref[...] = scores(keys)

    @pl.when(g == pl.num_programs(1) - 1)
    def _():
        onew_ref[...] = scores(new_ref[...])


def _idx_scores(pool_idx, page_table, iqm, iwm, ik_new, pg):
    db, n_pages = page_table.shape
    ng = n_pages // pg
    kern = functools.partial(_idx_scores_kernel, pg=pg)
    per_b = lambda bi, g, pt: (bi, 0, 0)
    grid_spec = pltpu.PrefetchScalarGridSpec(
        num_scalar_prefetch=1,
        grid=(db, ng),
        in_specs=_page_specs((None, PAGE_SIZE, IDX_DIM), pg)
                 + [pl.BlockSpec((None, IDX_HEADS * SAMPLE_Q, IDX_DIM), per_b),
                    pl.BlockSpec((None, IDX_HEADS * SAMPLE_Q, LANES), per_b),
                    pl.BlockSpec((None, LANES, IDX_DIM), per_b)],
        out_specs=[pl.BlockSpec((None, None, SAMPLE_Q, pg * PAGE_SIZE), lambda bi, g, pt: (bi, g, 0, 0)),
                   pl.BlockSpec((None, SAMPLE_Q, LANES), per_b)],
    )
    return pl.pallas_call(
        kern, grid_spec=grid_spec,
        out_shape=[jax.ShapeDtypeStruct((db, ng, SAMPLE_Q, pg * PAGE_SIZE), F32),
                   jax.ShapeDtypeStruct((db, SAMPLE_Q, LANES), F32)],
        compiler_params=_params(2),
        name="idx_scores",
    )(page_table, *([pool_idx] * pg), iqm, iwm, ik_new)


def _dsa_topk_kernel(sc_in_ref, new_ref, o_ref, onew_ref, sc_ref, *, ng, cw, past, n_new, k_top, idx_bits):
    row = lax.broadcasted_iota(jnp.int32, (SAMPLE_Q, 1), 0)
    qpos = past + row
    lane = lax.broadcasted_iota(jnp.int32, (1, cw), 1)
    for g in range(ng):
        sc_ref[g] = sc_in_ref[g]
    newv = jnp.concatenate([new_ref[...], jnp.full((SAMPLE_Q, cw - LANES), NEG, F32)], axis=1)
    sc_ref[ng] = jnp.where((past + lane <= qpos) & (lane < n_new), newv, NEG)

    thr, vmax = _topk_threshold(lambda c: sc_ref[c], ng + 1, k_top, idx_bits, SAMPLE_Q, cw)
    for g in range(ng):
        o_ref[g] = _topk_mask(sc_ref[g], g * cw + lane, thr, vmax)
    mnew = _topk_mask(sc_ref[ng], ng * cw + lane, thr, vmax)
    mnew = mnew * jnp.where((past + lane <= qpos) & (lane < n_new), 1.0, 0.0)
    onew_ref[...] = mnew[:, :LANES]


def _dsa_topk(scores, scores_new, past, n_new):
    db, ng, _, cw = scores.shape
    total = past + n_new
    kern = functools.partial(_dsa_topk_kernel, ng=ng, cw=cw, past=past, n_new=n_new,
                             k_top=min(A_TOPK, total // 4), idx_bits=((ng + 1) * cw - 1).bit_length())
    return pl.pallas_call(
        kern,
        grid=(db,),
        in_specs=[pl.BlockSpec((None, ng, SAMPLE_Q, cw), lambda bi: (bi, 0, 0, 0)),
                  pl.BlockSpec((None, SAMPLE_Q, LANES), lambda bi: (bi, 0, 0))],
        out_specs=[pl.BlockSpec((None, ng, SAMPLE_Q, cw), lambda bi: (bi, 0, 0, 0)),
                   pl.BlockSpec((None, SAMPLE_Q, LANES), lambda bi: (bi, 0, 0))],
        out_shape=[jax.ShapeDtypeStruct((db, ng, SAMPLE_Q, cw), F32),
                   jax.ShapeDtypeStruct((db, SAMPLE_Q, LANES), F32)],
        scratch_shapes=[pltpu.VMEM((ng + 1, SAMPLE_Q, cw), F32)],
        compiler_params=_params(1),
        name="dsa_topk",
    )(scores, scores_new)


QROWS = GROUP * SAMPLE_Q


def _paged_attn_kernel(pt_ref, *refs, pg, mask_rows):
    pages = refs[:pg]
    q_ref, mk_ref, mknew_ref, new_ref, o_ref, m_ref, l_ref, acc_ref = refs[pg:pg + 8]
    g = pl.program_id(1)

    @pl.when(g == 0)
    def _():
        m_ref[...] = jnp.full(m_ref.shape, NEG, F32)
        l_ref[...] = jnp.zeros(l_ref.shape, F32)
        acc_ref[...] = jnp.zeros(acc_ref.shape, F32)

    def update(kv, mk_all):
        for kvh in range(KV_HEADS):
            k = kv[:, kvh * 2 * HEAD_DIM:(kvh * 2 + 1) * HEAD_DIM]
            v = kv[:, (kvh * 2 + 1) * HEAD_DIM:(kvh * 2 + 2) * HEAD_DIM]
            r0 = kvh * SAMPLE_Q if mask_rows == KV_HEADS * SAMPLE_Q else 0
            mk = jnp.concatenate([mk_all[r0:r0 + SAMPLE_Q]] * GROUP, axis=0)
            carry = (m_ref[kvh], l_ref[kvh], acc_ref[kvh])
            m, l, acc = _softmax_step(carry, _dot_nt(q_ref[kvh], k) * SCALE, mk, v)
            m_ref[kvh], l_ref[kvh], acc_ref[kvh] = m, l, acc

    update(jnp.concatenate([p[...] for p in pages], axis=0).astype(BF16), mk_ref[...])

    @pl.when(g == pl.num_programs(1) - 1)
    def _():
        update(new_ref[...], mknew_ref[...])
        for kvh in range(KV_HEADS):
            o_ref[kvh] = _softmax_done((m_ref[kvh], l_ref[kvh], acc_ref[kvh]))


def _paged_attn(pool, page_table, q, mask, mask_new, kv_new, pg):
    db, n_pages = page_table.shape
    ng = n_pages // pg
    mask_rows = mask.shape[2]
    pool_v = pool.reshape(pool.shape[0], PAGE_SIZE, KV_ROW)
    kern = functools.partial(_paged_attn_kernel, pg=pg, mask_rows=mask_rows)
    per_b3 = lambda bi, g, pt: (bi, 0, 0)
    per_b4 = lambda bi, g, pt: (bi, 0, 0, 0)
    grid_spec = pltpu.PrefetchScalarGridSpec(
        num_scalar_prefetch=1,
        grid=(db, ng),
        in_specs=_page_specs((None, PAGE_SIZE, KV_ROW), pg)
                 + [pl.BlockSpec((None, KV_HEADS, QROWS, HEAD_DIM), per_b4),
                    pl.BlockSpec((None, None, mask_rows, pg * PAGE_SIZE), lambda bi, g, pt: (bi, g, 0, 0)),
                    pl.BlockSpec((None, mask_rows, LANES), per_b3),
                    pl.BlockSpec((None, LANES, KV_ROW), per_b3)],
        out_specs=pl.BlockSpec((None, KV_HEADS, QROWS, HEAD_DIM), per_b4),
        scratch_shapes=[pltpu.VMEM((KV_HEADS, QROWS, 1), F32), pltpu.VMEM((KV_HEADS, QROWS, 1), F32),
                        pltpu.VMEM((KV_HEADS, QROWS, HEAD_DIM), F32)],
    )
    return pl.pallas_call(
        kern, grid_spec=grid_spec,
        out_shape=jax.ShapeDtypeStruct((db, KV_HEADS, QROWS, HEAD_DIM), F32),
        compiler_params=_params(2),
        name="paged_attn",
    )(page_table, *([pool_v] * pg), q, mask, mask_new, kv_new)


def _nsa_sample_kernel(q_ref, kc_ref, vc_ref, ov_ref, ex_ref, o_ref, tm_ref, tmnew_ref, sc_ref,
                       *, past, n_new, n_cmp, n_blk, n_sel, ng, gb):
    row = lax.broadcasted_iota(jnp.int32, (SAMPLE_Q, 1), 0)
    qpos = past + row
    ncp = kc_ref.shape[0]
    cidx = lax.broadcasted_iota(jnp.int32, (1, ncp), 1)
    cmask8 = jnp.where((CMP_STRIDE * cidx + CMP_LEN - 1 <= qpos) & (cidx < n_cmp), 1.0, 0.0)
    cmask = jnp.concatenate([cmask8] * GROUP, axis=0)
    lane = lax.broadcasted_iota(jnp.int32, (1, LANES), 1)

    for kvh in range(KV_HEADS):
        kc = kc_ref[:, kvh * HEAD_DIM:(kvh + 1) * HEAD_DIM]
        vc = vc_ref[:, kvh * HEAD_DIM:(kvh + 1) * HEAD_DIM]
        s = _dot_nt(q_ref[kvh], kc) * SCALE
        sm = jnp.where(cmask > 0.0, s, NEG)
        e = jnp.exp(sm - jnp.max(sm, axis=1, keepdims=True)) * cmask
        p = e / jnp.maximum(jnp.sum(e, axis=1, keepdims=True), 1e-30)
        o_ref[kvh] = _dot(p.astype(BF16), vc)
        psum = p[0:SAMPLE_Q]
        for g in range(1, GROUP):
            psum = psum + p[g * SAMPLE_Q:(g + 1) * SAMPLE_Q]
        imp = _split_dot(psum, ov_ref[...])
        for g in range(ng + 1):
            blk = g * gb + lane
            v = _block_rules(imp[:, g * LANES:(g + 1) * LANES], blk, qpos)
            sc_ref[g, kvh * SAMPLE_Q:(kvh + 1) * SAMPLE_Q, :] = jnp.where((lane < gb) & (blk < n_blk), v, -jnp.inf)

    rows = KV_HEADS * SAMPLE_Q
    idx_bits = ((ng + 1) * LANES - 1).bit_length()
    thr, vmax = _topk_threshold(lambda c: sc_ref[c], ng + 1, n_sel, idx_bits, rows, LANES)
    qpos2 = jnp.concatenate([qpos] * KV_HEADS, axis=0)
    for g in range(ng + 1):
        bm = _topk_mask(sc_ref[g], g * LANES + lane, thr, vmax).astype(BF16)
        tok = _dot(bm, ex_ref[...])
        if g < ng:
            tm_ref[g] = tok
        else:
            t = lax.broadcasted_iota(jnp.int32, (1, LANES), 1)
            tmnew_ref[...] = tok[:, :LANES] * jnp.where((past + t <= qpos2) & (t < n_new), 1.0, 0.0)


def _nsa_sample(q, kc, vc, past, n_new, ng, cw):
    db = q.shape[0]
    ncp = kc.shape[1]
    total = past + n_new
    n_blk = -(-total // SLC_BLOCK)
    n_cmp = total_chunks(past, n_new) - 1
    gb = cw // SLC_BLOCK
    assert gb <= LANES and past % cw == 0
    col = jnp.arange((ng + 1) * LANES)
    col_block = jnp.where(col % LANES < gb, (col // LANES) * gb + col % LANES, -1)
    col_block = jnp.where(col_block < n_blk, col_block, -1)
    ov = _overlap_matrix(ncp, (ng + 1) * LANES, col_block)
    ex = (jnp.arange(LANES)[:, None] == (jnp.arange(cw)[None, :] // SLC_BLOCK)).astype(BF16)
    kern = functools.partial(_nsa_sample_kernel, past=past, n_new=n_new, n_cmp=n_cmp, n_blk=n_blk,
                             n_sel=min(N_SEL, n_blk), ng=ng, gb=gb)
    rows = KV_HEADS * SAMPLE_Q
    b3 = lambda bi: (bi, 0, 0)
    b4 = lambda bi: (bi, 0, 0, 0)
    return pl.pallas_call(
        kern,
        grid=(db,),
        in_specs=[pl.BlockSpec((None, KV_HEADS, QROWS, HEAD_DIM), b4),
                  pl.BlockSpec((None, ncp, KV_HEADS * HEAD_DIM), b3),
                  pl.BlockSpec((None, ncp, KV_HEADS * HEAD_DIM), b3),
                  pl.BlockSpec((ncp, (ng + 1) * LANES), lambda bi: (0, 0)),
                  pl.BlockSpec((LANES, cw), lambda bi: (0, 0))],
        out_specs=[pl.BlockSpec((None, KV_HEADS, QROWS, HEAD_DIM), b4),
                   pl.BlockSpec((None, ng, rows, cw), b4), pl.BlockSpec((None, rows, LANES), b3)],
        out_shape=[jax.ShapeDtypeStruct((db, KV_HEADS, QROWS, HEAD_DIM), F32),
                   jax.ShapeDtypeStruct((db, ng, rows, cw), F32),
                   jax.ShapeDtypeStruct((db, rows, LANES), F32)],
        scratch_shapes=[pltpu.VMEM((ng + 1, rows, LANES), F32)],
        compiler_params=_params(1),
        name="nsa_sample",
    )(q, kc, vc, ov, ex)


def total_chunks(past, n_new):
    return past // CMP_STRIDE + -(-n_new // CMP_STRIDE)


def _win_sample_kernel(q_ref, st_ref, new_ref, o_ref, *, w_buf, n_new):
    row = lax.broadcasted_iota(jnp.int32, (SAMPLE_Q, 1), 0)
    row = jnp.concatenate([row] * GROUP, axis=0)
    j_old = lax.broadcasted_iota(jnp.int32, (1, w_buf), 1)
    j_new = w_buf + lax.broadcasted_iota(jnp.int32, (1, LANES), 1)
    def mask(j, valid):
        d = w_buf + row - j
        return jnp.where((d >= 0) & (d < WINDOW) & valid, 1.0, 0.0)
    st = st_ref[...].astype(BF16)
    new = new_ref[...]
    for kvh in range(KV_HEADS):
        q = q_ref[kvh]
        carry = _softmax_init(QROWS)
        for kv, mk in ((st, mask(j_old, True)), (new, mask(j_new, j_new < w_buf + n_new))):
            k = kv[:, kvh * 2 * HEAD_DIM:(kvh * 2 + 1) * HEAD_DIM]
            v = kv[:, (kvh * 2 + 1) * HEAD_DIM:(kvh * 2 + 2) * HEAD_DIM]
            carry = _softmax_step(carry, _dot_nt(q, k) * SCALE, mk, v)
        o_ref[kvh] = _softmax_done(carry)


def _win_sample(q, state, kv_new, n_new):
    db, w_buf = state.shape[0], state.shape[1]
    kern = functools.partial(_win_sample_kernel, w_buf=w_buf, n_new=n_new)
    b3 = lambda bi: (bi, 0, 0)
    b4 = lambda bi: (bi, 0, 0, 0)
    return pl.pallas_call(
        kern,
        grid=(db,),
        in_specs=[pl.BlockSpec((None, KV_HEADS, QROWS, HEAD_DIM), b4),
                  pl.BlockSpec((None, w_buf, KV_ROW), b3), pl.BlockSpec((None, LANES, KV_ROW), b3)],
        out_specs=pl.BlockSpec((None, KV_HEADS, QROWS, HEAD_DIM), b4),
        out_shape=jax.ShapeDtypeStruct((db, KV_HEADS, QROWS, HEAD_DIM), F32),
        compiler_params=_params(1),
        name="win_sample",
    )(q, state.reshape(db, w_buf, KV_ROW), kv_new)


def _gate_merge_kernel(oa_ref, oc_ref, os_ref, ow_ref, misc_ref, sza_ref, szb_ref, a_ref, b_ref):
    misc = misc_ref[...]
    a_ref[...] = (oa_ref[...] * sza_ref[...]).astype(BF16)
    for hd in range(N_HEADS):
        sl = slice(hd * HEAD_DIM, (hd + 1) * HEAD_DIM)
        g = [misc[:, MISC_IW + j * N_HEADS + hd:MISC_IW + j * N_HEADS + hd + 1] for j in range(3)]
        o_b = g[0] * oc_ref[:, sl] + g[1] * os_ref[:, sl] + g[2] * ow_ref[:, sl]
        b_ref[:, sl] = (o_b * szb_ref[:, sl]).astype(BF16)


def _gate_merge(o_a, o_cmp, o_slc, o_win, misc, sza, szb):
    n = o_a.shape[0]
    full = lambda w: pl.BlockSpec((n, w), lambda i: (0, 0))
    out = jax.ShapeDtypeStruct((n, WIDTH), BF16)
    return pl.pallas_call(
        _gate_merge_kernel,
        grid=(1,),
        in_specs=[full(WIDTH)] * 4 + [full(LANES), full(WIDTH), full(WIDTH)],
        out_specs=[full(WIDTH)] * 2,
        out_shape=[out, out],
        compiler_params=_params(1),
        name="gate_merge",
    )(o_a, o_cmp, o_slc, o_win, misc, sza, szb)


def _merge_kernel(a_ref, b_ref, h_ref, wa_ref, wb_ref, wg0_ref, wg1_ref, o_ref):
    h = h_ref[...]
    y_a = _dot(a_ref[...], wa_ref[...])
    y_b = _dot(b_ref[...], wb_ref[...])
    g0 = jax.nn.sigmoid(_dot(h, wg0_ref[...]))
    g1 = jax.nn.sigmoid(_dot(h, wg1_ref[...]))
    o_ref[...] = (g0 * y_a + g1 * y_b).astype(BF16)


def _merge(a, bm, h, w_a, w_b, w_g0, w_g1, tr, cb):
    n = a.shape[0]
    row = lambda j, i: (i, 0)
    col = lambda j, i: (0, j)
    return pl.pallas_call(
        _merge_kernel,
        grid=(D_MODEL // cb, n // tr),
        in_specs=[pl.BlockSpec((tr, WIDTH), row), pl.BlockSpec((tr, WIDTH), row),
                  pl.BlockSpec((tr, D_MODEL), row),
                  pl.BlockSpec((WIDTH, cb), col), pl.BlockSpec((WIDTH, cb), col),
                  pl.BlockSpec((D_MODEL, cb), col), pl.BlockSpec((D_MODEL, cb), col)],
        out_specs=pl.BlockSpec((tr, cb), lambda j, i: (i, j)),
        out_shape=jax.ShapeDtypeStruct((n, D_MODEL), BF16),
        compiler_params=_params(2),
        name="merge",
    )(a, bm, h, w_a, w_b, w_g0, w_g1)


def _out_proj_kernel(x_ref, m_ref, w_ref, o_ref):
    o_ref[...] = x_ref[...] + _dot(m_ref[...], w_ref[...])


def _out_proj(x, merged, w_out, tr):
    n = x.shape[0]
    row = lambda i: (i, 0)
    return pl.pallas_call(
        _out_proj_kernel,
        grid=(n // tr,),
        in_specs=[pl.BlockSpec((tr, D_MODEL), row), pl.BlockSpec((tr, D_MODEL), row),
                  pl.BlockSpec((D_MODEL, D_MODEL), lambda i: (0, 0))],
        out_specs=pl.BlockSpec((tr, D_MODEL), row),
        out_shape=jax.ShapeDtypeStruct((n, D_MODEL), F32),
        compiler_params=_params(1),
        name="out_proj",
    )(x, merged, w_out)


def _project(x2d, pos_period, tr, wts):
    n = x2d.shape[0]
    n_tab = pos_period.shape[0] // tr
    tabs_head = _rope_tables(pos_period, ROT_DIM, HEAD_DIM)
    tabs_iq = _rope_tables(pos_period, IDX_ROT_DIM, IDX_DIM)
    tabs_iq = tuple(jnp.concatenate([t, t], axis=1) for t in tabs_iq)
    ident = (jnp.ones_like(tabs_iq[0][:, :IDX_DIM]), jnp.zeros_like(tabs_iq[0][:, :IDX_DIM]),
             jnp.zeros_like(tabs_iq[0][:, :IDX_DIM]))
    tabs_misc = tuple(jnp.concatenate([t[:, :IDX_DIM], e], axis=1) for t, e in zip(tabs_iq, ident))

    h = _rmsnorm(x2d, wts["norm_gain"], tr)
    (qa,) = _proj_q(h, wts["w_qa"], wts["q_norm_a"], tabs_head, tr, n_tab, False)
    qb_rot, qb = _proj_q(h, wts["w_qb"], wts["q_norm_b"], tabs_head, tr, n_tab, True)
    akv, cmp_kv, slc, win, akv_b, slc_b, win_b = _proj_kv(
        h, wts["w_kv"], wts["k_norm_a"], wts["k_norm_slc"], wts["k_norm_win"], tabs_head, tr, n_tab)
    iq, misc = _proj_idx(h, wts["w_idx"], tabs_iq, tabs_misc, tr, n_tab)
    sza, szb = _proj_z(h, wts["w_z"], tr)
    return dict(h=h, qa=qa, qb=qb, qb_rot=qb_rot, akv=akv, cmp=cmp_kv, slc=slc, win=win,
                akv_b=akv_b, slc_b=slc_b, win_b=win_b, iq=iq, misc=misc, sza=sza, szb=szb)


def _sample_rows(a, db, ds):
    a = a.reshape(db, ds, KV_HEADS, GROUP, HEAD_DIM).transpose(0, 2, 3, 1, 4)
    a = jnp.pad(a, ((0, 0), (0, 0), (0, 0), (0, SAMPLE_Q - ds), (0, 0)))
    return a.reshape(db, KV_HEADS, QROWS, HEAD_DIM)


def _unsample_rows(o, db, ds):
    o = o.reshape(db, KV_HEADS, GROUP, SAMPLE_Q, HEAD_DIM)[:, :, :, :ds]
    return o.transpose(0, 3, 1, 2, 4).reshape(db * ds, WIDTH)


def _pad_new(a, db, ds):
    return jnp.pad(a.reshape(db, ds, a.shape[-1]), ((0, 0), (0, LANES - ds), (0, 0)))


def kernel(x_prompt, x_sample, cache_a_kv, cache_a_idx, cache_cmp_kv, cache_slc_kv, state_win_kv,
           page_table, norm_gain, w_in, q_norm_a, k_norm_a, q_norm_b, k_norm_cmp, k_norm_slc,
           k_norm_win, pe_cmp, w1_cmp, w2_cmp, w_proj_a, w_proj_b, w_out):
    b, t, _ = x_prompt.shape
    db, ds, _ = x_sample.shape
    n_pages = page_table.shape[1]
    past = n_pages * PAGE_SIZE
    assert ds <= SAMPLE_Q and ds <= CMP_STRIDE

    sizes = (WIDTH, KV_ROW, IDX_COLS, IDX_DIM, IDX_HEADS, WIDTH, WIDTH, 3 * KV_ROW, 3 * N_HEADS, WIDTH,
             2 * D_MODEL)
    offs = [0]
    for s in sizes:
        offs.append(offs[-1] + s)
    wb = w_in.astype(BF16)
    seg = lambda j: wb[:, offs[j]:offs[j + 1]]
    misc_pad = jnp.zeros((D_MODEL, LANES - (IDX_DIM + IDX_HEADS + 3 * N_HEADS)), BF16)
    wts = dict(
        norm_gain=norm_gain, q_norm_a=q_norm_a, q_norm_b=q_norm_b, k_norm_a=k_norm_a,
        k_norm_slc=k_norm_slc, k_norm_win=k_norm_win,
        w_qa=seg(0), w_qb=seg(6),
        w_kv=jnp.concatenate([seg(1), seg(7)], axis=1),
        w_idx=jnp.concatenate([seg(2), seg(3), seg(4), seg(8), misc_pad], axis=1),
        w_z=jnp.concatenate([seg(5), seg(9)], axis=1),
    )
    w_g0 = wb[:, offs[10]:offs[10] + D_MODEL]
    w_g1 = wb[:, offs[10] + D_MODEL:offs[11]]
    w_pa, w_pb, w_o = w_proj_a.astype(BF16), w_proj_b.astype(BF16), w_out.astype(BF16)
    w1_fs = jnp.concatenate([w1_cmp[:, :CMP_STRIDE].reshape(2, CMP_STRIDE * HEAD_DIM, CMP_HIDDEN),
                             w1_cmp[:, CMP_STRIDE:].reshape(2, CMP_STRIDE * HEAD_DIM, CMP_HIDDEN)],
                            axis=2).astype(BF16)
    w1_r = w1_cmp.reshape(2, CMP_LEN * HEAD_DIM, CMP_HIDDEN).astype(BF16)
    pe_r = pe_cmp.reshape(2, CMP_LEN * HEAD_DIM)
    w2_b = w2_cmp.astype(BF16)

    tr = min(512, t)
    tq = min(128, t)
    tk = min(512, t)
    xp = x_prompt.reshape(b * t, D_MODEL)
    pp = _project(xp, jnp.arange(t, dtype=jnp.int32), tr, wts)

    n_chunk_p = t // CMP_STRIDE
    fs_p = _chunk_terms_dense(pp["cmp"].reshape(b * n_chunk_p, CHUNK_FLAT), w1_fs, min(256, b * n_chunk_p))
    kc_p, vc_p = _compress_mlp(fs_p.reshape(b, n_chunk_p, FS_COLS), jnp.zeros((b, SUBLANES, FS_COLS), F32),
                               pe_r, w1_r, w2_b, k_norm_cmp)

    ik_b = pp["misc"][:, :IDX_DIM].astype(BF16)
    ik2 = jnp.concatenate([ik_b, ik_b], axis=1).reshape(b, t, LANES)
    a_p = _dsa_prompt(pp["iq"], pp["misc"], ik2, pp["qa"], pp["akv_b"].reshape(b, t, KV_ROW), pp["sza"],
                      b, t, tq, tk)
    b_p = _nsa_prompt(pp["qb"], pp["qb_rot"], kc_p, vc_p, pp["slc_b"].reshape(b, t, KV_ROW),
                      pp["win_b"].reshape(b, t, KV_ROW), pp["misc"], pp["szb"], b, t, tq, tk)
    merged_p = _merge(a_p, b_p, pp["h"], w_pa, w_pb, w_g0, w_g1, tr, 1024)
    y_prompt = _out_proj(xp, merged_p, w_o, tr).reshape(b, t, D_MODEL)

    kv5 = lambda a, n0, n1: a.reshape(n0, n1, KV_HEADS, 2, HEAD_DIM)
    p_a_kv = kv5(pp["akv"], b, t)
    p_a_idx = pp["misc"][:, :IDX_DIM].reshape(b, t, IDX_DIM)
    p_cmp_kv = kv5(pp["cmp"], b, t)
    p_slc_kv = kv5(pp["slc"], b, t)
    p_win_kv = kv5(pp["win"], b, t)[:, t - min(WINDOW, t):]

    ns = db * ds
    xs = x_sample.reshape(ns, D_MODEL)
    pos_s = past + jnp.tile(jnp.arange(ds, dtype=jnp.int32), db)
    ps = _project(xs, pos_s, ns, wts)
    pg = min(PAGES_PER_STEP, n_pages)
    ng = n_pages // pg
    cw = pg * PAGE_SIZE

    iq_s = ps["iq"].reshape(db, ds, IDX_HEADS, IDX_DIM).transpose(0, 2, 1, 3)
    iqm = jnp.pad(iq_s, ((0, 0), (0, 0), (0, SAMPLE_Q - ds), (0, 0))).reshape(db, IDX_HEADS * SAMPLE_Q, IDX_DIM)
    iw_s = ps["misc"][:, MISC_IK:MISC_IW].reshape(db, ds, IDX_HEADS).transpose(0, 2, 1) * (IDX_DIM ** -0.5)
    iwm = jnp.pad(iw_s, ((0, 0), (0, 0), (0, SAMPLE_Q - ds))).reshape(db, IDX_HEADS * SAMPLE_Q, 1)
    iwm = jnp.broadcast_to(iwm, (db, IDX_HEADS * SAMPLE_Q, LANES))
    ik_new = _pad_new(ps["misc"][:, :IDX_DIM].astype(BF16), db, ds)
    sc_s, sc_new = _idx_scores(cache_a_idx, page_table, iqm, iwm, ik_new, pg)
    mk_a, mk_a_new = _dsa_topk(sc_s, sc_new, past, ds)
    o_a_s = _paged_attn(cache_a_kv, page_table, _sample_rows(ps["qa"], db, ds), mk_a, mk_a_new,
                        _pad_new(ps["akv_b"], db, ds), pg)

    fs_past = _chunk_terms_paged(cache_cmp_kv, page_table, w1_fs, pg)
    new_chunk = jnp.pad(ps["cmp"].reshape(db, ds, KV_ROW), ((0, 0), (0, CMP_STRIDE - ds), (0, 0)))
    fs_new = _chunk_terms_dense(new_chunk.reshape(db, CHUNK_FLAT), w1_fs, db)
    fs_new = jnp.pad(fs_new.reshape(db, 1, FS_COLS), ((0, 0), (0, SUBLANES - 1), (0, 0)))
    kc_s, vc_s = _compress_mlp(fs_past, fs_new, pe_r, w1_r, w2_b, k_norm_cmp)
    o_cmp_s, mk_s, mk_s_new = _nsa_sample(_sample_rows(ps["qb"], db, ds), kc_s, vc_s, past, ds, ng, cw)
    q_rot_s = _sample_rows(ps["qb_rot"], db, ds)
    o_slc_s = _paged_attn(cache_slc_kv, page_table, q_rot_s, mk_s, mk_s_new, _pad_new(ps["slc_b"], db, ds), pg)
    o_win_s = _win_sample(q_rot_s, state_win_kv, _pad_new(ps["win_b"], db, ds), ds)

    a_s, b_s = _gate_merge(_unsample_rows(o_a_s, db, ds), _unsample_rows(o_cmp_s, db, ds),
                           _unsample_rows(o_slc_s, db, ds), _unsample_rows(o_win_s, db, ds),
                           ps["misc"], ps["sza"], ps["szb"])
    merged_s = _merge(a_s, b_s, ps["h"], w_pa, w_pb, w_g0, w_g1, ns, 1024)
    y_sample = _out_proj(xs, merged_s, w_o, ns).reshape(db, ds, D_MODEL)

    s_win = kv5(ps["win"], db, ds)
    w_buf = state_win_kv.shape[1]
    s_win_kv = jnp.concatenate([state_win_kv, s_win], axis=1)[:, ds:ds + w_buf]

    return (y_prompt, y_sample, p_a_kv, p_a_idx, p_cmp_kv, p_slc_kv, p_win_kv,
            kv5(ps["akv"], db, ds), ps["misc"][:, :IDX_DIM].reshape(db, ds, IDX_DIM),
            kv5(ps["cmp"], db, ds), kv5(ps["slc"], db, ds), s_win_kv)
```

```python
import functools

import jax
import jax.numpy as jnp
from jax import lax
from jax.experimental import pallas as pl
from jax.experimental.pallas import tpu as pltpu

D_MODEL = 2048
HEAD_DIM = 128
ROT_DIM = HEAD_DIM // 4
N_HEADS = D_MODEL // (2 * HEAD_DIM)
KV_HEADS = 2
GROUP = N_HEADS // KV_HEADS
IDX_HEADS = 8
IDX_DIM = 64
IDX_ROT_DIM = IDX_DIM // 4
A_TOPK = 256
CMP_LEN = 32
CMP_STRIDE = 16
CMP_HIDDEN = 128
SLC_BLOCK = 64
N_SEL = 16
WINDOW = 512
PAGE_SIZE = 128
ROPE_THETA = 500000.0
EPS = 1e-6
NEG = -1e30
WIDTH = N_HEADS * HEAD_DIM
KV_ROW = KV_HEADS * 2 * HEAD_DIM
SCALE = HEAD_DIM ** -0.5

LANES = 128
SUBLANES = 8
VMEM_LIMIT = 48 * 1024 * 1024
PAGES_PER_STEP = 16
SAMPLE_Q = 8

F32 = jnp.float32
BF16 = jnp.bfloat16
NT_DIMS = (((1,), (1,)), ((), ()))


def _params(n_axes):
    return pltpu.CompilerParams(dimension_semantics=("arbitrary",) * n_axes,
                                vmem_limit_bytes=VMEM_LIMIT)


def _dot(a, b):
    return jnp.dot(a, b, preferred_element_type=F32)


def _dot_nt(a, b):
    return lax.dot_general(a, b, NT_DIMS, preferred_element_type=F32)


def _rope_tables(pos, rot, period, width=LANES):
    half = rot // 2
    inv = ROPE_THETA ** (-jnp.arange(half, dtype=F32) / half)
    ang = pos.astype(F32)[:, None] * inv
    cos, sin = jnp.cos(ang), jnp.sin(ang)
    n = pos.shape[0]
    zeros = lambda w: jnp.zeros((n, w), F32)
    ones = lambda w: jnp.ones((n, w), F32)
    c = jnp.concatenate([cos, cos, ones(period - rot)], axis=1)
    s1 = jnp.concatenate([-sin, zeros(period - half)], axis=1)
    s2 = jnp.concatenate([zeros(half), sin, zeros(period - rot)], axis=1)
    return c, s1, s2


def _apply_rope(y, c, s1, s2, half):
    return y * c + pltpu.roll(y, LANES - half, 1) * s1 + pltpu.roll(y, half, 1) * s2


def _head_norm(z, g):
    return z * lax.rsqrt(jnp.mean(z * z, axis=-1, keepdims=True) + EPS) * g


def _rmsnorm_kernel(x_ref, g_ref, o_ref):
    x = x_ref[...]
    y = x * lax.rsqrt(jnp.mean(x * x, axis=-1, keepdims=True) + EPS)
    o_ref[...] = (y * g_ref[...]).astype(o_ref.dtype)


def _rmsnorm(x, gain, tr):
    n, d = x.shape
    return pl.pallas_call(
        _rmsnorm_kernel,
        grid=(n // tr,),
        in_specs=[pl.BlockSpec((tr, d), lambda i: (i, 0)), pl.BlockSpec((1, d), lambda i: (0, 0))],
        out_specs=pl.BlockSpec((tr, d), lambda i: (i, 0)),
        out_shape=jax.ShapeDtypeStruct((n, d), BF16),
        compiler_params=_params(1),
        name="rmsnorm",
    )(x, gain.reshape(1, d))


def _proj_q_kernel(h_ref, w_ref, g_ref, c_ref, s1_ref, s2_ref, rot_ref, *plain_ref):
    z = _dot(h_ref[...], w_ref[...])
    g = g_ref[...]
    c, s1, s2 = c_ref[...], s1_ref[...], s2_ref[...]
    for hd in range(N_HEADS):
        sl = slice(hd * HEAD_DIM, (hd + 1) * HEAD_DIM)
        y = _head_norm(z[:, sl], g)
        if plain_ref:
            plain_ref[0][:, sl] = y.astype(BF16)
        rot_ref[:, sl] = _apply_rope(y, c, s1, s2, ROT_DIM // 2).astype(BF16)


def _proj_q(h, w, gain, tabs, tr, n_tab_blocks, want_plain):
    n = h.shape[0]
    row = lambda i: (i, 0)
    fix = lambda i: (0, 0)
    tab = lambda i: (i % n_tab_blocks, 0)
    out_shape = [jax.ShapeDtypeStruct((n, WIDTH), BF16)] * (2 if want_plain else 1)
    out_specs = [pl.BlockSpec((tr, WIDTH), row)] * (2 if want_plain else 1)
    return pl.pallas_call(
        _proj_q_kernel,
        grid=(n // tr,),
        in_specs=[pl.BlockSpec((tr, D_MODEL), row), pl.BlockSpec((D_MODEL, WIDTH), fix),
                  pl.BlockSpec((1, HEAD_DIM), fix)] + [pl.BlockSpec((tr, LANES), tab)] * 3,
        out_specs=out_specs,
        out_shape=out_shape,
        compiler_params=_params(1),
        name="proj_q",
    )(h, w, gain.reshape(1, HEAD_DIM), *tabs)


def _proj_kv_kernel(h_ref, w_ref, ga_ref, gs_ref, gw_ref, c_ref, s1_ref, s2_ref,
                    akv_ref, cmp_ref, slc_ref, win_ref, akv_b_ref, slc_b_ref, win_b_ref):
    z = _dot(h_ref[...], w_ref[...])
    c, s1, s2 = c_ref[...], s1_ref[...], s2_ref[...]

    def emit(base, gain_ref, f32_ref, bf_ref):
        for j in range(KV_HEADS * 2):
            sl = slice(j * HEAD_DIM, (j + 1) * HEAD_DIM)
            v = z[:, base + j * HEAD_DIM: base + (j + 1) * HEAD_DIM]
            if gain_ref is not None and j % 2 == 0:
                v = _apply_rope(_head_norm(v, gain_ref[...]), c, s1, s2, ROT_DIM // 2)
            f32_ref[:, sl] = v
            if bf_ref is not None:
                bf_ref[:, sl] = v.astype(BF16)

    emit(0 * KV_ROW, ga_ref, akv_ref, akv_b_ref)
    emit(1 * KV_ROW, None, cmp_ref, None)
    emit(2 * KV_ROW, gs_ref, slc_ref, slc_b_ref)
    emit(3 * KV_ROW, gw_ref, win_ref, win_b_ref)


def _proj_kv(h, w, g_a, g_slc, g_win, tabs, tr, n_tab_blocks):
    n = h.shape[0]
    row = lambda i: (i, 0)
    fix = lambda i: (0, 0)
    tab = lambda i: (i % n_tab_blocks, 0)
    f32_out = jax.ShapeDtypeStruct((n, KV_ROW), F32)
    bf_out = jax.ShapeDtypeStruct((n, KV_ROW), BF16)
    return pl.pallas_call(
        _proj_kv_kernel,
        grid=(n // tr,),
        in_specs=[pl.BlockSpec((tr, D_MODEL), row), pl.BlockSpec((D_MODEL, 4 * KV_ROW), fix)]
                 + [pl.BlockSpec((1, HEAD_DIM), fix)] * 3 + [pl.BlockSpec((tr, LANES), tab)] * 3,
        out_specs=[pl.BlockSpec((tr, KV_ROW), row)] * 7,
        out_shape=[f32_out] * 4 + [bf_out] * 3,
        compiler_params=_params(1),
        name="proj_kv",
    )(h, w, g_a.reshape(1, HEAD_DIM), g_slc.reshape(1, HEAD_DIM), g_win.reshape(1, HEAD_DIM), *tabs)


IDX_COLS = IDX_HEADS * IDX_DIM
MISC_IK = IDX_DIM
MISC_IW = MISC_IK + IDX_HEADS
MISC_G = MISC_IW + 3 * N_HEADS


def _proj_idx_kernel(h_ref, w_ref, c_ref, s1_ref, s2_ref, cm_ref, s1m_ref, s2m_ref, iq_ref, misc_ref):
    z = _dot(h_ref[...], w_ref[...])
    half = IDX_ROT_DIM // 2
    c, s1, s2 = c_ref[...], s1_ref[...], s2_ref[...]
    for j in range(IDX_COLS // LANES):
        sl = slice(j * LANES, (j + 1) * LANES)
        iq_ref[:, sl] = _apply_rope(z[:, sl], c, s1, s2, half).astype(BF16)
    m = z[:, IDX_COLS:IDX_COLS + LANES]
    roped = _apply_rope(m, cm_ref[...], s1m_ref[...], s2m_ref[...], half)
    lane = lax.broadcasted_iota(jnp.int32, m.shape, 1)
    misc_ref[...] = jnp.where(lane < MISC_IK, roped,
                              jnp.where(lane < MISC_IW, m * (IDX_HEADS ** -0.5),
                                        jnp.where(lane < MISC_G, jax.nn.sigmoid(m), 0.0)))


def _proj_idx(h, w, tabs_iq, tabs_misc, tr, n_tab_blocks):
    n = h.shape[0]
    row = lambda i: (i, 0)
    fix = lambda i: (0, 0)
    tab = lambda i: (i % n_tab_blocks, 0)
    return pl.pallas_call(
        _proj_idx_kernel,
        grid=(n // tr,),
        in_specs=[pl.BlockSpec((tr, D_MODEL), row), pl.BlockSpec((D_MODEL, IDX_COLS + LANES), fix)]
                 + [pl.BlockSpec((tr, LANES), tab)] * 6,
        out_specs=[pl.BlockSpec((tr, IDX_COLS), row), pl.BlockSpec((tr, LANES), row)],
        out_shape=[jax.ShapeDtypeStruct((n, IDX_COLS), BF16), jax.ShapeDtypeStruct((n, LANES), F32)],
        compiler_params=_params(1),
        name="proj_idx",
    )(h, w, *tabs_iq, *tabs_misc)


def _proj_z_kernel(h_ref, w_ref, za_ref, zb_ref):
    z = _dot(h_ref[...], w_ref[...])
    s = z * jax.nn.sigmoid(z)
    za_ref[...] = s[:, :WIDTH]
    zb_ref[...] = s[:, WIDTH:]


def _proj_z(h, w, tr):
    n = h.shape[0]
    row = lambda i: (i, 0)
    out = jax.ShapeDtypeStruct((n, WIDTH), F32)
    return pl.pallas_call(
        _proj_z_kernel,
        grid=(n // tr,),
        in_specs=[pl.BlockSpec((tr, D_MODEL), row), pl.BlockSpec((D_MODEL, 2 * WIDTH), lambda i: (0, 0))],
        out_specs=[pl.BlockSpec((tr, WIDTH), row)] * 2,
        out_shape=[out, out],
        compiler_params=_params(1),
        name="proj_z",
    )(h, w)


def _key_to_float(u):
    int_min = jnp.int32(-2 ** 31)
    bits = jnp.where(u < 0, u ^ int_min, ~u)
    return lax.bitcast_convert_type(bits, F32)


def _lane_fold(w):
    acc = w[:, :LANES]
    for j in range(1, w.shape[1] // LANES):
        acc = acc + w[:, j * LANES:(j + 1) * LANES]
    return acc


def _topk_threshold(load, nck, k, idx_bits, rows, cw, tail=None):
    kf = jnp.float32(k)
    lane = lax.broadcasted_iota(jnp.int32, (1, cw), 1)

    def count(pred_value):
        def body(c, acc):
            return acc + _lane_fold(pred_value(load(c), c * cw + lane))
        acc = lax.fori_loop(0, nck, body, jnp.zeros((rows, LANES), F32))
        if tail is not None:
            acc = acc + pred_value(*tail)
        return jnp.sum(acc, axis=1, keepdims=True)

    def value_bit(it, u):
        cand = u | jnp.left_shift(jnp.int32(1), 31 - it)
        candf = _key_to_float(cand)
        cnt = count(lambda x, idx: jnp.where(x >= candf, 1.0, 0.0))
        return jnp.where(cnt >= kf, cand, u)

    u = lax.fori_loop(0, 32, value_bit, jnp.zeros((rows, 1), jnp.int32))
    thr = jnp.where((u >= 0) & (u < 2 ** 23), -jnp.inf, _key_to_float(u))
    need = kf - count(lambda x, idx: jnp.where(x > thr, 1.0, 0.0))

    def index_bit(it, v):
        cand = v | jnp.left_shift(jnp.int32(1), idx_bits - 1 - it)
        below = count(lambda x, idx: jnp.where(x == thr, jnp.where(idx < cand, 1.0, 0.0), 0.0))
        return jnp.where(below < need, cand, v)

    tied = jnp.max(count(lambda x, idx: jnp.where(x >= thr, 1.0, 0.0))) > kf
    v = lax.cond(tied,
                 lambda: lax.fori_loop(0, idx_bits, index_bit, jnp.zeros((rows, 1), jnp.int32)),
                 lambda: jnp.full((rows, 1), (1 << idx_bits) - 1, jnp.int32))
    return thr, jnp.where(need >= 1.0, v, -1)


def _topk_mask(x, idx, thr, vmax):
    gt = jnp.where(x > thr, 1.0, 0.0)
    eq = jnp.where(x == thr, jnp.where(idx <= vmax, 1.0, 0.0), 0.0)
    return jnp.maximum(gt, eq)


LOG2E = 1.4426950408889634
SCORE_SCALE = SCALE * LOG2E


def _softmax_init(rows):
    return (jnp.full((rows, 1), NEG, F32), jnp.zeros((rows, 1), F32), jnp.zeros((rows, HEAD_DIM), F32))


def _softmax_step(carry, q, k, v, bias):
    m, l, acc = carry
    t = _dot_nt(q, k) * SCORE_SCALE + bias
    m_new = jnp.maximum(m, jnp.max(t, axis=1, keepdims=True))
    alpha = jnp.exp2(m - m_new)
    e = jnp.exp2(t - m_new)
    l = alpha * l + jnp.sum(e, axis=1, keepdims=True)
    acc = alpha * acc + _dot(e.astype(BF16), v)
    return m_new, l, acc


def _softmax_done(carry):
    _, l, acc = carry
    return acc / jnp.maximum(l, 1e-30)


def _mask_bias(maskf):
    return jnp.where(maskf > 0.0, 0.0, NEG)


def _stack_heads(ref, kvh):
    return jnp.concatenate([ref[:, (kvh * GROUP + g) * HEAD_DIM:(kvh * GROUP + g + 1) * HEAD_DIM]
                            for g in range(GROUP)], axis=0)


def _kv_cols(kvh):
    return (slice(kvh * 2 * HEAD_DIM, (kvh * 2 + 1) * HEAD_DIM),
            slice((kvh * 2 + 1) * HEAD_DIM, (kvh * 2 + 2) * HEAD_DIM))


def _dsa_prompt_kernel(iq_ref, misc_ref, ik2_ref, qa_ref, akv_ref, sza_ref, o_ref, sc_ref, mk_ref,
                       *, tq, tk, k_top, idx_bits):
    i = pl.program_id(1)
    nck = ((i + 1) * tq + tk - 1) // tk
    qpos = i * tq + lax.broadcasted_iota(jnp.int32, (tq, 1), 0)
    lane = lax.broadcasted_iota(jnp.int32, (1, LANES), 1)
    kiota = lax.broadcasted_iota(jnp.int32, (1, tk), 1)
    misc = misc_ref[...]

    iqs, iws = [], []
    for hd in range(IDX_HEADS):
        pair = iq_ref[:, (hd // 2) * LANES:(hd // 2 + 1) * LANES]
        keep = (lane < IDX_DIM) if hd % 2 == 0 else (lane >= IDX_DIM)
        iqs.append(jnp.where(keep, pair, jnp.zeros_like(pair)))
        iws.append(misc[:, MISC_IK + hd:MISC_IK + hd + 1] * (IDX_DIM ** -0.5))

    def score_chunk(c, carry):
        k0 = pl.multiple_of(c * tk, tk)
        ikc = ik2_ref[pl.ds(k0, tk), :]
        acc = jnp.zeros((tq, tk), F32)
        for hd in range(IDX_HEADS):
            acc = acc + jnp.maximum(_dot_nt(iqs[hd], ikc), 0.0) * iws[hd]
        sc_ref[c] = jnp.where(k0 + kiota <= qpos, acc, NEG)
        return carry

    lax.fori_loop(0, nck, score_chunk, 0)

    thr, vmax = _topk_threshold(lambda c: sc_ref[c], nck, k_top, idx_bits, tq, tk)

    def mask_chunk(c, carry):
        idx = c * tk + kiota
        sel = _topk_mask(sc_ref[c], idx, thr, vmax) * jnp.where(idx <= qpos, 1.0, 0.0)
        mk_ref[c] = _mask_bias(sel)
        return carry

    lax.fori_loop(0, nck, mask_chunk, 0)

    for kvh in range(KV_HEADS):
        q = _stack_heads(qa_ref, kvh)
        kcol, vcol = _kv_cols(kvh)

        def attend(c, carry, q=q, kcol=kcol, vcol=vcol):
            k0 = pl.multiple_of(c * tk, tk)
            bias = jnp.concatenate([mk_ref[c]] * GROUP, axis=0)
            return _softmax_step(carry, q, akv_ref[pl.ds(k0, tk), kcol], akv_ref[pl.ds(k0, tk), vcol], bias)

        o = _softmax_done(lax.fori_loop(0, nck, attend, _softmax_init(GROUP * tq)))
        for g in range(GROUP):
            sl = slice((kvh * GROUP + g) * HEAD_DIM, (kvh * GROUP + g + 1) * HEAD_DIM)
            o_ref[:, sl] = (o[g * tq:(g + 1) * tq] * sza_ref[:, sl]).astype(BF16)


def _dsa_prompt(iq, misc, ik2, qa, akv_b, sza, b, t, tq, tk):
    nqb = t // tq
    k_top = min(A_TOPK, t // 4)
    idx_bits = max(1, (t - 1).bit_length())
    row = lambda bi, i: (bi * nqb + i, 0)
    whole = lambda bi, i: (bi, 0, 0)
    kern = functools.partial(_dsa_prompt_kernel, tq=tq, tk=tk, k_top=k_top, idx_bits=idx_bits)
    return pl.pallas_call(
        kern,
        grid=(b, nqb),
        in_specs=[pl.BlockSpec((tq, IDX_COLS), row), pl.BlockSpec((tq, LANES), row),
                  pl.BlockSpec((None, t, LANES), whole), pl.BlockSpec((tq, WIDTH), row),
                  pl.BlockSpec((None, t, KV_ROW), whole), pl.BlockSpec((tq, WIDTH), row)],
        out_specs=pl.BlockSpec((tq, WIDTH), row),
        out_shape=jax.ShapeDtypeStruct((b * t, WIDTH), BF16),
        scratch_shapes=[pltpu.VMEM((t // tk, tq, tk), F32), pltpu.VMEM((t // tk, tq, tk), F32)],
        compiler_params=_params(2),
        name="dsa_prompt",
    )(iq, misc, ik2, qa, akv_b, sza)


SLC_SHIFT = SLC_BLOCK.bit_length() - 1
assert 1 << SLC_SHIFT == SLC_BLOCK


def _block_rules(imp, blk, qpos):
    cur = lax.shift_right_logical(qpos, SLC_SHIFT)
    return jnp.where(blk == cur, -NEG, jnp.where(blk * SLC_BLOCK > qpos, NEG, imp))


def _split_dot(p, m):
    hi = p.astype(BF16)
    lo = (p - hi.astype(F32)).astype(BF16)
    return _dot(hi, m) + _dot(lo, m)


def _nsa_prompt_kernel(qb_ref, qr_ref, kc_ref, vc_ref, slc_ref, win_ref, ov_ref, misc_ref, szb_ref,
                       o_ref, sc_ref, *, tq, tk, n_cmp, n_blk, n_sel):
    i = pl.program_id(1)
    nck = ((i + 1) * tq + tk - 1) // tk
    qpos = i * tq + lax.broadcasted_iota(jnp.int32, (tq, 1), 0)
    kiota = lax.broadcasted_iota(jnp.int32, (1, tk), 1)
    misc = misc_ref[...]
    ncp = kc_ref.shape[0]
    cidx = lax.broadcasted_iota(jnp.int32, (1, ncp), 1)
    cmask = jnp.where((CMP_STRIDE * cidx + CMP_LEN - 1 <= qpos) & (cidx < n_cmp), 1.0, 0.0)
    cmask = jnp.concatenate([cmask] * GROUP, axis=0)
    blk = lax.broadcasted_iota(jnp.int32, (1, LANES), 1)

    o_cmp = []
    for kvh in range(KV_HEADS):
        kc = kc_ref[:, kvh * HEAD_DIM:(kvh + 1) * HEAD_DIM]
        vc = vc_ref[:, kvh * HEAD_DIM:(kvh + 1) * HEAD_DIM]
        s = _dot_nt(_stack_heads(qb_ref, kvh), kc) * SCALE
        sm = jnp.where(cmask > 0.0, s, NEG)
        e = jnp.exp(sm - jnp.max(sm, axis=1, keepdims=True)) * cmask
        p = e / jnp.maximum(jnp.sum(e, axis=1, keepdims=True), 1e-30)
        o_cmp.append(_dot(p.astype(BF16), vc))
        psum = p[0:tq]
        for g in range(1, GROUP):
            psum = psum + p[g * tq:(g + 1) * tq]
        imp = _split_dot(psum, ov_ref[...])
        imp = _block_rules(imp, blk, qpos)
        sc_ref[kvh] = jnp.where(blk < n_blk, imp, -jnp.inf)

    rows = KV_HEADS * tq
    load = lambda c: sc_ref[...].reshape(rows, LANES)
    thr, vmax = _topk_threshold(load, 1, n_sel, 7, rows, LANES)
    bmask = _topk_mask(load(0), blk, thr, vmax).astype(BF16)

    riota = lax.broadcasted_iota(jnp.int32, (LANES, 1), 0)
    nwk = WINDOW + tq
    w0 = pl.multiple_of(i * tq, tq)
    kposw = i * tq - WINDOW + lax.broadcasted_iota(jnp.int32, (1, nwk), 1)
    dw = qpos - kposw
    bias_w = jnp.where((dw >= 0) & (dw < WINDOW) & (kposw >= 0), 0.0, NEG)
    bias_w = jnp.concatenate([bias_w] * GROUP, axis=0)

    for kvh in range(KV_HEADS):
        q = _stack_heads(qr_ref, kvh)
        kcol, vcol = _kv_cols(kvh)
        bm = bmask[kvh * tq:(kvh + 1) * tq]

        def attend(c, carry, q=q, kcol=kcol, vcol=vcol, bm=bm):
            k0 = pl.multiple_of(c * tk, tk)
            kpos = k0 + kiota
            expand = jnp.where(riota == lax.shift_right_logical(kpos, SLC_SHIFT), 1.0, 0.0).astype(BF16)
            sel = _dot(bm, expand) * jnp.where(kpos <= qpos, 1.0, 0.0)
            bias = jnp.concatenate([_mask_bias(sel)] * GROUP, axis=0)
            return _softmax_step(carry, q, slc_ref[pl.ds(k0, tk), kcol], slc_ref[pl.ds(k0, tk), vcol], bias)

        o_slc = _softmax_done(lax.fori_loop(0, nck, attend, _softmax_init(GROUP * tq)))
        o_win = _softmax_done(_softmax_step(_softmax_init(GROUP * tq), q, win_ref[pl.ds(w0, nwk), kcol],
                                            win_ref[pl.ds(w0, nwk), vcol], bias_w))
        for g in range(GROUP):
            hd = kvh * GROUP + g
            sl = slice(hd * HEAD_DIM, (hd + 1) * HEAD_DIM)
            rs = slice(g * tq, (g + 1) * tq)
            gate = [misc[:, MISC_IW + j * N_HEADS + hd:MISC_IW + j * N_HEADS + hd + 1] for j in range(3)]
            o_b = gate[0] * o_cmp[kvh][rs] + gate[1] * o_slc[rs] + gate[2] * o_win[rs]
            o_ref[:, sl] = (o_b * szb_ref[:, sl]).astype(BF16)


def _overlap_matrix(n_cmp_pad, n_cols, col_block):
    i = jnp.arange(n_cmp_pad)[:, None]
    j = col_block[None, :]
    ov = (CMP_STRIDE * i < SLC_BLOCK * (j + 1)) & (CMP_STRIDE * i + CMP_LEN > SLC_BLOCK * j) & (j >= 0)
    return ov.astype(BF16)


def _nsa_prompt(qb, qr, kc, vc, slc_b, win_b, misc, szb, b, t, tq, tk):
    nqb = t // tq
    n_cmp = t // CMP_STRIDE - 1
    n_blk = t // SLC_BLOCK
    assert n_blk <= LANES
    ncp = kc.shape[1]
    ov = _overlap_matrix(ncp, LANES, jnp.where(jnp.arange(LANES) < n_blk, jnp.arange(LANES), -1))
    row = lambda bi, i: (bi * nqb + i, 0)
    whole = lambda bi, i: (bi, 0, 0)
    kern = functools.partial(_nsa_prompt_kernel, tq=tq, tk=tk, n_cmp=n_cmp, n_blk=n_blk,
                             n_sel=min(N_SEL, n_blk))
    return pl.pallas_call(
        kern,
        grid=(b, nqb),
        in_specs=[pl.BlockSpec((tq, WIDTH), row), pl.BlockSpec((tq, WIDTH), row),
                  pl.BlockSpec((None, ncp, KV_HEADS * HEAD_DIM), whole),
                  pl.BlockSpec((None, ncp, KV_HEADS * HEAD_DIM), whole),
                  pl.BlockSpec((None, t, KV_ROW), whole), pl.BlockSpec((None, t + WINDOW, KV_ROW), whole),
                  pl.BlockSpec((ncp, LANES), lambda bi, i: (0, 0)),
                  pl.BlockSpec((tq, LANES), row), pl.BlockSpec((tq, WIDTH), row)],
        out_specs=pl.BlockSpec((tq, WIDTH), row),
        out_shape=jax.ShapeDtypeStruct((b * t, WIDTH), BF16),
        scratch_shapes=[pltpu.VMEM((KV_HEADS, tq, LANES), F32)],
        compiler_params=_params(2),
        name="nsa_prompt",
    )(qb, qr, kc, vc, slc_b, win_b, ov, misc, szb)


CHUNK_FLAT = CMP_STRIDE * KV_ROW
FS_COLS = KV_HEADS * 2 * 2 * CMP_HIDDEN


def _chunk_terms_body(x, w_ref, o_ref):
    for kc in range(KV_HEADS * 2):
        xk = jnp.concatenate(
            [x[:, p * KV_ROW + kc * HEAD_DIM: p * KV_ROW + (kc + 1) * HEAD_DIM] for p in range(CMP_STRIDE)],
            axis=1).astype(BF16)
        o_ref[:, kc * 2 * CMP_HIDDEN:(kc + 1) * 2 * CMP_HIDDEN] = _dot(xk, w_ref[kc % 2])


def _chunk_terms_dense_kernel(x_ref, w_ref, o_ref):
    _chunk_terms_body(x_ref[...], w_ref, o_ref)


def _chunk_terms_dense(x, w, tr):
    n = x.shape[0]
    return pl.pallas_call(
        _chunk_terms_dense_kernel,
        grid=(n // tr,),
        in_specs=[pl.BlockSpec((tr, CHUNK_FLAT), lambda i: (i, 0)),
                  pl.BlockSpec((2, CMP_STRIDE * HEAD_DIM, 2 * CMP_HIDDEN), lambda i: (0, 0, 0))],
        out_specs=pl.BlockSpec((tr, FS_COLS), lambda i: (i, 0)),
        out_shape=jax.ShapeDtypeStruct((n, FS_COLS), F32),
        compiler_params=_params(1),
        name="chunk_terms_dense",
    )(x, w)


ROWS_PER_TOKEN = KV_HEADS * 2
PAGE_ROWS = PAGE_SIZE * ROWS_PER_TOKEN
CHUNKS_PER_PAGE = PAGE_SIZE // CMP_STRIDE


def _cache_rows(cache):
    return cache.reshape(-1, HEAD_DIM)


def _token_rows(ref, slot, n_tokens):
    return ref[pl.ds(slot, n_tokens, stride=ROWS_PER_TOKEN), :]


def _chunk_terms_paged_kernel(pt_ref, *refs, pg):
    pages, w_ref, o_ref = refs[:pg], refs[pg], refs[pg + 1]
    chunk_rows = CMP_STRIDE * ROWS_PER_TOKEN
    for kc in range(ROWS_PER_TOKEN):
        xk = jnp.concatenate(
            [jnp.concatenate([pr[pl.ds(p * ROWS_PER_TOKEN + kc, CHUNKS_PER_PAGE, stride=chunk_rows), :]
                              for pr in pages], axis=0) for p in range(CMP_STRIDE)], axis=1).astype(BF16)
        o_ref[:, kc * 2 * CMP_HIDDEN:(kc + 1) * 2 * CMP_HIDDEN] = _dot(xk, w_ref[kc % 2])


def _page_specs(block, pg):
    tail = (0,) * (len(block) - 1)

    def spec(j):
        return pl.BlockSpec(block, lambda bi, g, pt: (pt[bi, g * pg + j],) + tail)
    return [spec(j) for j in range(pg)]


def _chunk_terms_paged(pool, page_table, w, pg):
    db, n_pages = page_table.shape
    cpp = CHUNKS_PER_PAGE
    pool_v = _cache_rows(pool)
    kern = functools.partial(_chunk_terms_paged_kernel, pg=pg)
    grid_spec = pltpu.PrefetchScalarGridSpec(
        num_scalar_prefetch=1,
        grid=(db, n_pages // pg),
        in_specs=_page_specs((PAGE_ROWS, HEAD_DIM), pg)
                 + [pl.BlockSpec((2, CMP_STRIDE * HEAD_DIM, 2 * CMP_HIDDEN), lambda bi, g, pt: (0, 0, 0))],
        out_specs=pl.BlockSpec((None, pg * cpp, FS_COLS), lambda bi, g, pt: (bi, g, 0)),
    )
    return pl.pallas_call(
        kern, grid_spec=grid_spec,
        out_shape=jax.ShapeDtypeStruct((db, n_pages * cpp, FS_COLS), F32),
        compiler_params=_params(2),
        name="chunk_terms_paged",
    )(page_table, *([pool_v] * pg), w)


def _compress_mlp_kernel(fs_ref, new_ref, pe_ref, w1_ref, w2_ref, g_ref, kc_ref, vc_ref):
    n = fs_ref.shape[0]
    last = lax.broadcasted_iota(jnp.int32, (n, 1), 0) == n - 1
    for kc in range(KV_HEADS * 2):
        kvh, c = kc // 2, kc % 2
        base = kc * 2 * CMP_HIDDEN
        first = fs_ref[:, base:base + CMP_HIDDEN]
        second = fs_ref[:, base + CMP_HIDDEN:base + 2 * CMP_HIDDEN]
        shifted = jnp.where(last, new_ref[0:1, base + CMP_HIDDEN:base + 2 * CMP_HIDDEN],
                            pltpu.roll(second, n - 1, 0))
        pe = jnp.broadcast_to(pe_ref[c:c + 1, :], (SUBLANES, pe_ref.shape[1])).astype(BF16)
        bias = _dot(pe, w1_ref[c])[0:1, :]
        pre = first + shifted + bias
        hid = pre * jax.nn.sigmoid(pre)
        out = _dot(hid.astype(BF16), w2_ref[c])
        sl = slice(kvh * HEAD_DIM, (kvh + 1) * HEAD_DIM)
        if c == 0:
            kc_ref[:, sl] = _head_norm(out, g_ref[...]).astype(BF16)
        else:
            vc_ref[:, sl] = out.astype(BF16)


def _compress_mlp(fs, fs_new, pe, w1r, w2, g_cmp):
    nb, n, _ = fs.shape
    whole = lambda bi: (bi, 0, 0)
    fix2 = lambda bi: (0, 0)
    fix3 = lambda bi: (0, 0, 0)
    out = jax.ShapeDtypeStruct((nb, n, KV_HEADS * HEAD_DIM), BF16)
    return pl.pallas_call(
        _compress_mlp_kernel,
        grid=(nb,),
        in_specs=[pl.BlockSpec((None, n, FS_COLS), whole), pl.BlockSpec((None, SUBLANES, FS_COLS), whole),
                  pl.BlockSpec((2, CMP_LEN * HEAD_DIM), fix2),
                  pl.BlockSpec((2, CMP_LEN * HEAD_DIM, CMP_HIDDEN), fix3),
                  pl.BlockSpec((2, CMP_HIDDEN, HEAD_DIM), fix3), pl.BlockSpec((1, HEAD_DIM), fix2)],
        out_specs=[pl.BlockSpec((None, n, KV_HEADS * HEAD_DIM), whole)] * 2,
        out_shape=[out, out],
        compiler_params=_params(1),
        name="compress_mlp",
    )(fs, fs_new, pe, w1r, w2, g_cmp.reshape(1, HEAD_DIM))


def _idx_scores_kernel(pt_ref, *refs, pg, n_new):
    pages = refs[:pg]
    iq_ref, iw_ref, new_ref, o_ref, onew_ref = refs[pg:pg + 5]
    g = pl.program_id(1)
    iq = iq_ref[...]
    iw = iw_ref[...]
    row = lax.broadcasted_iota(jnp.int32, (SAMPLE_Q, 1), 0)

    def scores(keys_t, first_idx):
        s = jnp.maximum(_dot(iq, keys_t), 0.0)
        n = s.shape[1]
        w = s * jnp.concatenate([iw] * (n // LANES), axis=1)
        acc = w[0:SAMPLE_Q]
        for hd in range(1, IDX_HEADS):
            acc = acc + w[hd * SAMPLE_Q:(hd + 1) * SAMPLE_Q]
        ramp = -(first_idx + lax.broadcasted_iota(jnp.int32, (1, n), 1)).astype(F32)
        return jnp.where(row < n_new, acc, ramp)

    cw = pg * PAGE_SIZE
    o_ref[...] = scores(jnp.concatenate([p[...] for p in pages], axis=1).astype(BF16), g * cw)

    @pl.when(g == pl.num_programs(1) - 1)
    def _():
        onew_ref[...] = scores(new_ref[...], pl.num_programs(1) * cw)


def _idx_scores(pool_idx_t, page_table, iqm, iwm, ik_new_t, pg, n_new):
    db, n_pages = page_table.shape
    ng = n_pages // pg
    kern = functools.partial(_idx_scores_kernel, pg=pg, n_new=n_new)
    per_b = lambda bi, g, pt: (bi, 0, 0)
    grid_spec = pltpu.PrefetchScalarGridSpec(
        num_scalar_prefetch=1,
        grid=(db, ng),
        in_specs=_page_specs((None, IDX_DIM, PAGE_SIZE), pg)
                 + [pl.BlockSpec((None, IDX_HEADS * SAMPLE_Q, IDX_DIM), per_b),
                    pl.BlockSpec((None, IDX_HEADS * SAMPLE_Q, LANES), per_b),
                    pl.BlockSpec((None, IDX_DIM, LANES), per_b)],
        out_specs=[pl.BlockSpec((None, SAMPLE_Q, pg * PAGE_SIZE), lambda bi, g, pt: (g, bi, 0)),
                   pl.BlockSpec((SAMPLE_Q, LANES), lambda bi, g, pt: (bi, 0))],
    )
    return pl.pallas_call(
        kern, grid_spec=grid_spec,
        out_shape=[jax.ShapeDtypeStruct((ng, db * SAMPLE_Q, pg * PAGE_SIZE), F32),
                   jax.ShapeDtypeStruct((db * SAMPLE_Q, LANES), F32)],
        compiler_params=_params(2),
        name="idx_scores",
    )(page_table, *([pool_idx_t] * pg), iqm, iwm, ik_new_t)


TOPK_ROWS = 64


def _dsa_topk_kernel(sc_ref, new_ref, o_ref, onew_ref, *, ng, cw, n_new, k_top, idx_bits, rb):
    row = lax.broadcasted_iota(jnp.int32, (rb, 1), 0)
    q = row & (SAMPLE_Q - 1)
    lane = lax.broadcasted_iota(jnp.int32, (1, LANES), 1)
    lane_cw = lax.broadcasted_iota(jnp.int32, (1, cw), 1)
    ok_new = (lane <= q) & (lane < n_new)
    x_new = jnp.where(ok_new, new_ref[...], NEG)
    idx_new = ng * cw + lane

    thr, vmax = _topk_threshold(lambda c: sc_ref[c], ng, k_top, idx_bits, rb, cw, tail=(x_new, idx_new))
    for g in range(ng):
        o_ref[g] = _mask_bias(_topk_mask(sc_ref[g], g * cw + lane_cw, thr, vmax))
    onew_ref[...] = _mask_bias(_topk_mask(x_new, idx_new, thr, vmax) * jnp.where(ok_new, 1.0, 0.0))


def _dsa_topk(scores, scores_new, past, n_new):
    ng, rows, cw = scores.shape
    rb = min(TOPK_ROWS, rows)
    total = past + n_new
    kern = functools.partial(_dsa_topk_kernel, ng=ng, cw=cw, n_new=n_new, k_top=min(A_TOPK, total // 4),
                             idx_bits=(ng * cw + LANES - 1).bit_length(), rb=rb)
    big = pl.BlockSpec((ng, rb, cw), lambda r: (0, r, 0))
    small = pl.BlockSpec((rb, LANES), lambda r: (r, 0))
    return pl.pallas_call(
        kern,
        grid=(rows // rb,),
        in_specs=[big, small],
        out_specs=[big, small],
        out_shape=[jax.ShapeDtypeStruct((ng, rows, cw), F32), jax.ShapeDtypeStruct((rows, LANES), F32)],
        compiler_params=_params(1),
        name="dsa_topk",
    )(scores, scores_new)


QROWS = GROUP * SAMPLE_Q


def _paged_attn_kernel(pt_ref, *refs, pg, mask_rows):
    pages = refs[:pg]
    q_ref, mk_ref, mknew_ref, new_ref, o_ref, m_ref, l_ref, acc_ref = refs[pg:pg + 8]
    g = pl.program_id(1)

    @pl.when(g == 0)
    def _():
        m_ref[...] = jnp.full(m_ref.shape, NEG, F32)
        l_ref[...] = jnp.zeros(l_ref.shape, F32)
        acc_ref[...] = jnp.zeros(acc_ref.shape, F32)

    def update(keys, values, bias_all):
        carries = [(m_ref[kvh], l_ref[kvh], acc_ref[kvh]) for kvh in range(KV_HEADS)]
        for kvh in range(KV_HEADS):
            r0 = kvh * SAMPLE_Q if mask_rows == KV_HEADS * SAMPLE_Q else 0
            bias = jnp.concatenate([bias_all[r0:r0 + SAMPLE_Q]] * GROUP, axis=0)
            carries[kvh] = _softmax_step(carries[kvh], q_ref[kvh], keys(kvh), values(kvh), bias)
        for kvh in range(KV_HEADS):
            m_ref[kvh], l_ref[kvh], acc_ref[kvh] = carries[kvh]

    def paged(slot):
        return jnp.concatenate([_token_rows(p, slot, PAGE_SIZE) for p in pages], axis=0).astype(BF16)

    update(lambda kvh: paged(2 * kvh), lambda kvh: paged(2 * kvh + 1), mk_ref[...])

    @pl.when(g == pl.num_programs(1) - 1)
    def _():
        update(lambda kvh: new_ref[:, _kv_cols(kvh)[0]], lambda kvh: new_ref[:, _kv_cols(kvh)[1]], mknew_ref[...])
        for kvh in range(KV_HEADS):
            o_ref[kvh] = _softmax_done((m_ref[kvh], l_ref[kvh], acc_ref[kvh]))


def _paged_attn(pool, page_table, q, bias, bias_new, kv_new, pg):
    db, n_pages = page_table.shape
    ng = n_pages // pg
    mask_rows = bias.shape[1] // db
    pool_v = _cache_rows(pool)
    kern = functools.partial(_paged_attn_kernel, pg=pg, mask_rows=mask_rows)
    per_b3 = lambda bi, g, pt: (bi, 0, 0)
    per_b4 = lambda bi, g, pt: (bi, 0, 0, 0)
    grid_spec = pltpu.PrefetchScalarGridSpec(
        num_scalar_prefetch=1,
        grid=(db, ng),
        in_specs=_page_specs((PAGE_ROWS, HEAD_DIM), pg)
                 + [pl.BlockSpec((None, KV_HEADS, QROWS, HEAD_DIM), per_b4),
                    pl.BlockSpec((None, mask_rows, pg * PAGE_SIZE), lambda bi, g, pt: (g, bi, 0)),
                    pl.BlockSpec((mask_rows, LANES), lambda bi, g, pt: (bi, 0)),
                    pl.BlockSpec((None, LANES, KV_ROW), per_b3)],
        out_specs=pl.BlockSpec((None, KV_HEADS, QROWS, HEAD_DIM), per_b4),
        scratch_shapes=[pltpu.VMEM((KV_HEADS, QROWS, 1), F32), pltpu.VMEM((KV_HEADS, QROWS, 1), F32),
                        pltpu.VMEM((KV_HEADS, QROWS, HEAD_DIM), F32)],
    )
    return pl.pallas_call(
        kern, grid_spec=grid_spec,
        out_shape=jax.ShapeDtypeStruct((db, KV_HEADS, QROWS, HEAD_DIM), F32),
        compiler_params=_params(2),
        name="paged_attn",
    )(page_table, *([pool_v] * pg), q, bias, bias_new, kv_new)


def _nsa_sample_kernel(q_ref, kc_ref, vc_ref, ov_ref, ex_ref, o_ref, tm_ref, tmnew_ref, sc_ref,
                       *, past, n_new, n_cmp, n_blk, n_sel, ng, gb):
    row = lax.broadcasted_iota(jnp.int32, (SAMPLE_Q, 1), 0)
    qpos = past + row
    ncp = kc_ref.shape[0]
    cidx = lax.broadcasted_iota(jnp.int32, (1, ncp), 1)
    cmask8 = jnp.where((CMP_STRIDE * cidx + CMP_LEN - 1 <= qpos) & (cidx < n_cmp), 1.0, 0.0)
    cmask = jnp.concatenate([cmask8] * GROUP, axis=0)
    lane = lax.broadcasted_iota(jnp.int32, (1, LANES), 1)

    for kvh in range(KV_HEADS):
        kc = kc_ref[:, kvh * HEAD_DIM:(kvh + 1) * HEAD_DIM]
        vc = vc_ref[:, kvh * HEAD_DIM:(kvh + 1) * HEAD_DIM]
        s = _dot_nt(q_ref[kvh], kc) * SCALE
        sm = jnp.where(cmask > 0.0, s, NEG)
        e = jnp.exp(sm - jnp.max(sm, axis=1, keepdims=True)) * cmask
        p = e / jnp.maximum(jnp.sum(e, axis=1, keepdims=True), 1e-30)
        o_ref[kvh] = _dot(p.astype(BF16), vc)
        psum = p[0:SAMPLE_Q]
        for g in range(1, GROUP):
            psum = psum + p[g * SAMPLE_Q:(g + 1) * SAMPLE_Q]
        imp = _split_dot(psum, ov_ref[...])
        for g in range(ng + 1):
            blk = g * gb + lane
            v = _block_rules(imp[:, g * LANES:(g + 1) * LANES], blk, qpos)
            v = jnp.where(row < n_new, v, -blk.astype(F32))
            sc_ref[g, kvh * SAMPLE_Q:(kvh + 1) * SAMPLE_Q, :] = jnp.where((lane < gb) & (blk < n_blk), v, -jnp.inf)

    rows = KV_HEADS * SAMPLE_Q
    idx_bits = ((ng + 1) * LANES - 1).bit_length()
    thr, vmax = _topk_threshold(lambda c: sc_ref[c], ng + 1, n_sel, idx_bits, rows, LANES)
    qpos2 = jnp.concatenate([qpos] * KV_HEADS, axis=0)
    for g in range(ng + 1):
        bm = _topk_mask(sc_ref[g], g * LANES + lane, thr, vmax).astype(BF16)
        tok = _dot(bm, ex_ref[...])
        if g < ng:
            tm_ref[g] = _mask_bias(tok)
        else:
            t = lax.broadcasted_iota(jnp.int32, (1, LANES), 1)
            ok = jnp.where((past + t <= qpos2) & (t < n_new), 1.0, 0.0)
            tmnew_ref[...] = _mask_bias(tok[:, :LANES] * ok)


def _nsa_sample(q, kc, vc, past, n_new, ng, cw):
    db = q.shape[0]
    ncp = kc.shape[1]
    total = past + n_new
    n_blk = -(-total // SLC_BLOCK)
    n_cmp = total_chunks(past, n_new) - 1
    gb = cw // SLC_BLOCK
    assert gb <= LANES and past % cw == 0
    col = jnp.arange((ng + 1) * LANES)
    col_block = jnp.where(col % LANES < gb, (col // LANES) * gb + col % LANES, -1)
    col_block = jnp.where(col_block < n_blk, col_block, -1)
    ov = _overlap_matrix(ncp, (ng + 1) * LANES, col_block)
    ex = (jnp.arange(LANES)[:, None] == (jnp.arange(cw)[None, :] // SLC_BLOCK)).astype(BF16)
    kern = functools.partial(_nsa_sample_kernel, past=past, n_new=n_new, n_cmp=n_cmp, n_blk=n_blk,
                             n_sel=min(N_SEL, n_blk), ng=ng, gb=gb)
    rows = KV_HEADS * SAMPLE_Q
    b3 = lambda bi: (bi, 0, 0)
    b4 = lambda bi: (bi, 0, 0, 0)
    return pl.pallas_call(
        kern,
        grid=(db,),
        in_specs=[pl.BlockSpec((None, KV_HEADS, QROWS, HEAD_DIM), b4),
                  pl.BlockSpec((None, ncp, KV_HEADS * HEAD_DIM), b3),
                  pl.BlockSpec((None, ncp, KV_HEADS * HEAD_DIM), b3),
                  pl.BlockSpec((ncp, (ng + 1) * LANES), lambda bi: (0, 0)),
                  pl.BlockSpec((LANES, cw), lambda bi: (0, 0))],
        out_specs=[pl.BlockSpec((None, KV_HEADS, QROWS, HEAD_DIM), b4),
                   pl.BlockSpec((ng, rows, cw), lambda bi: (0, bi, 0)),
                   pl.BlockSpec((rows, LANES), lambda bi: (bi, 0))],
        out_shape=[jax.ShapeDtypeStruct((db, KV_HEADS, QROWS, HEAD_DIM), F32),
                   jax.ShapeDtypeStruct((ng, db * rows, cw), F32),
                   jax.ShapeDtypeStruct((db * rows, LANES), F32)],
        scratch_shapes=[pltpu.VMEM((ng + 1, rows, LANES), F32)],
        compiler_params=_params(1),
        name="nsa_sample",
    )(q, kc, vc, ov, ex)


def total_chunks(past, n_new):
    return past // CMP_STRIDE + -(-n_new // CMP_STRIDE)


def _win_sample_kernel(q_ref, st_ref, new_ref, o_ref, *, w_buf, n_new):
    row = lax.broadcasted_iota(jnp.int32, (SAMPLE_Q, 1), 0)
    row = jnp.concatenate([row] * GROUP, axis=0)
    j_old = lax.broadcasted_iota(jnp.int32, (1, w_buf), 1)
    j_new = w_buf + lax.broadcasted_iota(jnp.int32, (1, LANES), 1)
    def bias(j, valid):
        d = w_buf + row - j
        return jnp.where((d >= 0) & (d < WINDOW) & valid, 0.0, NEG)
    b_old, b_new = bias(j_old, True), bias(j_new, j_new < w_buf + n_new)
    for kvh in range(KV_HEADS):
        q = q_ref[kvh]
        kcol, vcol = _kv_cols(kvh)
        carry = _softmax_step(_softmax_init(QROWS), q, _token_rows(st_ref, 2 * kvh, w_buf).astype(BF16),
                              _token_rows(st_ref, 2 * kvh + 1, w_buf).astype(BF16), b_old)
        carry = _softmax_step(carry, q, new_ref[:, kcol], new_ref[:, vcol], b_new)
        o_ref[kvh] = _softmax_done(carry)


def _win_sample(q, state, kv_new, n_new):
    db, w_buf = state.shape[0], state.shape[1]
    kern = functools.partial(_win_sample_kernel, w_buf=w_buf, n_new=n_new)
    b3 = lambda bi: (bi, 0, 0)
    b4 = lambda bi: (bi, 0, 0, 0)
    return pl.pallas_call(
        kern,
        grid=(db,),
        in_specs=[pl.BlockSpec((None, KV_HEADS, QROWS, HEAD_DIM), b4),
                  pl.BlockSpec((w_buf * ROWS_PER_TOKEN, HEAD_DIM), lambda bi: (bi, 0)),
                  pl.BlockSpec((None, LANES, KV_ROW), b3)],
        out_specs=pl.BlockSpec((None, KV_HEADS, QROWS, HEAD_DIM), b4),
        out_shape=jax.ShapeDtypeStruct((db, KV_HEADS, QROWS, HEAD_DIM), F32),
        compiler_params=_params(1),
        name="win_sample",
    )(q, _cache_rows(state), kv_new)


def _gate_merge_kernel(oa_ref, oc_ref, os_ref, ow_ref, misc_ref, sza_ref, szb_ref, a_ref, b_ref):
    misc = misc_ref[...]
    a_ref[...] = (oa_ref[...] * sza_ref[...]).astype(BF16)
    for hd in range(N_HEADS):
        sl = slice(hd * HEAD_DIM, (hd + 1) * HEAD_DIM)
        g = [misc[:, MISC_IW + j * N_HEADS + hd:MISC_IW + j * N_HEADS + hd + 1] for j in range(3)]
        o_b = g[0] * oc_ref[:, sl] + g[1] * os_ref[:, sl] + g[2] * ow_ref[:, sl]
        b_ref[:, sl] = (o_b * szb_ref[:, sl]).astype(BF16)


def _gate_merge(o_a, o_cmp, o_slc, o_win, misc, sza, szb):
    n = o_a.shape[0]
    full = lambda w: pl.BlockSpec((n, w), lambda i: (0, 0))
    out = jax.ShapeDtypeStruct((n, WIDTH), BF16)
    return pl.pallas_call(
        _gate_merge_kernel,
        grid=(1,),
        in_specs=[full(WIDTH)] * 4 + [full(LANES), full(WIDTH), full(WIDTH)],
        out_specs=[full(WIDTH)] * 2,
        out_shape=[out, out],
        compiler_params=_params(1),
        name="gate_merge",
    )(o_a, o_cmp, o_slc, o_win, misc, sza, szb)


def _merge_kernel(a_ref, b_ref, h_ref, wa_ref, wb_ref, wg0_ref, wg1_ref, o_ref):
    h = h_ref[...]
    y_a = _dot(a_ref[...], wa_ref[...])
    y_b = _dot(b_ref[...], wb_ref[...])
    g0 = jax.nn.sigmoid(_dot(h, wg0_ref[...]))
    g1 = jax.nn.sigmoid(_dot(h, wg1_ref[...]))
    o_ref[...] = (g0 * y_a + g1 * y_b).astype(BF16)


def _merge(a, bm, h, w_a, w_b, w_g0, w_g1, tr, cb):
    n = a.shape[0]
    row = lambda j, i: (i, 0)
    col = lambda j, i: (0, j)
    return pl.pallas_call(
        _merge_kernel,
        grid=(D_MODEL // cb, n // tr),
        in_specs=[pl.BlockSpec((tr, WIDTH), row), pl.BlockSpec((tr, WIDTH), row),
                  pl.BlockSpec((tr, D_MODEL), row),
                  pl.BlockSpec((WIDTH, cb), col), pl.BlockSpec((WIDTH, cb), col),
                  pl.BlockSpec((D_MODEL, cb), col), pl.BlockSpec((D_MODEL, cb), col)],
        out_specs=pl.BlockSpec((tr, cb), lambda j, i: (i, j)),
        out_shape=jax.ShapeDtypeStruct((n, D_MODEL), BF16),
        compiler_params=_params(2),
        name="merge",
    )(a, bm, h, w_a, w_b, w_g0, w_g1)


def _out_proj_kernel(x_ref, m_ref, w_ref, o_ref):
    o_ref[...] = x_ref[...] + _dot(m_ref[...], w_ref[...])


def _out_proj(x, merged, w_out, tr):
    n = x.shape[0]
    row = lambda i: (i, 0)
    return pl.pallas_call(
        _out_proj_kernel,
        grid=(n // tr,),
        in_specs=[pl.BlockSpec((tr, D_MODEL), row), pl.BlockSpec((tr, D_MODEL), row),
                  pl.BlockSpec((D_MODEL, D_MODEL), lambda i: (0, 0))],
        out_specs=pl.BlockSpec((tr, D_MODEL), row),
        out_shape=jax.ShapeDtypeStruct((n, D_MODEL), F32),
        compiler_params=_params(1),
        name="out_proj",
    )(x, merged, w_out)


def _project(x2d, pos_period, tr, wts):
    n = x2d.shape[0]
    n_tab = pos_period.shape[0] // tr
    tabs_head = _rope_tables(pos_period, ROT_DIM, HEAD_DIM)
    tabs_iq = _rope_tables(pos_period, IDX_ROT_DIM, IDX_DIM)
    tabs_iq = tuple(jnp.concatenate([t, t], axis=1) for t in tabs_iq)
    ident = (jnp.ones_like(tabs_iq[0][:, :IDX_DIM]), jnp.zeros_like(tabs_iq[0][:, :IDX_DIM]),
             jnp.zeros_like(tabs_iq[0][:, :IDX_DIM]))
    tabs_misc = tuple(jnp.concatenate([t[:, :IDX_DIM], e], axis=1) for t, e in zip(tabs_iq, ident))

    h = _rmsnorm(x2d, wts["norm_gain"], tr)
    (qa,) = _proj_q(h, wts["w_qa"], wts["q_norm_a"], tabs_head, tr, n_tab, False)
    qb_rot, qb = _proj_q(h, wts["w_qb"], wts["q_norm_b"], tabs_head, tr, n_tab, True)
    akv, cmp_kv, slc, win, akv_b, slc_b, win_b = _proj_kv(
        h, wts["w_kv"], wts["k_norm_a"], wts["k_norm_slc"], wts["k_norm_win"], tabs_head, tr, n_tab)
    iq, misc = _proj_idx(h, wts["w_idx"], tabs_iq, tabs_misc, tr, n_tab)
    sza, szb = _proj_z(h, wts["w_z"], tr)
    return dict(h=h, qa=qa, qb=qb, qb_rot=qb_rot, akv=akv, cmp=cmp_kv, slc=slc, win=win,
                akv_b=akv_b, slc_b=slc_b, win_b=win_b, iq=iq, misc=misc, sza=sza, szb=szb)


def _sample_rows(a, db, ds):
    a = a.reshape(db, ds, KV_HEADS, GROUP, HEAD_DIM).transpose(0, 2, 3, 1, 4)
    a = jnp.pad(a, ((0, 0), (0, 0), (0, 0), (0, SAMPLE_Q - ds), (0, 0)))
    return a.reshape(db, KV_HEADS, QROWS, HEAD_DIM)


def _unsample_rows(o, db, ds):
    o = o.reshape(db, KV_HEADS, GROUP, SAMPLE_Q, HEAD_DIM)[:, :, :, :ds]
    return o.transpose(0, 3, 1, 2, 4).reshape(db * ds, WIDTH)


def _pad_new(a, db, ds):
    return jnp.pad(a.reshape(db, ds, a.shape[-1]), ((0, 0), (0, LANES - ds), (0, 0)))


def kernel(x_prompt, x_sample, cache_a_kv, cache_a_idx, cache_cmp_kv, cache_slc_kv, state_win_kv,
           page_table, norm_gain, w_in, q_norm_a, k_norm_a, q_norm_b, k_norm_cmp, k_norm_slc,
           k_norm_win, pe_cmp, w1_cmp, w2_cmp, w_proj_a, w_proj_b, w_out):
    b, t, _ = x_prompt.shape
    db, ds, _ = x_sample.shape
    n_pages = page_table.shape[1]
    past = n_pages * PAGE_SIZE
    assert ds <= SAMPLE_Q and ds <= CMP_STRIDE

    sizes = (WIDTH, KV_ROW, IDX_COLS, IDX_DIM, IDX_HEADS, WIDTH, WIDTH, 3 * KV_ROW, 3 * N_HEADS, WIDTH,
             2 * D_MODEL)
    offs = [0]
    for s in sizes:
        offs.append(offs[-1] + s)
    wb = w_in.astype(BF16)
    seg = lambda j: wb[:, offs[j]:offs[j + 1]]
    misc_pad = jnp.zeros((D_MODEL, LANES - (IDX_DIM + IDX_HEADS + 3 * N_HEADS)), BF16)
    wts = dict(
        norm_gain=norm_gain, q_norm_a=q_norm_a, q_norm_b=q_norm_b, k_norm_a=k_norm_a,
        k_norm_slc=k_norm_slc, k_norm_win=k_norm_win,
        w_qa=seg(0), w_qb=seg(6),
        w_kv=jnp.concatenate([seg(1), seg(7)], axis=1),
        w_idx=jnp.concatenate([seg(2), seg(3), seg(4), seg(8), misc_pad], axis=1),
        w_z=jnp.concatenate([seg(5), seg(9)], axis=1),
    )
    w_g0 = wb[:, offs[10]:offs[10] + D_MODEL]
    w_g1 = wb[:, offs[10] + D_MODEL:offs[11]]
    w_pa, w_pb, w_o = w_proj_a.astype(BF16), w_proj_b.astype(BF16), w_out.astype(BF16)
    w1_fs = jnp.concatenate([w1_cmp[:, :CMP_STRIDE].reshape(2, CMP_STRIDE * HEAD_DIM, CMP_HIDDEN),
                             w1_cmp[:, CMP_STRIDE:].reshape(2, CMP_STRIDE * HEAD_DIM, CMP_HIDDEN)],
                            axis=2).astype(BF16)
    w1_r = w1_cmp.reshape(2, CMP_LEN * HEAD_DIM, CMP_HIDDEN).astype(BF16)
    pe_r = pe_cmp.reshape(2, CMP_LEN * HEAD_DIM)
    w2_b = w2_cmp.astype(BF16)

    tr = min(512, t)
    tq = min(128, t)
    tk = min(512, t)
    xp = x_prompt.reshape(b * t, D_MODEL)
    pp = _project(xp, jnp.arange(t, dtype=jnp.int32), tr, wts)

    n_chunk_p = t // CMP_STRIDE
    fs_p = _chunk_terms_dense(pp["cmp"].reshape(b * n_chunk_p, CHUNK_FLAT), w1_fs, min(256, b * n_chunk_p))
    kc_p, vc_p = _compress_mlp(fs_p.reshape(b, n_chunk_p, FS_COLS), jnp.zeros((b, SUBLANES, FS_COLS), F32),
                               pe_r, w1_r, w2_b, k_norm_cmp)

    ik_b = pp["misc"][:, :IDX_DIM].astype(BF16)
    ik2 = jnp.concatenate([ik_b, ik_b], axis=1).reshape(b, t, LANES)
    a_p = _dsa_prompt(pp["iq"], pp["misc"], ik2, pp["qa"], pp["akv_b"].reshape(b, t, KV_ROW), pp["sza"],
                      b, t, tq, tk)
    win_front = jnp.pad(pp["win_b"].reshape(b, t, KV_ROW), ((0, 0), (WINDOW, 0), (0, 0)))
    b_p = _nsa_prompt(pp["qb"], pp["qb_rot"], kc_p, vc_p, pp["slc_b"].reshape(b, t, KV_ROW),
                      win_front, pp["misc"], pp["szb"], b, t, tq, tk)
    merged_p = _merge(a_p, b_p, pp["h"], w_pa, w_pb, w_g0, w_g1, tr, 1024)
    y_prompt = _out_proj(xp, merged_p, w_o, tr).reshape(b, t, D_MODEL)

    kv5 = lambda a, n0, n1: a.reshape(n0, n1, KV_HEADS, 2, HEAD_DIM)
    p_a_kv = kv5(pp["akv"], b, t)
    p_a_idx = pp["misc"][:, :IDX_DIM].reshape(b, t, IDX_DIM)
    p_cmp_kv = kv5(pp["cmp"], b, t)
    p_slc_kv = kv5(pp["slc"], b, t)
    p_win_kv = kv5(pp["win"], b, t)[:, t - min(WINDOW, t):]

    ns = db * ds
    xs = x_sample.reshape(ns, D_MODEL)
    pos_s = past + jnp.tile(jnp.arange(ds, dtype=jnp.int32), db)
    ps = _project(xs, pos_s, ns, wts)
    pg = min(PAGES_PER_STEP, n_pages)
    ng = n_pages // pg
    cw = pg * PAGE_SIZE

    iq_s = ps["iq"].reshape(db, ds, IDX_HEADS, IDX_DIM).transpose(0, 2, 1, 3)
    iqm = jnp.pad(iq_s, ((0, 0), (0, 0), (0, SAMPLE_Q - ds), (0, 0))).reshape(db, IDX_HEADS * SAMPLE_Q, IDX_DIM)
    iw_s = ps["misc"][:, MISC_IK:MISC_IW].reshape(db, ds, IDX_HEADS).transpose(0, 2, 1) * (IDX_DIM ** -0.5)
    iwm = jnp.pad(iw_s, ((0, 0), (0, 0), (0, SAMPLE_Q - ds))).reshape(db, IDX_HEADS * SAMPLE_Q, 1)
    iwm = jnp.broadcast_to(iwm, (db, IDX_HEADS * SAMPLE_Q, LANES))
    ik_new_t = jnp.swapaxes(_pad_new(ps["misc"][:, :IDX_DIM].astype(BF16), db, ds), 1, 2)
    sc_s, sc_new = _idx_scores(jnp.swapaxes(cache_a_idx, 1, 2), page_table, iqm, iwm, ik_new_t, pg, ds)
    mk_a, mk_a_new = _dsa_topk(sc_s, sc_new, past, ds)
    o_a_s = _paged_attn(cache_a_kv, page_table, _sample_rows(ps["qa"], db, ds), mk_a, mk_a_new,
                        _pad_new(ps["akv_b"], db, ds), pg)

    fs_past = _chunk_terms_paged(cache_cmp_kv, page_table, w1_fs, pg)
    new_chunk = jnp.pad(ps["cmp"].reshape(db, ds, KV_ROW), ((0, 0), (0, CMP_STRIDE - ds), (0, 0)))
    fs_new = _chunk_terms_dense(new_chunk.reshape(db, CHUNK_FLAT), w1_fs, db)
    fs_new = jnp.pad(fs_new.reshape(db, 1, FS_COLS), ((0, 0), (0, SUBLANES - 1), (0, 0)))
    kc_s, vc_s = _compress_mlp(fs_past, fs_new, pe_r, w1_r, w2_b, k_norm_cmp)
    o_cmp_s, mk_s, mk_s_new = _nsa_sample(_sample_rows(ps["qb"], db, ds), kc_s, vc_s, past, ds, ng, cw)
    q_rot_s = _sample_rows(ps["qb_rot"], db, ds)
    o_slc_s = _paged_attn(cache_slc_kv, page_table, q_rot_s, mk_s, mk_s_new, _pad_new(ps["slc_b"], db, ds), pg)
    o_win_s = _win_sample(q_rot_s, state_win_kv, _pad_new(ps["win_b"], db, ds), ds)

    a_s, b_s = _gate_merge(_unsample_rows(o_a_s, db, ds), _unsample_rows(o_cmp_s, db, ds),
                           _unsample_rows(o_slc_s, db, ds), _unsample_rows(o_win_s, db, ds),
                           ps["misc"], ps["sza"], ps["szb"])
    merged_s = _merge(a_s, b_s, ps["h"], w_pa, w_pb, w_g0, w_g1, ns, 1024)
    y_sample = _out_proj(xs, merged_s, w_o, ns).reshape(db, ds, D_MODEL)

    s_win = kv5(ps["win"], db, ds)
    w_buf = state_win_kv.shape[1]
    s_win_kv = jnp.concatenate([state_win_kv, s_win], axis=1)[:, ds:ds + w_buf]

    return (y_prompt, y_sample, p_a_kv, p_a_idx, p_cmp_kv, p_slc_kv, p_win_kv,
            kv5(ps["akv"], db, ds), ps["misc"][:, :IDX_DIM].reshape(db, ds, IDX_DIM),
            kv5(ps["cmp"], db, ds), kv5(ps["slc"], db, ds), s_win_kv)
```

```python
import functools

import jax
import jax.numpy as jnp
from jax import lax
from jax.experimental import pallas as pl
from jax.experimental.pallas import tpu as pltpu

D_MODEL = 2048
HEAD_DIM = 128
ROT_DIM = HEAD_DIM // 4
N_HEADS = D_MODEL // (2 * HEAD_DIM)
KV_HEADS = 2
GROUP = N_HEADS // KV_HEADS
IDX_HEADS = 8
IDX_DIM = 64
IDX_ROT_DIM = IDX_DIM // 4
A_TOPK = 256
CMP_LEN = 32
CMP_STRIDE = 16
CMP_HIDDEN = 128
SLC_BLOCK = 64
N_SEL = 16
WINDOW = 512
PAGE_SIZE = 128
ROPE_THETA = 500000.0
EPS = 1e-6
NEG = -1e30
WIDTH = N_HEADS * HEAD_DIM
KV_ROW = KV_HEADS * 2 * HEAD_DIM
SCALE = HEAD_DIM ** -0.5

LANES = 128
SUBLANES = 8
VMEM_LIMIT = 48 * 1024 * 1024
PAGES_PER_STEP = 16
SAMPLE_Q = 8

F32 = jnp.float32
BF16 = jnp.bfloat16
NT_DIMS = (((1,), (1,)), ((), ()))


def _params(n_axes):
    return pltpu.CompilerParams(dimension_semantics=("arbitrary",) * n_axes,
                                vmem_limit_bytes=VMEM_LIMIT)


def _dot(a, b):
    return jnp.dot(a, b, preferred_element_type=F32)


def _dot_nt(a, b):
    return lax.dot_general(a, b, NT_DIMS, preferred_element_type=F32)


def _rope_tables(pos, rot, period, width=LANES):
    half = rot // 2
    inv = ROPE_THETA ** (-jnp.arange(half, dtype=F32) / half)
    ang = pos.astype(F32)[:, None] * inv
    cos, sin = jnp.cos(ang), jnp.sin(ang)
    n = pos.shape[0]
    zeros = lambda w: jnp.zeros((n, w), F32)
    ones = lambda w: jnp.ones((n, w), F32)
    c = jnp.concatenate([cos, cos, ones(period - rot)], axis=1)
    s1 = jnp.concatenate([-sin, zeros(period - half)], axis=1)
    s2 = jnp.concatenate([zeros(half), sin, zeros(period - rot)], axis=1)
    return c, s1, s2


def _apply_rope(y, c, s1, s2, half):
    return y * c + pltpu.roll(y, LANES - half, 1) * s1 + pltpu.roll(y, half, 1) * s2


def _head_norm(z, g):
    return z * lax.rsqrt(jnp.mean(z * z, axis=-1, keepdims=True) + EPS) * g


def _rmsnorm_kernel(x_ref, g_ref, o_ref):
    x = x_ref[...]
    y = x * lax.rsqrt(jnp.mean(x * x, axis=-1, keepdims=True) + EPS)
    o_ref[...] = (y * g_ref[...]).astype(o_ref.dtype)


def _rmsnorm(x, gain, tr):
    n, d = x.shape
    return pl.pallas_call(
        _rmsnorm_kernel,
        grid=(n // tr,),
        in_specs=[pl.BlockSpec((tr, d), lambda i: (i, 0)), pl.BlockSpec((1, d), lambda i: (0, 0))],
        out_specs=pl.BlockSpec((tr, d), lambda i: (i, 0)),
        out_shape=jax.ShapeDtypeStruct((n, d), BF16),
        compiler_params=_params(1),
        name="rmsnorm",
    )(x, gain.reshape(1, d))


def _proj_q_kernel(h_ref, w_ref, g_ref, c_ref, s1_ref, s2_ref, rot_ref, *plain_ref):
    z = _dot(h_ref[...], w_ref[...])
    g = g_ref[...]
    c, s1, s2 = c_ref[...], s1_ref[...], s2_ref[...]
    for hd in range(N_HEADS):
        sl = slice(hd * HEAD_DIM, (hd + 1) * HEAD_DIM)
        y = _head_norm(z[:, sl], g)
        if plain_ref:
            plain_ref[0][:, sl] = y.astype(BF16)
        rot_ref[:, sl] = _apply_rope(y, c, s1, s2, ROT_DIM // 2).astype(BF16)


def _proj_q(h, w, gain, tabs, tr, n_tab_blocks, want_plain):
    n = h.shape[0]
    row = lambda i: (i, 0)
    fix = lambda i: (0, 0)
    tab = lambda i: (i % n_tab_blocks, 0)
    out_shape = [jax.ShapeDtypeStruct((n, WIDTH), BF16)] * (2 if want_plain else 1)
    out_specs = [pl.BlockSpec((tr, WIDTH), row)] * (2 if want_plain else 1)
    return pl.pallas_call(
        _proj_q_kernel,
        grid=(n // tr,),
        in_specs=[pl.BlockSpec((tr, D_MODEL), row), pl.BlockSpec((D_MODEL, WIDTH), fix),
                  pl.BlockSpec((1, HEAD_DIM), fix)] + [pl.BlockSpec((tr, LANES), tab)] * 3,
        out_specs=out_specs,
        out_shape=out_shape,
        compiler_params=_params(1),
        name="proj_q",
    )(h, w, gain.reshape(1, HEAD_DIM), *tabs)


def _proj_kv_kernel(h_ref, w_ref, ga_ref, gs_ref, gw_ref, c_ref, s1_ref, s2_ref,
                    akv_ref, cmp_ref, slc_ref, win_ref, akv_b_ref, slc_b_ref, win_b_ref):
    z = _dot(h_ref[...], w_ref[...])
    c, s1, s2 = c_ref[...], s1_ref[...], s2_ref[...]

    def emit(base, gain_ref, f32_ref, bf_ref):
        for j in range(KV_HEADS * 2):
            sl = slice(j * HEAD_DIM, (j + 1) * HEAD_DIM)
            v = z[:, base + j * HEAD_DIM: base + (j + 1) * HEAD_DIM]
            if gain_ref is not None and j % 2 == 0:
                v = _apply_rope(_head_norm(v, gain_ref[...]), c, s1, s2, ROT_DIM // 2)
            f32_ref[:, sl] = v
            if bf_ref is not None:
                bf_ref[:, sl] = v.astype(BF16)

    emit(0 * KV_ROW, ga_ref, akv_ref, akv_b_ref)
    emit(1 * KV_ROW, None, cmp_ref, None)
    emit(2 * KV_ROW, gs_ref, slc_ref, slc_b_ref)
    emit(3 * KV_ROW, gw_ref, win_ref, win_b_ref)


def _proj_kv(h, w, g_a, g_slc, g_win, tabs, tr, n_tab_blocks):
    n = h.shape[0]
    row = lambda i: (i, 0)
    fix = lambda i: (0, 0)
    tab = lambda i: (i % n_tab_blocks, 0)
    f32_out = jax.ShapeDtypeStruct((n, KV_ROW), F32)
    bf_out = jax.ShapeDtypeStruct((n, KV_ROW), BF16)
    return pl.pallas_call(
        _proj_kv_kernel,
        grid=(n // tr,),
        in_specs=[pl.BlockSpec((tr, D_MODEL), row), pl.BlockSpec((D_MODEL, 4 * KV_ROW), fix)]
                 + [pl.BlockSpec((1, HEAD_DIM), fix)] * 3 + [pl.BlockSpec((tr, LANES), tab)] * 3,
        out_specs=[pl.BlockSpec((tr, KV_ROW), row)] * 7,
        out_shape=[f32_out] * 4 + [bf_out] * 3,
        compiler_params=_params(1),
        name="proj_kv",
    )(h, w, g_a.reshape(1, HEAD_DIM), g_slc.reshape(1, HEAD_DIM), g_win.reshape(1, HEAD_DIM), *tabs)


IDX_COLS = IDX_HEADS * IDX_DIM
MISC_IK = IDX_DIM
MISC_IW = MISC_IK + IDX_HEADS
MISC_G = MISC_IW + 3 * N_HEADS


def _proj_idx_kernel(h_ref, w_ref, c_ref, s1_ref, s2_ref, cm_ref, s1m_ref, s2m_ref, iq_ref, misc_ref):
    z = _dot(h_ref[...], w_ref[...])
    half = IDX_ROT_DIM // 2
    c, s1, s2 = c_ref[...], s1_ref[...], s2_ref[...]
    for j in range(IDX_COLS // LANES):
        sl = slice(j * LANES, (j + 1) * LANES)
        iq_ref[:, sl] = _apply_rope(z[:, sl], c, s1, s2, half).astype(BF16)
    m = z[:, IDX_COLS:IDX_COLS + LANES]
    roped = _apply_rope(m, cm_ref[...], s1m_ref[...], s2m_ref[...], half)
    lane = lax.broadcasted_iota(jnp.int32, m.shape, 1)
    misc_ref[...] = jnp.where(lane < MISC_IK, roped,
                              jnp.where(lane < MISC_IW, m * (IDX_HEADS ** -0.5),
                                        jnp.where(lane < MISC_G, jax.nn.sigmoid(m), 0.0)))


def _proj_idx(h, w, tabs_iq, tabs_misc, tr, n_tab_blocks):
    n = h.shape[0]
    row = lambda i: (i, 0)
    fix = lambda i: (0, 0)
    tab = lambda i: (i % n_tab_blocks, 0)
    return pl.pallas_call(
        _proj_idx_kernel,
        grid=(n // tr,),
        in_specs=[pl.BlockSpec((tr, D_MODEL), row), pl.BlockSpec((D_MODEL, IDX_COLS + LANES), fix)]
                 + [pl.BlockSpec((tr, LANES), tab)] * 6,
        out_specs=[pl.BlockSpec((tr, IDX_COLS), row), pl.BlockSpec((tr, LANES), row)],
        out_shape=[jax.ShapeDtypeStruct((n, IDX_COLS), BF16), jax.ShapeDtypeStruct((n, LANES), F32)],
        compiler_params=_params(1),
        name="proj_idx",
    )(h, w, *tabs_iq, *tabs_misc)


def _proj_z_kernel(h_ref, w_ref, za_ref, zb_ref):
    z = _dot(h_ref[...], w_ref[...])
    s = z * jax.nn.sigmoid(z)
    za_ref[...] = s[:, :WIDTH]
    zb_ref[...] = s[:, WIDTH:]


def _proj_z(h, w, tr):
    n = h.shape[0]
    row = lambda i: (i, 0)
    out = jax.ShapeDtypeStruct((n, WIDTH), F32)
    return pl.pallas_call(
        _proj_z_kernel,
        grid=(n // tr,),
        in_specs=[pl.BlockSpec((tr, D_MODEL), row), pl.BlockSpec((D_MODEL, 2 * WIDTH), lambda i: (0, 0))],
        out_specs=[pl.BlockSpec((tr, WIDTH), row)] * 2,
        out_shape=[out, out],
        compiler_params=_params(1),
        name="proj_z",
    )(h, w)


def _key_to_float(u):
    int_min = jnp.int32(-2 ** 31)
    bits = jnp.where(u < 0, u ^ int_min, ~u)
    return lax.bitcast_convert_type(bits, F32)


def _lane_fold(w):
    acc = w[:, :LANES]
    for j in range(1, w.shape[1] // LANES):
        acc = acc + w[:, j * LANES:(j + 1) * LANES]
    return acc


def _topk_threshold(load, nck, k, idx_bits, rows, cw, tail=None):
    kf = jnp.float32(k)
    lane = lax.broadcasted_iota(jnp.int32, (1, cw), 1)

    def count(pred_value):
        def body(c, acc):
            return acc + _lane_fold(pred_value(load(c), c * cw + lane))
        acc = lax.fori_loop(0, nck, body, jnp.zeros((rows, LANES), F32))
        if tail is not None:
            acc = acc + pred_value(*tail)
        return jnp.sum(acc, axis=1, keepdims=True)

    def value_bit(it, u):
        cand = u | jnp.left_shift(jnp.int32(1), 31 - it)
        candf = _key_to_float(cand)
        cnt = count(lambda x, idx: jnp.where(x >= candf, 1.0, 0.0))
        return jnp.where(cnt >= kf, cand, u)

    u = lax.fori_loop(0, 32, value_bit, jnp.zeros((rows, 1), jnp.int32))
    thr = jnp.where((u >= 0) & (u < 2 ** 23), -jnp.inf, _key_to_float(u))
    need = kf - count(lambda x, idx: jnp.where(x > thr, 1.0, 0.0))

    def index_bit(it, v):
        cand = v | jnp.left_shift(jnp.int32(1), idx_bits - 1 - it)
        below = count(lambda x, idx: jnp.where(x == thr, jnp.where(idx < cand, 1.0, 0.0), 0.0))
        return jnp.where(below < need, cand, v)

    tied = jnp.max(count(lambda x, idx: jnp.where(x >= thr, 1.0, 0.0))) > kf
    v = lax.cond(tied,
                 lambda: lax.fori_loop(0, idx_bits, index_bit, jnp.zeros((rows, 1), jnp.int32)),
                 lambda: jnp.full((rows, 1), (1 << idx_bits) - 1, jnp.int32))
    return thr, jnp.where(need >= 1.0, v, -1)


def _topk_mask(x, idx, thr, vmax):
    gt = jnp.where(x > thr, 1.0, 0.0)
    eq = jnp.where(x == thr, jnp.where(idx <= vmax, 1.0, 0.0), 0.0)
    return jnp.maximum(gt, eq)


LOG2E = 1.4426950408889634
SCORE_SCALE = SCALE * LOG2E


def _softmax_init(rows):
    return (jnp.full((rows, 1), NEG, F32), jnp.zeros((rows, 1), F32), jnp.zeros((rows, HEAD_DIM), F32))


def _softmax_step(carry, q, k, v, bias):
    m, l, acc = carry
    t = _dot_nt(q, k) * SCORE_SCALE + bias
    m_new = jnp.maximum(m, jnp.max(t, axis=1, keepdims=True))
    alpha = jnp.exp2(m - m_new)
    e = jnp.exp2(t - m_new)
    l = alpha * l + jnp.sum(e, axis=1, keepdims=True)
    acc = alpha * acc + _dot(e.astype(BF16), v)
    return m_new, l, acc


def _softmax_done(carry):
    _, l, acc = carry
    return acc / jnp.maximum(l, 1e-30)


def _mask_bias(maskf):
    return jnp.where(maskf > 0.0, 0.0, NEG)


def _stack_heads(ref, kvh):
    return jnp.concatenate([ref[:, (kvh * GROUP + g) * HEAD_DIM:(kvh * GROUP + g + 1) * HEAD_DIM]
                            for g in range(GROUP)], axis=0)


def _kv_cols(kvh):
    return (slice(kvh * 2 * HEAD_DIM, (kvh * 2 + 1) * HEAD_DIM),
            slice((kvh * 2 + 1) * HEAD_DIM, (kvh * 2 + 2) * HEAD_DIM))


def _dsa_prompt_kernel(iq_ref, misc_ref, ik2_ref, qa_ref, akv_ref, sza_ref, o_ref, sc_ref, mk_ref,
                       *, tq, tk, k_top, idx_bits):
    i = pl.program_id(1)
    nck = ((i + 1) * tq + tk - 1) // tk
    qpos = i * tq + lax.broadcasted_iota(jnp.int32, (tq, 1), 0)
    lane = lax.broadcasted_iota(jnp.int32, (1, LANES), 1)
    kiota = lax.broadcasted_iota(jnp.int32, (1, tk), 1)
    misc = misc_ref[...]

    iqs, iws = [], []
    for hd in range(IDX_HEADS):
        pair = iq_ref[:, (hd // 2) * LANES:(hd // 2 + 1) * LANES]
        keep = (lane < IDX_DIM) if hd % 2 == 0 else (lane >= IDX_DIM)
        iqs.append(jnp.where(keep, pair, jnp.zeros_like(pair)))
        iws.append(misc[:, MISC_IK + hd:MISC_IK + hd + 1] * (IDX_DIM ** -0.5))

    def score_chunk(c, carry):
        k0 = pl.multiple_of(c * tk, tk)
        ikc = ik2_ref[pl.ds(k0, tk), :]
        acc = jnp.zeros((tq, tk), F32)
        for hd in range(IDX_HEADS):
            acc = acc + jnp.maximum(_dot_nt(iqs[hd], ikc), 0.0) * iws[hd]
        sc_ref[c] = jnp.where(k0 + kiota <= qpos, acc, NEG)
        return carry

    lax.fori_loop(0, nck, score_chunk, 0)

    thr, vmax = _topk_threshold(lambda c: sc_ref[c], nck, k_top, idx_bits, tq, tk)

    def mask_chunk(c, carry):
        idx = c * tk + kiota
        sel = _topk_mask(sc_ref[c], idx, thr, vmax) * jnp.where(idx <= qpos, 1.0, 0.0)
        mk_ref[c] = _mask_bias(sel)
        return carry

    lax.fori_loop(0, nck, mask_chunk, 0)

    qs = [_stack_heads(qa_ref, kvh) for kvh in range(KV_HEADS)]

    def attend(c, carries):
        k0 = pl.multiple_of(c * tk, tk)
        bias = jnp.concatenate([mk_ref[c]] * GROUP, axis=0)
        out = []
        for kvh in range(KV_HEADS):
            kcol, vcol = _kv_cols(kvh)
            out.append(_softmax_step(carries[kvh], qs[kvh], akv_ref[pl.ds(k0, tk), kcol],
                                     akv_ref[pl.ds(k0, tk), vcol], bias))
        return tuple(out)

    carries = lax.fori_loop(0, nck, attend, tuple(_softmax_init(GROUP * tq) for _ in range(KV_HEADS)))
    for kvh in range(KV_HEADS):
        o = _softmax_done(carries[kvh])
        for g in range(GROUP):
            sl = slice((kvh * GROUP + g) * HEAD_DIM, (kvh * GROUP + g + 1) * HEAD_DIM)
            o_ref[:, sl] = (o[g * tq:(g + 1) * tq] * sza_ref[:, sl]).astype(BF16)


def _dsa_prompt(iq, misc, ik2, qa, akv_b, sza, b, t, tq, tk):
    nqb = t // tq
    k_top = min(A_TOPK, t // 4)
    idx_bits = max(1, (t - 1).bit_length())
    row = lambda bi, i: (bi * nqb + i, 0)
    whole = lambda bi, i: (bi, 0, 0)
    kern = functools.partial(_dsa_prompt_kernel, tq=tq, tk=tk, k_top=k_top, idx_bits=idx_bits)
    return pl.pallas_call(
        kern,
        grid=(b, nqb),
        in_specs=[pl.BlockSpec((tq, IDX_COLS), row), pl.BlockSpec((tq, LANES), row),
                  pl.BlockSpec((None, t, LANES), whole), pl.BlockSpec((tq, WIDTH), row),
                  pl.BlockSpec((None, t, KV_ROW), whole), pl.BlockSpec((tq, WIDTH), row)],
        out_specs=pl.BlockSpec((tq, WIDTH), row),
        out_shape=jax.ShapeDtypeStruct((b * t, WIDTH), BF16),
        scratch_shapes=[pltpu.VMEM((t // tk, tq, tk), F32), pltpu.VMEM((t // tk, tq, tk), F32)],
        compiler_params=_params(2),
        name="dsa_prompt",
    )(iq, misc, ik2, qa, akv_b, sza)


SLC_SHIFT = SLC_BLOCK.bit_length() - 1
assert 1 << SLC_SHIFT == SLC_BLOCK


def _block_rules(imp, blk, qpos):
    cur = lax.shift_right_logical(qpos, SLC_SHIFT)
    return jnp.where(blk == cur, -NEG, jnp.where(blk * SLC_BLOCK > qpos, NEG, imp))


def _split_dot(p, m):
    hi = p.astype(BF16)
    lo = (p - hi.astype(F32)).astype(BF16)
    return _dot(hi, m) + _dot(lo, m)


def _nsa_prompt_kernel(qb_ref, qr_ref, kc_ref, vc_ref, slc_ref, win_ref, ov_ref, misc_ref, szb_ref,
                       o_ref, sc_ref, *, tq, tk, n_cmp, n_blk, n_sel):
    i = pl.program_id(1)
    nck = ((i + 1) * tq + tk - 1) // tk
    qpos = i * tq + lax.broadcasted_iota(jnp.int32, (tq, 1), 0)
    kiota = lax.broadcasted_iota(jnp.int32, (1, tk), 1)
    misc = misc_ref[...]
    ncp = kc_ref.shape[0]
    cidx = lax.broadcasted_iota(jnp.int32, (1, ncp), 1)
    cmask = jnp.where((CMP_STRIDE * cidx + CMP_LEN - 1 <= qpos) & (cidx < n_cmp), 1.0, 0.0)
    cmask = jnp.concatenate([cmask] * GROUP, axis=0)
    blk = lax.broadcasted_iota(jnp.int32, (1, LANES), 1)

    o_cmp = []
    for kvh in range(KV_HEADS):
        kc = kc_ref[:, kvh * HEAD_DIM:(kvh + 1) * HEAD_DIM]
        vc = vc_ref[:, kvh * HEAD_DIM:(kvh + 1) * HEAD_DIM]
        s = _dot_nt(_stack_heads(qb_ref, kvh), kc) * SCALE
        sm = jnp.where(cmask > 0.0, s, NEG)
        e = jnp.exp(sm - jnp.max(sm, axis=1, keepdims=True)) * cmask
        p = e / jnp.maximum(jnp.sum(e, axis=1, keepdims=True), 1e-30)
        o_cmp.append(_dot(p.astype(BF16), vc))
        psum = p[0:tq]
        for g in range(1, GROUP):
            psum = psum + p[g * tq:(g + 1) * tq]
        imp = _split_dot(psum, ov_ref[...])
        imp = _block_rules(imp, blk, qpos)
        sc_ref[kvh] = jnp.where(blk < n_blk, imp, -jnp.inf)

    rows = KV_HEADS * tq
    load = lambda c: sc_ref[...].reshape(rows, LANES)
    thr, vmax = _topk_threshold(load, 1, n_sel, 7, rows, LANES)
    bmask = _topk_mask(load(0), blk, thr, vmax).astype(BF16)

    riota = lax.broadcasted_iota(jnp.int32, (LANES, 1), 0)
    nwk = WINDOW + tq
    w0 = pl.multiple_of(i * tq, tq)
    kposw = i * tq - WINDOW + lax.broadcasted_iota(jnp.int32, (1, nwk), 1)
    dw = qpos - kposw
    bias_w = jnp.where((dw >= 0) & (dw < WINDOW) & (kposw >= 0), 0.0, NEG)
    bias_w = jnp.concatenate([bias_w] * GROUP, axis=0)

    qs = [_stack_heads(qr_ref, kvh) for kvh in range(KV_HEADS)]

    def attend(c, carries):
        k0 = pl.multiple_of(c * tk, tk)
        kpos = k0 + kiota
        expand = jnp.where(riota == lax.shift_right_logical(kpos, SLC_SHIFT), 1.0, 0.0).astype(BF16)
        causal = jnp.where(kpos <= qpos, 1.0, 0.0)
        out = []
        for kvh in range(KV_HEADS):
            kcol, vcol = _kv_cols(kvh)
            sel = _dot(bmask[kvh * tq:(kvh + 1) * tq], expand) * causal
            bias = jnp.concatenate([_mask_bias(sel)] * GROUP, axis=0)
            out.append(_softmax_step(carries[kvh], qs[kvh], slc_ref[pl.ds(k0, tk), kcol],
                                     slc_ref[pl.ds(k0, tk), vcol], bias))
        return tuple(out)

    slc_carries = lax.fori_loop(0, nck, attend, tuple(_softmax_init(GROUP * tq) for _ in range(KV_HEADS)))

    for kvh in range(KV_HEADS):
        q = qs[kvh]
        kcol, vcol = _kv_cols(kvh)
        o_slc = _softmax_done(slc_carries[kvh])
        o_win = _softmax_done(_softmax_step(_softmax_init(GROUP * tq), q, win_ref[pl.ds(w0, nwk), kcol],
                                            win_ref[pl.ds(w0, nwk), vcol], bias_w))
        for g in range(GROUP):
            hd = kvh * GROUP + g
            sl = slice(hd * HEAD_DIM, (hd + 1) * HEAD_DIM)
            rs = slice(g * tq, (g + 1) * tq)
            gate = [misc[:, MISC_IW + j * N_HEADS + hd:MISC_IW + j * N_HEADS + hd + 1] for j in range(3)]
            o_b = gate[0] * o_cmp[kvh][rs] + gate[1] * o_slc[rs] + gate[2] * o_win[rs]
            o_ref[:, sl] = (o_b * szb_ref[:, sl]).astype(BF16)


def _overlap_matrix(n_cmp_pad, n_cols, col_block):
    i = jnp.arange(n_cmp_pad)[:, None]
    j = col_block[None, :]
    ov = (CMP_STRIDE * i < SLC_BLOCK * (j + 1)) & (CMP_STRIDE * i + CMP_LEN > SLC_BLOCK * j) & (j >= 0)
    return ov.astype(BF16)


def _nsa_prompt(qb, qr, kc, vc, slc_b, win_b, misc, szb, b, t, tq, tk):
    nqb = t // tq
    n_cmp = t // CMP_STRIDE - 1
    n_blk = t // SLC_BLOCK
    assert n_blk <= LANES
    ncp = kc.shape[1]
    ov = _overlap_matrix(ncp, LANES, jnp.where(jnp.arange(LANES) < n_blk, jnp.arange(LANES), -1))
    row = lambda bi, i: (bi * nqb + i, 0)
    whole = lambda bi, i: (bi, 0, 0)
    kern = functools.partial(_nsa_prompt_kernel, tq=tq, tk=tk, n_cmp=n_cmp, n_blk=n_blk,
                             n_sel=min(N_SEL, n_blk))
    return pl.pallas_call(
        kern,
        grid=(b, nqb),
        in_specs=[pl.BlockSpec((tq, WIDTH), row), pl.BlockSpec((tq, WIDTH), row),
                  pl.BlockSpec((None, ncp, KV_HEADS * HEAD_DIM), whole),
                  pl.BlockSpec((None, ncp, KV_HEADS * HEAD_DIM), whole),
                  pl.BlockSpec((None, t, KV_ROW), whole), pl.BlockSpec((None, t + WINDOW, KV_ROW), whole),
                  pl.BlockSpec((ncp, LANES), lambda bi, i: (0, 0)),
                  pl.BlockSpec((tq, LANES), row), pl.BlockSpec((tq, WIDTH), row)],
        out_specs=pl.BlockSpec((tq, WIDTH), row),
        out_shape=jax.ShapeDtypeStruct((b * t, WIDTH), BF16),
        scratch_shapes=[pltpu.VMEM((KV_HEADS, tq, LANES), F32)],
        compiler_params=_params(2),
        name="nsa_prompt",
    )(qb, qr, kc, vc, slc_b, win_b, ov, misc, szb)


CHUNK_FLAT = CMP_STRIDE * KV_ROW
FS_COLS = KV_HEADS * 2 * 2 * CMP_HIDDEN


def _chunk_terms_body(x, w_ref, o_ref):
    for kc in range(KV_HEADS * 2):
        xk = jnp.concatenate(
            [x[:, p * KV_ROW + kc * HEAD_DIM: p * KV_ROW + (kc + 1) * HEAD_DIM] for p in range(CMP_STRIDE)],
            axis=1).astype(BF16)
        o_ref[:, kc * 2 * CMP_HIDDEN:(kc + 1) * 2 * CMP_HIDDEN] = _dot(xk, w_ref[kc % 2])


def _chunk_terms_dense_kernel(x_ref, w_ref, o_ref):
    _chunk_terms_body(x_ref[...], w_ref, o_ref)


def _chunk_terms_dense(x, w, tr):
    n = x.shape[0]
    return pl.pallas_call(
        _chunk_terms_dense_kernel,
        grid=(n // tr,),
        in_specs=[pl.BlockSpec((tr, CHUNK_FLAT), lambda i: (i, 0)),
                  pl.BlockSpec((2, CMP_STRIDE * HEAD_DIM, 2 * CMP_HIDDEN), lambda i: (0, 0, 0))],
        out_specs=pl.BlockSpec((tr, FS_COLS), lambda i: (i, 0)),
        out_shape=jax.ShapeDtypeStruct((n, FS_COLS), F32),
        compiler_params=_params(1),
        name="chunk_terms_dense",
    )(x, w)


ROWS_PER_TOKEN = KV_HEADS * 2
PAGE_ROWS = PAGE_SIZE * ROWS_PER_TOKEN
CHUNKS_PER_PAGE = PAGE_SIZE // CMP_STRIDE


def _cache_rows(cache):
    return cache.reshape(-1, HEAD_DIM)


def _token_rows(ref, slot, n_tokens):
    return ref[pl.ds(slot, n_tokens, stride=ROWS_PER_TOKEN), :]


def _sublane_transpose(tiles):
    tiles = list(tiles)
    sub = lax.broadcasted_iota(jnp.int32, (SUBLANES, LANES), 0)
    for d in (4, 2, 1):
        keep = (sub & d) == 0
        for i in range(SUBLANES):
            if i & d:
                continue
            lo, hi = tiles[i], tiles[i + d]
            tiles[i] = jnp.where(keep, lo, pltpu.roll(hi, d, 0))
            tiles[i + d] = jnp.where(keep, pltpu.roll(lo, SUBLANES - d, 0), hi)
    return tiles


def _chunk_terms_paged_kernel(pt_ref, *refs, pg):
    pages, w_ref, o_ref = refs[:pg], refs[pg], refs[pg + 1]
    chunk_rows = CMP_STRIDE * ROWS_PER_TOKEN
    assert CHUNKS_PER_PAGE == SUBLANES and chunk_rows % SUBLANES == 0
    pieces = [[] for _ in range(chunk_rows)]
    for pr in pages:
        for t in range(chunk_rows // SUBLANES):
            tiles = [pr[n * chunk_rows + t * SUBLANES:n * chunk_rows + (t + 1) * SUBLANES, :]
                     for n in range(CHUNKS_PER_PAGE)]
            for s, tile in enumerate(_sublane_transpose(tiles)):
                pieces[t * SUBLANES + s].append(tile)
    for kc in range(ROWS_PER_TOKEN):
        xk = jnp.concatenate([jnp.concatenate(pieces[p * ROWS_PER_TOKEN + kc], axis=0)
                              for p in range(CMP_STRIDE)], axis=1).astype(BF16)
        o_ref[:, kc * 2 * CMP_HIDDEN:(kc + 1) * 2 * CMP_HIDDEN] = _dot(xk, w_ref[kc % 2])


def _page_specs(block, pg):
    tail = (0,) * (len(block) - 1)

    def spec(j):
        return pl.BlockSpec(block, lambda bi, g, pt: (pt[bi, g * pg + j],) + tail)
    return [spec(j) for j in range(pg)]


def _chunk_terms_paged(pool, page_table, w, pg):
    db, n_pages = page_table.shape
    cpp = CHUNKS_PER_PAGE
    pool_v = _cache_rows(pool)
    kern = functools.partial(_chunk_terms_paged_kernel, pg=pg)
    grid_spec = pltpu.PrefetchScalarGridSpec(
        num_scalar_prefetch=1,
        grid=(db, n_pages // pg),
        in_specs=_page_specs((PAGE_ROWS, HEAD_DIM), pg)
                 + [pl.BlockSpec((2, CMP_STRIDE * HEAD_DIM, 2 * CMP_HIDDEN), lambda bi, g, pt: (0, 0, 0))],
        out_specs=pl.BlockSpec((None, pg * cpp, FS_COLS), lambda bi, g, pt: (bi, g, 0)),
    )
    return pl.pallas_call(
        kern, grid_spec=grid_spec,
        out_shape=jax.ShapeDtypeStruct((db, n_pages * cpp, FS_COLS), F32),
        compiler_params=_params(2),
        name="chunk_terms_paged",
    )(page_table, *([pool_v] * pg), w)


def _compress_mlp_kernel(fs_ref, new_ref, pe_ref, w1_ref, w2_ref, g_ref, kc_ref, vc_ref):
    n = fs_ref.shape[0]
    last = lax.broadcasted_iota(jnp.int32, (n, 1), 0) == n - 1
    for kc in range(KV_HEADS * 2):
        kvh, c = kc // 2, kc % 2
        base = kc * 2 * CMP_HIDDEN
        first = fs_ref[:, base:base + CMP_HIDDEN]
        second = fs_ref[:, base + CMP_HIDDEN:base + 2 * CMP_HIDDEN]
        shifted = jnp.where(last, new_ref[0:1, base + CMP_HIDDEN:base + 2 * CMP_HIDDEN],
                            pltpu.roll(second, n - 1, 0))
        pe = jnp.broadcast_to(pe_ref[c:c + 1, :], (SUBLANES, pe_ref.shape[1])).astype(BF16)
        bias = _dot(pe, w1_ref[c])[0:1, :]
        pre = first + shifted + bias
        hid = pre * jax.nn.sigmoid(pre)
        out = _dot(hid.astype(BF16), w2_ref[c])
        sl = slice(kvh * HEAD_DIM, (kvh + 1) * HEAD_DIM)
        if c == 0:
            kc_ref[:, sl] = _head_norm(out, g_ref[...]).astype(BF16)
        else:
            vc_ref[:, sl] = out.astype(BF16)


def _compress_mlp(fs, fs_new, pe, w1r, w2, g_cmp):
    nb, n, _ = fs.shape
    whole = lambda bi: (bi, 0, 0)
    fix2 = lambda bi: (0, 0)
    fix3 = lambda bi: (0, 0, 0)
    out = jax.ShapeDtypeStruct((nb, n, KV_HEADS * HEAD_DIM), BF16)
    return pl.pallas_call(
        _compress_mlp_kernel,
        grid=(nb,),
        in_specs=[pl.BlockSpec((None, n, FS_COLS), whole), pl.BlockSpec((None, SUBLANES, FS_COLS), whole),
                  pl.BlockSpec((2, CMP_LEN * HEAD_DIM), fix2),
                  pl.BlockSpec((2, CMP_LEN * HEAD_DIM, CMP_HIDDEN), fix3),
                  pl.BlockSpec((2, CMP_HIDDEN, HEAD_DIM), fix3), pl.BlockSpec((1, HEAD_DIM), fix2)],
        out_specs=[pl.BlockSpec((None, n, KV_HEADS * HEAD_DIM), whole)] * 2,
        out_shape=[out, out],
        compiler_params=_params(1),
        name="compress_mlp",
    )(fs, fs_new, pe, w1r, w2, g_cmp.reshape(1, HEAD_DIM))


def _idx_scores_kernel(pt_ref, *refs, pg, n_new):
    pages = refs[:pg]
    iq_ref, iw_ref, new_ref, o_ref, onew_ref = refs[pg:pg + 5]
    g = pl.program_id(1)
    iq = iq_ref[...]
    iw = iw_ref[...]
    row = lax.broadcasted_iota(jnp.int32, (SAMPLE_Q, 1), 0)

    def scores(keys_t, first_idx):
        s = jnp.maximum(_dot(iq, keys_t), 0.0)
        n = s.shape[1]
        w = s * jnp.concatenate([iw] * (n // LANES), axis=1)
        acc = w[0:SAMPLE_Q]
        for hd in range(1, IDX_HEADS):
            acc = acc + w[hd * SAMPLE_Q:(hd + 1) * SAMPLE_Q]
        ramp = -(first_idx + lax.broadcasted_iota(jnp.int32, (1, n), 1)).astype(F32)
        return jnp.where(row < n_new, acc, ramp)

    cw = pg * PAGE_SIZE
    o_ref[...] = scores(jnp.concatenate([p[...] for p in pages], axis=1).astype(BF16), g * cw)

    @pl.when(g == pl.num_programs(1) - 1)
    def _():
        onew_ref[...] = scores(new_ref[...], pl.num_programs(1) * cw)


def _idx_scores(pool_idx_t, page_table, iqm, iwm, ik_new_t, pg, n_new):
    db, n_pages = page_table.shape
    ng = n_pages // pg
    kern = functools.partial(_idx_scores_kernel, pg=pg, n_new=n_new)
    per_b = lambda bi, g, pt: (bi, 0, 0)
    grid_spec = pltpu.PrefetchScalarGridSpec(
        num_scalar_prefetch=1,
        grid=(db, ng),
        in_specs=_page_specs((None, IDX_DIM, PAGE_SIZE), pg)
                 + [pl.BlockSpec((None, IDX_HEADS * SAMPLE_Q, IDX_DIM), per_b),
                    pl.BlockSpec((None, IDX_HEADS * SAMPLE_Q, LANES), per_b),
                    pl.BlockSpec((None, IDX_DIM, LANES), per_b)],
        out_specs=[pl.BlockSpec((None, SAMPLE_Q, pg * PAGE_SIZE), lambda bi, g, pt: (g, bi, 0)),
                   pl.BlockSpec((SAMPLE_Q, LANES), lambda bi, g, pt: (bi, 0))],
    )
    return pl.pallas_call(
        kern, grid_spec=grid_spec,
        out_shape=[jax.ShapeDtypeStruct((ng, db * SAMPLE_Q, pg * PAGE_SIZE), F32),
                   jax.ShapeDtypeStruct((db * SAMPLE_Q, LANES), F32)],
        compiler_params=_params(2),
        name="idx_scores",
    )(page_table, *([pool_idx_t] * pg), iqm, iwm, ik_new_t)


TOPK_ROWS = 64


def _dsa_topk_kernel(sc_ref, new_ref, o_ref, onew_ref, *, ng, cw, n_new, k_top, idx_bits, rb):
    row = lax.broadcasted_iota(jnp.int32, (rb, 1), 0)
    q = row & (SAMPLE_Q - 1)
    lane = lax.broadcasted_iota(jnp.int32, (1, LANES), 1)
    lane_cw = lax.broadcasted_iota(jnp.int32, (1, cw), 1)
    ok_new = (lane <= q) & (lane < n_new)
    x_new = jnp.where(ok_new, new_ref[...], NEG)
    idx_new = ng * cw + lane

    thr, vmax = _topk_threshold(lambda c: sc_ref[c], ng, k_top, idx_bits, rb, cw, tail=(x_new, idx_new))
    for g in range(ng):
        o_ref[g] = _mask_bias(_topk_mask(sc_ref[g], g * cw + lane_cw, thr, vmax))
    onew_ref[...] = _mask_bias(_topk_mask(x_new, idx_new, thr, vmax) * jnp.where(ok_new, 1.0, 0.0))


def _dsa_topk(scores, scores_new, past, n_new):
    ng, rows, cw = scores.shape
    rb = min(TOPK_ROWS, rows)
    total = past + n_new
    kern = functools.partial(_dsa_topk_kernel, ng=ng, cw=cw, n_new=n_new, k_top=min(A_TOPK, total // 4),
                             idx_bits=(ng * cw + LANES - 1).bit_length(), rb=rb)
    big = pl.BlockSpec((ng, rb, cw), lambda r: (0, r, 0))
    small = pl.BlockSpec((rb, LANES), lambda r: (r, 0))
    return pl.pallas_call(
        kern,
        grid=(rows // rb,),
        in_specs=[big, small],
        out_specs=[big, small],
        out_shape=[jax.ShapeDtypeStruct((ng, rows, cw), F32), jax.ShapeDtypeStruct((rows, LANES), F32)],
        compiler_params=_params(1),
        name="dsa_topk",
    )(scores, scores_new)


QROWS = GROUP * SAMPLE_Q


def _paged_attn_kernel(pt_ref, *refs, pg, mask_rows):
    pages = refs[:pg]
    q_ref, mk_ref, mknew_ref, new_ref, o_ref, m_ref, l_ref, acc_ref = refs[pg:pg + 8]
    g = pl.program_id(1)

    @pl.when(g == 0)
    def _():
        m_ref[...] = jnp.full(m_ref.shape, NEG, F32)
        l_ref[...] = jnp.zeros(l_ref.shape, F32)
        acc_ref[...] = jnp.zeros(acc_ref.shape, F32)

    def update(keys, values, bias_all):
        carries = [(m_ref[kvh], l_ref[kvh], acc_ref[kvh]) for kvh in range(KV_HEADS)]
        for kvh in range(KV_HEADS):
            r0 = kvh * SAMPLE_Q if mask_rows == KV_HEADS * SAMPLE_Q else 0
            bias = jnp.concatenate([bias_all[r0:r0 + SAMPLE_Q]] * GROUP, axis=0)
            carries[kvh] = _softmax_step(carries[kvh], q_ref[kvh], keys(kvh), values(kvh), bias)
        for kvh in range(KV_HEADS):
            m_ref[kvh], l_ref[kvh], acc_ref[kvh] = carries[kvh]

    def paged(slot):
        return jnp.concatenate([_token_rows(p, slot, PAGE_SIZE) for p in pages], axis=0).astype(BF16)

    update(lambda kvh: paged(2 * kvh), lambda kvh: paged(2 * kvh + 1), mk_ref[...])

    @pl.when(g == pl.num_programs(1) - 1)
    def _():
        update(lambda kvh: new_ref[:, _kv_cols(kvh)[0]], lambda kvh: new_ref[:, _kv_cols(kvh)[1]], mknew_ref[...])
        for kvh in range(KV_HEADS):
            o_ref[kvh] = _softmax_done((m_ref[kvh], l_ref[kvh], acc_ref[kvh]))


def _paged_attn(pool, page_table, q, bias, bias_new, kv_new, pg):
    db, n_pages = page_table.shape
    ng = n_pages // pg
    mask_rows = bias.shape[1] // db
    pool_v = _cache_rows(pool)
    kern = functools.partial(_paged_attn_kernel, pg=pg, mask_rows=mask_rows)
    per_b3 = lambda bi, g, pt: (bi, 0, 0)
    per_b4 = lambda bi, g, pt: (bi, 0, 0, 0)
    grid_spec = pltpu.PrefetchScalarGridSpec(
        num_scalar_prefetch=1,
        grid=(db, ng),
        in_specs=_page_specs((PAGE_ROWS, HEAD_DIM), pg)
                 + [pl.BlockSpec((None, KV_HEADS, QROWS, HEAD_DIM), per_b4),
                    pl.BlockSpec((None, mask_rows, pg * PAGE_SIZE), lambda bi, g, pt: (g, bi, 0)),
                    pl.BlockSpec((mask_rows, LANES), lambda bi, g, pt: (bi, 0)),
                    pl.BlockSpec((None, LANES, KV_ROW), per_b3)],
        out_specs=pl.BlockSpec((None, KV_HEADS, QROWS, HEAD_DIM), per_b4),
        scratch_shapes=[pltpu.VMEM((KV_HEADS, QROWS, 1), F32), pltpu.VMEM((KV_HEADS, QROWS, 1), F32),
                        pltpu.VMEM((KV_HEADS, QROWS, HEAD_DIM), F32)],
    )
    return pl.pallas_call(
        kern, grid_spec=grid_spec,
        out_shape=jax.ShapeDtypeStruct((db, KV_HEADS, QROWS, HEAD_DIM), F32),
        compiler_params=_params(2),
        name="paged_attn",
    )(page_table, *([pool_v] * pg), q, bias, bias_new, kv_new)


def _nsa_sample_kernel(q_ref, kc_ref, vc_ref, ov_ref, ex_ref, o_ref, tm_ref, tmnew_ref, sc_ref,
                       *, past, n_new, n_cmp, n_blk, n_sel, ng, gb):
    row = lax.broadcasted_iota(jnp.int32, (SAMPLE_Q, 1), 0)
    qpos = past + row
    ncp = kc_ref.shape[0]
    cidx = lax.broadcasted_iota(jnp.int32, (1, ncp), 1)
    cmask8 = jnp.where((CMP_STRIDE * cidx + CMP_LEN - 1 <= qpos) & (cidx < n_cmp), 1.0, 0.0)
    cmask = jnp.concatenate([cmask8] * GROUP, axis=0)
    lane = lax.broadcasted_iota(jnp.int32, (1, LANES), 1)

    for kvh in range(KV_HEADS):
        kc = kc_ref[:, kvh * HEAD_DIM:(kvh + 1) * HEAD_DIM]
        vc = vc_ref[:, kvh * HEAD_DIM:(kvh + 1) * HEAD_DIM]
        s = _dot_nt(q_ref[kvh], kc) * SCALE
        sm = jnp.where(cmask > 0.0, s, NEG)
        e = jnp.exp(sm - jnp.max(sm, axis=1, keepdims=True)) * cmask
        p = e / jnp.maximum(jnp.sum(e, axis=1, keepdims=True), 1e-30)
        o_ref[kvh] = _dot(p.astype(BF16), vc)
        psum = p[0:SAMPLE_Q]
        for g in range(1, GROUP):
            psum = psum + p[g * SAMPLE_Q:(g + 1) * SAMPLE_Q]
        imp = _split_dot(psum, ov_ref[...])
        for g in range(ng + 1):
            blk = g * gb + lane
            v = _block_rules(imp[:, g * LANES:(g + 1) * LANES], blk, qpos)
            v = jnp.where(row < n_new, v, -blk.astype(F32))
            sc_ref[g, kvh * SAMPLE_Q:(kvh + 1) * SAMPLE_Q, :] = jnp.where((lane < gb) & (blk < n_blk), v, -jnp.inf)

    rows = KV_HEADS * SAMPLE_Q
    idx_bits = ((ng + 1) * LANES - 1).bit_length()
    thr, vmax = _topk_threshold(lambda c: sc_ref[c], ng + 1, n_sel, idx_bits, rows, LANES)
    qpos2 = jnp.concatenate([qpos] * KV_HEADS, axis=0)
    for g in range(ng + 1):
        bm = _topk_mask(sc_ref[g], g * LANES + lane, thr, vmax).astype(BF16)
        tok = _dot(bm, ex_ref[...])
        if g < ng:
            tm_ref[g] = _mask_bias(tok)
        else:
            t = lax.broadcasted_iota(jnp.int32, (1, LANES), 1)
            ok = jnp.where((past + t <= qpos2) & (t < n_new), 1.0, 0.0)
            tmnew_ref[...] = _mask_bias(tok[:, :LANES] * ok)


def _nsa_sample(q, kc, vc, past, n_new, ng, cw):
    db = q.shape[0]
    ncp = kc.shape[1]
    total = past + n_new
    n_blk = -(-total // SLC_BLOCK)
    n_cmp = total_chunks(past, n_new) - 1
    gb = cw // SLC_BLOCK
    assert gb <= LANES and past % cw == 0
    col = jnp.arange((ng + 1) * LANES)
    col_block = jnp.where(col % LANES < gb, (col // LANES) * gb + col % LANES, -1)
    col_block = jnp.where(col_block < n_blk, col_block, -1)
    ov = _overlap_matrix(ncp, (ng + 1) * LANES, col_block)
    ex = (jnp.arange(LANES)[:, None] == (jnp.arange(cw)[None, :] // SLC_BLOCK)).astype(BF16)
    kern = functools.partial(_nsa_sample_kernel, past=past, n_new=n_new, n_cmp=n_cmp, n_blk=n_blk,
                             n_sel=min(N_SEL, n_blk), ng=ng, gb=gb)
    rows = KV_HEADS * SAMPLE_Q
    b3 = lambda bi: (bi, 0, 0)
    b4 = lambda bi: (bi, 0, 0, 0)
    return pl.pallas_call(
        kern,
        grid=(db,),
        in_specs=[pl.BlockSpec((None, KV_HEADS, QROWS, HEAD_DIM), b4),
                  pl.BlockSpec((None, ncp, KV_HEADS * HEAD_DIM), b3),
                  pl.BlockSpec((None, ncp, KV_HEADS * HEAD_DIM), b3),
                  pl.BlockSpec((ncp, (ng + 1) * LANES), lambda bi: (0, 0)),
                  pl.BlockSpec((LANES, cw), lambda bi: (0, 0))],
        out_specs=[pl.BlockSpec((None, KV_HEADS, QROWS, HEAD_DIM), b4),
                   pl.BlockSpec((ng, rows, cw), lambda bi: (0, bi, 0)),
                   pl.BlockSpec((rows, LANES), lambda bi: (bi, 0))],
        out_shape=[jax.ShapeDtypeStruct((db, KV_HEADS, QROWS, HEAD_DIM), F32),
                   jax.ShapeDtypeStruct((ng, db * rows, cw), F32),
                   jax.ShapeDtypeStruct((db * rows, LANES), F32)],
        scratch_shapes=[pltpu.VMEM((ng + 1, rows, LANES), F32)],
        compiler_params=_params(1),
        name="nsa_sample",
    )(q, kc, vc, ov, ex)


def total_chunks(past, n_new):
    return past // CMP_STRIDE + -(-n_new // CMP_STRIDE)


def _win_sample_kernel(q_ref, st_ref, new_ref, o_ref, *, w_buf, n_new):
    row = lax.broadcasted_iota(jnp.int32, (SAMPLE_Q, 1), 0)
    row = jnp.concatenate([row] * GROUP, axis=0)
    j_old = lax.broadcasted_iota(jnp.int32, (1, w_buf), 1)
    j_new = w_buf + lax.broadcasted_iota(jnp.int32, (1, LANES), 1)
    def bias(j, valid):
        d = w_buf + row - j
        return jnp.where((d >= 0) & (d < WINDOW) & valid, 0.0, NEG)
    b_old, b_new = bias(j_old, True), bias(j_new, j_new < w_buf + n_new)
    for kvh in range(KV_HEADS):
        q = q_ref[kvh]
        kcol, vcol = _kv_cols(kvh)
        carry = _softmax_step(_softmax_init(QROWS), q, _token_rows(st_ref, 2 * kvh, w_buf).astype(BF16),
                              _token_rows(st_ref, 2 * kvh + 1, w_buf).astype(BF16), b_old)
        carry = _softmax_step(carry, q, new_ref[:, kcol], new_ref[:, vcol], b_new)
        o_ref[kvh] = _softmax_done(carry)


def _win_sample(q, state, kv_new, n_new):
    db, w_buf = state.shape[0], state.shape[1]
    kern = functools.partial(_win_sample_kernel, w_buf=w_buf, n_new=n_new)
    b3 = lambda bi: (bi, 0, 0)
    b4 = lambda bi: (bi, 0, 0, 0)
    return pl.pallas_call(
        kern,
        grid=(db,),
        in_specs=[pl.BlockSpec((None, KV_HEADS, QROWS, HEAD_DIM), b4),
                  pl.BlockSpec((w_buf * ROWS_PER_TOKEN, HEAD_DIM), lambda bi: (bi, 0)),
                  pl.BlockSpec((None, LANES, KV_ROW), b3)],
        out_specs=pl.BlockSpec((None, KV_HEADS, QROWS, HEAD_DIM), b4),
        out_shape=jax.ShapeDtypeStruct((db, KV_HEADS, QROWS, HEAD_DIM), F32),
        compiler_params=_params(1),
        name="win_sample",
    )(q, _cache_rows(state), kv_new)


def _gate_merge_kernel(oa_ref, oc_ref, os_ref, ow_ref, misc_ref, sza_ref, szb_ref, a_ref, b_ref):
    misc = misc_ref[...]
    a_ref[...] = (oa_ref[...] * sza_ref[...]).astype(BF16)
    for hd in range(N_HEADS):
        sl = slice(hd * HEAD_DIM, (hd + 1) * HEAD_DIM)
        g = [misc[:, MISC_IW + j * N_HEADS + hd:MISC_IW + j * N_HEADS + hd + 1] for j in range(3)]
        o_b = g[0] * oc_ref[:, sl] + g[1] * os_ref[:, sl] + g[2] * ow_ref[:, sl]
        b_ref[:, sl] = (o_b * szb_ref[:, sl]).astype(BF16)


def _gate_merge(o_a, o_cmp, o_slc, o_win, misc, sza, szb):
    n = o_a.shape[0]
    full = lambda w: pl.BlockSpec((n, w), lambda i: (0, 0))
    out = jax.ShapeDtypeStruct((n, WIDTH), BF16)
    return pl.pallas_call(
        _gate_merge_kernel,
        grid=(1,),
        in_specs=[full(WIDTH)] * 4 + [full(LANES), full(WIDTH), full(WIDTH)],
        out_specs=[full(WIDTH)] * 2,
        out_shape=[out, out],
        compiler_params=_params(1),
        name="gate_merge",
    )(o_a, o_cmp, o_slc, o_win, misc, sza, szb)


def _merge_kernel(a_ref, b_ref, h_ref, wa_ref, wb_ref, wg0_ref, wg1_ref, o_ref):
    h = h_ref[...]
    y_a = _dot(a_ref[...], wa_ref[...])
    y_b = _dot(b_ref[...], wb_ref[...])
    g0 = jax.nn.sigmoid(_dot(h, wg0_ref[...]))
    g1 = jax.nn.sigmoid(_dot(h, wg1_ref[...]))
    o_ref[...] = (g0 * y_a + g1 * y_b).astype(BF16)


def _merge(a, bm, h, w_a, w_b, w_g0, w_g1, tr, cb):
    n = a.shape[0]
    row = lambda j, i: (i, 0)
    col = lambda j, i: (0, j)
    return pl.pallas_call(
        _merge_kernel,
        grid=(D_MODEL // cb, n // tr),
        in_specs=[pl.BlockSpec((tr, WIDTH), row), pl.BlockSpec((tr, WIDTH), row),
                  pl.BlockSpec((tr, D_MODEL), row),
                  pl.BlockSpec((WIDTH, cb), col), pl.BlockSpec((WIDTH, cb), col),
                  pl.BlockSpec((D_MODEL, cb), col), pl.BlockSpec((D_MODEL, cb), col)],
        out_specs=pl.BlockSpec((tr, cb), lambda j, i: (i, j)),
        out_shape=jax.ShapeDtypeStruct((n, D_MODEL), BF16),
        compiler_params=_params(2),
        name="merge",
    )(a, bm, h, w_a, w_b, w_g0, w_g1)


def _out_proj_kernel(x_ref, m_ref, w_ref, o_ref):
    o_ref[...] = x_ref[...] + _dot(m_ref[...], w_ref[...])


def _out_proj(x, merged, w_out, tr):
    n = x.shape[0]
    row = lambda i: (i, 0)
    return pl.pallas_call(
        _out_proj_kernel,
        grid=(n // tr,),
        in_specs=[pl.BlockSpec((tr, D_MODEL), row), pl.BlockSpec((tr, D_MODEL), row),
                  pl.BlockSpec((D_MODEL, D_MODEL), lambda i: (0, 0))],
        out_specs=pl.BlockSpec((tr, D_MODEL), row),
        out_shape=jax.ShapeDtypeStruct((n, D_MODEL), F32),
        compiler_params=_params(1),
        name="out_proj",
    )(x, merged, w_out)


def _project(x2d, pos_period, tr, wts):
    n = x2d.shape[0]
    n_tab = pos_period.shape[0] // tr
    tabs_head = _rope_tables(pos_period, ROT_DIM, HEAD_DIM)
    tabs_iq = _rope_tables(pos_period, IDX_ROT_DIM, IDX_DIM)
    tabs_iq = tuple(jnp.concatenate([t, t], axis=1) for t in tabs_iq)
    ident = (jnp.ones_like(tabs_iq[0][:, :IDX_DIM]), jnp.zeros_like(tabs_iq[0][:, :IDX_DIM]),
             jnp.zeros_like(tabs_iq[0][:, :IDX_DIM]))
    tabs_misc = tuple(jnp.concatenate([t[:, :IDX_DIM], e], axis=1) for t, e in zip(tabs_iq, ident))

    h = _rmsnorm(x2d, wts["norm_gain"], tr)
    (qa,) = _proj_q(h, wts["w_qa"], wts["q_norm_a"], tabs_head, tr, n_tab, False)
    qb_rot, qb = _proj_q(h, wts["w_qb"], wts["q_norm_b"], tabs_head, tr, n_tab, True)
    akv, cmp_kv, slc, win, akv_b, slc_b, win_b = _proj_kv(
        h, wts["w_kv"], wts["k_norm_a"], wts["k_norm_slc"], wts["k_norm_win"], tabs_head, tr, n_tab)
    iq, misc = _proj_idx(h, wts["w_idx"], tabs_iq, tabs_misc, tr, n_tab)
    sza, szb = _proj_z(h, wts["w_z"], tr)
    return dict(h=h, qa=qa, qb=qb, qb_rot=qb_rot, akv=akv, cmp=cmp_kv, slc=slc, win=win,
                akv_b=akv_b, slc_b=slc_b, win_b=win_b, iq=iq, misc=misc, sza=sza, szb=szb)


def _sample_rows(a, db, ds):
    a = a.reshape(db, ds, KV_HEADS, GROUP, HEAD_DIM).transpose(0, 2, 3, 1, 4)
    a = jnp.pad(a, ((0, 0), (0, 0), (0, 0), (0, SAMPLE_Q - ds), (0, 0)))
    return a.reshape(db, KV_HEADS, QROWS, HEAD_DIM)


def _unsample_rows(o, db, ds):
    o = o.reshape(db, KV_HEADS, GROUP, SAMPLE_Q, HEAD_DIM)[:, :, :, :ds]
    return o.transpose(0, 3, 1, 2, 4).reshape(db * ds, WIDTH)


def _pad_new(a, db, ds):
    return jnp.pad(a.reshape(db, ds, a.shape[-1]), ((0, 0), (0, LANES - ds), (0, 0)))


def kernel(x_prompt, x_sample, cache_a_kv, cache_a_idx, cache_cmp_kv, cache_slc_kv, state_win_kv,
           page_table, norm_gain, w_in, q_norm_a, k_norm_a, q_norm_b, k_norm_cmp, k_norm_slc,
           k_norm_win, pe_cmp, w1_cmp, w2_cmp, w_proj_a, w_proj_b, w_out):
    b, t, _ = x_prompt.shape
    db, ds, _ = x_sample.shape
    n_pages = page_table.shape[1]
    past = n_pages * PAGE_SIZE
    assert ds <= SAMPLE_Q and ds <= CMP_STRIDE

    sizes = (WIDTH, KV_ROW, IDX_COLS, IDX_DIM, IDX_HEADS, WIDTH, WIDTH, 3 * KV_ROW, 3 * N_HEADS, WIDTH,
             2 * D_MODEL)
    offs = [0]
    for s in sizes:
        offs.append(offs[-1] + s)
    wb = w_in.astype(BF16)
    seg = lambda j: wb[:, offs[j]:offs[j + 1]]
    misc_pad = jnp.zeros((D_MODEL, LANES - (IDX_DIM + IDX_HEADS + 3 * N_HEADS)), BF16)
    wts = dict(
        norm_gain=norm_gain, q_norm_a=q_norm_a, q_norm_b=q_norm_b, k_norm_a=k_norm_a,
        k_norm_slc=k_norm_slc, k_norm_win=k_norm_win,
        w_qa=seg(0), w_qb=seg(6),
        w_kv=jnp.concatenate([seg(1), seg(7)], axis=1),
        w_idx=jnp.concatenate([seg(2), seg(3), seg(4), seg(8), misc_pad], axis=1),
        w_z=jnp.concatenate([seg(5), seg(9)], axis=1),
    )
    w_g0 = wb[:, offs[10]:offs[10] + D_MODEL]
    w_g1 = wb[:, offs[10] + D_MODEL:offs[11]]
    w_pa, w_pb, w_o = w_proj_a.astype(BF16), w_proj_b.astype(BF16), w_out.astype(BF16)
    w1_fs = jnp.concatenate([w1_cmp[:, :CMP_STRIDE].reshape(2, CMP_STRIDE * HEAD_DIM, CMP_HIDDEN),
                             w1_cmp[:, CMP_STRIDE:].reshape(2, CMP_STRIDE * HEAD_DIM, CMP_HIDDEN)],
                            axis=2).astype(BF16)
    w1_r = w1_cmp.reshape(2, CMP_LEN * HEAD_DIM, CMP_HIDDEN).astype(BF16)
    pe_r = pe_cmp.reshape(2, CMP_LEN * HEAD_DIM)
    w2_b = w2_cmp.astype(BF16)

    tr = min(512, t)
    tq = min(256, t)
    tk = min(512, t)
    xp = x_prompt.reshape(b * t, D_MODEL)
    pp = _project(xp, jnp.arange(t, dtype=jnp.int32), tr, wts)

    n_chunk_p = t // CMP_STRIDE
    fs_p = _chunk_terms_dense(pp["cmp"].reshape(b * n_chunk_p, CHUNK_FLAT), w1_fs, min(256, b * n_chunk_p))
    kc_p, vc_p = _compress_mlp(fs_p.reshape(b, n_chunk_p, FS_COLS), jnp.zeros((b, SUBLANES, FS_COLS), F32),
                               pe_r, w1_r, w2_b, k_norm_cmp)

    ik_b = pp["misc"][:, :IDX_DIM].astype(BF16)
    ik2 = jnp.concatenate([ik_b, ik_b], axis=1).reshape(b, t, LANES)
    a_p = _dsa_prompt(pp["iq"], pp["misc"], ik2, pp["qa"], pp["akv_b"].reshape(b, t, KV_ROW), pp["sza"],
                      b, t, tq, tk)
    win_front = jnp.pad(pp["win_b"].reshape(b, t, KV_ROW), ((0, 0), (WINDOW, 0), (0, 0)))
    b_p = _nsa_prompt(pp["qb"], pp["qb_rot"], kc_p, vc_p, pp["slc_b"].reshape(b, t, KV_ROW),
                      win_front, pp["misc"], pp["szb"], b, t, tq, tk)
    merged_p = _merge(a_p, b_p, pp["h"], w_pa, w_pb, w_g0, w_g1, tr, 1024)
    y_prompt = _out_proj(xp, merged_p, w_o, tr).reshape(b, t, D_MODEL)

    kv5 = lambda a, n0, n1: a.reshape(n0, n1, KV_HEADS, 2, HEAD_DIM)
    p_a_kv = kv5(pp["akv"], b, t)
    p_a_idx = pp["misc"][:, :IDX_DIM].reshape(b, t, IDX_DIM)
    p_cmp_kv = kv5(pp["cmp"], b, t)
    p_slc_kv = kv5(pp["slc"], b, t)
    p_win_kv = kv5(pp["win"], b, t)[:, t - min(WINDOW, t):]

    ns = db * ds
    xs = x_sample.reshape(ns, D_MODEL)
    pos_s = past + jnp.tile(jnp.arange(ds, dtype=jnp.int32), db)
    ps = _project(xs, pos_s, ns, wts)
    pg = min(PAGES_PER_STEP, n_pages)
    ng = n_pages // pg
    cw = pg * PAGE_SIZE

    iq_s = ps["iq"].reshape(db, ds, IDX_HEADS, IDX_DIM).transpose(0, 2, 1, 3)
    iqm = jnp.pad(iq_s, ((0, 0), (0, 0), (0, SAMPLE_Q - ds), (0, 0))).reshape(db, IDX_HEADS * SAMPLE_Q, IDX_DIM)
    iw_s = ps["misc"][:, MISC_IK:MISC_IW].reshape(db, ds, IDX_HEADS).transpose(0, 2, 1) * (IDX_DIM ** -0.5)
    iwm = jnp.pad(iw_s, ((0, 0), (0, 0), (0, SAMPLE_Q - ds))).reshape(db, IDX_HEADS * SAMPLE_Q, 1)
    iwm = jnp.broadcast_to(iwm, (db, IDX_HEADS * SAMPLE_Q, LANES))
    ik_new_t = jnp.swapaxes(_pad_new(ps["misc"][:, :IDX_DIM].astype(BF16), db, ds), 1, 2)
    sc_s, sc_new = _idx_scores(jnp.swapaxes(cache_a_idx, 1, 2), page_table, iqm, iwm, ik_new_t, pg, ds)
    mk_a, mk_a_new = _dsa_topk(sc_s, sc_new, past, ds)
    o_a_s = _paged_attn(cache_a_kv, page_table, _sample_rows(ps["qa"], db, ds), mk_a, mk_a_new,
                        _pad_new(ps["akv_b"], db, ds), pg)

    fs_past = _chunk_terms_paged(cache_cmp_kv, page_table, w1_fs, pg)
    new_chunk = jnp.pad(ps["cmp"].reshape(db, ds, KV_ROW), ((0, 0), (0, CMP_STRIDE - ds), (0, 0)))
    fs_new = _chunk_terms_dense(new_chunk.reshape(db, CHUNK_FLAT), w1_fs, db)
    fs_new = jnp.pad(fs_new.reshape(db, 1, FS_COLS), ((0, 0), (0, SUBLANES - 1), (0, 0)))
    kc_s, vc_s = _compress_mlp(fs_past, fs_new, pe_r, w1_r, w2_b, k_norm_cmp)
    o_cmp_s, mk_s, mk_s_new = _nsa_sample(_sample_rows(ps["qb"], db, ds), kc_s, vc_s, past, ds, ng, cw)
    q_rot_s = _sample_rows(ps["qb_rot"], db, ds)
    o_slc_s = _paged_attn(cache_slc_kv, page_table, q_rot_s, mk_s, mk_s_new, _pad_new(ps["slc_b"], db, ds), pg)
    o_win_s = _win_sample(q_rot_s, state_win_kv, _pad_new(ps["win_b"], db, ds), ds)

    a_s, b_s = _gate_merge(_unsample_rows(o_a_s, db, ds), _unsample_rows(o_cmp_s, db, ds),
                           _unsample_rows(o_slc_s, db, ds), _unsample_rows(o_win_s, db, ds),
                           ps["misc"], ps["sza"], ps["szb"])
    merged_s = _merge(a_s, b_s, ps["h"], w_pa, w_pb, w_g0, w_g1, ns, 1024)
    y_sample = _out_proj(xs, merged_s, w_o, ns).reshape(db, ds, D_MODEL)

    s_win = kv5(ps["win"], db, ds)
    w_buf = state_win_kv.shape[1]
    s_win_kv = jnp.concatenate([state_win_kv, s_win], axis=1)[:, ds:ds + w_buf]

    return (y_prompt, y_sample, p_a_kv, p_a_idx, p_cmp_kv, p_slc_kv, p_win_kv,
            kv5(ps["akv"], db, ds), ps["misc"][:, :IDX_DIM].reshape(db, ds, IDX_DIM),
            kv5(ps["cmp"], db, ds), kv5(ps["slc"], db, ds), s_win_kv)
```

```python
import functools

import jax
import jax.numpy as jnp
from jax import lax
from jax.experimental import pallas as pl
from jax.experimental.pallas import tpu as pltpu

D_MODEL = 2048
HEAD_DIM = 128
ROT_DIM = HEAD_DIM // 4
N_HEADS = D_MODEL // (2 * HEAD_DIM)
KV_HEADS = 2
GROUP = N_HEADS // KV_HEADS
IDX_HEADS = 8
IDX_DIM = 64
IDX_ROT_DIM = IDX_DIM // 4
A_TOPK = 256
CMP_LEN = 32
CMP_STRIDE = 16
CMP_HIDDEN = 128
SLC_BLOCK = 64
N_SEL = 16
WINDOW = 512
PAGE_SIZE = 128
ROPE_THETA = 500000.0
EPS = 1e-6
NEG = -1e30
WIDTH = N_HEADS * HEAD_DIM
KV_ROW = KV_HEADS * 2 * HEAD_DIM
SCALE = HEAD_DIM ** -0.5

LANES = 128
SUBLANES = 8
VMEM_LIMIT = 48 * 1024 * 1024
PAGES_PER_STEP = 16
SAMPLE_Q = 8

F32 = jnp.float32
BF16 = jnp.bfloat16
NT_DIMS = (((1,), (1,)), ((), ()))


def _params(n_axes):
    return pltpu.CompilerParams(dimension_semantics=("arbitrary",) * n_axes,
                                vmem_limit_bytes=VMEM_LIMIT)


def _dot(a, b):
    return jnp.dot(a, b, preferred_element_type=F32)


def _dot_nt(a, b):
    return lax.dot_general(a, b, NT_DIMS, preferred_element_type=F32)


def _rope_tables(pos, rot, period, width=LANES):
    half = rot // 2
    inv = ROPE_THETA ** (-jnp.arange(half, dtype=F32) / half)
    ang = pos.astype(F32)[:, None] * inv
    cos, sin = jnp.cos(ang), jnp.sin(ang)
    n = pos.shape[0]
    zeros = lambda w: jnp.zeros((n, w), F32)
    ones = lambda w: jnp.ones((n, w), F32)
    c = jnp.concatenate([cos, cos, ones(period - rot)], axis=1)
    s1 = jnp.concatenate([-sin, zeros(period - half)], axis=1)
    s2 = jnp.concatenate([zeros(half), sin, zeros(period - rot)], axis=1)
    return c, s1, s2


def _apply_rope(y, c, s1, s2, half):
    return y * c + pltpu.roll(y, LANES - half, 1) * s1 + pltpu.roll(y, half, 1) * s2


def _head_norm(z, g):
    return z * lax.rsqrt(jnp.mean(z * z, axis=-1, keepdims=True) + EPS) * g


def _rmsnorm_kernel(x_ref, g_ref, o_ref):
    x = x_ref[...]
    y = x * lax.rsqrt(jnp.mean(x * x, axis=-1, keepdims=True) + EPS)
    o_ref[...] = (y * g_ref[...]).astype(o_ref.dtype)


def _rmsnorm(x, gain, tr):
    n, d = x.shape
    return pl.pallas_call(
        _rmsnorm_kernel,
        grid=(n // tr,),
        in_specs=[pl.BlockSpec((tr, d), lambda i: (i, 0)), pl.BlockSpec((1, d), lambda i: (0, 0))],
        out_specs=pl.BlockSpec((tr, d), lambda i: (i, 0)),
        out_shape=jax.ShapeDtypeStruct((n, d), BF16),
        compiler_params=_params(1),
        name="rmsnorm",
    )(x, gain.reshape(1, d))


def _proj_q_kernel(h_ref, w_ref, g_ref, c_ref, s1_ref, s2_ref, rot_ref, *plain_ref):
    z = _dot(h_ref[...], w_ref[...])
    g = g_ref[...]
    c, s1, s2 = c_ref[...], s1_ref[...], s2_ref[...]
    for hd in range(N_HEADS):
        sl = slice(hd * HEAD_DIM, (hd + 1) * HEAD_DIM)
        y = _head_norm(z[:, sl], g)
        if plain_ref:
            plain_ref[0][:, sl] = y.astype(BF16)
        rot_ref[:, sl] = _apply_rope(y, c, s1, s2, ROT_DIM // 2).astype(BF16)


def _proj_q(h, w, gain, tabs, tr, n_tab_blocks, want_plain):
    n = h.shape[0]
    row = lambda i: (i, 0)
    fix = lambda i: (0, 0)
    tab = lambda i: (i % n_tab_blocks, 0)
    out_shape = [jax.ShapeDtypeStruct((n, WIDTH), BF16)] * (2 if want_plain else 1)
    out_specs = [pl.BlockSpec((tr, WIDTH), row)] * (2 if want_plain else 1)
    return pl.pallas_call(
        _proj_q_kernel,
        grid=(n // tr,),
        in_specs=[pl.BlockSpec((tr, D_MODEL), row), pl.BlockSpec((D_MODEL, WIDTH), fix),
                  pl.BlockSpec((1, HEAD_DIM), fix)] + [pl.BlockSpec((tr, LANES), tab)] * 3,
        out_specs=out_specs,
        out_shape=out_shape,
        compiler_params=_params(1),
        name="proj_q",
    )(h, w, gain.reshape(1, HEAD_DIM), *tabs)


def _proj_kv_kernel(h_ref, w_ref, ga_ref, gs_ref, gw_ref, c_ref, s1_ref, s2_ref,
                    akv_ref, cmp_ref, slc_ref, win_ref, akv_b_ref, slc_b_ref, win_b_ref):
    z = _dot(h_ref[...], w_ref[...])
    c, s1, s2 = c_ref[...], s1_ref[...], s2_ref[...]

    def emit(base, gain_ref, f32_ref, bf_ref):
        for j in range(KV_HEADS * 2):
            sl = slice(j * HEAD_DIM, (j + 1) * HEAD_DIM)
            v = z[:, base + j * HEAD_DIM: base + (j + 1) * HEAD_DIM]
            if gain_ref is not None and j % 2 == 0:
                v = _apply_rope(_head_norm(v, gain_ref[...]), c, s1, s2, ROT_DIM // 2)
            f32_ref[pl.ds(j, z.shape[0], stride=KV_HEADS * 2), :] = v
            if bf_ref is not None:
                bf_ref[:, sl] = v.astype(BF16)

    emit(0 * KV_ROW, ga_ref, akv_ref, akv_b_ref)
    emit(1 * KV_ROW, None, cmp_ref, None)
    emit(2 * KV_ROW, gs_ref, slc_ref, slc_b_ref)
    emit(3 * KV_ROW, gw_ref, win_ref, win_b_ref)


def _proj_kv(h, w, g_a, g_slc, g_win, tabs, tr, n_tab_blocks):
    n = h.shape[0]
    row = lambda i: (i, 0)
    fix = lambda i: (0, 0)
    tab = lambda i: (i % n_tab_blocks, 0)
    f32_out = jax.ShapeDtypeStruct((n * KV_HEADS * 2, HEAD_DIM), F32)
    bf_out = jax.ShapeDtypeStruct((n, KV_ROW), BF16)
    return pl.pallas_call(
        _proj_kv_kernel,
        grid=(n // tr,),
        in_specs=[pl.BlockSpec((tr, D_MODEL), row), pl.BlockSpec((D_MODEL, 4 * KV_ROW), fix)]
                 + [pl.BlockSpec((1, HEAD_DIM), fix)] * 3 + [pl.BlockSpec((tr, LANES), tab)] * 3,
        out_specs=[pl.BlockSpec((tr * KV_HEADS * 2, HEAD_DIM), row)] * 4 + [pl.BlockSpec((tr, KV_ROW), row)] * 3,
        out_shape=[f32_out] * 4 + [bf_out] * 3,
        compiler_params=_params(1),
        name="proj_kv",
    )(h, w, g_a.reshape(1, HEAD_DIM), g_slc.reshape(1, HEAD_DIM), g_win.reshape(1, HEAD_DIM), *tabs)


IDX_COLS = IDX_HEADS * IDX_DIM
MISC_IK = IDX_DIM
MISC_IW = MISC_IK + IDX_HEADS
MISC_G = MISC_IW + 3 * N_HEADS


def _proj_idx_kernel(h_ref, w_ref, c_ref, s1_ref, s2_ref, cm_ref, s1m_ref, s2m_ref, iq_ref, misc_ref):
    z = _dot(h_ref[...], w_ref[...])
    half = IDX_ROT_DIM // 2
    c, s1, s2 = c_ref[...], s1_ref[...], s2_ref[...]
    for j in range(IDX_COLS // LANES):
        sl = slice(j * LANES, (j + 1) * LANES)
        iq_ref[:, sl] = _apply_rope(z[:, sl], c, s1, s2, half).astype(BF16)
    m = z[:, IDX_COLS:IDX_COLS + LANES]
    roped = _apply_rope(m, cm_ref[...], s1m_ref[...], s2m_ref[...], half)
    lane = lax.broadcasted_iota(jnp.int32, m.shape, 1)
    misc_ref[...] = jnp.where(lane < MISC_IK, roped,
                              jnp.where(lane < MISC_IW, m * (IDX_HEADS ** -0.5),
                                        jnp.where(lane < MISC_G, jax.nn.sigmoid(m), 0.0)))


def _proj_idx(h, w, tabs_iq, tabs_misc, tr, n_tab_blocks):
    n = h.shape[0]
    row = lambda i: (i, 0)
    fix = lambda i: (0, 0)
    tab = lambda i: (i % n_tab_blocks, 0)
    return pl.pallas_call(
        _proj_idx_kernel,
        grid=(n // tr,),
        in_specs=[pl.BlockSpec((tr, D_MODEL), row), pl.BlockSpec((D_MODEL, IDX_COLS + LANES), fix)]
                 + [pl.BlockSpec((tr, LANES), tab)] * 6,
        out_specs=[pl.BlockSpec((tr, IDX_COLS), row), pl.BlockSpec((tr, LANES), row)],
        out_shape=[jax.ShapeDtypeStruct((n, IDX_COLS), BF16), jax.ShapeDtypeStruct((n, LANES), F32)],
        compiler_params=_params(1),
        name="proj_idx",
    )(h, w, *tabs_iq, *tabs_misc)


def _proj_z_kernel(h_ref, w_ref, za_ref, zb_ref):
    z = _dot(h_ref[...], w_ref[...])
    s = z * jax.nn.sigmoid(z)
    za_ref[...] = s[:, :WIDTH]
    zb_ref[...] = s[:, WIDTH:]


def _proj_z(h, w, tr):
    n = h.shape[0]
    row = lambda i: (i, 0)
    out = jax.ShapeDtypeStruct((n, WIDTH), F32)
    return pl.pallas_call(
        _proj_z_kernel,
        grid=(n // tr,),
        in_specs=[pl.BlockSpec((tr, D_MODEL), row), pl.BlockSpec((D_MODEL, 2 * WIDTH), lambda i: (0, 0))],
        out_specs=[pl.BlockSpec((tr, WIDTH), row)] * 2,
        out_shape=[out, out],
        compiler_params=_params(1),
        name="proj_z",
    )(h, w)


def _key_to_float(u):
    int_min = jnp.int32(-2 ** 31)
    bits = jnp.where(u < 0, u ^ int_min, ~u)
    return lax.bitcast_convert_type(bits, F32)


def _lane_fold(w):
    acc = w[:, :LANES]
    for j in range(1, w.shape[1] // LANES):
        acc = acc + w[:, j * LANES:(j + 1) * LANES]
    return acc


COUNT_ROWS = 1024


def _topk_threshold(load, nck, k, idx_bits, rows, cw, tail=None):
    kf = jnp.float32(k)
    lane = lax.broadcasted_iota(jnp.int32, (1, cw), 1)
    nr = min(rows, COUNT_ROWS)

    def count(pred_value):
        out = []
        for r0 in range(0, rows, nr):
            rs = slice(r0, r0 + nr)

            def body(c, acc, r0=r0, rs=rs):
                return acc + _lane_fold(pred_value(load(c, r0, nr), c * cw + lane, rs))
            acc = lax.fori_loop(0, nck, body, jnp.zeros((nr, LANES), F32))
            if tail is not None:
                acc = acc + pred_value(tail[0][rs], tail[1], rs)
            out.append(jnp.sum(acc, axis=1, keepdims=True))
        return out[0] if len(out) == 1 else jnp.concatenate(out, axis=0)

    def value_bit(it, u):
        cand = u | jnp.left_shift(jnp.int32(1), 31 - it)
        candf = _key_to_float(cand)
        cnt = count(lambda x, idx, rs: jnp.where(x >= candf[rs], 1.0, 0.0))
        return jnp.where(cnt >= kf, cand, u)

    u = lax.fori_loop(0, 32, value_bit, jnp.zeros((rows, 1), jnp.int32))
    thr = jnp.where((u >= 0) & (u < 2 ** 23), -jnp.inf, _key_to_float(u))
    need = kf - count(lambda x, idx, rs: jnp.where(x > thr[rs], 1.0, 0.0))

    def index_bit(it, v):
        cand = v | jnp.left_shift(jnp.int32(1), idx_bits - 1 - it)
        below = count(lambda x, idx, rs: jnp.where(x == thr[rs], jnp.where(idx < cand[rs], 1.0, 0.0), 0.0))
        return jnp.where(below < need, cand, v)

    tied = jnp.max(count(lambda x, idx, rs: jnp.where(x >= thr[rs], 1.0, 0.0))) > kf
    v = lax.cond(tied,
                 lambda: lax.fori_loop(0, idx_bits, index_bit, jnp.zeros((rows, 1), jnp.int32)),
                 lambda: jnp.full((rows, 1), (1 << idx_bits) - 1, jnp.int32))
    return thr, jnp.where(need >= 1.0, v, -1)


def _topk_mask(x, idx, thr, vmax):
    gt = jnp.where(x > thr, 1.0, 0.0)
    eq = jnp.where(x == thr, jnp.where(idx <= vmax, 1.0, 0.0), 0.0)
    return jnp.maximum(gt, eq)


LOG2E = 1.4426950408889634
SCORE_SCALE = SCALE * LOG2E


def _softmax_init(rows):
    return (jnp.full((rows, 1), NEG, F32), jnp.zeros((rows, 1), F32), jnp.zeros((rows, HEAD_DIM), F32))


def _softmax_step(carry, q, k, v, bias):
    m, l, acc = carry
    t = _dot_nt(q, k) * SCORE_SCALE + bias
    m_new = jnp.maximum(m, jnp.max(t, axis=1, keepdims=True))
    alpha = jnp.exp2(m - m_new)
    e = jnp.exp2(t - m_new)
    l = alpha * l + jnp.sum(e, axis=1, keepdims=True)
    acc = alpha * acc + _dot(e.astype(BF16), v)
    return m_new, l, acc


def _softmax_done(carry):
    _, l, acc = carry
    return acc / jnp.maximum(l, 1e-30)


def _mask_bias(maskf):
    return jnp.where(maskf > 0.0, 0.0, NEG)


def _stack_heads(ref, kvh):
    return jnp.concatenate([ref[:, (kvh * GROUP + g) * HEAD_DIM:(kvh * GROUP + g + 1) * HEAD_DIM]
                            for g in range(GROUP)], axis=0)


def _kv_cols(kvh):
    return (slice(kvh * 2 * HEAD_DIM, (kvh * 2 + 1) * HEAD_DIM),
            slice((kvh * 2 + 1) * HEAD_DIM, (kvh * 2 + 2) * HEAD_DIM))


def _dsa_prompt_kernel(iq_ref, misc_ref, ik2_ref, qa_ref, akv_ref, sza_ref, o_ref, sc_ref, mk_ref,
                       *, tq, tk, k_top, idx_bits):
    i = pl.program_id(1)
    nck = ((i + 1) * tq + tk - 1) // tk
    qpos = i * tq + lax.broadcasted_iota(jnp.int32, (tq, 1), 0)
    lane = lax.broadcasted_iota(jnp.int32, (1, LANES), 1)
    kiota = lax.broadcasted_iota(jnp.int32, (1, tk), 1)
    misc = misc_ref[...]

    iqs, iws = [], []
    for hd in range(IDX_HEADS):
        pair = iq_ref[:, (hd // 2) * LANES:(hd // 2 + 1) * LANES]
        keep = (lane < IDX_DIM) if hd % 2 == 0 else (lane >= IDX_DIM)
        iqs.append(jnp.where(keep, pair, jnp.zeros_like(pair)))
        iws.append(misc[:, MISC_IK + hd:MISC_IK + hd + 1] * (IDX_DIM ** -0.5))

    def score_chunk(c, carry):
        k0 = pl.multiple_of(c * tk, tk)
        ikc = ik2_ref[pl.ds(k0, tk), :]
        acc = jnp.zeros((tq, tk), F32)
        for hd in range(IDX_HEADS):
            acc = acc + jnp.maximum(_dot_nt(iqs[hd], ikc), 0.0) * iws[hd]
        sc_ref[c] = jnp.where(k0 + kiota <= qpos, acc, NEG)
        return carry

    lax.fori_loop(0, nck, score_chunk, 0)

    thr, vmax = _topk_threshold(lambda c, r0, nr: sc_ref[c, r0:r0 + nr, :], nck, k_top, idx_bits, tq, tk)

    def mask_chunk(c, carry):
        idx = c * tk + kiota
        sel = _topk_mask(sc_ref[c], idx, thr, vmax) * jnp.where(idx <= qpos, 1.0, 0.0)
        mk_ref[c] = _mask_bias(sel)
        return carry

    lax.fori_loop(0, nck, mask_chunk, 0)

    qs = [_stack_heads(qa_ref, kvh) for kvh in range(KV_HEADS)]

    def attend(c, carries):
        k0 = pl.multiple_of(c * tk, tk)
        bias = jnp.concatenate([mk_ref[c]] * GROUP, axis=0)
        out = []
        for kvh in range(KV_HEADS):
            kcol, vcol = _kv_cols(kvh)
            out.append(_softmax_step(carries[kvh], qs[kvh], akv_ref[pl.ds(k0, tk), kcol],
                                     akv_ref[pl.ds(k0, tk), vcol], bias))
        return tuple(out)

    carries = lax.fori_loop(0, nck, attend, tuple(_softmax_init(GROUP * tq) for _ in range(KV_HEADS)))
    for kvh in range(KV_HEADS):
        o = _softmax_done(carries[kvh])
        for g in range(GROUP):
            sl = slice((kvh * GROUP + g) * HEAD_DIM, (kvh * GROUP + g + 1) * HEAD_DIM)
            o_ref[:, sl] = (o[g * tq:(g + 1) * tq] * sza_ref[:, sl]).astype(BF16)


def _dsa_prompt(iq, misc, ik2, qa, akv_b, sza, b, t, tq, tk):
    nqb = t // tq
    k_top = min(A_TOPK, t // 4)
    idx_bits = max(1, (t - 1).bit_length())
    row = lambda bi, i: (bi * nqb + i, 0)
    whole = lambda bi, i: (bi, 0, 0)
    kern = functools.partial(_dsa_prompt_kernel, tq=tq, tk=tk, k_top=k_top, idx_bits=idx_bits)
    return pl.pallas_call(
        kern,
        grid=(b, nqb),
        in_specs=[pl.BlockSpec((tq, IDX_COLS), row), pl.BlockSpec((tq, LANES), row),
                  pl.BlockSpec((None, t, LANES), whole), pl.BlockSpec((tq, WIDTH), row),
                  pl.BlockSpec((None, t, KV_ROW), whole), pl.BlockSpec((tq, WIDTH), row)],
        out_specs=pl.BlockSpec((tq, WIDTH), row),
        out_shape=jax.ShapeDtypeStruct((b * t, WIDTH), BF16),
        scratch_shapes=[pltpu.VMEM((t // tk, tq, tk), F32), pltpu.VMEM((t // tk, tq, tk), F32)],
        compiler_params=_params(2),
        name="dsa_prompt",
    )(iq, misc, ik2, qa, akv_b, sza)


SLC_SHIFT = SLC_BLOCK.bit_length() - 1
assert 1 << SLC_SHIFT == SLC_BLOCK


def _block_rules(imp, blk, qpos):
    cur = lax.shift_right_logical(qpos, SLC_SHIFT)
    return jnp.where(blk == cur, -NEG, jnp.where(blk * SLC_BLOCK > qpos, NEG, imp))


def _split_dot(p, m):
    hi = p.astype(BF16)
    lo = (p - hi.astype(F32)).astype(BF16)
    return _dot(hi, m) + _dot(lo, m)


def _nsa_prompt_kernel(qb_ref, qr_ref, kc_ref, vc_ref, slc_ref, win_ref, ov_ref, misc_ref, szb_ref,
                       o_ref, sc_ref, *, tq, tk, n_cmp, n_blk, n_sel):
    i = pl.program_id(1)
    nck = ((i + 1) * tq + tk - 1) // tk
    qpos = i * tq + lax.broadcasted_iota(jnp.int32, (tq, 1), 0)
    kiota = lax.broadcasted_iota(jnp.int32, (1, tk), 1)
    misc = misc_ref[...]
    ncp = kc_ref.shape[0]
    cidx = lax.broadcasted_iota(jnp.int32, (1, ncp), 1)
    cmask = jnp.where((CMP_STRIDE * cidx + CMP_LEN - 1 <= qpos) & (cidx < n_cmp), 1.0, 0.0)
    cmask = jnp.concatenate([cmask] * GROUP, axis=0)
    blk = lax.broadcasted_iota(jnp.int32, (1, LANES), 1)

    o_cmp = []
    for kvh in range(KV_HEADS):
        kc = kc_ref[:, kvh * HEAD_DIM:(kvh + 1) * HEAD_DIM]
        vc = vc_ref[:, kvh * HEAD_DIM:(kvh + 1) * HEAD_DIM]
        s = _dot_nt(_stack_heads(qb_ref, kvh), kc) * SCALE
        sm = jnp.where(cmask > 0.0, s, NEG)
        e = jnp.exp(sm - jnp.max(sm, axis=1, keepdims=True)) * cmask
        p = e / jnp.maximum(jnp.sum(e, axis=1, keepdims=True), 1e-30)
        o_cmp.append(_dot(p.astype(BF16), vc))
        psum = p[0:tq]
        for g in range(1, GROUP):
            psum = psum + p[g * tq:(g + 1) * tq]
        imp = _split_dot(psum, ov_ref[...])
        imp = _block_rules(imp, blk, qpos)
        sc_ref[kvh] = jnp.where(blk < n_blk, imp, -jnp.inf)

    rows = KV_HEADS * tq
    load = lambda c, r0, nr: sc_ref[...].reshape(rows, LANES)[r0:r0 + nr]
    thr, vmax = _topk_threshold(load, 1, n_sel, 7, rows, LANES)
    bmask = _topk_mask(load(0, 0, rows), blk, thr, vmax).astype(BF16)

    riota = lax.broadcasted_iota(jnp.int32, (LANES, 1), 0)
    nwk = WINDOW + tq
    w0 = pl.multiple_of(i * tq, tq)
    kposw = i * tq - WINDOW + lax.broadcasted_iota(jnp.int32, (1, nwk), 1)
    dw = qpos - kposw
    bias_w = jnp.where((dw >= 0) & (dw < WINDOW) & (kposw >= 0), 0.0, NEG)
    bias_w = jnp.concatenate([bias_w] * GROUP, axis=0)

    qs = [_stack_heads(qr_ref, kvh) for kvh in range(KV_HEADS)]

    def attend(c, carries):
        k0 = pl.multiple_of(c * tk, tk)
        kpos = k0 + kiota
        expand = jnp.where(riota == lax.shift_right_logical(kpos, SLC_SHIFT), 1.0, 0.0).astype(BF16)
        causal = jnp.where(kpos <= qpos, 1.0, 0.0)
        out = []
        for kvh in range(KV_HEADS):
            kcol, vcol = _kv_cols(kvh)
            sel = _dot(bmask[kvh * tq:(kvh + 1) * tq], expand) * causal
            bias = jnp.concatenate([_mask_bias(sel)] * GROUP, axis=0)
            out.append(_softmax_step(carries[kvh], qs[kvh], slc_ref[pl.ds(k0, tk), kcol],
                                     slc_ref[pl.ds(k0, tk), vcol], bias))
        return tuple(out)

    slc_carries = lax.fori_loop(0, nck, attend, tuple(_softmax_init(GROUP * tq) for _ in range(KV_HEADS)))

    for kvh in range(KV_HEADS):
        q = qs[kvh]
        kcol, vcol = _kv_cols(kvh)
        o_slc = _softmax_done(slc_carries[kvh])
        o_win = _softmax_done(_softmax_step(_softmax_init(GROUP * tq), q, win_ref[pl.ds(w0, nwk), kcol],
                                            win_ref[pl.ds(w0, nwk), vcol], bias_w))
        for g in range(GROUP):
            hd = kvh * GROUP + g
            sl = slice(hd * HEAD_DIM, (hd + 1) * HEAD_DIM)
            rs = slice(g * tq, (g + 1) * tq)
            gate = [misc[:, MISC_IW + j * N_HEADS + hd:MISC_IW + j * N_HEADS + hd + 1] for j in range(3)]
            o_b = gate[0] * o_cmp[kvh][rs] + gate[1] * o_slc[rs] + gate[2] * o_win[rs]
            o_ref[:, sl] = (o_b * szb_ref[:, sl]).astype(BF16)


def _overlap_matrix(n_cmp_pad, n_cols, col_block):
    i = jnp.arange(n_cmp_pad)[:, None]
    j = col_block[None, :]
    ov = (CMP_STRIDE * i < SLC_BLOCK * (j + 1)) & (CMP_STRIDE * i + CMP_LEN > SLC_BLOCK * j) & (j >= 0)
    return ov.astype(BF16)


def _nsa_prompt(qb, qr, kc, vc, slc_b, win_b, misc, szb, b, t, tq, tk):
    nqb = t // tq
    n_cmp = t // CMP_STRIDE - 1
    n_blk = t // SLC_BLOCK
    assert n_blk <= LANES
    ncp = kc.shape[1]
    ov = _overlap_matrix(ncp, LANES, jnp.where(jnp.arange(LANES) < n_blk, jnp.arange(LANES), -1))
    row = lambda bi, i: (bi * nqb + i, 0)
    whole = lambda bi, i: (bi, 0, 0)
    kern = functools.partial(_nsa_prompt_kernel, tq=tq, tk=tk, n_cmp=n_cmp, n_blk=n_blk,
                             n_sel=min(N_SEL, n_blk))
    return pl.pallas_call(
        kern,
        grid=(b, nqb),
        in_specs=[pl.BlockSpec((tq, WIDTH), row), pl.BlockSpec((tq, WIDTH), row),
                  pl.BlockSpec((None, ncp, KV_HEADS * HEAD_DIM), whole),
                  pl.BlockSpec((None, ncp, KV_HEADS * HEAD_DIM), whole),
                  pl.BlockSpec((None, t, KV_ROW), whole), pl.BlockSpec((None, t + WINDOW, KV_ROW), whole),
                  pl.BlockSpec((ncp, LANES), lambda bi, i: (0, 0)),
                  pl.BlockSpec((tq, LANES), row), pl.BlockSpec((tq, WIDTH), row)],
        out_specs=pl.BlockSpec((tq, WIDTH), row),
        out_shape=jax.ShapeDtypeStruct((b * t, WIDTH), BF16),
        scratch_shapes=[pltpu.VMEM((KV_HEADS, tq, LANES), F32)],
        compiler_params=_params(2),
        name="nsa_prompt",
    )(qb, qr, kc, vc, slc_b, win_b, ov, misc, szb)


CHUNK_FLAT = CMP_STRIDE * KV_ROW
FS_COLS = KV_HEADS * 2 * 2 * CMP_HIDDEN


def _chunk_terms_body(x, w_ref, o_ref):
    for kc in range(KV_HEADS * 2):
        xk = jnp.concatenate(
            [x[:, p * KV_ROW + kc * HEAD_DIM: p * KV_ROW + (kc + 1) * HEAD_DIM] for p in range(CMP_STRIDE)],
            axis=1).astype(BF16)
        o_ref[:, kc * 2 * CMP_HIDDEN:(kc + 1) * 2 * CMP_HIDDEN] = _dot(xk, w_ref[kc % 2])


def _chunk_terms_dense_kernel(x_ref, w_ref, o_ref):
    _chunk_terms_body(x_ref[...], w_ref, o_ref)


def _chunk_terms_dense(x, w, tr):
    n = x.shape[0]
    return pl.pallas_call(
        _chunk_terms_dense_kernel,
        grid=(n // tr,),
        in_specs=[pl.BlockSpec((tr, CHUNK_FLAT), lambda i: (i, 0)),
                  pl.BlockSpec((2, CMP_STRIDE * HEAD_DIM, 2 * CMP_HIDDEN), lambda i: (0, 0, 0))],
        out_specs=pl.BlockSpec((tr, FS_COLS), lambda i: (i, 0)),
        out_shape=jax.ShapeDtypeStruct((n, FS_COLS), F32),
        compiler_params=_params(1),
        name="chunk_terms_dense",
    )(x, w)


ROWS_PER_TOKEN = KV_HEADS * 2
PAGE_ROWS = PAGE_SIZE * ROWS_PER_TOKEN
CHUNKS_PER_PAGE = PAGE_SIZE // CMP_STRIDE


def _cache_rows(cache):
    return cache.reshape(-1, HEAD_DIM)


def _token_rows(ref, slot, n_tokens):
    return ref[pl.ds(slot, n_tokens, stride=ROWS_PER_TOKEN), :]


def _sublane_transpose(tiles):
    tiles = list(tiles)
    sub = lax.broadcasted_iota(jnp.int32, (SUBLANES, LANES), 0)
    for d in (4, 2, 1):
        keep = (sub & d) == 0
        for i in range(SUBLANES):
            if i & d:
                continue
            lo, hi = tiles[i], tiles[i + d]
            tiles[i] = jnp.where(keep, lo, pltpu.roll(hi, d, 0))
            tiles[i + d] = jnp.where(keep, pltpu.roll(lo, SUBLANES - d, 0), hi)
    return tiles


def _chunk_terms_paged_kernel(pt_ref, *refs, pg):
    pages, w_ref, o_ref = refs[:pg], refs[pg], refs[pg + 1]
    chunk_rows = CMP_STRIDE * ROWS_PER_TOKEN
    assert CHUNKS_PER_PAGE == SUBLANES and chunk_rows % SUBLANES == 0
    pieces = [[] for _ in range(chunk_rows)]
    for pr in pages:
        for t in range(chunk_rows // SUBLANES):
            tiles = [pr[n * chunk_rows + t * SUBLANES:n * chunk_rows + (t + 1) * SUBLANES, :]
                     for n in range(CHUNKS_PER_PAGE)]
            for s, tile in enumerate(_sublane_transpose(tiles)):
                pieces[t * SUBLANES + s].append(tile)
    for kc in range(ROWS_PER_TOKEN):
        xk = jnp.concatenate([jnp.concatenate(pieces[p * ROWS_PER_TOKEN + kc], axis=0)
                              for p in range(CMP_STRIDE)], axis=1).astype(BF16)
        o_ref[:, kc * 2 * CMP_HIDDEN:(kc + 1) * 2 * CMP_HIDDEN] = _dot(xk, w_ref[kc % 2])


def _page_specs(block, pg):
    tail = (0,) * (len(block) - 1)

    def spec(j):
        return pl.BlockSpec(block, lambda bi, g, pt: (pt[bi, g * pg + j],) + tail)
    return [spec(j) for j in range(pg)]


def _chunk_terms_paged(pool, page_table, w, pg):
    db, n_pages = page_table.shape
    cpp = CHUNKS_PER_PAGE
    pool_v = _cache_rows(pool)
    kern = functools.partial(_chunk_terms_paged_kernel, pg=pg)
    grid_spec = pltpu.PrefetchScalarGridSpec(
        num_scalar_prefetch=1,
        grid=(db, n_pages // pg),
        in_specs=_page_specs((PAGE_ROWS, HEAD_DIM), pg)
                 + [pl.BlockSpec((2, CMP_STRIDE * HEAD_DIM, 2 * CMP_HIDDEN), lambda bi, g, pt: (0, 0, 0))],
        out_specs=pl.BlockSpec((None, pg * cpp, FS_COLS), lambda bi, g, pt: (bi, g, 0)),
    )
    return pl.pallas_call(
        kern, grid_spec=grid_spec,
        out_shape=jax.ShapeDtypeStruct((db, n_pages * cpp, FS_COLS), F32),
        compiler_params=_params(2),
        name="chunk_terms_paged",
    )(page_table, *([pool_v] * pg), w)


def _compress_mlp_kernel(fs_ref, new_ref, pe_ref, w1_ref, w2_ref, g_ref, kc_ref, vc_ref):
    n = fs_ref.shape[0]
    last = lax.broadcasted_iota(jnp.int32, (n, 1), 0) == n - 1
    for kc in range(KV_HEADS * 2):
        kvh, c = kc // 2, kc % 2
        base = kc * 2 * CMP_HIDDEN
        first = fs_ref[:, base:base + CMP_HIDDEN]
        second = fs_ref[:, base + CMP_HIDDEN:base + 2 * CMP_HIDDEN]
        shifted = jnp.where(last, new_ref[0:1, base + CMP_HIDDEN:base + 2 * CMP_HIDDEN],
                            pltpu.roll(second, n - 1, 0))
        pe = jnp.broadcast_to(pe_ref[c:c + 1, :], (SUBLANES, pe_ref.shape[1])).astype(BF16)
        bias = _dot(pe, w1_ref[c])[0:1, :]
        pre = first + shifted + bias
        hid = pre * jax.nn.sigmoid(pre)
        out = _dot(hid.astype(BF16), w2_ref[c])
        sl = slice(kvh * HEAD_DIM, (kvh + 1) * HEAD_DIM)
        if c == 0:
            kc_ref[:, sl] = _head_norm(out, g_ref[...]).astype(BF16)
        else:
            vc_ref[:, sl] = out.astype(BF16)


def _compress_mlp(fs, fs_new, pe, w1r, w2, g_cmp):
    nb, n, _ = fs.shape
    whole = lambda bi: (bi, 0, 0)
    fix2 = lambda bi: (0, 0)
    fix3 = lambda bi: (0, 0, 0)
    out = jax.ShapeDtypeStruct((nb, n, KV_HEADS * HEAD_DIM), BF16)
    return pl.pallas_call(
        _compress_mlp_kernel,
        grid=(nb,),
        in_specs=[pl.BlockSpec((None, n, FS_COLS), whole), pl.BlockSpec((None, SUBLANES, FS_COLS), whole),
                  pl.BlockSpec((2, CMP_LEN * HEAD_DIM), fix2),
                  pl.BlockSpec((2, CMP_LEN * HEAD_DIM, CMP_HIDDEN), fix3),
                  pl.BlockSpec((2, CMP_HIDDEN, HEAD_DIM), fix3), pl.BlockSpec((1, HEAD_DIM), fix2)],
        out_specs=[pl.BlockSpec((None, n, KV_HEADS * HEAD_DIM), whole)] * 2,
        out_shape=[out, out],
        compiler_params=_params(1),
        name="compress_mlp",
    )(fs, fs_new, pe, w1r, w2, g_cmp.reshape(1, HEAD_DIM))


def _idx_scores_kernel(pt_ref, *refs, pg, n_new):
    pages = refs[:pg]
    iq_ref, iw_ref, new_ref, o_ref, onew_ref = refs[pg:pg + 5]
    g = pl.program_id(1)
    iq = iq_ref[...]
    iw = iw_ref[...]
    row = lax.broadcasted_iota(jnp.int32, (SAMPLE_Q, 1), 0)

    def scores(keys_t, first_idx):
        s = jnp.maximum(_dot(iq, keys_t), 0.0)
        n = s.shape[1]
        w = s * jnp.concatenate([iw] * (n // LANES), axis=1)
        acc = w[0:SAMPLE_Q]
        for hd in range(1, IDX_HEADS):
            acc = acc + w[hd * SAMPLE_Q:(hd + 1) * SAMPLE_Q]
        ramp = -(first_idx + lax.broadcasted_iota(jnp.int32, (1, n), 1)).astype(F32)
        return jnp.where(row < n_new, acc, ramp)

    cw = pg * PAGE_SIZE
    o_ref[...] = scores(jnp.concatenate([p[...] for p in pages], axis=1).astype(BF16), g * cw)

    @pl.when(g == pl.num_programs(1) - 1)
    def _():
        onew_ref[...] = scores(new_ref[...], pl.num_programs(1) * cw)


def _idx_scores(pool_idx_t, page_table, iqm, iwm, ik_new_t, pg, n_new):
    db, n_pages = page_table.shape
    ng = n_pages // pg
    kern = functools.partial(_idx_scores_kernel, pg=pg, n_new=n_new)
    per_b = lambda bi, g, pt: (bi, 0, 0)
    grid_spec = pltpu.PrefetchScalarGridSpec(
        num_scalar_prefetch=1,
        grid=(db, ng),
        in_specs=_page_specs((None, IDX_DIM, PAGE_SIZE), pg)
                 + [pl.BlockSpec((None, IDX_HEADS * SAMPLE_Q, IDX_DIM), per_b),
                    pl.BlockSpec((None, IDX_HEADS * SAMPLE_Q, LANES), per_b),
                    pl.BlockSpec((None, IDX_DIM, LANES), per_b)],
        out_specs=[pl.BlockSpec((None, SAMPLE_Q, pg * PAGE_SIZE), lambda bi, g, pt: (g, bi, 0)),
                   pl.BlockSpec((SAMPLE_Q, LANES), lambda bi, g, pt: (bi, 0))],
    )
    return pl.pallas_call(
        kern, grid_spec=grid_spec,
        out_shape=[jax.ShapeDtypeStruct((ng, db * SAMPLE_Q, pg * PAGE_SIZE), F32),
                   jax.ShapeDtypeStruct((db * SAMPLE_Q, LANES), F32)],
        compiler_params=_params(2),
        name="idx_scores",
    )(page_table, *([pool_idx_t] * pg), iqm, iwm, ik_new_t)


TOPK_ROWS = 64


def _dsa_topk_kernel(sc_ref, new_ref, o_ref, onew_ref, *, ng, cw, n_new, k_top, idx_bits, rb):
    row = lax.broadcasted_iota(jnp.int32, (rb, 1), 0)
    q = row & (SAMPLE_Q - 1)
    lane = lax.broadcasted_iota(jnp.int32, (1, LANES), 1)
    lane_cw = lax.broadcasted_iota(jnp.int32, (1, cw), 1)
    ok_new = (lane <= q) & (lane < n_new)
    x_new = jnp.where(ok_new, new_ref[...], NEG)
    idx_new = ng * cw + lane

    thr, vmax = _topk_threshold(lambda c, r0, nr: sc_ref[c, r0:r0 + nr, :], ng, k_top, idx_bits, rb, cw,
                                tail=(x_new, idx_new))
    for g in range(ng):
        o_ref[g] = _mask_bias(_topk_mask(sc_ref[g], g * cw + lane_cw, thr, vmax))
    onew_ref[...] = _mask_bias(_topk_mask(x_new, idx_new, thr, vmax) * jnp.where(ok_new, 1.0, 0.0))


def _dsa_topk(scores, scores_new, past, n_new):
    ng, rows, cw = scores.shape
    rb = min(TOPK_ROWS, rows)
    total = past + n_new
    kern = functools.partial(_dsa_topk_kernel, ng=ng, cw=cw, n_new=n_new, k_top=min(A_TOPK, total // 4),
                             idx_bits=(ng * cw + LANES - 1).bit_length(), rb=rb)
    big = pl.BlockSpec((ng, rb, cw), lambda r: (0, r, 0))
    small = pl.BlockSpec((rb, LANES), lambda r: (r, 0))
    return pl.pallas_call(
        kern,
        grid=(rows // rb,),
        in_specs=[big, small],
        out_specs=[big, small],
        out_shape=[jax.ShapeDtypeStruct((ng, rows, cw), F32), jax.ShapeDtypeStruct((rows, LANES), F32)],
        compiler_params=_params(1),
        name="dsa_topk",
    )(scores, scores_new)


QROWS = GROUP * SAMPLE_Q


def _paged_attn_kernel(pt_ref, *refs, pg, mask_rows):
    pages = refs[:pg]
    q_ref, mk_ref, mknew_ref, new_ref, o_ref, m_ref, l_ref, acc_ref = refs[pg:pg + 8]
    g = pl.program_id(1)

    @pl.when(g == 0)
    def _():
        m_ref[...] = jnp.full(m_ref.shape, NEG, F32)
        l_ref[...] = jnp.zeros(l_ref.shape, F32)
        acc_ref[...] = jnp.zeros(acc_ref.shape, F32)

    def update(keys, values, bias_all):
        carries = [(m_ref[kvh], l_ref[kvh], acc_ref[kvh]) for kvh in range(KV_HEADS)]
        for kvh in range(KV_HEADS):
            r0 = kvh * SAMPLE_Q if mask_rows == KV_HEADS * SAMPLE_Q else 0
            bias = jnp.concatenate([bias_all[r0:r0 + SAMPLE_Q]] * GROUP, axis=0)
            carries[kvh] = _softmax_step(carries[kvh], q_ref[kvh], keys(kvh), values(kvh), bias)
        for kvh in range(KV_HEADS):
            m_ref[kvh], l_ref[kvh], acc_ref[kvh] = carries[kvh]

    def paged(slot):
        return jnp.concatenate([_token_rows(p, slot, PAGE_SIZE) for p in pages], axis=0).astype(BF16)

    update(lambda kvh: paged(2 * kvh), lambda kvh: paged(2 * kvh + 1), mk_ref[...])

    @pl.when(g == pl.num_programs(1) - 1)
    def _():
        update(lambda kvh: new_ref[:, _kv_cols(kvh)[0]], lambda kvh: new_ref[:, _kv_cols(kvh)[1]], mknew_ref[...])
        for kvh in range(KV_HEADS):
            o_ref[kvh] = _softmax_done((m_ref[kvh], l_ref[kvh], acc_ref[kvh]))


def _paged_attn(pool, page_table, q, bias, bias_new, kv_new, pg):
    db, n_pages = page_table.shape
    ng = n_pages // pg
    mask_rows = bias.shape[1] // db
    pool_v = _cache_rows(pool)
    kern = functools.partial(_paged_attn_kernel, pg=pg, mask_rows=mask_rows)
    per_b3 = lambda bi, g, pt: (bi, 0, 0)
    per_b4 = lambda bi, g, pt: (bi, 0, 0, 0)
    grid_spec = pltpu.PrefetchScalarGridSpec(
        num_scalar_prefetch=1,
        grid=(db, ng),
        in_specs=_page_specs((PAGE_ROWS, HEAD_DIM), pg)
                 + [pl.BlockSpec((None, KV_HEADS, QROWS, HEAD_DIM), per_b4),
                    pl.BlockSpec((None, mask_rows, pg * PAGE_SIZE), lambda bi, g, pt: (g, bi, 0)),
                    pl.BlockSpec((mask_rows, LANES), lambda bi, g, pt: (bi, 0)),
                    pl.BlockSpec((None, LANES, KV_ROW), per_b3)],
        out_specs=pl.BlockSpec((None, KV_HEADS, QROWS, HEAD_DIM), per_b4),
        scratch_shapes=[pltpu.VMEM((KV_HEADS, QROWS, 1), F32), pltpu.VMEM((KV_HEADS, QROWS, 1), F32),
                        pltpu.VMEM((KV_HEADS, QROWS, HEAD_DIM), F32)],
    )
    return pl.pallas_call(
        kern, grid_spec=grid_spec,
        out_shape=jax.ShapeDtypeStruct((db, KV_HEADS, QROWS, HEAD_DIM), F32),
        compiler_params=_params(2),
        name="paged_attn",
    )(page_table, *([pool_v] * pg), q, bias, bias_new, kv_new)


BLOCKS_PER_PAGE = PAGE_SIZE // SLC_BLOCK
BLOCK_ROWS = SLC_BLOCK * ROWS_PER_TOKEN
assert BLOCKS_PER_PAGE == 2


def _slc_gather_kernel(pt_ref, sel_ref, *refs, nsel, ds):
    nblk = KV_HEADS * nsel
    blocks = refs[:nblk]
    q_ref, bnew_ref, new_ref, o_ref = refs[nblk:nblk + 4]
    bi, qi = pl.program_id(0), pl.program_id(1)
    lane = lax.broadcasted_iota(jnp.int32, (1, LANES), 1)
    mine = (lax.broadcasted_iota(jnp.int32, (QROWS, 1), 0) & (SAMPLE_Q - 1)) == qi

    @pl.when(qi == 0)
    def _():
        o_ref[...] = jnp.zeros(o_ref.shape, F32)

    for kvh in range(KV_HEADS):
        base = ((bi * ds + qi) * KV_HEADS + kvh) * nsel
        kcol, vcol = _kv_cols(kvh)
        keys = jnp.concatenate([_token_rows(blocks[kvh * nsel + j], 2 * kvh, SLC_BLOCK)
                                for j in range(nsel)], axis=0).astype(BF16)
        vals = jnp.concatenate([_token_rows(blocks[kvh * nsel + j], 2 * kvh + 1, SLC_BLOCK)
                                for j in range(nsel)], axis=0).astype(BF16)
        slot_bias = [jnp.where(sel_ref[base + j] >= 0, 0.0, NEG) for j in range(nsel)]
        pairs = [jnp.where(lane < SLC_BLOCK, slot_bias[j], slot_bias[min(j + 1, nsel - 1)])
                 for j in range(0, nsel, 2)]
        bias = jnp.concatenate(pairs, axis=1)[:, :nsel * SLC_BLOCK]
        carry = _softmax_step(_softmax_init(QROWS), q_ref[kvh], keys, vals, bias)
        bnew = jnp.concatenate([bnew_ref[kvh * SAMPLE_Q:(kvh + 1) * SAMPLE_Q]] * GROUP, axis=0)
        carry = _softmax_step(carry, q_ref[kvh], new_ref[:, kcol], new_ref[:, vcol], bnew)
        o_ref[kvh] = jnp.where(mine, _softmax_done(carry), o_ref[kvh])


def _slc_gather(pool, page_table, sel, q, bias_new, kv_new, nsel, ds):
    db = page_table.shape[0]
    pool_v = _cache_rows(pool)

    def spec(kvh, j):
        def imap(bi, qi, pt, sl):
            blk = jnp.maximum(sl[((bi * ds + qi) * KV_HEADS + kvh) * nsel + j], 0)
            return (pt[bi, lax.shift_right_logical(blk, 1)] * BLOCKS_PER_PAGE + (blk & 1), 0)
        return pl.BlockSpec((BLOCK_ROWS, HEAD_DIM), imap)

    kern = functools.partial(_slc_gather_kernel, nsel=nsel, ds=ds)
    grid_spec = pltpu.PrefetchScalarGridSpec(
        num_scalar_prefetch=2,
        grid=(db, ds),
        in_specs=[spec(kvh, j) for kvh in range(KV_HEADS) for j in range(nsel)]
                 + [pl.BlockSpec((None, KV_HEADS, QROWS, HEAD_DIM), lambda bi, qi, pt, sl: (bi, 0, 0, 0)),
                    pl.BlockSpec((KV_HEADS * SAMPLE_Q, LANES), lambda bi, qi, pt, sl: (bi, 0)),
                    pl.BlockSpec((None, LANES, KV_ROW), lambda bi, qi, pt, sl: (bi, 0, 0))],
        out_specs=pl.BlockSpec((None, KV_HEADS, QROWS, HEAD_DIM), lambda bi, qi, pt, sl: (bi, 0, 0, 0)),
    )
    return pl.pallas_call(
        kern, grid_spec=grid_spec,
        out_shape=jax.ShapeDtypeStruct((db, KV_HEADS, QROWS, HEAD_DIM), F32),
        compiler_params=_params(2),
        name="slc_gather",
    )(page_table, sel, *([pool_v] * (KV_HEADS * nsel)), q, bias_new, kv_new)


def _nsa_sample_kernel(q_ref, kc_ref, vc_ref, ov_ref, o_ref, sel_ref, tmnew_ref, sc_ref,
                       *, past, n_new, n_cmp, n_blk, n_sel, ng, gb):
    row = lax.broadcasted_iota(jnp.int32, (SAMPLE_Q, 1), 0)
    qpos = past + row
    ncp = kc_ref.shape[0]
    cidx = lax.broadcasted_iota(jnp.int32, (1, ncp), 1)
    cmask8 = jnp.where((CMP_STRIDE * cidx + CMP_LEN - 1 <= qpos) & (cidx < n_cmp), 1.0, 0.0)
    cmask = jnp.concatenate([cmask8] * GROUP, axis=0)
    lane = lax.broadcasted_iota(jnp.int32, (1, LANES), 1)

    for kvh in range(KV_HEADS):
        kc = kc_ref[:, kvh * HEAD_DIM:(kvh + 1) * HEAD_DIM]
        vc = vc_ref[:, kvh * HEAD_DIM:(kvh + 1) * HEAD_DIM]
        s = _dot_nt(q_ref[kvh], kc) * SCALE
        sm = jnp.where(cmask > 0.0, s, NEG)
        e = jnp.exp(sm - jnp.max(sm, axis=1, keepdims=True)) * cmask
        p = e / jnp.maximum(jnp.sum(e, axis=1, keepdims=True), 1e-30)
        o_ref[kvh] = _dot(p.astype(BF16), vc)
        psum = p[0:SAMPLE_Q]
        for g in range(1, GROUP):
            psum = psum + p[g * SAMPLE_Q:(g + 1) * SAMPLE_Q]
        imp = _split_dot(psum, ov_ref[...])
        for g in range(ng + 1):
            blk = g * gb + lane
            v = _block_rules(imp[:, g * LANES:(g + 1) * LANES], blk, qpos)
            v = jnp.where(row < n_new, v, -blk.astype(F32))
            sc_ref[g, kvh * SAMPLE_Q:(kvh + 1) * SAMPLE_Q, :] = jnp.where((lane < gb) & (blk < n_blk), v, -jnp.inf)

    rows = KV_HEADS * SAMPLE_Q
    idx_bits = ((ng + 1) * LANES - 1).bit_length()
    thr, vmax = _topk_threshold(lambda c, r0, nr: sc_ref[c, r0:r0 + nr, :], ng + 1, n_sel, idx_bits, rows, LANES)
    qpos2 = jnp.concatenate([qpos] * KV_HEADS, axis=0)
    n_past_blk = past // SLC_BLOCK
    tri = jnp.where(lax.broadcasted_iota(jnp.int32, (LANES, 1), 0) <= lane, 1.0, 0.0).astype(BF16)
    offs = jnp.zeros((rows, 1), F32)
    slots = [jnp.zeros((rows, 1), F32) for _ in range(n_sel)]
    for g in range(ng + 1):
        blk = g * gb + lane
        bm = _topk_mask(sc_ref[g], g * LANES + lane, thr, vmax)
        if g == ng:
            ok = jnp.where((past + lane <= qpos2) & (lane < n_new), 1.0, 0.0)
            tmnew_ref[...] = _mask_bias(bm[:, 0:1] * ok)
        bm = bm * jnp.where((lane < gb) & (blk < n_past_blk), 1.0, 0.0)
        rank = offs + _dot(bm.astype(BF16), tri)
        for j in range(n_sel):
            hit = jnp.where(bm > 0.0, jnp.where(rank == j + 1.0, blk.astype(F32), 0.0), 0.0)
            slots[j] = slots[j] + jnp.sum(hit, axis=1, keepdims=True)
        offs = offs + jnp.sum(bm, axis=1, keepdims=True)
    out = jnp.full((rows, LANES), -1.0, F32)
    for j in range(n_sel):
        out = jnp.where((lane == j) & (offs > j), slots[j], out)
    sel_ref[...] = out.astype(jnp.int32)


def _nsa_sample(q, kc, vc, past, n_new, ng, cw):
    db = q.shape[0]
    ncp = kc.shape[1]
    total = past + n_new
    n_blk = -(-total // SLC_BLOCK)
    n_cmp = total_chunks(past, n_new) - 1
    gb = cw // SLC_BLOCK
    assert gb <= LANES and past % cw == 0
    col = jnp.arange((ng + 1) * LANES)
    col_block = jnp.where(col % LANES < gb, (col // LANES) * gb + col % LANES, -1)
    col_block = jnp.where(col_block < n_blk, col_block, -1)
    ov = _overlap_matrix(ncp, (ng + 1) * LANES, col_block)
    kern = functools.partial(_nsa_sample_kernel, past=past, n_new=n_new, n_cmp=n_cmp, n_blk=n_blk,
                             n_sel=min(N_SEL, n_blk), ng=ng, gb=gb)
    rows = KV_HEADS * SAMPLE_Q
    b3 = lambda bi: (bi, 0, 0)
    b4 = lambda bi: (bi, 0, 0, 0)
    return pl.pallas_call(
        kern,
        grid=(db,),
        in_specs=[pl.BlockSpec((None, KV_HEADS, QROWS, HEAD_DIM), b4),
                  pl.BlockSpec((None, ncp, KV_HEADS * HEAD_DIM), b3),
                  pl.BlockSpec((None, ncp, KV_HEADS * HEAD_DIM), b3),
                  pl.BlockSpec((ncp, (ng + 1) * LANES), lambda bi: (0, 0))],
        out_specs=[pl.BlockSpec((None, KV_HEADS, QROWS, HEAD_DIM), b4),
                   pl.BlockSpec((rows, LANES), lambda bi: (bi, 0)),
                   pl.BlockSpec((rows, LANES), lambda bi: (bi, 0))],
        out_shape=[jax.ShapeDtypeStruct((db, KV_HEADS, QROWS, HEAD_DIM), F32),
                   jax.ShapeDtypeStruct((db * rows, LANES), jnp.int32),
                   jax.ShapeDtypeStruct((db * rows, LANES), F32)],
        scratch_shapes=[pltpu.VMEM((ng + 1, rows, LANES), F32)],
        compiler_params=_params(1),
        name="nsa_sample",
    )(q, kc, vc, ov)


def total_chunks(past, n_new):
    return past // CMP_STRIDE + -(-n_new // CMP_STRIDE)


def _win_sample_kernel(q_ref, st_ref, new_ref, o_ref, *, w_buf, n_new):
    row = lax.broadcasted_iota(jnp.int32, (SAMPLE_Q, 1), 0)
    row = jnp.concatenate([row] * GROUP, axis=0)
    j_old = lax.broadcasted_iota(jnp.int32, (1, w_buf), 1)
    j_new = w_buf + lax.broadcasted_iota(jnp.int32, (1, LANES), 1)
    def bias(j, valid):
        d = w_buf + row - j
        return jnp.where((d >= 0) & (d < WINDOW) & valid, 0.0, NEG)
    b_old, b_new = bias(j_old, True), bias(j_new, j_new < w_buf + n_new)
    for kvh in range(KV_HEADS):
        q = q_ref[kvh]
        kcol, vcol = _kv_cols(kvh)
        carry = _softmax_step(_softmax_init(QROWS), q, _token_rows(st_ref, 2 * kvh, w_buf).astype(BF16),
                              _token_rows(st_ref, 2 * kvh + 1, w_buf).astype(BF16), b_old)
        carry = _softmax_step(carry, q, new_ref[:, kcol], new_ref[:, vcol], b_new)
        o_ref[kvh] = _softmax_done(carry)


def _win_sample(q, state, kv_new, n_new):
    db, w_buf = state.shape[0], state.shape[1]
    kern = functools.partial(_win_sample_kernel, w_buf=w_buf, n_new=n_new)
    b3 = lambda bi: (bi, 0, 0)
    b4 = lambda bi: (bi, 0, 0, 0)
    return pl.pallas_call(
        kern,
        grid=(db,),
        in_specs=[pl.BlockSpec((None, KV_HEADS, QROWS, HEAD_DIM), b4),
                  pl.BlockSpec((w_buf * ROWS_PER_TOKEN, HEAD_DIM), lambda bi: (bi, 0)),
                  pl.BlockSpec((None, LANES, KV_ROW), b3)],
        out_specs=pl.BlockSpec((None, KV_HEADS, QROWS, HEAD_DIM), b4),
        out_shape=jax.ShapeDtypeStruct((db, KV_HEADS, QROWS, HEAD_DIM), F32),
        compiler_params=_params(1),
        name="win_sample",
    )(q, _cache_rows(state), kv_new)


def _gate_merge_kernel(oa_ref, oc_ref, os_ref, ow_ref, misc_ref, sza_ref, szb_ref, a_ref, b_ref):
    misc = misc_ref[...]
    a_ref[...] = (oa_ref[...] * sza_ref[...]).astype(BF16)
    for hd in range(N_HEADS):
        sl = slice(hd * HEAD_DIM, (hd + 1) * HEAD_DIM)
        g = [misc[:, MISC_IW + j * N_HEADS + hd:MISC_IW + j * N_HEADS + hd + 1] for j in range(3)]
        o_b = g[0] * oc_ref[:, sl] + g[1] * os_ref[:, sl] + g[2] * ow_ref[:, sl]
        b_ref[:, sl] = (o_b * szb_ref[:, sl]).astype(BF16)


def _gate_merge(o_a, o_cmp, o_slc, o_win, misc, sza, szb):
    n = o_a.shape[0]
    full = lambda w: pl.BlockSpec((n, w), lambda i: (0, 0))
    out = jax.ShapeDtypeStruct((n, WIDTH), BF16)
    return pl.pallas_call(
        _gate_merge_kernel,
        grid=(1,),
        in_specs=[full(WIDTH)] * 4 + [full(LANES), full(WIDTH), full(WIDTH)],
        out_specs=[full(WIDTH)] * 2,
        out_shape=[out, out],
        compiler_params=_params(1),
        name="gate_merge",
    )(o_a, o_cmp, o_slc, o_win, misc, sza, szb)


def _merge_kernel(a_ref, b_ref, h_ref, wa_ref, wb_ref, wg0_ref, wg1_ref, o_ref):
    h = h_ref[...]
    y_a = _dot(a_ref[...], wa_ref[...])
    y_b = _dot(b_ref[...], wb_ref[...])
    g0 = jax.nn.sigmoid(_dot(h, wg0_ref[...]))
    g1 = jax.nn.sigmoid(_dot(h, wg1_ref[...]))
    o_ref[...] = (g0 * y_a + g1 * y_b).astype(BF16)


def _merge(a, bm, h, w_a, w_b, w_g0, w_g1, tr, cb):
    n = a.shape[0]
    row = lambda j, i: (i, 0)
    col = lambda j, i: (0, j)
    return pl.pallas_call(
        _merge_kernel,
        grid=(D_MODEL // cb, n // tr),
        in_specs=[pl.BlockSpec((tr, WIDTH), row), pl.BlockSpec((tr, WIDTH), row),
                  pl.BlockSpec((tr, D_MODEL), row),
                  pl.BlockSpec((WIDTH, cb), col), pl.BlockSpec((WIDTH, cb), col),
                  pl.BlockSpec((D_MODEL, cb), col), pl.BlockSpec((D_MODEL, cb), col)],
        out_specs=pl.BlockSpec((tr, cb), lambda j, i: (i, j)),
        out_shape=jax.ShapeDtypeStruct((n, D_MODEL), BF16),
        compiler_params=_params(2),
        name="merge",
    )(a, bm, h, w_a, w_b, w_g0, w_g1)


def _out_proj_kernel(x_ref, m_ref, w_ref, o_ref):
    o_ref[...] = x_ref[...] + _dot(m_ref[...], w_ref[...])


def _out_proj(x, merged, w_out, tr):
    n = x.shape[0]
    row = lambda i: (i, 0)
    return pl.pallas_call(
        _out_proj_kernel,
        grid=(n // tr,),
        in_specs=[pl.BlockSpec((tr, D_MODEL), row), pl.BlockSpec((tr, D_MODEL), row),
                  pl.BlockSpec((D_MODEL, D_MODEL), lambda i: (0, 0))],
        out_specs=pl.BlockSpec((tr, D_MODEL), row),
        out_shape=jax.ShapeDtypeStruct((n, D_MODEL), F32),
        compiler_params=_params(1),
        name="out_proj",
    )(x, merged, w_out)


def _project(x2d, pos_period, tr, wts):
    n = x2d.shape[0]
    n_tab = pos_period.shape[0] // tr
    tabs_head = _rope_tables(pos_period, ROT_DIM, HEAD_DIM)
    tabs_iq = _rope_tables(pos_period, IDX_ROT_DIM, IDX_DIM)
    tabs_iq = tuple(jnp.concatenate([t, t], axis=1) for t in tabs_iq)
    ident = (jnp.ones_like(tabs_iq[0][:, :IDX_DIM]), jnp.zeros_like(tabs_iq[0][:, :IDX_DIM]),
             jnp.zeros_like(tabs_iq[0][:, :IDX_DIM]))
    tabs_misc = tuple(jnp.concatenate([t[:, :IDX_DIM], e], axis=1) for t, e in zip(tabs_iq, ident))

    h = _rmsnorm(x2d, wts["norm_gain"], tr)
    (qa,) = _proj_q(h, wts["w_qa"], wts["q_norm_a"], tabs_head, tr, n_tab, False)
    qb_rot, qb = _proj_q(h, wts["w_qb"], wts["q_norm_b"], tabs_head, tr, n_tab, True)
    akv, cmp_kv, slc, win, akv_b, slc_b, win_b = _proj_kv(
        h, wts["w_kv"], wts["k_norm_a"], wts["k_norm_slc"], wts["k_norm_win"], tabs_head, tr, n_tab)
    iq, misc = _proj_idx(h, wts["w_idx"], tabs_iq, tabs_misc, tr, n_tab)
    sza, szb = _proj_z(h, wts["w_z"], tr)
    return dict(h=h, qa=qa, qb=qb, qb_rot=qb_rot, akv=akv, cmp=cmp_kv, slc=slc, win=win,
                akv_b=akv_b, slc_b=slc_b, win_b=win_b, iq=iq, misc=misc, sza=sza, szb=szb)


def _sample_rows(a, db, ds):
    a = a.reshape(db, ds, KV_HEADS, GROUP, HEAD_DIM).transpose(0, 2, 3, 1, 4)
    a = jnp.pad(a, ((0, 0), (0, 0), (0, 0), (0, SAMPLE_Q - ds), (0, 0)))
    return a.reshape(db, KV_HEADS, QROWS, HEAD_DIM)


def _unsample_rows(o, db, ds):
    o = o.reshape(db, KV_HEADS, GROUP, SAMPLE_Q, HEAD_DIM)[:, :, :, :ds]
    return o.transpose(0, 3, 1, 2, 4).reshape(db * ds, WIDTH)


def _pad_new(a, db, ds):
    return jnp.pad(a.reshape(db, ds, a.shape[-1]), ((0, 0), (0, LANES - ds), (0, 0)))


def kernel(x_prompt, x_sample, cache_a_kv, cache_a_idx, cache_cmp_kv, cache_slc_kv, state_win_kv,
           page_table, norm_gain, w_in, q_norm_a, k_norm_a, q_norm_b, k_norm_cmp, k_norm_slc,
           k_norm_win, pe_cmp, w1_cmp, w2_cmp, w_proj_a, w_proj_b, w_out):
    b, t, _ = x_prompt.shape
    db, ds, _ = x_sample.shape
    n_pages = page_table.shape[1]
    past = n_pages * PAGE_SIZE
    assert ds <= SAMPLE_Q and ds <= CMP_STRIDE

    sizes = (WIDTH, KV_ROW, IDX_COLS, IDX_DIM, IDX_HEADS, WIDTH, WIDTH, 3 * KV_ROW, 3 * N_HEADS, WIDTH,
             2 * D_MODEL)
    offs = [0]
    for s in sizes:
        offs.append(offs[-1] + s)
    wb = w_in.astype(BF16)
    seg = lambda j: wb[:, offs[j]:offs[j + 1]]
    misc_pad = jnp.zeros((D_MODEL, LANES - (IDX_DIM + IDX_HEADS + 3 * N_HEADS)), BF16)
    wts = dict(
        norm_gain=norm_gain, q_norm_a=q_norm_a, q_norm_b=q_norm_b, k_norm_a=k_norm_a,
        k_norm_slc=k_norm_slc, k_norm_win=k_norm_win,
        w_qa=seg(0), w_qb=seg(6),
        w_kv=jnp.concatenate([seg(1), seg(7)], axis=1),
        w_idx=jnp.concatenate([seg(2), seg(3), seg(4), seg(8), misc_pad], axis=1),
        w_z=jnp.concatenate([seg(5), seg(9)], axis=1),
    )
    w_g0 = wb[:, offs[10]:offs[10] + D_MODEL]
    w_g1 = wb[:, offs[10] + D_MODEL:offs[11]]
    w_pa, w_pb, w_o = w_proj_a.astype(BF16), w_proj_b.astype(BF16), w_out.astype(BF16)
    w1_fs = jnp.concatenate([w1_cmp[:, :CMP_STRIDE].reshape(2, CMP_STRIDE * HEAD_DIM, CMP_HIDDEN),
                             w1_cmp[:, CMP_STRIDE:].reshape(2, CMP_STRIDE * HEAD_DIM, CMP_HIDDEN)],
                            axis=2).astype(BF16)
    w1_r = w1_cmp.reshape(2, CMP_LEN * HEAD_DIM, CMP_HIDDEN).astype(BF16)
    pe_r = pe_cmp.reshape(2, CMP_LEN * HEAD_DIM)
    w2_b = w2_cmp.astype(BF16)

    tr = min(512, t)
    tq = min(256, t)
    tk = min(512, t)
    xp = x_prompt.reshape(b * t, D_MODEL)
    pp = _project(xp, jnp.arange(t, dtype=jnp.int32), tr, wts)

    pages_p = t // PAGE_SIZE
    table_p = jnp.arange(b * pages_p, dtype=jnp.int32).reshape(b, pages_p)
    fs_p = _chunk_terms_paged(pp["cmp"], table_p, w1_fs, min(PAGES_PER_STEP, pages_p))
    kc_p, vc_p = _compress_mlp(fs_p, jnp.zeros((b, SUBLANES, FS_COLS), F32), pe_r, w1_r, w2_b, k_norm_cmp)

    ik_b = pp["misc"][:, :IDX_DIM].astype(BF16)
    ik2 = jnp.concatenate([ik_b, ik_b], axis=1).reshape(b, t, LANES)
    a_p = _dsa_prompt(pp["iq"], pp["misc"], ik2, pp["qa"], pp["akv_b"].reshape(b, t, KV_ROW), pp["sza"],
                      b, t, tq, tk)
    win_front = jnp.pad(pp["win_b"].reshape(b, t, KV_ROW), ((0, 0), (WINDOW, 0), (0, 0)))
    b_p = _nsa_prompt(pp["qb"], pp["qb_rot"], kc_p, vc_p, pp["slc_b"].reshape(b, t, KV_ROW),
                      win_front, pp["misc"], pp["szb"], b, t, tq, tk)
    merged_p = _merge(a_p, b_p, pp["h"], w_pa, w_pb, w_g0, w_g1, tr, 1024)
    y_prompt = _out_proj(xp, merged_p, w_o, tr).reshape(b, t, D_MODEL)

    kv5 = lambda a, n0, n1: a.reshape(n0, n1, KV_HEADS, 2, HEAD_DIM)
    p_a_kv = kv5(pp["akv"], b, t)
    p_a_idx = pp["misc"][:, :IDX_DIM].reshape(b, t, IDX_DIM)
    p_cmp_kv = kv5(pp["cmp"], b, t)
    p_slc_kv = kv5(pp["slc"], b, t)
    p_win_kv = kv5(pp["win"], b, t)[:, t - min(WINDOW, t):]

    ns = db * ds
    xs = x_sample.reshape(ns, D_MODEL)
    pos_s = past + jnp.tile(jnp.arange(ds, dtype=jnp.int32), db)
    ps = _project(xs, pos_s, ns, wts)
    pg = min(PAGES_PER_STEP, n_pages)
    ng = n_pages // pg
    cw = pg * PAGE_SIZE

    iq_s = ps["iq"].reshape(db, ds, IDX_HEADS, IDX_DIM).transpose(0, 2, 1, 3)
    iqm = jnp.pad(iq_s, ((0, 0), (0, 0), (0, SAMPLE_Q - ds), (0, 0))).reshape(db, IDX_HEADS * SAMPLE_Q, IDX_DIM)
    iw_s = ps["misc"][:, MISC_IK:MISC_IW].reshape(db, ds, IDX_HEADS).transpose(0, 2, 1) * (IDX_DIM ** -0.5)
    iwm = jnp.pad(iw_s, ((0, 0), (0, 0), (0, SAMPLE_Q - ds))).reshape(db, IDX_HEADS * SAMPLE_Q, 1)
    iwm = jnp.broadcast_to(iwm, (db, IDX_HEADS * SAMPLE_Q, LANES))
    ik_new_t = jnp.swapaxes(_pad_new(ps["misc"][:, :IDX_DIM].astype(BF16), db, ds), 1, 2)
    sc_s, sc_new = _idx_scores(jnp.swapaxes(cache_a_idx, 1, 2), page_table, iqm, iwm, ik_new_t, pg, ds)
    mk_a, mk_a_new = _dsa_topk(sc_s, sc_new, past, ds)
    o_a_s = _paged_attn(cache_a_kv, page_table, _sample_rows(ps["qa"], db, ds), mk_a, mk_a_new,
                        _pad_new(ps["akv_b"], db, ds), pg)

    fs_past = _chunk_terms_paged(cache_cmp_kv, page_table, w1_fs, pg)
    new_chunk = jnp.pad(ps["cmp"].reshape(db, ds, KV_ROW), ((0, 0), (0, CMP_STRIDE - ds), (0, 0)))
    fs_new = _chunk_terms_dense(new_chunk.reshape(db, CHUNK_FLAT), w1_fs, db)
    fs_new = jnp.pad(fs_new.reshape(db, 1, FS_COLS), ((0, 0), (0, SUBLANES - 1), (0, 0)))
    kc_s, vc_s = _compress_mlp(fs_past, fs_new, pe_r, w1_r, w2_b, k_norm_cmp)
    o_cmp_s, sel_s, mk_s_new = _nsa_sample(_sample_rows(ps["qb"], db, ds), kc_s, vc_s, past, ds, ng, cw)
    q_rot_s = _sample_rows(ps["qb_rot"], db, ds)
    n_sel_s = min(N_SEL, -(-(past + ds) // SLC_BLOCK))
    sel_s = sel_s[:, :n_sel_s].reshape(db, KV_HEADS, SAMPLE_Q, n_sel_s)[:, :, :ds]
    sel_s = sel_s.transpose(0, 2, 1, 3).reshape(-1)
    o_slc_s = _slc_gather(cache_slc_kv, page_table, sel_s, q_rot_s, mk_s_new, _pad_new(ps["slc_b"], db, ds),
                          n_sel_s, ds)
    o_win_s = _win_sample(q_rot_s, state_win_kv, _pad_new(ps["win_b"], db, ds), ds)

    a_s, b_s = _gate_merge(_unsample_rows(o_a_s, db, ds), _unsample_rows(o_cmp_s, db, ds),
                           _unsample_rows(o_slc_s, db, ds), _unsample_rows(o_win_s, db, ds),
                           ps["misc"], ps["sza"], ps["szb"])
    merged_s = _merge(a_s, b_s, ps["h"], w_pa, w_pb, w_g0, w_g1, ns, 1024)
    y_sample = _out_proj(xs, merged_s, w_o, ns).reshape(db, ds, D_MODEL)

    s_win = kv5(ps["win"], db, ds)
    w_buf = state_win_kv.shape[1]
    s_win_kv = jnp.concatenate([state_win_kv, s_win], axis=1)[:, ds:ds + w_buf]

    return (y_prompt, y_sample, p_a_kv, p_a_idx, p_cmp_kv, p_slc_kv, p_win_kv,
            kv5(ps["akv"], db, ds), ps["misc"][:, :IDX_DIM].reshape(db, ds, IDX_DIM),
            kv5(ps["cmp"], db, ds), kv5(ps["slc"], db, ds), s_win_kv)
```

```python
import functools

import jax
import jax.numpy as jnp
from jax import lax
from jax.experimental import pallas as pl
from jax.experimental.pallas import tpu as pltpu

D_MODEL = 2048
HEAD_DIM = 128
ROT_DIM = HEAD_DIM // 4
N_HEADS = D_MODEL // (2 * HEAD_DIM)
KV_HEADS = 2
GROUP = N_HEADS // KV_HEADS
IDX_HEADS = 8
IDX_DIM = 64
IDX_ROT_DIM = IDX_DIM // 4
A_TOPK = 256
CMP_LEN = 32
CMP_STRIDE = 16
CMP_HIDDEN = 128
SLC_BLOCK = 64
N_SEL = 16
WINDOW = 512
PAGE_SIZE = 128
ROPE_THETA = 500000.0
EPS = 1e-6
NEG = -1e30
WIDTH = N_HEADS * HEAD_DIM
KV_ROW = KV_HEADS * 2 * HEAD_DIM
SCALE = HEAD_DIM ** -0.5

LANES = 128
SUBLANES = 8
VMEM_LIMIT = 48 * 1024 * 1024
PAGES_PER_STEP = 16
SAMPLE_Q = 8

F32 = jnp.float32
BF16 = jnp.bfloat16
NT_DIMS = (((1,), (1,)), ((), ()))


def _params(n_axes):
    return pltpu.CompilerParams(dimension_semantics=("arbitrary",) * n_axes,
                                vmem_limit_bytes=VMEM_LIMIT)


def _dot(a, b):
    return jnp.dot(a, b, preferred_element_type=F32)


def _dot_nt(a, b):
    return lax.dot_general(a, b, NT_DIMS, preferred_element_type=F32)


def _rope_tables(pos, rot, period, width=LANES):
    half = rot // 2
    inv = ROPE_THETA ** (-jnp.arange(half, dtype=F32) / half)
    ang = pos.astype(F32)[:, None] * inv
    cos, sin = jnp.cos(ang), jnp.sin(ang)
    n = pos.shape[0]
    zeros = lambda w: jnp.zeros((n, w), F32)
    ones = lambda w: jnp.ones((n, w), F32)
    c = jnp.concatenate([cos, cos, ones(period - rot)], axis=1)
    s1 = jnp.concatenate([-sin, zeros(period - half)], axis=1)
    s2 = jnp.concatenate([zeros(half), sin, zeros(period - rot)], axis=1)
    return c, s1, s2


def _apply_rope(y, c, s1, s2, half):
    return y * c + pltpu.roll(y, LANES - half, 1) * s1 + pltpu.roll(y, half, 1) * s2


def _head_norm(z, g):
    return z * lax.rsqrt(jnp.mean(z * z, axis=-1, keepdims=True) + EPS) * g


def _rmsnorm_kernel(x_ref, g_ref, o_ref):
    x = x_ref[...]
    y = x * lax.rsqrt(jnp.mean(x * x, axis=-1, keepdims=True) + EPS)
    o_ref[...] = (y * g_ref[...]).astype(o_ref.dtype)


def _rmsnorm(x, gain, tr):
    n, d = x.shape
    return pl.pallas_call(
        _rmsnorm_kernel,
        grid=(n // tr,),
        in_specs=[pl.BlockSpec((tr, d), lambda i: (i, 0)), pl.BlockSpec((1, d), lambda i: (0, 0))],
        out_specs=pl.BlockSpec((tr, d), lambda i: (i, 0)),
        out_shape=jax.ShapeDtypeStruct((n, d), BF16),
        compiler_params=_params(1),
        name="rmsnorm",
    )(x, gain.reshape(1, d))


def _proj_q_kernel(h_ref, w_ref, g_ref, c_ref, s1_ref, s2_ref, rot_ref, *plain_ref):
    z = _dot(h_ref[...], w_ref[...])
    g = g_ref[...]
    c, s1, s2 = c_ref[...], s1_ref[...], s2_ref[...]
    for hd in range(N_HEADS):
        sl = slice(hd * HEAD_DIM, (hd + 1) * HEAD_DIM)
        y = _head_norm(z[:, sl], g)
        if plain_ref:
            plain_ref[0][:, sl] = y.astype(BF16)
        rot_ref[:, sl] = _apply_rope(y, c, s1, s2, ROT_DIM // 2).astype(BF16)


def _proj_q(h, w, gain, tabs, tr, n_tab_blocks, want_plain):
    n = h.shape[0]
    row = lambda i: (i, 0)
    fix = lambda i: (0, 0)
    tab = lambda i: (i % n_tab_blocks, 0)
    out_shape = [jax.ShapeDtypeStruct((n, WIDTH), BF16)] * (2 if want_plain else 1)
    out_specs = [pl.BlockSpec((tr, WIDTH), row)] * (2 if want_plain else 1)
    return pl.pallas_call(
        _proj_q_kernel,
        grid=(n // tr,),
        in_specs=[pl.BlockSpec((tr, D_MODEL), row), pl.BlockSpec((D_MODEL, WIDTH), fix),
                  pl.BlockSpec((1, HEAD_DIM), fix)] + [pl.BlockSpec((tr, LANES), tab)] * 3,
        out_specs=out_specs,
        out_shape=out_shape,
        compiler_params=_params(1),
        name="proj_q",
    )(h, w, gain.reshape(1, HEAD_DIM), *tabs)


def _proj_kv_kernel(h_ref, w_ref, ga_ref, gs_ref, gw_ref, c_ref, s1_ref, s2_ref,
                    akv_ref, cmp_ref, slc_ref, win_ref, akv_b_ref, slc_b_ref, win_b_ref):
    z = _dot(h_ref[...], w_ref[...])
    c, s1, s2 = c_ref[...], s1_ref[...], s2_ref[...]

    def emit(base, gain_ref, f32_ref, bf_ref):
        for j in range(KV_HEADS * 2):
            sl = slice(j * HEAD_DIM, (j + 1) * HEAD_DIM)
            v = z[:, base + j * HEAD_DIM: base + (j + 1) * HEAD_DIM]
            if gain_ref is not None and j % 2 == 0:
                v = _apply_rope(_head_norm(v, gain_ref[...]), c, s1, s2, ROT_DIM // 2)
            f32_ref[pl.ds(j, z.shape[0], stride=KV_HEADS * 2), :] = v
            if bf_ref is not None:
                bf_ref[:, sl] = v.astype(BF16)

    emit(0 * KV_ROW, ga_ref, akv_ref, akv_b_ref)
    emit(1 * KV_ROW, None, cmp_ref, None)
    emit(2 * KV_ROW, gs_ref, slc_ref, slc_b_ref)
    emit(3 * KV_ROW, gw_ref, win_ref, win_b_ref)


def _proj_kv(h, w, g_a, g_slc, g_win, tabs, tr, n_tab_blocks):
    n = h.shape[0]
    row = lambda i: (i, 0)
    fix = lambda i: (0, 0)
    tab = lambda i: (i % n_tab_blocks, 0)
    f32_out = jax.ShapeDtypeStruct((n * KV_HEADS * 2, HEAD_DIM), F32)
    bf_out = jax.ShapeDtypeStruct((n, KV_ROW), BF16)
    return pl.pallas_call(
        _proj_kv_kernel,
        grid=(n // tr,),
        in_specs=[pl.BlockSpec((tr, D_MODEL), row), pl.BlockSpec((D_MODEL, 4 * KV_ROW), fix)]
                 + [pl.BlockSpec((1, HEAD_DIM), fix)] * 3 + [pl.BlockSpec((tr, LANES), tab)] * 3,
        out_specs=[pl.BlockSpec((tr * KV_HEADS * 2, HEAD_DIM), row)] * 4 + [pl.BlockSpec((tr, KV_ROW), row)] * 3,
        out_shape=[f32_out] * 4 + [bf_out] * 3,
        compiler_params=_params(1),
        name="proj_kv",
    )(h, w, g_a.reshape(1, HEAD_DIM), g_slc.reshape(1, HEAD_DIM), g_win.reshape(1, HEAD_DIM), *tabs)


IDX_COLS = IDX_HEADS * IDX_DIM
MISC_IK = IDX_DIM
MISC_IW = MISC_IK + IDX_HEADS
MISC_G = MISC_IW + 3 * N_HEADS


def _proj_idx_kernel(h_ref, w_ref, c_ref, s1_ref, s2_ref, cm_ref, s1m_ref, s2m_ref, iq_ref, misc_ref):
    z = _dot(h_ref[...], w_ref[...])
    half = IDX_ROT_DIM // 2
    c, s1, s2 = c_ref[...], s1_ref[...], s2_ref[...]
    for j in range(IDX_COLS // LANES):
        sl = slice(j * LANES, (j + 1) * LANES)
        iq_ref[:, sl] = _apply_rope(z[:, sl], c, s1, s2, half).astype(BF16)
    m = z[:, IDX_COLS:IDX_COLS + LANES]
    roped = _apply_rope(m, cm_ref[...], s1m_ref[...], s2m_ref[...], half)
    lane = lax.broadcasted_iota(jnp.int32, m.shape, 1)
    misc_ref[...] = jnp.where(lane < MISC_IK, roped,
                              jnp.where(lane < MISC_IW, m * (IDX_HEADS ** -0.5),
                                        jnp.where(lane < MISC_G, jax.nn.sigmoid(m), 0.0)))


def _proj_idx(h, w, tabs_iq, tabs_misc, tr, n_tab_blocks):
    n = h.shape[0]
    row = lambda i: (i, 0)
    fix = lambda i: (0, 0)
    tab = lambda i: (i % n_tab_blocks, 0)
    return pl.pallas_call(
        _proj_idx_kernel,
        grid=(n // tr,),
        in_specs=[pl.BlockSpec((tr, D_MODEL), row), pl.BlockSpec((D_MODEL, IDX_COLS + LANES), fix)]
                 + [pl.BlockSpec((tr, LANES), tab)] * 6,
        out_specs=[pl.BlockSpec((tr, IDX_COLS), row), pl.BlockSpec((tr, LANES), row)],
        out_shape=[jax.ShapeDtypeStruct((n, IDX_COLS), BF16), jax.ShapeDtypeStruct((n, LANES), F32)],
        compiler_params=_params(1),
        name="proj_idx",
    )(h, w, *tabs_iq, *tabs_misc)


def _proj_z_kernel(h_ref, w_ref, za_ref, zb_ref):
    z = _dot(h_ref[...], w_ref[...])
    s = z * jax.nn.sigmoid(z)
    za_ref[...] = s[:, :WIDTH]
    zb_ref[...] = s[:, WIDTH:]


def _proj_z(h, w, tr):
    n = h.shape[0]
    row = lambda i: (i, 0)
    out = jax.ShapeDtypeStruct((n, WIDTH), F32)
    return pl.pallas_call(
        _proj_z_kernel,
        grid=(n // tr,),
        in_specs=[pl.BlockSpec((tr, D_MODEL), row), pl.BlockSpec((D_MODEL, 2 * WIDTH), lambda i: (0, 0))],
        out_specs=[pl.BlockSpec((tr, WIDTH), row)] * 2,
        out_shape=[out, out],
        compiler_params=_params(1),
        name="proj_z",
    )(h, w)


def _key_to_float(u):
    int_min = jnp.int32(-2 ** 31)
    bits = jnp.where(u < 0, u ^ int_min, ~u)
    return lax.bitcast_convert_type(bits, F32)


def _lane_fold(w):
    acc = w[:, :LANES]
    for j in range(1, w.shape[1] // LANES):
        acc = acc + w[:, j * LANES:(j + 1) * LANES]
    return acc


COUNT_ROWS = 1024


def _topk_threshold(load, nck, k, idx_bits, rows, cw, tail=None):
    kf = jnp.float32(k)
    lane = lax.broadcasted_iota(jnp.int32, (1, cw), 1)
    nr = min(rows, COUNT_ROWS)

    def count(pred_value):
        out = []
        for r0 in range(0, rows, nr):
            rs = slice(r0, r0 + nr)

            def body(c, acc, r0=r0, rs=rs):
                return acc + _lane_fold(pred_value(load(c, r0, nr), c * cw + lane, rs))
            acc = lax.fori_loop(0, nck, body, jnp.zeros((nr, LANES), F32))
            if tail is not None:
                acc = acc + pred_value(tail[0][rs], tail[1], rs)
            out.append(jnp.sum(acc, axis=1, keepdims=True))
        return out[0] if len(out) == 1 else jnp.concatenate(out, axis=0)

    def value_bit(it, u):
        cand = u | jnp.left_shift(jnp.int32(1), 31 - it)
        candf = _key_to_float(cand)
        cnt = count(lambda x, idx, rs: jnp.where(x >= candf[rs], 1.0, 0.0))
        return jnp.where(cnt >= kf, cand, u)

    u = lax.fori_loop(0, 32, value_bit, jnp.zeros((rows, 1), jnp.int32))
    thr = jnp.where((u >= 0) & (u < 2 ** 23), -jnp.inf, _key_to_float(u))
    need = kf - count(lambda x, idx, rs: jnp.where(x > thr[rs], 1.0, 0.0))

    def index_bit(it, v):
        cand = v | jnp.left_shift(jnp.int32(1), idx_bits - 1 - it)
        below = count(lambda x, idx, rs: jnp.where(x == thr[rs], jnp.where(idx < cand[rs], 1.0, 0.0), 0.0))
        return jnp.where(below < need, cand, v)

    tied = jnp.max(count(lambda x, idx, rs: jnp.where(x >= thr[rs], 1.0, 0.0))) > kf
    v = lax.cond(tied,
                 lambda: lax.fori_loop(0, idx_bits, index_bit, jnp.zeros((rows, 1), jnp.int32)),
                 lambda: jnp.full((rows, 1), (1 << idx_bits) - 1, jnp.int32))
    return thr, jnp.where(need >= 1.0, v, -1)


def _topk_mask(x, idx, thr, vmax):
    gt = jnp.where(x > thr, 1.0, 0.0)
    eq = jnp.where(x == thr, jnp.where(idx <= vmax, 1.0, 0.0), 0.0)
    return jnp.maximum(gt, eq)


def _sublane_fold(w):
    parts = [w[j * SUBLANES:(j + 1) * SUBLANES] for j in range(w.shape[0] // SUBLANES)]
    while len(parts) > 1:
        parts = [parts[j] + parts[j + 1] if j + 1 < len(parts) else parts[j] for j in range(0, len(parts), 2)]
    return parts[0]


def _topk_threshold_cols(load, nck, k, idx_bits, cols, ck):
    kf = jnp.float32(k)
    sub = lax.broadcasted_iota(jnp.int32, (ck, 1), 0)

    def count(pred_value):
        def body(c, acc):
            return acc + _sublane_fold(pred_value(load(c), c * ck + sub))
        acc = lax.fori_loop(0, nck, body, jnp.zeros((SUBLANES, cols), F32))
        return jnp.sum(acc, axis=0, keepdims=True)

    def value_bit(it, u):
        cand = u | jnp.left_shift(jnp.int32(1), 31 - it)
        candf = _key_to_float(cand)
        cnt = count(lambda x, idx: jnp.where(x >= candf, 1.0, 0.0))
        return jnp.where(cnt >= kf, cand, u)

    u = lax.fori_loop(0, 32, value_bit, jnp.zeros((1, cols), jnp.int32))
    thr = jnp.where((u >= 0) & (u < 2 ** 23), -jnp.inf, _key_to_float(u))
    need = kf - count(lambda x, idx: jnp.where(x > thr, 1.0, 0.0))

    def index_bit(it, v):
        cand = v | jnp.left_shift(jnp.int32(1), idx_bits - 1 - it)
        below = count(lambda x, idx: jnp.where(x == thr, jnp.where(idx < cand, 1.0, 0.0), 0.0))
        return jnp.where(below < need, cand, v)

    tied = jnp.max(count(lambda x, idx: jnp.where(x >= thr, 1.0, 0.0))) > kf
    v = lax.cond(tied,
                 lambda: lax.fori_loop(0, idx_bits, index_bit, jnp.zeros((1, cols), jnp.int32)),
                 lambda: jnp.full((1, cols), (1 << idx_bits) - 1, jnp.int32))
    return thr, jnp.where(need >= 1.0, v, -1)


LOG2E = 1.4426950408889634
SCORE_SCALE = SCALE * LOG2E


def _softmax_init(rows):
    return (jnp.full((rows, 1), NEG, F32), jnp.zeros((rows, 1), F32), jnp.zeros((rows, HEAD_DIM), F32))


def _softmax_step(carry, q, k, v, bias):
    m, l, acc = carry
    t = _dot_nt(q, k) * SCORE_SCALE + bias
    m_new = jnp.maximum(m, jnp.max(t, axis=1, keepdims=True))
    alpha = jnp.exp2(m - m_new)
    e = jnp.exp2(t - m_new)
    l = alpha * l + jnp.sum(e, axis=1, keepdims=True)
    acc = alpha * acc + _dot(e.astype(BF16), v)
    return m_new, l, acc


def _softmax_done(carry):
    _, l, acc = carry
    return acc / jnp.maximum(l, 1e-30)


def _mask_bias(maskf):
    return jnp.where(maskf > 0.0, 0.0, NEG)


def _stack_heads(ref, kvh):
    return jnp.concatenate([ref[:, (kvh * GROUP + g) * HEAD_DIM:(kvh * GROUP + g + 1) * HEAD_DIM]
                            for g in range(GROUP)], axis=0)


def _kv_cols(kvh):
    return (slice(kvh * 2 * HEAD_DIM, (kvh * 2 + 1) * HEAD_DIM),
            slice((kvh * 2 + 1) * HEAD_DIM, (kvh * 2 + 2) * HEAD_DIM))


def _dsa_prompt_kernel(iq_ref, iwt_ref, ik2_ref, qa_ref, akv_ref, sza_ref, o_ref, sc_ref, mk_ref,
                       *, tq, tk, k_top, idx_bits):
    i = pl.program_id(1)
    nck = ((i + 1) * tq + tk - 1) // tk
    lane = lax.broadcasted_iota(jnp.int32, (1, LANES), 1)
    qpos = i * tq + lax.broadcasted_iota(jnp.int32, (1, tq), 1)
    krow = lax.broadcasted_iota(jnp.int32, (tk, 1), 0)
    iwt = iwt_ref[...] * (IDX_DIM ** -0.5)

    iqs = []
    for hd in range(IDX_HEADS):
        pair = iq_ref[:, (hd // 2) * LANES:(hd // 2 + 1) * LANES]
        keep = (lane < IDX_DIM) if hd % 2 == 0 else (lane >= IDX_DIM)
        iqs.append(jnp.where(keep, pair, jnp.zeros_like(pair)))

    def score_chunk(c, carry):
        k0 = pl.multiple_of(c * tk, tk)
        ikc = ik2_ref[pl.ds(k0, tk), :]
        acc = jnp.zeros((tk, tq), F32)
        for hd in range(IDX_HEADS):
            acc = acc + jnp.maximum(_dot_nt(ikc, iqs[hd]), 0.0) * iwt[hd:hd + 1, :]
        sc_ref[c] = jnp.where(k0 + krow <= qpos, acc, NEG)
        return carry

    lax.fori_loop(0, nck, score_chunk, 0)

    thr, vmax = _topk_threshold_cols(lambda c: sc_ref[c], nck, k_top, idx_bits, tq, tk)

    def mask_chunk(c, carry):
        idx = c * tk + krow
        sel = _topk_mask(sc_ref[c], idx, thr, vmax) * jnp.where(idx <= qpos, 1.0, 0.0)
        mk_ref[c] = _mask_bias(sel).T
        return carry

    lax.fori_loop(0, nck, mask_chunk, 0)

    qs = [_stack_heads(qa_ref, kvh) for kvh in range(KV_HEADS)]

    def attend(c, carries):
        k0 = pl.multiple_of(c * tk, tk)
        bias = jnp.concatenate([mk_ref[c]] * GROUP, axis=0)
        out = []
        for kvh in range(KV_HEADS):
            kcol, vcol = _kv_cols(kvh)
            out.append(_softmax_step(carries[kvh], qs[kvh], akv_ref[pl.ds(k0, tk), kcol],
                                     akv_ref[pl.ds(k0, tk), vcol], bias))
        return tuple(out)

    carries = lax.fori_loop(0, nck, attend, tuple(_softmax_init(GROUP * tq) for _ in range(KV_HEADS)))
    for kvh in range(KV_HEADS):
        o = _softmax_done(carries[kvh])
        for g in range(GROUP):
            sl = slice((kvh * GROUP + g) * HEAD_DIM, (kvh * GROUP + g + 1) * HEAD_DIM)
            o_ref[:, sl] = (o[g * tq:(g + 1) * tq] * sza_ref[:, sl]).astype(BF16)


def _dsa_prompt(iq, iw_t, ik2, qa, akv_b, sza, b, t, tq, tk):
    nqb = t // tq
    k_top = min(A_TOPK, t // 4)
    idx_bits = max(1, (t - 1).bit_length())
    row = lambda bi, i: (bi * nqb + i, 0)
    whole = lambda bi, i: (bi, 0, 0)
    kern = functools.partial(_dsa_prompt_kernel, tq=tq, tk=tk, k_top=k_top, idx_bits=idx_bits)
    return pl.pallas_call(
        kern,
        grid=(b, nqb),
        in_specs=[pl.BlockSpec((tq, IDX_COLS), row),
                  pl.BlockSpec((IDX_HEADS, tq), lambda bi, i: (0, bi * nqb + i)),
                  pl.BlockSpec((None, t, LANES), whole), pl.BlockSpec((tq, WIDTH), row),
                  pl.BlockSpec((None, t, KV_ROW), whole), pl.BlockSpec((tq, WIDTH), row)],
        out_specs=pl.BlockSpec((tq, WIDTH), row),
        out_shape=jax.ShapeDtypeStruct((b * t, WIDTH), BF16),
        scratch_shapes=[pltpu.VMEM((t // tk, tk, tq), F32), pltpu.VMEM((t // tk, tq, tk), F32)],
        compiler_params=_params(2),
        name="dsa_prompt",
    )(iq, iw_t, ik2, qa, akv_b, sza)


SLC_SHIFT = SLC_BLOCK.bit_length() - 1
assert 1 << SLC_SHIFT == SLC_BLOCK


def _block_rules(imp, blk, qpos):
    cur = lax.shift_right_logical(qpos, SLC_SHIFT)
    return jnp.where(blk == cur, -NEG, jnp.where(blk * SLC_BLOCK > qpos, NEG, imp))


def _split_dot(p, m):
    hi = p.astype(BF16)
    lo = (p - hi.astype(F32)).astype(BF16)
    return _dot(hi, m) + _dot(lo, m)


def _nsa_prompt_kernel(qb_ref, qr_ref, kc_ref, vc_ref, slc_ref, win_ref, ovt_ref, misc_ref, szb_ref,
                       o_ref, sc_ref, *, tq, tk, n_cmp, n_blk, n_sel):
    i = pl.program_id(1)
    nck = ((i + 1) * tq + tk - 1) // tk
    qpos = i * tq + lax.broadcasted_iota(jnp.int32, (tq, 1), 0)
    kiota = lax.broadcasted_iota(jnp.int32, (1, tk), 1)
    misc = misc_ref[...]
    ncp = kc_ref.shape[0]
    cidx = lax.broadcasted_iota(jnp.int32, (1, ncp), 1)
    cmask = jnp.where((CMP_STRIDE * cidx + CMP_LEN - 1 <= qpos) & (cidx < n_cmp), 1.0, 0.0)
    cmask = jnp.concatenate([cmask] * GROUP, axis=0)
    blk = lax.broadcasted_iota(jnp.int32, (LANES, 1), 0)
    qlane = i * tq + lax.broadcasted_iota(jnp.int32, (1, tq), 1)

    o_cmp = []
    for kvh in range(KV_HEADS):
        kc = kc_ref[:, kvh * HEAD_DIM:(kvh + 1) * HEAD_DIM]
        vc = vc_ref[:, kvh * HEAD_DIM:(kvh + 1) * HEAD_DIM]
        s = _dot_nt(_stack_heads(qb_ref, kvh), kc) * SCALE
        sm = jnp.where(cmask > 0.0, s, NEG)
        e = jnp.exp(sm - jnp.max(sm, axis=1, keepdims=True)) * cmask
        p = e / jnp.maximum(jnp.sum(e, axis=1, keepdims=True), 1e-30)
        o_cmp.append(_dot(p.astype(BF16), vc))
        psum = p[0:tq]
        for g in range(1, GROUP):
            psum = psum + p[g * tq:(g + 1) * tq]
        hi = psum.astype(BF16)
        lo = (psum - hi.astype(F32)).astype(BF16)
        imp = _dot_nt(ovt_ref[...], hi) + _dot_nt(ovt_ref[...], lo)
        imp = _block_rules(imp, blk, qlane)
        sc_ref[:, kvh * tq:(kvh + 1) * tq] = jnp.where(blk < n_blk, imp, -jnp.inf)

    thr, vmax = _topk_threshold_cols(lambda c: sc_ref[...], 1, n_sel, 7, KV_HEADS * tq, LANES)
    bmask_t = _topk_mask(sc_ref[...], blk, thr, vmax)
    bmask = [bmask_t[:, kvh * tq:(kvh + 1) * tq].T.astype(BF16) for kvh in range(KV_HEADS)]

    riota = lax.broadcasted_iota(jnp.int32, (LANES, 1), 0)
    nwk = WINDOW + tq
    w0 = pl.multiple_of(i * tq, tq)
    kposw = i * tq - WINDOW + lax.broadcasted_iota(jnp.int32, (1, nwk), 1)
    dw = qpos - kposw
    bias_w = jnp.where((dw >= 0) & (dw < WINDOW) & (kposw >= 0), 0.0, NEG)
    bias_w = jnp.concatenate([bias_w] * GROUP, axis=0)

    qs = [_stack_heads(qr_ref, kvh) for kvh in range(KV_HEADS)]

    def attend(c, carries):
        k0 = pl.multiple_of(c * tk, tk)
        kpos = k0 + kiota
        expand = jnp.where(riota == lax.shift_right_logical(kpos, SLC_SHIFT), 1.0, 0.0).astype(BF16)
        causal = jnp.where(kpos <= qpos, 1.0, 0.0)
        out = []
        for kvh in range(KV_HEADS):
            kcol, vcol = _kv_cols(kvh)
            sel = _dot(bmask[kvh], expand) * causal
            bias = jnp.concatenate([_mask_bias(sel)] * GROUP, axis=0)
            out.append(_softmax_step(carries[kvh], qs[kvh], slc_ref[pl.ds(k0, tk), kcol],
                                     slc_ref[pl.ds(k0, tk), vcol], bias))
        return tuple(out)

    slc_carries = lax.fori_loop(0, nck, attend, tuple(_softmax_init(GROUP * tq) for _ in range(KV_HEADS)))

    for kvh in range(KV_HEADS):
        q = qs[kvh]
        kcol, vcol = _kv_cols(kvh)
        o_slc = _softmax_done(slc_carries[kvh])
        o_win = _softmax_done(_softmax_step(_softmax_init(GROUP * tq), q, win_ref[pl.ds(w0, nwk), kcol],
                                            win_ref[pl.ds(w0, nwk), vcol], bias_w))
        for g in range(GROUP):
            hd = kvh * GROUP + g
            sl = slice(hd * HEAD_DIM, (hd + 1) * HEAD_DIM)
            rs = slice(g * tq, (g + 1) * tq)
            gate = [misc[:, MISC_IW + j * N_HEADS + hd:MISC_IW + j * N_HEADS + hd + 1] for j in range(3)]
            o_b = gate[0] * o_cmp[kvh][rs] + gate[1] * o_slc[rs] + gate[2] * o_win[rs]
            o_ref[:, sl] = (o_b * szb_ref[:, sl]).astype(BF16)


def _overlap_matrix(n_cmp_pad, n_cols, col_block):
    i = jnp.arange(n_cmp_pad)[:, None]
    j = col_block[None, :]
    ov = (CMP_STRIDE * i < SLC_BLOCK * (j + 1)) & (CMP_STRIDE * i + CMP_LEN > SLC_BLOCK * j) & (j >= 0)
    return ov.astype(BF16)


def _nsa_prompt(qb, qr, kc, vc, slc_b, win_b, misc, szb, b, t, tq, tk):
    nqb = t // tq
    n_cmp = t // CMP_STRIDE - 1
    n_blk = t // SLC_BLOCK
    assert n_blk <= LANES
    ncp = kc.shape[1]
    ov = _overlap_matrix(ncp, LANES, jnp.where(jnp.arange(LANES) < n_blk, jnp.arange(LANES), -1))
    row = lambda bi, i: (bi * nqb + i, 0)
    whole = lambda bi, i: (bi, 0, 0)
    kern = functools.partial(_nsa_prompt_kernel, tq=tq, tk=tk, n_cmp=n_cmp, n_blk=n_blk,
                             n_sel=min(N_SEL, n_blk))
    return pl.pallas_call(
        kern,
        grid=(b, nqb),
        in_specs=[pl.BlockSpec((tq, WIDTH), row), pl.BlockSpec((tq, WIDTH), row),
                  pl.BlockSpec((None, ncp, KV_HEADS * HEAD_DIM), whole),
                  pl.BlockSpec((None, ncp, KV_HEADS * HEAD_DIM), whole),
                  pl.BlockSpec((None, t, KV_ROW), whole), pl.BlockSpec((None, t + WINDOW, KV_ROW), whole),
                  pl.BlockSpec((LANES, ncp), lambda bi, i: (0, 0)),
                  pl.BlockSpec((tq, LANES), row), pl.BlockSpec((tq, WIDTH), row)],
        out_specs=pl.BlockSpec((tq, WIDTH), row),
        out_shape=jax.ShapeDtypeStruct((b * t, WIDTH), BF16),
        scratch_shapes=[pltpu.VMEM((LANES, KV_HEADS * tq), F32)],
        compiler_params=_params(2),
        name="nsa_prompt",
    )(qb, qr, kc, vc, slc_b, win_b, ov.T, misc, szb)


CHUNK_FLAT = CMP_STRIDE * KV_ROW
FS_COLS = KV_HEADS * 2 * 2 * CMP_HIDDEN


def _chunk_terms_body(x, w_ref, o_ref):
    for kc in range(KV_HEADS * 2):
        xk = jnp.concatenate(
            [x[:, p * KV_ROW + kc * HEAD_DIM: p * KV_ROW + (kc + 1) * HEAD_DIM] for p in range(CMP_STRIDE)],
            axis=1).astype(BF16)
        o_ref[:, kc * 2 * CMP_HIDDEN:(kc + 1) * 2 * CMP_HIDDEN] = _dot(xk, w_ref[kc % 2])


def _chunk_terms_dense_kernel(x_ref, w_ref, o_ref):
    _chunk_terms_body(x_ref[...], w_ref, o_ref)


def _chunk_terms_dense(x, w, tr):
    n = x.shape[0]
    return pl.pallas_call(
        _chunk_terms_dense_kernel,
        grid=(n // tr,),
        in_specs=[pl.BlockSpec((tr, CHUNK_FLAT), lambda i: (i, 0)),
                  pl.BlockSpec((2, CMP_STRIDE * HEAD_DIM, 2 * CMP_HIDDEN), lambda i: (0, 0, 0))],
        out_specs=pl.BlockSpec((tr, FS_COLS), lambda i: (i, 0)),
        out_shape=jax.ShapeDtypeStruct((n, FS_COLS), F32),
        compiler_params=_params(1),
        name="chunk_terms_dense",
    )(x, w)


ROWS_PER_TOKEN = KV_HEADS * 2
PAGE_ROWS = PAGE_SIZE * ROWS_PER_TOKEN
CHUNKS_PER_PAGE = PAGE_SIZE // CMP_STRIDE


def _cache_rows(cache):
    return cache.reshape(-1, HEAD_DIM)


def _token_rows(ref, slot, n_tokens):
    return ref[pl.ds(slot, n_tokens, stride=ROWS_PER_TOKEN), :]


def _sublane_transpose(tiles):
    tiles = list(tiles)
    sub = lax.broadcasted_iota(jnp.int32, (SUBLANES, LANES), 0)
    for d in (4, 2, 1):
        keep = (sub & d) == 0
        for i in range(SUBLANES):
            if i & d:
                continue
            lo, hi = tiles[i], tiles[i + d]
            tiles[i] = jnp.where(keep, lo, pltpu.roll(hi, d, 0))
            tiles[i + d] = jnp.where(keep, pltpu.roll(lo, SUBLANES - d, 0), hi)
    return tiles


def _chunk_terms_paged_kernel(pt_ref, *refs, pg):
    pages, w_ref, o_ref = refs[:pg], refs[pg], refs[pg + 1]
    chunk_rows = CMP_STRIDE * ROWS_PER_TOKEN
    assert CHUNKS_PER_PAGE == SUBLANES and chunk_rows % SUBLANES == 0
    pieces = [[] for _ in range(chunk_rows)]
    for pr in pages:
        for t in range(chunk_rows // SUBLANES):
            tiles = [pr[n * chunk_rows + t * SUBLANES:n * chunk_rows + (t + 1) * SUBLANES, :]
                     for n in range(CHUNKS_PER_PAGE)]
            for s, tile in enumerate(_sublane_transpose(tiles)):
                pieces[t * SUBLANES + s].append(tile)
    for kc in range(ROWS_PER_TOKEN):
        xk = jnp.concatenate([jnp.concatenate(pieces[p * ROWS_PER_TOKEN + kc], axis=0)
                              for p in range(CMP_STRIDE)], axis=1).astype(BF16)
        o_ref[:, kc * 2 * CMP_HIDDEN:(kc + 1) * 2 * CMP_HIDDEN] = _dot(xk, w_ref[kc % 2])


def _page_specs(block, pg):
    tail = (0,) * (len(block) - 1)

    def spec(j):
        return pl.BlockSpec(block, lambda bi, g, pt: (pt[bi, g * pg + j],) + tail)
    return [spec(j) for j in range(pg)]


def _chunk_terms_paged(pool, page_table, w, pg):
    db, n_pages = page_table.shape
    cpp = CHUNKS_PER_PAGE
    pool_v = _cache_rows(pool)
    kern = functools.partial(_chunk_terms_paged_kernel, pg=pg)
    grid_spec = pltpu.PrefetchScalarGridSpec(
        num_scalar_prefetch=1,
        grid=(db, n_pages // pg),
        in_specs=_page_specs((PAGE_ROWS, HEAD_DIM), pg)
                 + [pl.BlockSpec((2, CMP_STRIDE * HEAD_DIM, 2 * CMP_HIDDEN), lambda bi, g, pt: (0, 0, 0))],
        out_specs=pl.BlockSpec((None, pg * cpp, FS_COLS), lambda bi, g, pt: (bi, g, 0)),
    )
    return pl.pallas_call(
        kern, grid_spec=grid_spec,
        out_shape=jax.ShapeDtypeStruct((db, n_pages * cpp, FS_COLS), F32),
        compiler_params=_params(2),
        name="chunk_terms_paged",
    )(page_table, *([pool_v] * pg), w)


def _compress_mlp_kernel(fs_ref, new_ref, pe_ref, w1_ref, w2_ref, g_ref, kc_ref, vc_ref):
    n = fs_ref.shape[0]
    last = lax.broadcasted_iota(jnp.int32, (n, 1), 0) == n - 1
    for kc in range(KV_HEADS * 2):
        kvh, c = kc // 2, kc % 2
        base = kc * 2 * CMP_HIDDEN
        first = fs_ref[:, base:base + CMP_HIDDEN]
        second = fs_ref[:, base + CMP_HIDDEN:base + 2 * CMP_HIDDEN]
        shifted = jnp.where(last, new_ref[0:1, base + CMP_HIDDEN:base + 2 * CMP_HIDDEN],
                            pltpu.roll(second, n - 1, 0))
        pe = jnp.broadcast_to(pe_ref[c:c + 1, :], (SUBLANES, pe_ref.shape[1])).astype(BF16)
        bias = _dot(pe, w1_ref[c])[0:1, :]
        pre = first + shifted + bias
        hid = pre * jax.nn.sigmoid(pre)
        out = _dot(hid.astype(BF16), w2_ref[c])
        sl = slice(kvh * HEAD_DIM, (kvh + 1) * HEAD_DIM)
        if c == 0:
            kc_ref[:, sl] = _head_norm(out, g_ref[...]).astype(BF16)
        else:
            vc_ref[:, sl] = out.astype(BF16)


def _compress_mlp(fs, fs_new, pe, w1r, w2, g_cmp):
    nb, n, _ = fs.shape
    whole = lambda bi: (bi, 0, 0)
    fix2 = lambda bi: (0, 0)
    fix3 = lambda bi: (0, 0, 0)
    out = jax.ShapeDtypeStruct((nb, n, KV_HEADS * HEAD_DIM), BF16)
    return pl.pallas_call(
        _compress_mlp_kernel,
        grid=(nb,),
        in_specs=[pl.BlockSpec((None, n, FS_COLS), whole), pl.BlockSpec((None, SUBLANES, FS_COLS), whole),
                  pl.BlockSpec((2, CMP_LEN * HEAD_DIM), fix2),
                  pl.BlockSpec((2, CMP_LEN * HEAD_DIM, CMP_HIDDEN), fix3),
                  pl.BlockSpec((2, CMP_HIDDEN, HEAD_DIM), fix3), pl.BlockSpec((1, HEAD_DIM), fix2)],
        out_specs=[pl.BlockSpec((None, n, KV_HEADS * HEAD_DIM), whole)] * 2,
        out_shape=[out, out],
        compiler_params=_params(1),
        name="compress_mlp",
    )(fs, fs_new, pe, w1r, w2, g_cmp.reshape(1, HEAD_DIM))


def _idx_scores_kernel(pt_ref, *refs, pg, n_new):
    pages = refs[:pg]
    iq_ref, iw_ref, new_ref, o_ref, onew_ref = refs[pg:pg + 5]
    g = pl.program_id(1)
    iq = iq_ref[...]
    iw = iw_ref[...]
    row = lax.broadcasted_iota(jnp.int32, (SAMPLE_Q, 1), 0)

    def scores(keys_t, first_idx):
        s = jnp.maximum(_dot(iq, keys_t), 0.0)
        n = s.shape[1]
        w = s * jnp.concatenate([iw] * (n // LANES), axis=1)
        acc = w[0:SAMPLE_Q]
        for hd in range(1, IDX_HEADS):
            acc = acc + w[hd * SAMPLE_Q:(hd + 1) * SAMPLE_Q]
        ramp = -(first_idx + lax.broadcasted_iota(jnp.int32, (1, n), 1)).astype(F32)
        return jnp.where(row < n_new, acc, ramp)

    cw = pg * PAGE_SIZE
    o_ref[...] = scores(jnp.concatenate([p[...] for p in pages], axis=1).astype(BF16), g * cw)

    @pl.when(g == pl.num_programs(1) - 1)
    def _():
        onew_ref[...] = scores(new_ref[...], pl.num_programs(1) * cw)


def _idx_scores(pool_idx_t, page_table, iqm, iwm, ik_new_t, pg, n_new):
    db, n_pages = page_table.shape
    ng = n_pages // pg
    kern = functools.partial(_idx_scores_kernel, pg=pg, n_new=n_new)
    per_b = lambda bi, g, pt: (bi, 0, 0)
    grid_spec = pltpu.PrefetchScalarGridSpec(
        num_scalar_prefetch=1,
        grid=(db, ng),
        in_specs=_page_specs((None, IDX_DIM, PAGE_SIZE), pg)
                 + [pl.BlockSpec((None, IDX_HEADS * SAMPLE_Q, IDX_DIM), per_b),
                    pl.BlockSpec((None, IDX_HEADS * SAMPLE_Q, LANES), per_b),
                    pl.BlockSpec((None, IDX_DIM, LANES), per_b)],
        out_specs=[pl.BlockSpec((None, SAMPLE_Q, pg * PAGE_SIZE), lambda bi, g, pt: (g, bi, 0)),
                   pl.BlockSpec((SAMPLE_Q, LANES), lambda bi, g, pt: (bi, 0))],
    )
    return pl.pallas_call(
        kern, grid_spec=grid_spec,
        out_shape=[jax.ShapeDtypeStruct((ng, db * SAMPLE_Q, pg * PAGE_SIZE), F32),
                   jax.ShapeDtypeStruct((db * SAMPLE_Q, LANES), F32)],
        compiler_params=_params(2),
        name="idx_scores",
    )(page_table, *([pool_idx_t] * pg), iqm, iwm, ik_new_t)


TOPK_ROWS = 64


def _dsa_topk_kernel(sc_ref, new_ref, o_ref, onew_ref, *, ng, cw, n_new, k_top, idx_bits, rb):
    row = lax.broadcasted_iota(jnp.int32, (rb, 1), 0)
    q = row & (SAMPLE_Q - 1)
    lane = lax.broadcasted_iota(jnp.int32, (1, LANES), 1)
    lane_cw = lax.broadcasted_iota(jnp.int32, (1, cw), 1)
    ok_new = (lane <= q) & (lane < n_new)
    x_new = jnp.where(ok_new, new_ref[...], NEG)
    idx_new = ng * cw + lane

    thr, vmax = _topk_threshold(lambda c, r0, nr: sc_ref[c, r0:r0 + nr, :], ng, k_top, idx_bits, rb, cw,
                                tail=(x_new, idx_new))
    for g in range(ng):
        o_ref[g] = _mask_bias(_topk_mask(sc_ref[g], g * cw + lane_cw, thr, vmax))
    onew_ref[...] = _mask_bias(_topk_mask(x_new, idx_new, thr, vmax) * jnp.where(ok_new, 1.0, 0.0))


def _dsa_topk(scores, scores_new, past, n_new):
    ng, rows, cw = scores.shape
    rb = min(TOPK_ROWS, rows)
    total = past + n_new
    kern = functools.partial(_dsa_topk_kernel, ng=ng, cw=cw, n_new=n_new, k_top=min(A_TOPK, total // 4),
                             idx_bits=(ng * cw + LANES - 1).bit_length(), rb=rb)
    big = pl.BlockSpec((ng, rb, cw), lambda r: (0, r, 0))
    small = pl.BlockSpec((rb, LANES), lambda r: (r, 0))
    return pl.pallas_call(
        kern,
        grid=(rows // rb,),
        in_specs=[big, small],
        out_specs=[big, small],
        out_shape=[jax.ShapeDtypeStruct((ng, rows, cw), F32), jax.ShapeDtypeStruct((rows, LANES), F32)],
        compiler_params=_params(1),
        name="dsa_topk",
    )(scores, scores_new)


QROWS = GROUP * SAMPLE_Q


def _paged_attn_kernel(pt_ref, *refs, pg, mask_rows):
    pages = refs[:pg]
    q_ref, mk_ref, mknew_ref, new_ref, o_ref, m_ref, l_ref, acc_ref = refs[pg:pg + 8]
    g = pl.program_id(1)

    @pl.when(g == 0)
    def _():
        m_ref[...] = jnp.full(m_ref.shape, NEG, F32)
        l_ref[...] = jnp.zeros(l_ref.shape, F32)
        acc_ref[...] = jnp.zeros(acc_ref.shape, F32)

    def update(keys, values, bias_all):
        carries = [(m_ref[kvh], l_ref[kvh], acc_ref[kvh]) for kvh in range(KV_HEADS)]
        for kvh in range(KV_HEADS):
            r0 = kvh * SAMPLE_Q if mask_rows == KV_HEADS * SAMPLE_Q else 0
            bias = jnp.concatenate([bias_all[r0:r0 + SAMPLE_Q]] * GROUP, axis=0)
            carries[kvh] = _softmax_step(carries[kvh], q_ref[kvh], keys(kvh), values(kvh), bias)
        for kvh in range(KV_HEADS):
            m_ref[kvh], l_ref[kvh], acc_ref[kvh] = carries[kvh]

    def paged(slot):
        return jnp.concatenate([_token_rows(p, slot, PAGE_SIZE) for p in pages], axis=0).astype(BF16)

    update(lambda kvh: paged(2 * kvh), lambda kvh: paged(2 * kvh + 1), mk_ref[...])

    @pl.when(g == pl.num_programs(1) - 1)
    def _():
        update(lambda kvh: new_ref[:, _kv_cols(kvh)[0]], lambda kvh: new_ref[:, _kv_cols(kvh)[1]], mknew_ref[...])
        for kvh in range(KV_HEADS):
            o_ref[kvh] = _softmax_done((m_ref[kvh], l_ref[kvh], acc_ref[kvh]))


def _paged_attn(pool, page_table, q, bias, bias_new, kv_new, pg):
    db, n_pages = page_table.shape
    ng = n_pages // pg
    mask_rows = bias.shape[1] // db
    pool_v = _cache_rows(pool)
    kern = functools.partial(_paged_attn_kernel, pg=pg, mask_rows=mask_rows)
    per_b3 = lambda bi, g, pt: (bi, 0, 0)
    per_b4 = lambda bi, g, pt: (bi, 0, 0, 0)
    grid_spec = pltpu.PrefetchScalarGridSpec(
        num_scalar_prefetch=1,
        grid=(db, ng),
        in_specs=_page_specs((PAGE_ROWS, HEAD_DIM), pg)
                 + [pl.BlockSpec((None, KV_HEADS, QROWS, HEAD_DIM), per_b4),
                    pl.BlockSpec((None, mask_rows, pg * PAGE_SIZE), lambda bi, g, pt: (g, bi, 0)),
                    pl.BlockSpec((mask_rows, LANES), lambda bi, g, pt: (bi, 0)),
                    pl.BlockSpec((None, LANES, KV_ROW), per_b3)],
        out_specs=pl.BlockSpec((None, KV_HEADS, QROWS, HEAD_DIM), per_b4),
        scratch_shapes=[pltpu.VMEM((KV_HEADS, QROWS, 1), F32), pltpu.VMEM((KV_HEADS, QROWS, 1), F32),
                        pltpu.VMEM((KV_HEADS, QROWS, HEAD_DIM), F32)],
    )
    return pl.pallas_call(
        kern, grid_spec=grid_spec,
        out_shape=jax.ShapeDtypeStruct((db, KV_HEADS, QROWS, HEAD_DIM), F32),
        compiler_params=_params(2),
        name="paged_attn",
    )(page_table, *([pool_v] * pg), q, bias, bias_new, kv_new)


BLOCKS_PER_PAGE = PAGE_SIZE // SLC_BLOCK
BLOCK_ROWS = SLC_BLOCK * ROWS_PER_TOKEN
assert BLOCKS_PER_PAGE == 2


def _slc_gather_kernel(pt_ref, sel_ref, *refs, nsel, ds, kvh):
    blocks = refs[:nsel]
    q_ref, bnew_ref, new_ref, o_ref = refs[nsel:nsel + 4]
    bi, qi = pl.program_id(0), pl.program_id(1)
    lane = lax.broadcasted_iota(jnp.int32, (1, LANES), 1)
    mine = (lax.broadcasted_iota(jnp.int32, (QROWS, 1), 0) & (SAMPLE_Q - 1)) == qi

    @pl.when(qi == 0)
    def _():
        o_ref[...] = jnp.zeros(o_ref.shape, F32)

    base = ((bi * ds + qi) * KV_HEADS + kvh) * nsel
    kcol, vcol = _kv_cols(kvh)
    keys = jnp.concatenate([_token_rows(blk, 2 * kvh, SLC_BLOCK) for blk in blocks], axis=0).astype(BF16)
    vals = jnp.concatenate([_token_rows(blk, 2 * kvh + 1, SLC_BLOCK) for blk in blocks], axis=0).astype(BF16)
    slot_bias = [jnp.where(sel_ref[base + j] >= 0, 0.0, NEG) for j in range(nsel)]
    pairs = [jnp.where(lane < SLC_BLOCK, slot_bias[j], slot_bias[min(j + 1, nsel - 1)])
             for j in range(0, nsel, 2)]
    bias = jnp.concatenate(pairs, axis=1)[:, :nsel * SLC_BLOCK]
    carry = _softmax_step(_softmax_init(QROWS), q_ref[...], keys, vals, bias)
    bnew = jnp.concatenate([bnew_ref[kvh * SAMPLE_Q:(kvh + 1) * SAMPLE_Q]] * GROUP, axis=0)
    carry = _softmax_step(carry, q_ref[...], new_ref[:, kcol], new_ref[:, vcol], bnew)
    o_ref[...] = jnp.where(mine, _softmax_done(carry), o_ref[...])


def _slc_gather(pool, page_table, sel, q, bias_new, kv_new, nsel, ds):
    db = page_table.shape[0]
    pool_v = _cache_rows(pool)
    outs = []
    for kvh in range(KV_HEADS):
        def spec(j, kvh=kvh):
            def imap(bi, qi, pt, sl):
                blk = jnp.maximum(sl[((bi * ds + qi) * KV_HEADS + kvh) * nsel + j], 0)
                return (pt[bi, lax.shift_right_logical(blk, 1)] * BLOCKS_PER_PAGE + (blk & 1), 0)
            return pl.BlockSpec((BLOCK_ROWS, HEAD_DIM), imap)

        grid_spec = pltpu.PrefetchScalarGridSpec(
            num_scalar_prefetch=2,
            grid=(db, ds),
            in_specs=[spec(j) for j in range(nsel)]
                     + [pl.BlockSpec((None, None, QROWS, HEAD_DIM), lambda bi, qi, pt, sl, kvh=kvh: (bi, kvh, 0, 0)),
                        pl.BlockSpec((KV_HEADS * SAMPLE_Q, LANES), lambda bi, qi, pt, sl: (bi, 0)),
                        pl.BlockSpec((None, LANES, KV_ROW), lambda bi, qi, pt, sl: (bi, 0, 0))],
            out_specs=pl.BlockSpec((None, QROWS, HEAD_DIM), lambda bi, qi, pt, sl: (bi, 0, 0)),
        )
        outs.append(pl.pallas_call(
            functools.partial(_slc_gather_kernel, nsel=nsel, ds=ds, kvh=kvh), grid_spec=grid_spec,
            out_shape=jax.ShapeDtypeStruct((db, QROWS, HEAD_DIM), F32),
            compiler_params=_params(2),
            name="slc_gather",
        )(page_table, sel, *([pool_v] * nsel), q, bias_new, kv_new))
    return jnp.stack(outs, axis=1)


def _nsa_sample_kernel(q_ref, kc_ref, vc_ref, ov_ref, o_ref, sel_ref, tmnew_ref, sc_ref,
                       *, past, n_new, n_cmp, n_blk, n_sel, ng, gb):
    row = lax.broadcasted_iota(jnp.int32, (SAMPLE_Q, 1), 0)
    qpos = past + row
    ncp = kc_ref.shape[0]
    cidx = lax.broadcasted_iota(jnp.int32, (1, ncp), 1)
    cmask8 = jnp.where((CMP_STRIDE * cidx + CMP_LEN - 1 <= qpos) & (cidx < n_cmp), 1.0, 0.0)
    cmask = jnp.concatenate([cmask8] * GROUP, axis=0)
    lane = lax.broadcasted_iota(jnp.int32, (1, LANES), 1)

    for kvh in range(KV_HEADS):
        kc = kc_ref[:, kvh * HEAD_DIM:(kvh + 1) * HEAD_DIM]
        vc = vc_ref[:, kvh * HEAD_DIM:(kvh + 1) * HEAD_DIM]
        s = _dot_nt(q_ref[kvh], kc) * SCALE
        sm = jnp.where(cmask > 0.0, s, NEG)
        e = jnp.exp(sm - jnp.max(sm, axis=1, keepdims=True)) * cmask
        p = e / jnp.maximum(jnp.sum(e, axis=1, keepdims=True), 1e-30)
        o_ref[kvh] = _dot(p.astype(BF16), vc)
        psum = p[0:SAMPLE_Q]
        for g in range(1, GROUP):
            psum = psum + p[g * SAMPLE_Q:(g + 1) * SAMPLE_Q]
        imp = _split_dot(psum, ov_ref[...])
        for g in range(ng + 1):
            blk = g * gb + lane
            v = _block_rules(imp[:, g * LANES:(g + 1) * LANES], blk, qpos)
            v = jnp.where(row < n_new, v, -blk.astype(F32))
            sc_ref[g, kvh * SAMPLE_Q:(kvh + 1) * SAMPLE_Q, :] = jnp.where((lane < gb) & (blk < n_blk), v, -jnp.inf)

    rows = KV_HEADS * SAMPLE_Q
    idx_bits = ((ng + 1) * LANES - 1).bit_length()
    thr, vmax = _topk_threshold(lambda c, r0, nr: sc_ref[c, r0:r0 + nr, :], ng + 1, n_sel, idx_bits, rows, LANES)
    qpos2 = jnp.concatenate([qpos] * KV_HEADS, axis=0)
    n_past_blk = past // SLC_BLOCK
    tri = jnp.where(lax.broadcasted_iota(jnp.int32, (LANES, 1), 0) <= lane, 1.0, 0.0).astype(BF16)
    offs = jnp.zeros((rows, 1), F32)
    slots = [jnp.zeros((rows, 1), F32) for _ in range(n_sel)]
    for g in range(ng + 1):
        blk = g * gb + lane
        bm = _topk_mask(sc_ref[g], g * LANES + lane, thr, vmax)
        if g == ng:
            ok = jnp.where((past + lane <= qpos2) & (lane < n_new), 1.0, 0.0)
            tmnew_ref[...] = _mask_bias(bm[:, 0:1] * ok)
        bm = bm * jnp.where((lane < gb) & (blk < n_past_blk), 1.0, 0.0)
        rank = offs + _dot(bm.astype(BF16), tri)
        for j in range(n_sel):
            hit = jnp.where(bm > 0.0, jnp.where(rank == j + 1.0, blk.astype(F32), 0.0), 0.0)
            slots[j] = slots[j] + jnp.sum(hit, axis=1, keepdims=True)
        offs = offs + jnp.sum(bm, axis=1, keepdims=True)
    out = jnp.full((rows, LANES), -1.0, F32)
    for j in range(n_sel):
        out = jnp.where((lane == j) & (offs > j), slots[j], out)
    sel_ref[...] = out.astype(jnp.int32)


def _nsa_sample(q, kc, vc, past, n_new, ng, cw):
    db = q.shape[0]
    ncp = kc.shape[1]
    total = past + n_new
    n_blk = -(-total // SLC_BLOCK)
    n_cmp = total_chunks(past, n_new) - 1
    gb = cw // SLC_BLOCK
    assert gb <= LANES and past % cw == 0
    col = jnp.arange((ng + 1) * LANES)
    col_block = jnp.where(col % LANES < gb, (col // LANES) * gb + col % LANES, -1)
    col_block = jnp.where(col_block < n_blk, col_block, -1)
    ov = _overlap_matrix(ncp, (ng + 1) * LANES, col_block)
    kern = functools.partial(_nsa_sample_kernel, past=past, n_new=n_new, n_cmp=n_cmp, n_blk=n_blk,
                             n_sel=min(N_SEL, n_blk), ng=ng, gb=gb)
    rows = KV_HEADS * SAMPLE_Q
    b3 = lambda bi: (bi, 0, 0)
    b4 = lambda bi: (bi, 0, 0, 0)
    return pl.pallas_call(
        kern,
        grid=(db,),
        in_specs=[pl.BlockSpec((None, KV_HEADS, QROWS, HEAD_DIM), b4),
                  pl.BlockSpec((None, ncp, KV_HEADS * HEAD_DIM), b3),
                  pl.BlockSpec((None, ncp, KV_HEADS * HEAD_DIM), b3),
                  pl.BlockSpec((ncp, (ng + 1) * LANES), lambda bi: (0, 0))],
        out_specs=[pl.BlockSpec((None, KV_HEADS, QROWS, HEAD_DIM), b4),
                   pl.BlockSpec((rows, LANES), lambda bi: (bi, 0)),
                   pl.BlockSpec((rows, LANES), lambda bi: (bi, 0))],
        out_shape=[jax.ShapeDtypeStruct((db, KV_HEADS, QROWS, HEAD_DIM), F32),
                   jax.ShapeDtypeStruct((db * rows, LANES), jnp.int32),
                   jax.ShapeDtypeStruct((db * rows, LANES), F32)],
        scratch_shapes=[pltpu.VMEM((ng + 1, rows, LANES), F32)],
        compiler_params=_params(1),
        name="nsa_sample",
    )(q, kc, vc, ov)


def total_chunks(past, n_new):
    return past // CMP_STRIDE + -(-n_new // CMP_STRIDE)


def _win_sample_kernel(q_ref, st_ref, new_ref, o_ref, *, w_buf, n_new):
    row = lax.broadcasted_iota(jnp.int32, (SAMPLE_Q, 1), 0)
    row = jnp.concatenate([row] * GROUP, axis=0)
    j_old = lax.broadcasted_iota(jnp.int32, (1, w_buf), 1)
    j_new = w_buf + lax.broadcasted_iota(jnp.int32, (1, LANES), 1)
    def bias(j, valid):
        d = w_buf + row - j
        return jnp.where((d >= 0) & (d < WINDOW) & valid, 0.0, NEG)
    b_old, b_new = bias(j_old, True), bias(j_new, j_new < w_buf + n_new)
    for kvh in range(KV_HEADS):
        q = q_ref[kvh]
        kcol, vcol = _kv_cols(kvh)
        carry = _softmax_step(_softmax_init(QROWS), q, _token_rows(st_ref, 2 * kvh, w_buf).astype(BF16),
                              _token_rows(st_ref, 2 * kvh + 1, w_buf).astype(BF16), b_old)
        carry = _softmax_step(carry, q, new_ref[:, kcol], new_ref[:, vcol], b_new)
        o_ref[kvh] = _softmax_done(carry)


def _win_sample(q, state, kv_new, n_new):
    db, w_buf = state.shape[0], state.shape[1]
    kern = functools.partial(_win_sample_kernel, w_buf=w_buf, n_new=n_new)
    b3 = lambda bi: (bi, 0, 0)
    b4 = lambda bi: (bi, 0, 0, 0)
    return pl.pallas_call(
        kern,
        grid=(db,),
        in_specs=[pl.BlockSpec((None, KV_HEADS, QROWS, HEAD_DIM), b4),
                  pl.BlockSpec((w_buf * ROWS_PER_TOKEN, HEAD_DIM), lambda bi: (bi, 0)),
                  pl.BlockSpec((None, LANES, KV_ROW), b3)],
        out_specs=pl.BlockSpec((None, KV_HEADS, QROWS, HEAD_DIM), b4),
        out_shape=jax.ShapeDtypeStruct((db, KV_HEADS, QROWS, HEAD_DIM), F32),
        compiler_params=_params(1),
        name="win_sample",
    )(q, _cache_rows(state), kv_new)


def _gate_merge_kernel(oa_ref, oc_ref, os_ref, ow_ref, misc_ref, sza_ref, szb_ref, a_ref, b_ref):
    misc = misc_ref[...]
    a_ref[...] = (oa_ref[...] * sza_ref[...]).astype(BF16)
    for hd in range(N_HEADS):
        sl = slice(hd * HEAD_DIM, (hd + 1) * HEAD_DIM)
        g = [misc[:, MISC_IW + j * N_HEADS + hd:MISC_IW + j * N_HEADS + hd + 1] for j in range(3)]
        o_b = g[0] * oc_ref[:, sl] + g[1] * os_ref[:, sl] + g[2] * ow_ref[:, sl]
        b_ref[:, sl] = (o_b * szb_ref[:, sl]).astype(BF16)


def _gate_merge(o_a, o_cmp, o_slc, o_win, misc, sza, szb):
    n = o_a.shape[0]
    full = lambda w: pl.BlockSpec((n, w), lambda i: (0, 0))
    out = jax.ShapeDtypeStruct((n, WIDTH), BF16)
    return pl.pallas_call(
        _gate_merge_kernel,
        grid=(1,),
        in_specs=[full(WIDTH)] * 4 + [full(LANES), full(WIDTH), full(WIDTH)],
        out_specs=[full(WIDTH)] * 2,
        out_shape=[out, out],
        compiler_params=_params(1),
        name="gate_merge",
    )(o_a, o_cmp, o_slc, o_win, misc, sza, szb)


def _merge_kernel(a_ref, b_ref, h_ref, wa_ref, wb_ref, wg0_ref, wg1_ref, o_ref):
    h = h_ref[...]
    y_a = _dot(a_ref[...], wa_ref[...])
    y_b = _dot(b_ref[...], wb_ref[...])
    g0 = jax.nn.sigmoid(_dot(h, wg0_ref[...]))
    g1 = jax.nn.sigmoid(_dot(h, wg1_ref[...]))
    o_ref[...] = (g0 * y_a + g1 * y_b).astype(BF16)


def _merge(a, bm, h, w_a, w_b, w_g0, w_g1, tr, cb):
    n = a.shape[0]
    row = lambda j, i: (i, 0)
    col = lambda j, i: (0, j)
    return pl.pallas_call(
        _merge_kernel,
        grid=(D_MODEL // cb, n // tr),
        in_specs=[pl.BlockSpec((tr, WIDTH), row), pl.BlockSpec((tr, WIDTH), row),
                  pl.BlockSpec((tr, D_MODEL), row),
                  pl.BlockSpec((WIDTH, cb), col), pl.BlockSpec((WIDTH, cb), col),
                  pl.BlockSpec((D_MODEL, cb), col), pl.BlockSpec((D_MODEL, cb), col)],
        out_specs=pl.BlockSpec((tr, cb), lambda j, i: (i, j)),
        out_shape=jax.ShapeDtypeStruct((n, D_MODEL), BF16),
        compiler_params=_params(2),
        name="merge",
    )(a, bm, h, w_a, w_b, w_g0, w_g1)


def _out_proj_kernel(x_ref, m_ref, w_ref, o_ref):
    o_ref[...] = x_ref[...] + _dot(m_ref[...], w_ref[...])


def _out_proj(x, merged, w_out, tr):
    n = x.shape[0]
    row = lambda i: (i, 0)
    return pl.pallas_call(
        _out_proj_kernel,
        grid=(n // tr,),
        in_specs=[pl.BlockSpec((tr, D_MODEL), row), pl.BlockSpec((tr, D_MODEL), row),
                  pl.BlockSpec((D_MODEL, D_MODEL), lambda i: (0, 0))],
        out_specs=pl.BlockSpec((tr, D_MODEL), row),
        out_shape=jax.ShapeDtypeStruct((n, D_MODEL), F32),
        compiler_params=_params(1),
        name="out_proj",
    )(x, merged, w_out)


def _project(x2d, pos_period, tr, wts):
    n = x2d.shape[0]
    n_tab = pos_period.shape[0] // tr
    tabs_head = _rope_tables(pos_period, ROT_DIM, HEAD_DIM)
    tabs_iq = _rope_tables(pos_period, IDX_ROT_DIM, IDX_DIM)
    tabs_iq = tuple(jnp.concatenate([t, t], axis=1) for t in tabs_iq)
    ident = (jnp.ones_like(tabs_iq[0][:, :IDX_DIM]), jnp.zeros_like(tabs_iq[0][:, :IDX_DIM]),
             jnp.zeros_like(tabs_iq[0][:, :IDX_DIM]))
    tabs_misc = tuple(jnp.concatenate([t[:, :IDX_DIM], e], axis=1) for t, e in zip(tabs_iq, ident))

    h = _rmsnorm(x2d, wts["norm_gain"], tr)
    (qa,) = _proj_q(h, wts["w_qa"], wts["q_norm_a"], tabs_head, tr, n_tab, False)
    qb_rot, qb = _proj_q(h, wts["w_qb"], wts["q_norm_b"], tabs_head, tr, n_tab, True)
    akv, cmp_kv, slc, win, akv_b, slc_b, win_b = _proj_kv(
        h, wts["w_kv"], wts["k_norm_a"], wts["k_norm_slc"], wts["k_norm_win"], tabs_head, tr, n_tab)
    iq, misc = _proj_idx(h, wts["w_idx"], tabs_iq, tabs_misc, tr, n_tab)
    sza, szb = _proj_z(h, wts["w_z"], tr)
    return dict(h=h, qa=qa, qb=qb, qb_rot=qb_rot, akv=akv, cmp=cmp_kv, slc=slc, win=win,
                akv_b=akv_b, slc_b=slc_b, win_b=win_b, iq=iq, misc=misc, sza=sza, szb=szb)


def _sample_rows(a, db, ds):
    a = a.reshape(db, ds, KV_HEADS, GROUP, HEAD_DIM).transpose(0, 2, 3, 1, 4)
    a = jnp.pad(a, ((0, 0), (0, 0), (0, 0), (0, SAMPLE_Q - ds), (0, 0)))
    return a.reshape(db, KV_HEADS, QROWS, HEAD_DIM)


def _unsample_rows(o, db, ds):
    o = o.reshape(db, KV_HEADS, GROUP, SAMPLE_Q, HEAD_DIM)[:, :, :, :ds]
    return o.transpose(0, 3, 1, 2, 4).reshape(db * ds, WIDTH)


def _pad_new(a, db, ds):
    return jnp.pad(a.reshape(db, ds, a.shape[-1]), ((0, 0), (0, LANES - ds), (0, 0)))


def kernel(x_prompt, x_sample, cache_a_kv, cache_a_idx, cache_cmp_kv, cache_slc_kv, state_win_kv,
           page_table, norm_gain, w_in, q_norm_a, k_norm_a, q_norm_b, k_norm_cmp, k_norm_slc,
           k_norm_win, pe_cmp, w1_cmp, w2_cmp, w_proj_a, w_proj_b, w_out):
    b, t, _ = x_prompt.shape
    db, ds, _ = x_sample.shape
    n_pages = page_table.shape[1]
    past = n_pages * PAGE_SIZE
    assert ds <= SAMPLE_Q and ds <= CMP_STRIDE

    sizes = (WIDTH, KV_ROW, IDX_COLS, IDX_DIM, IDX_HEADS, WIDTH, WIDTH, 3 * KV_ROW, 3 * N_HEADS, WIDTH,
             2 * D_MODEL)
    offs = [0]
    for s in sizes:
        offs.append(offs[-1] + s)
    wb = w_in.astype(BF16)
    seg = lambda j: wb[:, offs[j]:offs[j + 1]]
    misc_pad = jnp.zeros((D_MODEL, LANES - (IDX_DIM + IDX_HEADS + 3 * N_HEADS)), BF16)
    wts = dict(
        norm_gain=norm_gain, q_norm_a=q_norm_a, q_norm_b=q_norm_b, k_norm_a=k_norm_a,
        k_norm_slc=k_norm_slc, k_norm_win=k_norm_win,
        w_qa=seg(0), w_qb=seg(6),
        w_kv=jnp.concatenate([seg(1), seg(7)], axis=1),
        w_idx=jnp.concatenate([seg(2), seg(3), seg(4), seg(8), misc_pad], axis=1),
        w_z=jnp.concatenate([seg(5), seg(9)], axis=1),
    )
    w_g0 = wb[:, offs[10]:offs[10] + D_MODEL]
    w_g1 = wb[:, offs[10] + D_MODEL:offs[11]]
    w_pa, w_pb, w_o = w_proj_a.astype(BF16), w_proj_b.astype(BF16), w_out.astype(BF16)
    w1_fs = jnp.concatenate([w1_cmp[:, :CMP_STRIDE].reshape(2, CMP_STRIDE * HEAD_DIM, CMP_HIDDEN),
                             w1_cmp[:, CMP_STRIDE:].reshape(2, CMP_STRIDE * HEAD_DIM, CMP_HIDDEN)],
                            axis=2).astype(BF16)
    w1_r = w1_cmp.reshape(2, CMP_LEN * HEAD_DIM, CMP_HIDDEN).astype(BF16)
    pe_r = pe_cmp.reshape(2, CMP_LEN * HEAD_DIM)
    w2_b = w2_cmp.astype(BF16)

    tr = min(512, t)
    tq = min(256, t)
    tk = min(512, t)
    xp = x_prompt.reshape(b * t, D_MODEL)
    pp = _project(xp, jnp.arange(t, dtype=jnp.int32), tr, wts)

    pages_p = t // PAGE_SIZE
    table_p = jnp.arange(b * pages_p, dtype=jnp.int32).reshape(b, pages_p)
    fs_p = _chunk_terms_paged(pp["cmp"], table_p, w1_fs, min(PAGES_PER_STEP, pages_p))
    kc_p, vc_p = _compress_mlp(fs_p, jnp.zeros((b, SUBLANES, FS_COLS), F32), pe_r, w1_r, w2_b, k_norm_cmp)

    ik_b = pp["misc"][:, :IDX_DIM].astype(BF16)
    ik2 = jnp.concatenate([ik_b, ik_b], axis=1).reshape(b, t, LANES)
    iw_t = pp["misc"][:, MISC_IK:MISC_IW].T
    a_p = _dsa_prompt(pp["iq"], iw_t, ik2, pp["qa"], pp["akv_b"].reshape(b, t, KV_ROW), pp["sza"],
                      b, t, tq, tk)
    win_front = jnp.pad(pp["win_b"].reshape(b, t, KV_ROW), ((0, 0), (WINDOW, 0), (0, 0)))
    b_p = _nsa_prompt(pp["qb"], pp["qb_rot"], kc_p, vc_p, pp["slc_b"].reshape(b, t, KV_ROW),
                      win_front, pp["misc"], pp["szb"], b, t, tq, tk)
    merged_p = _merge(a_p, b_p, pp["h"], w_pa, w_pb, w_g0, w_g1, tr, 1024)
    y_prompt = _out_proj(xp, merged_p, w_o, tr).reshape(b, t, D_MODEL)

    kv5 = lambda a, n0, n1: a.reshape(n0, n1, KV_HEADS, 2, HEAD_DIM)
    p_a_kv = kv5(pp["akv"], b, t)
    p_a_idx = pp["misc"][:, :IDX_DIM].reshape(b, t, IDX_DIM)
    p_cmp_kv = kv5(pp["cmp"], b, t)
    p_slc_kv = kv5(pp["slc"], b, t)
    p_win_kv = kv5(pp["win"], b, t)[:, t - min(WINDOW, t):]

    ns = db * ds
    xs = x_sample.reshape(ns, D_MODEL)
    pos_s = past + jnp.tile(jnp.arange(ds, dtype=jnp.int32), db)
    ps = _project(xs, pos_s, ns, wts)
    pg = min(PAGES_PER_STEP, n_pages)
    ng = n_pages // pg
    cw = pg * PAGE_SIZE

    iq_s = ps["iq"].reshape(db, ds, IDX_HEADS, IDX_DIM).transpose(0, 2, 1, 3)
    iqm = jnp.pad(iq_s, ((0, 0), (0, 0), (0, SAMPLE_Q - ds), (0, 0))).reshape(db, IDX_HEADS * SAMPLE_Q, IDX_DIM)
    iw_s = ps["misc"][:, MISC_IK:MISC_IW].reshape(db, ds, IDX_HEADS).transpose(0, 2, 1) * (IDX_DIM ** -0.5)
    iwm = jnp.pad(iw_s, ((0, 0), (0, 0), (0, SAMPLE_Q - ds))).reshape(db, IDX_HEADS * SAMPLE_Q, 1)
    iwm = jnp.broadcast_to(iwm, (db, IDX_HEADS * SAMPLE_Q, LANES))
    ik_new_t = jnp.swapaxes(_pad_new(ps["misc"][:, :IDX_DIM].astype(BF16), db, ds), 1, 2)
    sc_s, sc_new = _idx_scores(jnp.swapaxes(cache_a_idx, 1, 2), page_table, iqm, iwm, ik_new_t, pg, ds)
    mk_a, mk_a_new = _dsa_topk(sc_s, sc_new, past, ds)
    o_a_s = _paged_attn(cache_a_kv, page_table, _sample_rows(ps["qa"], db, ds), mk_a, mk_a_new,
                        _pad_new(ps["akv_b"], db, ds), pg)

    fs_past = _chunk_terms_paged(cache_cmp_kv, page_table, w1_fs, pg)
    new_chunk = jnp.pad(ps["cmp"].reshape(db, ds, KV_ROW), ((0, 0), (0, CMP_STRIDE - ds), (0, 0)))
    fs_new = _chunk_terms_dense(new_chunk.reshape(db, CHUNK_FLAT), w1_fs, db)
    fs_new = jnp.pad(fs_new.reshape(db, 1, FS_COLS), ((0, 0), (0, SUBLANES - 1), (0, 0)))
    kc_s, vc_s = _compress_mlp(fs_past, fs_new, pe_r, w1_r, w2_b, k_norm_cmp)
    o_cmp_s, sel_s, mk_s_new = _nsa_sample(_sample_rows(ps["qb"], db, ds), kc_s, vc_s, past, ds, ng, cw)
    q_rot_s = _sample_rows(ps["qb_rot"], db, ds)
    n_sel_s = min(N_SEL, -(-(past + ds) // SLC_BLOCK))
    sel_s = sel_s[:, :n_sel_s].reshape(db, KV_HEADS, SAMPLE_Q, n_sel_s)[:, :, :ds]
    sel_s = sel_s.transpose(0, 2, 1, 3).reshape(-1)
    o_slc_s = _slc_gather(cache_slc_kv, page_table, sel_s, q_rot_s, mk_s_new, _pad_new(ps["slc_b"], db, ds),
                          n_sel_s, ds)
    o_win_s = _win_sample(q_rot_s, state_win_kv, _pad_new(ps["win_b"], db, ds), ds)

    a_s, b_s = _gate_merge(_unsample_rows(o_a_s, db, ds), _unsample_rows(o_cmp_s, db, ds),
                           _unsample_rows(o_slc_s, db, ds), _unsample_rows(o_win_s, db, ds),
                           ps["misc"], ps["sza"], ps["szb"])
    merged_s = _merge(a_s, b_s, ps["h"], w_pa, w_pb, w_g0, w_g1, ns, 1024)
    y_sample = _out_proj(xs, merged_s, w_o, ns).reshape(db, ds, D_MODEL)

    s_win = kv5(ps["win"], db, ds)
    w_buf = state_win_kv.shape[1]
    s_win_kv = jnp.concatenate([state_win_kv, s_win], axis=1)[:, ds:ds + w_buf]

    return (y_prompt, y_sample, p_a_kv, p_a_idx, p_cmp_kv, p_slc_kv, p_win_kv,
            kv5(ps["akv"], db, ds), ps["misc"][:, :IDX_DIM].reshape(db, ds, IDX_DIM),
            kv5(ps["cmp"], db, ds), kv5(ps["slc"], db, ds), s_win_kv)
```

```python
import functools

import jax
import jax.numpy as jnp
from jax import lax
from jax.experimental import pallas as pl
from jax.experimental.pallas import tpu as pltpu

D_MODEL = 2048
HEAD_DIM = 128
ROT_DIM = HEAD_DIM // 4
N_HEADS = D_MODEL // (2 * HEAD_DIM)
KV_HEADS = 2
GROUP = N_HEADS // KV_HEADS
IDX_HEADS = 8
IDX_DIM = 64
IDX_ROT_DIM = IDX_DIM // 4
A_TOPK = 256
CMP_LEN = 32
CMP_STRIDE = 16
CMP_HIDDEN = 128
SLC_BLOCK = 64
N_SEL = 16
WINDOW = 512
PAGE_SIZE = 128
ROPE_THETA = 500000.0
EPS = 1e-6
NEG = -1e30
WIDTH = N_HEADS * HEAD_DIM
KV_ROW = KV_HEADS * 2 * HEAD_DIM
SCALE = HEAD_DIM ** -0.5

LANES = 128
SUBLANES = 8
VMEM_LIMIT = 48 * 1024 * 1024
PAGES_PER_STEP = 16
SAMPLE_Q = 8

F32 = jnp.float32
BF16 = jnp.bfloat16
NT_DIMS = (((1,), (1,)), ((), ()))


def _params(n_axes):
    return pltpu.CompilerParams(dimension_semantics=("arbitrary",) * n_axes,
                                vmem_limit_bytes=VMEM_LIMIT)


def _dot(a, b):
    return jnp.dot(a, b, preferred_element_type=F32)


def _dot_nt(a, b):
    return lax.dot_general(a, b, NT_DIMS, preferred_element_type=F32)


def _rope_tables(pos, rot, period, width=LANES):
    half = rot // 2
    inv = ROPE_THETA ** (-jnp.arange(half, dtype=F32) / half)
    ang = pos.astype(F32)[:, None] * inv
    cos, sin = jnp.cos(ang), jnp.sin(ang)
    n = pos.shape[0]
    zeros = lambda w: jnp.zeros((n, w), F32)
    ones = lambda w: jnp.ones((n, w), F32)
    c = jnp.concatenate([cos, cos, ones(period - rot)], axis=1)
    s1 = jnp.concatenate([-sin, zeros(period - half)], axis=1)
    s2 = jnp.concatenate([zeros(half), sin, zeros(period - rot)], axis=1)
    return c, s1, s2


def _apply_rope(y, c, s1, s2, half):
    return y * c + pltpu.roll(y, LANES - half, 1) * s1 + pltpu.roll(y, half, 1) * s2


def _head_norm(z, g):
    return z * lax.rsqrt(jnp.mean(z * z, axis=-1, keepdims=True) + EPS) * g


def _rmsnorm_kernel(x_ref, g_ref, o_ref):
    x = x_ref[...]
    y = x * lax.rsqrt(jnp.mean(x * x, axis=-1, keepdims=True) + EPS)
    o_ref[...] = (y * g_ref[...]).astype(o_ref.dtype)


def _rmsnorm(x, gain, tr):
    n, d = x.shape
    return pl.pallas_call(
        _rmsnorm_kernel,
        grid=(n // tr,),
        in_specs=[pl.BlockSpec((tr, d), lambda i: (i, 0)), pl.BlockSpec((1, d), lambda i: (0, 0))],
        out_specs=pl.BlockSpec((tr, d), lambda i: (i, 0)),
        out_shape=jax.ShapeDtypeStruct((n, d), BF16),
        compiler_params=_params(1),
        name="rmsnorm",
    )(x, gain.reshape(1, d))


PROJ_ROWS = 128


def _proj_q_kernel(h_ref, w_ref, g_ref, c_ref, s1_ref, s2_ref, rot_ref, *plain_ref):
    g = g_ref[...]
    n = h_ref.shape[0]
    nh = max(1, n // PROJ_ROWS)
    for r in range(nh):
        rs = slice(r * (n // nh), (r + 1) * (n // nh))
        z = _dot(h_ref[rs, :], w_ref[...])
        c, s1, s2 = c_ref[rs, :], s1_ref[rs, :], s2_ref[rs, :]
        for hd in range(N_HEADS):
            sl = slice(hd * HEAD_DIM, (hd + 1) * HEAD_DIM)
            y = _head_norm(z[:, sl], g)
            if plain_ref:
                plain_ref[0][rs, sl] = y.astype(BF16)
            rot_ref[rs, sl] = _apply_rope(y, c, s1, s2, ROT_DIM // 2).astype(BF16)


def _proj_q(h, w, gain, tabs, tr, n_tab_blocks, want_plain):
    n = h.shape[0]
    row = lambda i: (i, 0)
    fix = lambda i: (0, 0)
    tab = lambda i: (i % n_tab_blocks, 0)
    out_shape = [jax.ShapeDtypeStruct((n, WIDTH), BF16)] * (2 if want_plain else 1)
    out_specs = [pl.BlockSpec((tr, WIDTH), row)] * (2 if want_plain else 1)
    return pl.pallas_call(
        _proj_q_kernel,
        grid=(n // tr,),
        in_specs=[pl.BlockSpec((tr, D_MODEL), row), pl.BlockSpec((D_MODEL, WIDTH), fix),
                  pl.BlockSpec((1, HEAD_DIM), fix)] + [pl.BlockSpec((tr, LANES), tab)] * 3,
        out_specs=out_specs,
        out_shape=out_shape,
        compiler_params=_params(1),
        name="proj_q",
    )(h, w, gain.reshape(1, HEAD_DIM), *tabs)


def _proj_kv_kernel(h_ref, w_ref, ga_ref, gs_ref, gw_ref, c_ref, s1_ref, s2_ref,
                    akv_ref, cmp_ref, slc_ref, win_ref, akv_b_ref, slc_b_ref, win_b_ref):
    z = _dot(h_ref[...], w_ref[...])
    c, s1, s2 = c_ref[...], s1_ref[...], s2_ref[...]

    def emit(base, gain_ref, f32_ref, bf_ref):
        for j in range(KV_HEADS * 2):
            sl = slice(j * HEAD_DIM, (j + 1) * HEAD_DIM)
            v = z[:, base + j * HEAD_DIM: base + (j + 1) * HEAD_DIM]
            if gain_ref is not None and j % 2 == 0:
                v = _apply_rope(_head_norm(v, gain_ref[...]), c, s1, s2, ROT_DIM // 2)
            f32_ref[pl.ds(j, z.shape[0], stride=KV_HEADS * 2), :] = v
            if bf_ref is not None:
                bf_ref[:, sl] = v.astype(BF16)

    emit(0 * KV_ROW, ga_ref, akv_ref, akv_b_ref)
    emit(1 * KV_ROW, None, cmp_ref, None)
    emit(2 * KV_ROW, gs_ref, slc_ref, slc_b_ref)
    emit(3 * KV_ROW, gw_ref, win_ref, win_b_ref)


def _proj_kv(h, w, g_a, g_slc, g_win, tabs, tr, n_tab_blocks):
    n = h.shape[0]
    row = lambda i: (i, 0)
    fix = lambda i: (0, 0)
    tab = lambda i: (i % n_tab_blocks, 0)
    f32_out = jax.ShapeDtypeStruct((n * KV_HEADS * 2, HEAD_DIM), F32)
    bf_out = jax.ShapeDtypeStruct((n, KV_ROW), BF16)
    return pl.pallas_call(
        _proj_kv_kernel,
        grid=(n // tr,),
        in_specs=[pl.BlockSpec((tr, D_MODEL), row), pl.BlockSpec((D_MODEL, 4 * KV_ROW), fix)]
                 + [pl.BlockSpec((1, HEAD_DIM), fix)] * 3 + [pl.BlockSpec((tr, LANES), tab)] * 3,
        out_specs=[pl.BlockSpec((tr * KV_HEADS * 2, HEAD_DIM), row)] * 4 + [pl.BlockSpec((tr, KV_ROW), row)] * 3,
        out_shape=[f32_out] * 4 + [bf_out] * 3,
        compiler_params=_params(1),
        name="proj_kv",
    )(h, w, g_a.reshape(1, HEAD_DIM), g_slc.reshape(1, HEAD_DIM), g_win.reshape(1, HEAD_DIM), *tabs)


IDX_COLS = IDX_HEADS * IDX_DIM
MISC_IK = IDX_DIM
MISC_IW = MISC_IK + IDX_HEADS
MISC_G = MISC_IW + 3 * N_HEADS


def _proj_idx_kernel(h_ref, w_ref, c_ref, s1_ref, s2_ref, cm_ref, s1m_ref, s2m_ref, iq_ref, misc_ref):
    z = _dot(h_ref[...], w_ref[...])
    half = IDX_ROT_DIM // 2
    c, s1, s2 = c_ref[...], s1_ref[...], s2_ref[...]
    for j in range(IDX_COLS // LANES):
        sl = slice(j * LANES, (j + 1) * LANES)
        iq_ref[:, sl] = _apply_rope(z[:, sl], c, s1, s2, half).astype(BF16)
    m = z[:, IDX_COLS:IDX_COLS + LANES]
    roped = _apply_rope(m, cm_ref[...], s1m_ref[...], s2m_ref[...], half)
    lane = lax.broadcasted_iota(jnp.int32, m.shape, 1)
    misc_ref[...] = jnp.where(lane < MISC_IK, roped,
                              jnp.where(lane < MISC_IW, m * (IDX_HEADS ** -0.5),
                                        jnp.where(lane < MISC_G, jax.nn.sigmoid(m), 0.0)))


def _proj_idx(h, w, tabs_iq, tabs_misc, tr, n_tab_blocks):
    n = h.shape[0]
    row = lambda i: (i, 0)
    fix = lambda i: (0, 0)
    tab = lambda i: (i % n_tab_blocks, 0)
    return pl.pallas_call(
        _proj_idx_kernel,
        grid=(n // tr,),
        in_specs=[pl.BlockSpec((tr, D_MODEL), row), pl.BlockSpec((D_MODEL, IDX_COLS + LANES), fix)]
                 + [pl.BlockSpec((tr, LANES), tab)] * 6,
        out_specs=[pl.BlockSpec((tr, IDX_COLS), row), pl.BlockSpec((tr, LANES), row)],
        out_shape=[jax.ShapeDtypeStruct((n, IDX_COLS), BF16), jax.ShapeDtypeStruct((n, LANES), F32)],
        compiler_params=_params(1),
        name="proj_idx",
    )(h, w, *tabs_iq, *tabs_misc)


def _proj_z_kernel(h_ref, w_ref, za_ref, zb_ref):
    z = _dot(h_ref[...], w_ref[...])
    s = z * jax.nn.sigmoid(z)
    za_ref[...] = s[:, :WIDTH]
    zb_ref[...] = s[:, WIDTH:]


def _proj_z(h, w, tr):
    n = h.shape[0]
    row = lambda i: (i, 0)
    out = jax.ShapeDtypeStruct((n, WIDTH), F32)
    return pl.pallas_call(
        _proj_z_kernel,
        grid=(n // tr,),
        in_specs=[pl.BlockSpec((tr, D_MODEL), row), pl.BlockSpec((D_MODEL, 2 * WIDTH), lambda i: (0, 0))],
        out_specs=[pl.BlockSpec((tr, WIDTH), row)] * 2,
        out_shape=[out, out],
        compiler_params=_params(1),
        name="proj_z",
    )(h, w)


def _key_to_float(u):
    int_min = jnp.int32(-2 ** 31)
    bits = jnp.where(u < 0, u ^ int_min, ~u)
    return lax.bitcast_convert_type(bits, F32)


def _lane_fold(w):
    acc = w[:, :LANES]
    for j in range(1, w.shape[1] // LANES):
        acc = acc + w[:, j * LANES:(j + 1) * LANES]
    return acc


COUNT_ROWS = 1024


def _topk_threshold(load, nck, k, idx_bits, rows, cw, tail=None):
    kf = jnp.float32(k)
    lane = lax.broadcasted_iota(jnp.int32, (1, cw), 1)
    nr = min(rows, COUNT_ROWS)

    def count(pred_value):
        out = []
        for r0 in range(0, rows, nr):
            rs = slice(r0, r0 + nr)

            def body(c, acc, r0=r0, rs=rs):
                return acc + _lane_fold(pred_value(load(c, r0, nr), c * cw + lane, rs))
            acc = lax.fori_loop(0, nck, body, jnp.zeros((nr, LANES), F32))
            if tail is not None:
                acc = acc + pred_value(tail[0][rs], tail[1], rs)
            out.append(jnp.sum(acc, axis=1, keepdims=True))
        return out[0] if len(out) == 1 else jnp.concatenate(out, axis=0)

    def value_bit(it, u):
        cand = u | jnp.left_shift(jnp.int32(1), 31 - it)
        candf = _key_to_float(cand)
        cnt = count(lambda x, idx, rs: jnp.where(x >= candf[rs], 1.0, 0.0))
        return jnp.where(cnt >= kf, cand, u)

    u = lax.fori_loop(0, 32, value_bit, jnp.zeros((rows, 1), jnp.int32))
    thr = jnp.where((u >= 0) & (u < 2 ** 23), -jnp.inf, _key_to_float(u))
    need = kf - count(lambda x, idx, rs: jnp.where(x > thr[rs], 1.0, 0.0))

    def index_bit(it, v):
        cand = v | jnp.left_shift(jnp.int32(1), idx_bits - 1 - it)
        below = count(lambda x, idx, rs: jnp.where(x == thr[rs], jnp.where(idx < cand[rs], 1.0, 0.0), 0.0))
        return jnp.where(below < need, cand, v)

    tied = jnp.max(count(lambda x, idx, rs: jnp.where(x >= thr[rs], 1.0, 0.0))) > kf
    v = lax.cond(tied,
                 lambda: lax.fori_loop(0, idx_bits, index_bit, jnp.zeros((rows, 1), jnp.int32)),
                 lambda: jnp.full((rows, 1), (1 << idx_bits) - 1, jnp.int32))
    return thr, jnp.where(need >= 1.0, v, -1)


def _topk_mask(x, idx, thr, vmax):
    gt = jnp.where(x > thr, 1.0, 0.0)
    eq = jnp.where(x == thr, jnp.where(idx <= vmax, 1.0, 0.0), 0.0)
    return jnp.maximum(gt, eq)


def _sublane_fold(w):
    parts = [w[j * SUBLANES:(j + 1) * SUBLANES] for j in range(w.shape[0] // SUBLANES)]
    while len(parts) > 1:
        parts = [parts[j] + parts[j + 1] if j + 1 < len(parts) else parts[j] for j in range(0, len(parts), 2)]
    return parts[0]


def _topk_threshold_cols(load, nck, k, idx_bits, cols, ck):
    kf = jnp.float32(k)
    sub = lax.broadcasted_iota(jnp.int32, (ck, 1), 0)

    def count(pred_value):
        def body(c, acc):
            return acc + _sublane_fold(pred_value(load(c), c * ck + sub))
        acc = lax.fori_loop(0, nck, body, jnp.zeros((SUBLANES, cols), F32))
        return jnp.sum(acc, axis=0, keepdims=True)

    def value_bit(it, u):
        cand = u | jnp.left_shift(jnp.int32(1), 31 - it)
        candf = _key_to_float(cand)
        cnt = count(lambda x, idx: jnp.where(x >= candf, 1.0, 0.0))
        return jnp.where(cnt >= kf, cand, u)

    u = lax.fori_loop(0, 32, value_bit, jnp.zeros((1, cols), jnp.int32))
    thr = jnp.where((u >= 0) & (u < 2 ** 23), -jnp.inf, _key_to_float(u))
    need = kf - count(lambda x, idx: jnp.where(x > thr, 1.0, 0.0))

    def index_bit(it, v):
        cand = v | jnp.left_shift(jnp.int32(1), idx_bits - 1 - it)
        below = count(lambda x, idx: jnp.where(x == thr, jnp.where(idx < cand, 1.0, 0.0), 0.0))
        return jnp.where(below < need, cand, v)

    tied = jnp.max(count(lambda x, idx: jnp.where(x >= thr, 1.0, 0.0))) > kf
    v = lax.cond(tied,
                 lambda: lax.fori_loop(0, idx_bits, index_bit, jnp.zeros((1, cols), jnp.int32)),
                 lambda: jnp.full((1, cols), (1 << idx_bits) - 1, jnp.int32))
    return thr, jnp.where(need >= 1.0, v, -1)


LOG2E = 1.4426950408889634
SCORE_SCALE = SCALE * LOG2E


def _softmax_init(rows):
    return (jnp.full((rows, 1), NEG, F32), jnp.zeros((rows, 1), F32), jnp.zeros((rows, HEAD_DIM), F32))


def _softmax_step(carry, q, k, v, bias):
    m, l, acc = carry
    t = _dot_nt(q, k) * SCORE_SCALE + bias
    m_new = jnp.maximum(m, jnp.max(t, axis=1, keepdims=True))
    alpha = jnp.exp2(m - m_new)
    e = jnp.exp2(t - m_new)
    l = alpha * l + jnp.sum(e, axis=1, keepdims=True)
    acc = alpha * acc + _dot(e.astype(BF16), v)
    return m_new, l, acc


def _softmax_done(carry):
    _, l, acc = carry
    return acc / jnp.maximum(l, 1e-30)


def _mask_bias(maskf):
    return jnp.where(maskf > 0.0, 0.0, NEG)


def _stack_heads(ref, kvh):
    return jnp.concatenate([ref[:, (kvh * GROUP + g) * HEAD_DIM:(kvh * GROUP + g + 1) * HEAD_DIM]
                            for g in range(GROUP)], axis=0)


def _kv_cols(kvh):
    return (slice(kvh * 2 * HEAD_DIM, (kvh * 2 + 1) * HEAD_DIM),
            slice((kvh * 2 + 1) * HEAD_DIM, (kvh * 2 + 2) * HEAD_DIM))


def _dsa_prompt_kernel(iq_ref, iwt_ref, ik2_ref, qa_ref, akv_ref, sza_ref, o_ref, sc_ref, mk_ref,
                       *, tq, tk, k_top, idx_bits):
    i = pl.program_id(1)
    nck = ((i + 1) * tq + tk - 1) // tk
    lane = lax.broadcasted_iota(jnp.int32, (1, LANES), 1)
    qpos = i * tq + lax.broadcasted_iota(jnp.int32, (1, tq), 1)
    krow = lax.broadcasted_iota(jnp.int32, (tk, 1), 0)
    iwt = iwt_ref[...] * (IDX_DIM ** -0.5)

    iqs = []
    for hd in range(IDX_HEADS):
        pair = iq_ref[:, (hd // 2) * LANES:(hd // 2 + 1) * LANES]
        keep = (lane < IDX_DIM) if hd % 2 == 0 else (lane >= IDX_DIM)
        iqs.append(jnp.where(keep, pair, jnp.zeros_like(pair)))

    def score_chunk(c, carry):
        k0 = pl.multiple_of(c * tk, tk)
        ikc = ik2_ref[pl.ds(k0, tk), :]
        acc = jnp.zeros((tk, tq), F32)
        for hd in range(IDX_HEADS):
            acc = acc + jnp.maximum(_dot_nt(ikc, iqs[hd]), 0.0) * iwt[hd:hd + 1, :]
        sc_ref[c] = jnp.where(k0 + krow <= qpos, acc, NEG)
        return carry

    lax.fori_loop(0, nck, score_chunk, 0)

    thr, vmax = _topk_threshold_cols(lambda c: sc_ref[c], nck, k_top, idx_bits, tq, tk)

    def mask_chunk(c, carry):
        idx = c * tk + krow
        sel = _topk_mask(sc_ref[c], idx, thr, vmax) * jnp.where(idx <= qpos, 1.0, 0.0)
        mk_ref[c] = _mask_bias(sel).T
        return carry

    lax.fori_loop(0, nck, mask_chunk, 0)

    qs = [_stack_heads(qa_ref, kvh) for kvh in range(KV_HEADS)]

    def attend(c, carries):
        k0 = pl.multiple_of(c * tk, tk)
        bias = jnp.concatenate([mk_ref[c]] * GROUP, axis=0)
        out = []
        for kvh in range(KV_HEADS):
            kcol, vcol = _kv_cols(kvh)
            out.append(_softmax_step(carries[kvh], qs[kvh], akv_ref[pl.ds(k0, tk), kcol],
                                     akv_ref[pl.ds(k0, tk), vcol], bias))
        return tuple(out)

    carries = lax.fori_loop(0, nck, attend, tuple(_softmax_init(GROUP * tq) for _ in range(KV_HEADS)))
    for kvh in range(KV_HEADS):
        o = _softmax_done(carries[kvh])
        for g in range(GROUP):
            sl = slice((kvh * GROUP + g) * HEAD_DIM, (kvh * GROUP + g + 1) * HEAD_DIM)
            o_ref[:, sl] = (o[g * tq:(g + 1) * tq] * sza_ref[:, sl]).astype(BF16)


def _dsa_prompt(iq, iw_t, ik2, qa, akv_b, sza, b, t, tq, tk):
    nqb = t // tq
    k_top = min(A_TOPK, t // 4)
    idx_bits = max(1, (t - 1).bit_length())
    row = lambda bi, i: (bi * nqb + i, 0)
    whole = lambda bi, i: (bi, 0, 0)
    kern = functools.partial(_dsa_prompt_kernel, tq=tq, tk=tk, k_top=k_top, idx_bits=idx_bits)
    return pl.pallas_call(
        kern,
        grid=(b, nqb),
        in_specs=[pl.BlockSpec((tq, IDX_COLS), row),
                  pl.BlockSpec((IDX_HEADS, tq), lambda bi, i: (0, bi * nqb + i)),
                  pl.BlockSpec((None, t, LANES), whole), pl.BlockSpec((tq, WIDTH), row),
                  pl.BlockSpec((None, t, KV_ROW), whole), pl.BlockSpec((tq, WIDTH), row)],
        out_specs=pl.BlockSpec((tq, WIDTH), row),
        out_shape=jax.ShapeDtypeStruct((b * t, WIDTH), BF16),
        scratch_shapes=[pltpu.VMEM((t // tk, tk, tq), F32), pltpu.VMEM((t // tk, tq, tk), F32)],
        compiler_params=_params(2),
        name="dsa_prompt",
    )(iq, iw_t, ik2, qa, akv_b, sza)


SLC_SHIFT = SLC_BLOCK.bit_length() - 1
assert 1 << SLC_SHIFT == SLC_BLOCK


def _block_rules(imp, blk, qpos):
    cur = lax.shift_right_logical(qpos, SLC_SHIFT)
    return jnp.where(blk == cur, -NEG, jnp.where(blk * SLC_BLOCK > qpos, NEG, imp))


def _split_dot(p, m):
    hi = p.astype(BF16)
    lo = (p - hi.astype(F32)).astype(BF16)
    return _dot(hi, m) + _dot(lo, m)


def _nsa_prompt_kernel(qb_ref, qr_ref, kc_ref, vc_ref, slc_ref, win_ref, ovt_ref, misc_ref, szb_ref,
                       o_ref, sc_ref, *, tq, tk, n_cmp, n_blk, n_sel):
    i = pl.program_id(1)
    nck = ((i + 1) * tq + tk - 1) // tk
    qpos = i * tq + lax.broadcasted_iota(jnp.int32, (tq, 1), 0)
    kiota = lax.broadcasted_iota(jnp.int32, (1, tk), 1)
    misc = misc_ref[...]
    ncp = kc_ref.shape[0]
    cidx = lax.broadcasted_iota(jnp.int32, (1, ncp), 1)
    cmask = jnp.where((CMP_STRIDE * cidx + CMP_LEN - 1 <= qpos) & (cidx < n_cmp), 1.0, 0.0)
    cmask = jnp.concatenate([cmask] * GROUP, axis=0)
    blk = lax.broadcasted_iota(jnp.int32, (LANES, 1), 0)
    qlane = i * tq + lax.broadcasted_iota(jnp.int32, (1, tq), 1)

    o_cmp = []
    for kvh in range(KV_HEADS):
        kc = kc_ref[:, kvh * HEAD_DIM:(kvh + 1) * HEAD_DIM]
        vc = vc_ref[:, kvh * HEAD_DIM:(kvh + 1) * HEAD_DIM]
        s = _dot_nt(_stack_heads(qb_ref, kvh), kc) * SCALE
        sm = jnp.where(cmask > 0.0, s, NEG)
        e = jnp.exp(sm - jnp.max(sm, axis=1, keepdims=True)) * cmask
        p = e / jnp.maximum(jnp.sum(e, axis=1, keepdims=True), 1e-30)
        o_cmp.append(_dot(p.astype(BF16), vc))
        psum = p[0:tq]
        for g in range(1, GROUP):
            psum = psum + p[g * tq:(g + 1) * tq]
        hi = psum.astype(BF16)
        lo = (psum - hi.astype(F32)).astype(BF16)
        imp = _dot_nt(ovt_ref[...], hi) + _dot_nt(ovt_ref[...], lo)
        imp = _block_rules(imp, blk, qlane)
        sc_ref[:, kvh * tq:(kvh + 1) * tq] = jnp.where(blk < n_blk, imp, -jnp.inf)

    thr, vmax = _topk_threshold_cols(lambda c: sc_ref[...], 1, n_sel, 7, KV_HEADS * tq, LANES)
    bmask_t = _topk_mask(sc_ref[...], blk, thr, vmax)
    bmask = [bmask_t[:, kvh * tq:(kvh + 1) * tq].T.astype(BF16) for kvh in range(KV_HEADS)]

    riota = lax.broadcasted_iota(jnp.int32, (LANES, 1), 0)
    nwk = WINDOW + tq
    w0 = pl.multiple_of(i * tq, tq)
    kposw = i * tq - WINDOW + lax.broadcasted_iota(jnp.int32, (1, nwk), 1)
    dw = qpos - kposw
    bias_w = jnp.where((dw >= 0) & (dw < WINDOW) & (kposw >= 0), 0.0, NEG)
    bias_w = jnp.concatenate([bias_w] * GROUP, axis=0)

    qs = [_stack_heads(qr_ref, kvh) for kvh in range(KV_HEADS)]

    def attend(c, carries):
        k0 = pl.multiple_of(c * tk, tk)
        kpos = k0 + kiota
        expand = jnp.where(riota == lax.shift_right_logical(kpos, SLC_SHIFT), 1.0, 0.0).astype(BF16)
        causal = jnp.where(kpos <= qpos, 1.0, 0.0)
        out = []
        for kvh in range(KV_HEADS):
            kcol, vcol = _kv_cols(kvh)
            sel = _dot(bmask[kvh], expand) * causal
            bias = jnp.concatenate([_mask_bias(sel)] * GROUP, axis=0)
            out.append(_softmax_step(carries[kvh], qs[kvh], slc_ref[pl.ds(k0, tk), kcol],
                                     slc_ref[pl.ds(k0, tk), vcol], bias))
        return tuple(out)

    slc_carries = lax.fori_loop(0, nck, attend, tuple(_softmax_init(GROUP * tq) for _ in range(KV_HEADS)))

    for kvh in range(KV_HEADS):
        q = qs[kvh]
        kcol, vcol = _kv_cols(kvh)
        o_slc = _softmax_done(slc_carries[kvh])
        o_win = _softmax_done(_softmax_step(_softmax_init(GROUP * tq), q, win_ref[pl.ds(w0, nwk), kcol],
                                            win_ref[pl.ds(w0, nwk), vcol], bias_w))
        for g in range(GROUP):
            hd = kvh * GROUP + g
            sl = slice(hd * HEAD_DIM, (hd + 1) * HEAD_DIM)
            rs = slice(g * tq, (g + 1) * tq)
            gate = [misc[:, MISC_IW + j * N_HEADS + hd:MISC_IW + j * N_HEADS + hd + 1] for j in range(3)]
            o_b = gate[0] * o_cmp[kvh][rs] + gate[1] * o_slc[rs] + gate[2] * o_win[rs]
            o_ref[:, sl] = (o_b * szb_ref[:, sl]).astype(BF16)


def _overlap_matrix(n_cmp_pad, n_cols, col_block):
    i = jnp.arange(n_cmp_pad)[:, None]
    j = col_block[None, :]
    ov = (CMP_STRIDE * i < SLC_BLOCK * (j + 1)) & (CMP_STRIDE * i + CMP_LEN > SLC_BLOCK * j) & (j >= 0)
    return ov.astype(BF16)


def _nsa_prompt(qb, qr, kc, vc, slc_b, win_b, misc, szb, b, t, tq, tk):
    nqb = t // tq
    n_cmp = t // CMP_STRIDE - 1
    n_blk = t // SLC_BLOCK
    assert n_blk <= LANES
    ncp = kc.shape[1]
    ov = _overlap_matrix(ncp, LANES, jnp.where(jnp.arange(LANES) < n_blk, jnp.arange(LANES), -1))
    row = lambda bi, i: (bi * nqb + i, 0)
    whole = lambda bi, i: (bi, 0, 0)
    kern = functools.partial(_nsa_prompt_kernel, tq=tq, tk=tk, n_cmp=n_cmp, n_blk=n_blk,
                             n_sel=min(N_SEL, n_blk))
    return pl.pallas_call(
        kern,
        grid=(b, nqb),
        in_specs=[pl.BlockSpec((tq, WIDTH), row), pl.BlockSpec((tq, WIDTH), row),
                  pl.BlockSpec((None, ncp, KV_HEADS * HEAD_DIM), whole),
                  pl.BlockSpec((None, ncp, KV_HEADS * HEAD_DIM), whole),
                  pl.BlockSpec((None, t, KV_ROW), whole), pl.BlockSpec((None, t + WINDOW, KV_ROW), whole),
                  pl.BlockSpec((LANES, ncp), lambda bi, i: (0, 0)),
                  pl.BlockSpec((tq, LANES), row), pl.BlockSpec((tq, WIDTH), row)],
        out_specs=pl.BlockSpec((tq, WIDTH), row),
        out_shape=jax.ShapeDtypeStruct((b * t, WIDTH), BF16),
        scratch_shapes=[pltpu.VMEM((LANES, KV_HEADS * tq), F32)],
        compiler_params=_params(2),
        name="nsa_prompt",
    )(qb, qr, kc, vc, slc_b, win_b, ov.T, misc, szb)


CHUNK_FLAT = CMP_STRIDE * KV_ROW
FS_COLS = KV_HEADS * 2 * 2 * CMP_HIDDEN


def _chunk_terms_body(x, w_ref, o_ref):
    for kc in range(KV_HEADS * 2):
        xk = jnp.concatenate(
            [x[:, p * KV_ROW + kc * HEAD_DIM: p * KV_ROW + (kc + 1) * HEAD_DIM] for p in range(CMP_STRIDE)],
            axis=1).astype(BF16)
        o_ref[:, kc * 2 * CMP_HIDDEN:(kc + 1) * 2 * CMP_HIDDEN] = _dot(xk, w_ref[kc % 2])


def _chunk_terms_dense_kernel(x_ref, w_ref, o_ref):
    _chunk_terms_body(x_ref[...], w_ref, o_ref)


def _chunk_terms_dense(x, w, tr):
    n = x.shape[0]
    return pl.pallas_call(
        _chunk_terms_dense_kernel,
        grid=(n // tr,),
        in_specs=[pl.BlockSpec((tr, CHUNK_FLAT), lambda i: (i, 0)),
                  pl.BlockSpec((2, CMP_STRIDE * HEAD_DIM, 2 * CMP_HIDDEN), lambda i: (0, 0, 0))],
        out_specs=pl.BlockSpec((tr, FS_COLS), lambda i: (i, 0)),
        out_shape=jax.ShapeDtypeStruct((n, FS_COLS), F32),
        compiler_params=_params(1),
        name="chunk_terms_dense",
    )(x, w)


ROWS_PER_TOKEN = KV_HEADS * 2
PAGE_ROWS = PAGE_SIZE * ROWS_PER_TOKEN
CHUNKS_PER_PAGE = PAGE_SIZE // CMP_STRIDE


def _cache_rows(cache):
    return cache.reshape(-1, HEAD_DIM)


def _token_rows(ref, slot, n_tokens):
    return ref[pl.ds(slot, n_tokens, stride=ROWS_PER_TOKEN), :]


def _sublane_transpose(tiles):
    tiles = list(tiles)
    sub = lax.broadcasted_iota(jnp.int32, (SUBLANES, LANES), 0)
    for d in (4, 2, 1):
        keep = (sub & d) == 0
        for i in range(SUBLANES):
            if i & d:
                continue
            lo, hi = tiles[i], tiles[i + d]
            tiles[i] = jnp.where(keep, lo, pltpu.roll(hi, d, 0))
            tiles[i + d] = jnp.where(keep, pltpu.roll(lo, SUBLANES - d, 0), hi)
    return tiles


def _chunk_terms_paged_kernel(pt_ref, *refs, pg):
    pages, w_ref, o_ref = refs[:pg], refs[pg], refs[pg + 1]
    chunk_rows = CMP_STRIDE * ROWS_PER_TOKEN
    assert CHUNKS_PER_PAGE == SUBLANES and chunk_rows % SUBLANES == 0
    pieces = [[] for _ in range(chunk_rows)]
    for pr in pages:
        for t in range(chunk_rows // SUBLANES):
            tiles = [pr[n * chunk_rows + t * SUBLANES:n * chunk_rows + (t + 1) * SUBLANES, :]
                     for n in range(CHUNKS_PER_PAGE)]
            for s, tile in enumerate(_sublane_transpose(tiles)):
                pieces[t * SUBLANES + s].append(tile)
    for kc in range(ROWS_PER_TOKEN):
        xk = jnp.concatenate([jnp.concatenate(pieces[p * ROWS_PER_TOKEN + kc], axis=0)
                              for p in range(CMP_STRIDE)], axis=1).astype(BF16)
        o_ref[:, kc * 2 * CMP_HIDDEN:(kc + 1) * 2 * CMP_HIDDEN] = _dot(xk, w_ref[kc % 2])


def _page_specs(block, pg):
    tail = (0,) * (len(block) - 1)

    def spec(j):
        return pl.BlockSpec(block, lambda bi, g, pt: (pt[bi, g * pg + j],) + tail)
    return [spec(j) for j in range(pg)]


def _chunk_terms_paged(pool, page_table, w, pg):
    db, n_pages = page_table.shape
    cpp = CHUNKS_PER_PAGE
    pool_v = _cache_rows(pool)
    kern = functools.partial(_chunk_terms_paged_kernel, pg=pg)
    grid_spec = pltpu.PrefetchScalarGridSpec(
        num_scalar_prefetch=1,
        grid=(db, n_pages // pg),
        in_specs=_page_specs((PAGE_ROWS, HEAD_DIM), pg)
                 + [pl.BlockSpec((2, CMP_STRIDE * HEAD_DIM, 2 * CMP_HIDDEN), lambda bi, g, pt: (0, 0, 0))],
        out_specs=pl.BlockSpec((None, pg * cpp, FS_COLS), lambda bi, g, pt: (bi, g, 0)),
    )
    return pl.pallas_call(
        kern, grid_spec=grid_spec,
        out_shape=jax.ShapeDtypeStruct((db, n_pages * cpp, FS_COLS), F32),
        compiler_params=_params(2),
        name="chunk_terms_paged",
    )(page_table, *([pool_v] * pg), w)


def _compress_mlp_kernel(fs_ref, new_ref, pe_ref, w1_ref, w2_ref, g_ref, kc_ref, vc_ref):
    n = fs_ref.shape[0]
    last = lax.broadcasted_iota(jnp.int32, (n, 1), 0) == n - 1
    for kc in range(KV_HEADS * 2):
        kvh, c = kc // 2, kc % 2
        base = kc * 2 * CMP_HIDDEN
        first = fs_ref[:, base:base + CMP_HIDDEN]
        second = fs_ref[:, base + CMP_HIDDEN:base + 2 * CMP_HIDDEN]
        shifted = jnp.where(last, new_ref[0:1, base + CMP_HIDDEN:base + 2 * CMP_HIDDEN],
                            pltpu.roll(second, n - 1, 0))
        pe = jnp.broadcast_to(pe_ref[c:c + 1, :], (SUBLANES, pe_ref.shape[1])).astype(BF16)
        bias = _dot(pe, w1_ref[c])[0:1, :]
        pre = first + shifted + bias
        hid = pre * jax.nn.sigmoid(pre)
        out = _dot(hid.astype(BF16), w2_ref[c])
        sl = slice(kvh * HEAD_DIM, (kvh + 1) * HEAD_DIM)
        if c == 0:
            kc_ref[:, sl] = _head_norm(out, g_ref[...]).astype(BF16)
        else:
            vc_ref[:, sl] = out.astype(BF16)


def _compress_mlp(fs, fs_new, pe, w1r, w2, g_cmp):
    nb, n, _ = fs.shape
    whole = lambda bi: (bi, 0, 0)
    fix2 = lambda bi: (0, 0)
    fix3 = lambda bi: (0, 0, 0)
    out = jax.ShapeDtypeStruct((nb, n, KV_HEADS * HEAD_DIM), BF16)
    return pl.pallas_call(
        _compress_mlp_kernel,
        grid=(nb,),
        in_specs=[pl.BlockSpec((None, n, FS_COLS), whole), pl.BlockSpec((None, SUBLANES, FS_COLS), whole),
                  pl.BlockSpec((2, CMP_LEN * HEAD_DIM), fix2),
                  pl.BlockSpec((2, CMP_LEN * HEAD_DIM, CMP_HIDDEN), fix3),
                  pl.BlockSpec((2, CMP_HIDDEN, HEAD_DIM), fix3), pl.BlockSpec((1, HEAD_DIM), fix2)],
        out_specs=[pl.BlockSpec((None, n, KV_HEADS * HEAD_DIM), whole)] * 2,
        out_shape=[out, out],
        compiler_params=_params(1),
        name="compress_mlp",
    )(fs, fs_new, pe, w1r, w2, g_cmp.reshape(1, HEAD_DIM))


def _idx_scores_kernel(pt_ref, *refs, pg, n_new):
    pages = refs[:pg]
    iq_ref, iw_ref, new_ref, o_ref, onew_ref = refs[pg:pg + 5]
    g = pl.program_id(1)
    iq = iq_ref[...]
    iw = iw_ref[...]
    row = lax.broadcasted_iota(jnp.int32, (SAMPLE_Q, 1), 0)

    def scores(keys_t, first_idx):
        s = jnp.maximum(_dot(iq, keys_t), 0.0)
        n = s.shape[1]
        w = s * jnp.concatenate([iw] * (n // LANES), axis=1)
        acc = w[0:SAMPLE_Q]
        for hd in range(1, IDX_HEADS):
            acc = acc + w[hd * SAMPLE_Q:(hd + 1) * SAMPLE_Q]
        ramp = -(first_idx + lax.broadcasted_iota(jnp.int32, (1, n), 1)).astype(F32)
        return jnp.where(row < n_new, acc, ramp)

    cw = pg * PAGE_SIZE
    o_ref[...] = scores(jnp.concatenate([p[...] for p in pages], axis=1).astype(BF16), g * cw)

    @pl.when(g == pl.num_programs(1) - 1)
    def _():
        onew_ref[...] = scores(new_ref[...], pl.num_programs(1) * cw)


def _idx_scores(pool_idx_t, page_table, iqm, iwm, ik_new_t, pg, n_new):
    db, n_pages = page_table.shape
    ng = n_pages // pg
    kern = functools.partial(_idx_scores_kernel, pg=pg, n_new=n_new)
    per_b = lambda bi, g, pt: (bi, 0, 0)
    grid_spec = pltpu.PrefetchScalarGridSpec(
        num_scalar_prefetch=1,
        grid=(db, ng),
        in_specs=_page_specs((None, IDX_DIM, PAGE_SIZE), pg)
                 + [pl.BlockSpec((None, IDX_HEADS * SAMPLE_Q, IDX_DIM), per_b),
                    pl.BlockSpec((None, IDX_HEADS * SAMPLE_Q, LANES), per_b),
                    pl.BlockSpec((None, IDX_DIM, LANES), per_b)],
        out_specs=[pl.BlockSpec((None, SAMPLE_Q, pg * PAGE_SIZE), lambda bi, g, pt: (g, bi, 0)),
                   pl.BlockSpec((SAMPLE_Q, LANES), lambda bi, g, pt: (bi, 0))],
    )
    return pl.pallas_call(
        kern, grid_spec=grid_spec,
        out_shape=[jax.ShapeDtypeStruct((ng, db * SAMPLE_Q, pg * PAGE_SIZE), F32),
                   jax.ShapeDtypeStruct((db * SAMPLE_Q, LANES), F32)],
        compiler_params=_params(2),
        name="idx_scores",
    )(page_table, *([pool_idx_t] * pg), iqm, iwm, ik_new_t)


TOPK_ROWS = 64


def _dsa_topk_kernel(sc_ref, new_ref, o_ref, onew_ref, *, ng, cw, n_new, k_top, idx_bits, rb):
    row = lax.broadcasted_iota(jnp.int32, (rb, 1), 0)
    q = row & (SAMPLE_Q - 1)
    lane = lax.broadcasted_iota(jnp.int32, (1, LANES), 1)
    lane_cw = lax.broadcasted_iota(jnp.int32, (1, cw), 1)
    ok_new = (lane <= q) & (lane < n_new)
    x_new = jnp.where(ok_new, new_ref[...], NEG)
    idx_new = ng * cw + lane

    thr, vmax = _topk_threshold(lambda c, r0, nr: sc_ref[c, r0:r0 + nr, :], ng, k_top, idx_bits, rb, cw,
                                tail=(x_new, idx_new))
    for g in range(ng):
        o_ref[g] = _mask_bias(_topk_mask(sc_ref[g], g * cw + lane_cw, thr, vmax))
    onew_ref[...] = _mask_bias(_topk_mask(x_new, idx_new, thr, vmax) * jnp.where(ok_new, 1.0, 0.0))


def _dsa_topk(scores, scores_new, past, n_new):
    ng, rows, cw = scores.shape
    rb = min(TOPK_ROWS, rows)
    total = past + n_new
    kern = functools.partial(_dsa_topk_kernel, ng=ng, cw=cw, n_new=n_new, k_top=min(A_TOPK, total // 4),
                             idx_bits=(ng * cw + LANES - 1).bit_length(), rb=rb)
    big = pl.BlockSpec((ng, rb, cw), lambda r: (0, r, 0))
    small = pl.BlockSpec((rb, LANES), lambda r: (r, 0))
    return pl.pallas_call(
        kern,
        grid=(rows // rb,),
        in_specs=[big, small],
        out_specs=[big, small],
        out_shape=[jax.ShapeDtypeStruct((ng, rows, cw), F32), jax.ShapeDtypeStruct((rows, LANES), F32)],
        compiler_params=_params(1),
        name="dsa_topk",
    )(scores, scores_new)


QROWS = GROUP * SAMPLE_Q


def _paged_attn_kernel(pt_ref, *refs, pg, mask_rows):
    pages = refs[:pg]
    q_ref, mk_ref, mknew_ref, new_ref, o_ref, m_ref, l_ref, acc_ref = refs[pg:pg + 8]
    g = pl.program_id(1)

    @pl.when(g == 0)
    def _():
        m_ref[...] = jnp.full(m_ref.shape, NEG, F32)
        l_ref[...] = jnp.zeros(l_ref.shape, F32)
        acc_ref[...] = jnp.zeros(acc_ref.shape, F32)

    def update(keys, values, bias_all):
        carries = [(m_ref[kvh], l_ref[kvh], acc_ref[kvh]) for kvh in range(KV_HEADS)]
        for kvh in range(KV_HEADS):
            r0 = kvh * SAMPLE_Q if mask_rows == KV_HEADS * SAMPLE_Q else 0
            bias = jnp.concatenate([bias_all[r0:r0 + SAMPLE_Q]] * GROUP, axis=0)
            carries[kvh] = _softmax_step(carries[kvh], q_ref[kvh], keys(kvh), values(kvh), bias)
        for kvh in range(KV_HEADS):
            m_ref[kvh], l_ref[kvh], acc_ref[kvh] = carries[kvh]

    def paged(slot):
        return jnp.concatenate([_token_rows(p, slot, PAGE_SIZE) for p in pages], axis=0).astype(BF16)

    update(lambda kvh: paged(2 * kvh), lambda kvh: paged(2 * kvh + 1), mk_ref[...])

    @pl.when(g == pl.num_programs(1) - 1)
    def _():
        update(lambda kvh: new_ref[:, _kv_cols(kvh)[0]], lambda kvh: new_ref[:, _kv_cols(kvh)[1]], mknew_ref[...])
        for kvh in range(KV_HEADS):
            o_ref[kvh] = _softmax_done((m_ref[kvh], l_ref[kvh], acc_ref[kvh]))


def _paged_attn(pool, page_table, q, bias, bias_new, kv_new, pg):
    db, n_pages = page_table.shape
    ng = n_pages // pg
    mask_rows = bias.shape[1] // db
    pool_v = _cache_rows(pool)
    kern = functools.partial(_paged_attn_kernel, pg=pg, mask_rows=mask_rows)
    per_b3 = lambda bi, g, pt: (bi, 0, 0)
    per_b4 = lambda bi, g, pt: (bi, 0, 0, 0)
    grid_spec = pltpu.PrefetchScalarGridSpec(
        num_scalar_prefetch=1,
        grid=(db, ng),
        in_specs=_page_specs((PAGE_ROWS, HEAD_DIM), pg)
                 + [pl.BlockSpec((None, KV_HEADS, QROWS, HEAD_DIM), per_b4),
                    pl.BlockSpec((None, mask_rows, pg * PAGE_SIZE), lambda bi, g, pt: (g, bi, 0)),
                    pl.BlockSpec((mask_rows, LANES), lambda bi, g, pt: (bi, 0)),
                    pl.BlockSpec((None, LANES, KV_ROW), per_b3)],
        out_specs=pl.BlockSpec((None, KV_HEADS, QROWS, HEAD_DIM), per_b4),
        scratch_shapes=[pltpu.VMEM((KV_HEADS, QROWS, 1), F32), pltpu.VMEM((KV_HEADS, QROWS, 1), F32),
                        pltpu.VMEM((KV_HEADS, QROWS, HEAD_DIM), F32)],
    )
    return pl.pallas_call(
        kern, grid_spec=grid_spec,
        out_shape=jax.ShapeDtypeStruct((db, KV_HEADS, QROWS, HEAD_DIM), F32),
        compiler_params=_params(2),
        name="paged_attn",
    )(page_table, *([pool_v] * pg), q, bias, bias_new, kv_new)


BLOCKS_PER_PAGE = PAGE_SIZE // SLC_BLOCK
BLOCK_ROWS = SLC_BLOCK * ROWS_PER_TOKEN
assert BLOCKS_PER_PAGE == 2


def _slc_gather_kernel(pt_ref, sel_ref, *refs, nsel, ds, kvh):
    blocks = refs[:nsel]
    q_ref, bnew_ref, new_ref, o_ref = refs[nsel:nsel + 4]
    bi, qi = pl.program_id(0), pl.program_id(1)
    lane = lax.broadcasted_iota(jnp.int32, (1, LANES), 1)
    mine = (lax.broadcasted_iota(jnp.int32, (QROWS, 1), 0) & (SAMPLE_Q - 1)) == qi

    @pl.when(qi == 0)
    def _():
        o_ref[...] = jnp.zeros(o_ref.shape, F32)

    base = ((bi * ds + qi) * KV_HEADS + kvh) * nsel
    kcol, vcol = _kv_cols(kvh)
    keys = jnp.concatenate([_token_rows(blk, 2 * kvh, SLC_BLOCK) for blk in blocks], axis=0).astype(BF16)
    vals = jnp.concatenate([_token_rows(blk, 2 * kvh + 1, SLC_BLOCK) for blk in blocks], axis=0).astype(BF16)
    slot_bias = [jnp.where(sel_ref[base + j] >= 0, 0.0, NEG) for j in range(nsel)]
    pairs = [jnp.where(lane < SLC_BLOCK, slot_bias[j], slot_bias[min(j + 1, nsel - 1)])
             for j in range(0, nsel, 2)]
    bias = jnp.concatenate(pairs, axis=1)[:, :nsel * SLC_BLOCK]
    carry = _softmax_step(_softmax_init(QROWS), q_ref[...], keys, vals, bias)
    bnew = jnp.concatenate([bnew_ref[kvh * SAMPLE_Q:(kvh + 1) * SAMPLE_Q]] * GROUP, axis=0)
    carry = _softmax_step(carry, q_ref[...], new_ref[:, kcol], new_ref[:, vcol], bnew)
    o_ref[...] = jnp.where(mine, _softmax_done(carry), o_ref[...])


def _slc_gather(pool, page_table, sel, q, bias_new, kv_new, nsel, ds):
    db = page_table.shape[0]
    pool_v = _cache_rows(pool)
    outs = []
    for kvh in range(KV_HEADS):
        def spec(j, kvh=kvh):
            def imap(bi, qi, pt, sl):
                blk = jnp.maximum(sl[((bi * ds + qi) * KV_HEADS + kvh) * nsel + j], 0)
                return (pt[bi, lax.shift_right_logical(blk, 1)] * BLOCKS_PER_PAGE + (blk & 1), 0)
            return pl.BlockSpec((BLOCK_ROWS, HEAD_DIM), imap)

        grid_spec = pltpu.PrefetchScalarGridSpec(
            num_scalar_prefetch=2,
            grid=(db, ds),
            in_specs=[spec(j) for j in range(nsel)]
                     + [pl.BlockSpec((None, None, QROWS, HEAD_DIM), lambda bi, qi, pt, sl, kvh=kvh: (bi, kvh, 0, 0)),
                        pl.BlockSpec((KV_HEADS * SAMPLE_Q, LANES), lambda bi, qi, pt, sl: (bi, 0)),
                        pl.BlockSpec((None, LANES, KV_ROW), lambda bi, qi, pt, sl: (bi, 0, 0))],
            out_specs=pl.BlockSpec((None, QROWS, HEAD_DIM), lambda bi, qi, pt, sl: (bi, 0, 0)),
        )
        outs.append(pl.pallas_call(
            functools.partial(_slc_gather_kernel, nsel=nsel, ds=ds, kvh=kvh), grid_spec=grid_spec,
            out_shape=jax.ShapeDtypeStruct((db, QROWS, HEAD_DIM), F32),
            compiler_params=_params(2),
            name="slc_gather",
        )(page_table, sel, *([pool_v] * nsel), q, bias_new, kv_new))
    return jnp.stack(outs, axis=1)


def _nsa_sample_kernel(q_ref, kc_ref, vc_ref, ov_ref, o_ref, sel_ref, tmnew_ref, sc_ref,
                       *, past, n_new, n_cmp, n_blk, n_sel, ng, gb, nb):
    row = lax.broadcasted_iota(jnp.int32, (SAMPLE_Q, 1), 0)
    qpos = past + row
    ncp = kc_ref.shape[1]
    cidx = lax.broadcasted_iota(jnp.int32, (1, ncp), 1)
    cmask8 = jnp.where((CMP_STRIDE * cidx + CMP_LEN - 1 <= qpos) & (cidx < n_cmp), 1.0, 0.0)
    cmask = jnp.concatenate([cmask8] * GROUP, axis=0)
    lane = lax.broadcasted_iota(jnp.int32, (1, LANES), 1)

    for bb in range(nb):
        for kvh in range(KV_HEADS):
            kc = kc_ref[bb, :, kvh * HEAD_DIM:(kvh + 1) * HEAD_DIM]
            vc = vc_ref[bb, :, kvh * HEAD_DIM:(kvh + 1) * HEAD_DIM]
            s = _dot_nt(q_ref[bb, kvh], kc) * SCALE
            sm = jnp.where(cmask > 0.0, s, NEG)
            e = jnp.exp(sm - jnp.max(sm, axis=1, keepdims=True)) * cmask
            p = e / jnp.maximum(jnp.sum(e, axis=1, keepdims=True), 1e-30)
            o_ref[bb, kvh] = _dot(p.astype(BF16), vc)
            psum = p[0:SAMPLE_Q]
            for g in range(1, GROUP):
                psum = psum + p[g * SAMPLE_Q:(g + 1) * SAMPLE_Q]
            imp = _split_dot(psum, ov_ref[...])
            r0 = (bb * KV_HEADS + kvh) * SAMPLE_Q
            for g in range(ng + 1):
                blk = g * gb + lane
                v = _block_rules(imp[:, g * LANES:(g + 1) * LANES], blk, qpos)
                v = jnp.where(row < n_new, v, -blk.astype(F32))
                sc_ref[g, r0:r0 + SAMPLE_Q, :] = jnp.where((lane < gb) & (blk < n_blk), v, -jnp.inf)

    rows = nb * KV_HEADS * SAMPLE_Q
    idx_bits = ((ng + 1) * LANES - 1).bit_length()
    thr, vmax = _topk_threshold(lambda c, r0, nr: sc_ref[c, r0:r0 + nr, :], ng + 1, n_sel, idx_bits, rows, LANES)
    qpos2 = jnp.concatenate([qpos] * (nb * KV_HEADS), axis=0)
    n_past_blk = past // SLC_BLOCK
    tri = jnp.where(lax.broadcasted_iota(jnp.int32, (LANES, 1), 0) <= lane, 1.0, 0.0).astype(BF16)
    offs = jnp.zeros((rows, 1), F32)
    slots = [jnp.zeros((rows, 1), F32) for _ in range(n_sel)]
    for g in range(ng + 1):
        blk = g * gb + lane
        bm = _topk_mask(sc_ref[g], g * LANES + lane, thr, vmax)
        if g == ng:
            ok = jnp.where((past + lane <= qpos2) & (lane < n_new), 1.0, 0.0)
            tmnew_ref[...] = _mask_bias(bm[:, 0:1] * ok)
        bm = bm * jnp.where((lane < gb) & (blk < n_past_blk), 1.0, 0.0)
        rank = offs + _dot(bm.astype(BF16), tri)
        for j in range(n_sel):
            hit = jnp.where(bm > 0.0, jnp.where(rank == j + 1.0, blk.astype(F32), 0.0), 0.0)
            slots[j] = slots[j] + jnp.sum(hit, axis=1, keepdims=True)
        offs = offs + jnp.sum(bm, axis=1, keepdims=True)
    out = jnp.full((rows, LANES), -1.0, F32)
    for j in range(n_sel):
        out = jnp.where((lane == j) & (offs > j), slots[j], out)
    sel_ref[...] = out.astype(jnp.int32)


def _nsa_sample(q, kc, vc, past, n_new, ng, cw):
    db = q.shape[0]
    ncp = kc.shape[1]
    total = past + n_new
    n_blk = -(-total // SLC_BLOCK)
    n_cmp = total_chunks(past, n_new) - 1
    gb = cw // SLC_BLOCK
    assert gb <= LANES and past % cw == 0
    col = jnp.arange((ng + 1) * LANES)
    col_block = jnp.where(col % LANES < gb, (col // LANES) * gb + col % LANES, -1)
    col_block = jnp.where(col_block < n_blk, col_block, -1)
    ov = _overlap_matrix(ncp, (ng + 1) * LANES, col_block)
    nb = max(n for n in (4, 2, 1) if db % n == 0)
    kern = functools.partial(_nsa_sample_kernel, past=past, n_new=n_new, n_cmp=n_cmp, n_blk=n_blk,
                             n_sel=min(N_SEL, n_blk), ng=ng, gb=gb, nb=nb)
    rows = KV_HEADS * SAMPLE_Q
    b3 = lambda bi: (bi, 0, 0)
    b4 = lambda bi: (bi, 0, 0, 0)
    return pl.pallas_call(
        kern,
        grid=(db // nb,),
        in_specs=[pl.BlockSpec((nb, KV_HEADS, QROWS, HEAD_DIM), b4),
                  pl.BlockSpec((nb, ncp, KV_HEADS * HEAD_DIM), b3),
                  pl.BlockSpec((nb, ncp, KV_HEADS * HEAD_DIM), b3),
                  pl.BlockSpec((ncp, (ng + 1) * LANES), lambda bi: (0, 0))],
        out_specs=[pl.BlockSpec((nb, KV_HEADS, QROWS, HEAD_DIM), b4),
                   pl.BlockSpec((nb * rows, LANES), lambda bi: (bi, 0)),
                   pl.BlockSpec((nb * rows, LANES), lambda bi: (bi, 0))],
        out_shape=[jax.ShapeDtypeStruct((db, KV_HEADS, QROWS, HEAD_DIM), F32),
                   jax.ShapeDtypeStruct((db * rows, LANES), jnp.int32),
                   jax.ShapeDtypeStruct((db * rows, LANES), F32)],
        scratch_shapes=[pltpu.VMEM((ng + 1, nb * rows, LANES), F32)],
        compiler_params=_params(1),
        name="nsa_sample",
    )(q, kc, vc, ov)


def total_chunks(past, n_new):
    return past // CMP_STRIDE + -(-n_new // CMP_STRIDE)


def _win_sample_kernel(q_ref, st_ref, new_ref, o_ref, *, w_buf, n_new):
    row = lax.broadcasted_iota(jnp.int32, (SAMPLE_Q, 1), 0)
    row = jnp.concatenate([row] * GROUP, axis=0)
    j_old = lax.broadcasted_iota(jnp.int32, (1, w_buf), 1)
    j_new = w_buf + lax.broadcasted_iota(jnp.int32, (1, LANES), 1)
    def bias(j, valid):
        d = w_buf + row - j
        return jnp.where((d >= 0) & (d < WINDOW) & valid, 0.0, NEG)
    b_old, b_new = bias(j_old, True), bias(j_new, j_new < w_buf + n_new)
    for kvh in range(KV_HEADS):
        q = q_ref[kvh]
        kcol, vcol = _kv_cols(kvh)
        carry = _softmax_step(_softmax_init(QROWS), q, _token_rows(st_ref, 2 * kvh, w_buf).astype(BF16),
                              _token_rows(st_ref, 2 * kvh + 1, w_buf).astype(BF16), b_old)
        carry = _softmax_step(carry, q, new_ref[:, kcol], new_ref[:, vcol], b_new)
        o_ref[kvh] = _softmax_done(carry)


def _win_sample(q, state, kv_new, n_new):
    db, w_buf = state.shape[0], state.shape[1]
    kern = functools.partial(_win_sample_kernel, w_buf=w_buf, n_new=n_new)
    b3 = lambda bi: (bi, 0, 0)
    b4 = lambda bi: (bi, 0, 0, 0)
    return pl.pallas_call(
        kern,
        grid=(db,),
        in_specs=[pl.BlockSpec((None, KV_HEADS, QROWS, HEAD_DIM), b4),
                  pl.BlockSpec((w_buf * ROWS_PER_TOKEN, HEAD_DIM), lambda bi: (bi, 0)),
                  pl.BlockSpec((None, LANES, KV_ROW), b3)],
        out_specs=pl.BlockSpec((None, KV_HEADS, QROWS, HEAD_DIM), b4),
        out_shape=jax.ShapeDtypeStruct((db, KV_HEADS, QROWS, HEAD_DIM), F32),
        compiler_params=_params(1),
        name="win_sample",
    )(q, _cache_rows(state), kv_new)


def _gate_merge_kernel(oa_ref, oc_ref, os_ref, ow_ref, misc_ref, sza_ref, szb_ref, a_ref, b_ref):
    misc = misc_ref[...]
    a_ref[...] = (oa_ref[...] * sza_ref[...]).astype(BF16)
    for hd in range(N_HEADS):
        sl = slice(hd * HEAD_DIM, (hd + 1) * HEAD_DIM)
        g = [misc[:, MISC_IW + j * N_HEADS + hd:MISC_IW + j * N_HEADS + hd + 1] for j in range(3)]
        o_b = g[0] * oc_ref[:, sl] + g[1] * os_ref[:, sl] + g[2] * ow_ref[:, sl]
        b_ref[:, sl] = (o_b * szb_ref[:, sl]).astype(BF16)


def _gate_merge(o_a, o_cmp, o_slc, o_win, misc, sza, szb):
    n = o_a.shape[0]
    full = lambda w: pl.BlockSpec((n, w), lambda i: (0, 0))
    out = jax.ShapeDtypeStruct((n, WIDTH), BF16)
    return pl.pallas_call(
        _gate_merge_kernel,
        grid=(1,),
        in_specs=[full(WIDTH)] * 4 + [full(LANES), full(WIDTH), full(WIDTH)],
        out_specs=[full(WIDTH)] * 2,
        out_shape=[out, out],
        compiler_params=_params(1),
        name="gate_merge",
    )(o_a, o_cmp, o_slc, o_win, misc, sza, szb)


def _merge_kernel(a_ref, b_ref, h_ref, wa_ref, wb_ref, wg0_ref, wg1_ref, o_ref):
    h = h_ref[...]
    y_a = _dot(a_ref[...], wa_ref[...])
    y_b = _dot(b_ref[...], wb_ref[...])
    g0 = jax.nn.sigmoid(_dot(h, wg0_ref[...]))
    g1 = jax.nn.sigmoid(_dot(h, wg1_ref[...]))
    o_ref[...] = (g0 * y_a + g1 * y_b).astype(BF16)


def _merge(a, bm, h, w_a, w_b, w_g0, w_g1, tr, cb):
    n = a.shape[0]
    row = lambda j, i: (i, 0)
    col = lambda j, i: (0, j)
    return pl.pallas_call(
        _merge_kernel,
        grid=(D_MODEL // cb, n // tr),
        in_specs=[pl.BlockSpec((tr, WIDTH), row), pl.BlockSpec((tr, WIDTH), row),
                  pl.BlockSpec((tr, D_MODEL), row),
                  pl.BlockSpec((WIDTH, cb), col), pl.BlockSpec((WIDTH, cb), col),
                  pl.BlockSpec((D_MODEL, cb), col), pl.BlockSpec((D_MODEL, cb), col)],
        out_specs=pl.BlockSpec((tr, cb), lambda j, i: (i, j)),
        out_shape=jax.ShapeDtypeStruct((n, D_MODEL), BF16),
        compiler_params=_params(2),
        name="merge",
    )(a, bm, h, w_a, w_b, w_g0, w_g1)


def _out_proj_kernel(x_ref, m_ref, w_ref, o_ref):
    o_ref[...] = x_ref[...] + _dot(m_ref[...], w_ref[...])


def _out_proj(x, merged, w_out, tr):
    n = x.shape[0]
    row = lambda i: (i, 0)
    return pl.pallas_call(
        _out_proj_kernel,
        grid=(n // tr,),
        in_specs=[pl.BlockSpec((tr, D_MODEL), row), pl.BlockSpec((tr, D_MODEL), row),
                  pl.BlockSpec((D_MODEL, D_MODEL), lambda i: (0, 0))],
        out_specs=pl.BlockSpec((tr, D_MODEL), row),
        out_shape=jax.ShapeDtypeStruct((n, D_MODEL), F32),
        compiler_params=_params(1),
        name="out_proj",
    )(x, merged, w_out)


def _project(x2d, pos_period, tr, wts):
    n = x2d.shape[0]
    n_tab = pos_period.shape[0] // tr
    tabs_head = _rope_tables(pos_period, ROT_DIM, HEAD_DIM)
    tabs_iq = _rope_tables(pos_period, IDX_ROT_DIM, IDX_DIM)
    tabs_iq = tuple(jnp.concatenate([t, t], axis=1) for t in tabs_iq)
    ident = (jnp.ones_like(tabs_iq[0][:, :IDX_DIM]), jnp.zeros_like(tabs_iq[0][:, :IDX_DIM]),
             jnp.zeros_like(tabs_iq[0][:, :IDX_DIM]))
    tabs_misc = tuple(jnp.concatenate([t[:, :IDX_DIM], e], axis=1) for t, e in zip(tabs_iq, ident))

    h = _rmsnorm(x2d, wts["norm_gain"], tr)
    (qa,) = _proj_q(h, wts["w_qa"], wts["q_norm_a"], tabs_head, tr, n_tab, False)
    qb_rot, qb = _proj_q(h, wts["w_qb"], wts["q_norm_b"], tabs_head, tr, n_tab, True)
    akv, cmp_kv, slc, win, akv_b, slc_b, win_b = _proj_kv(
        h, wts["w_kv"], wts["k_norm_a"], wts["k_norm_slc"], wts["k_norm_win"], tabs_head, tr, n_tab)
    iq, misc = _proj_idx(h, wts["w_idx"], tabs_iq, tabs_misc, tr, n_tab)
    sza, szb = _proj_z(h, wts["w_z"], tr)
    return dict(h=h, qa=qa, qb=qb, qb_rot=qb_rot, akv=akv, cmp=cmp_kv, slc=slc, win=win,
                akv_b=akv_b, slc_b=slc_b, win_b=win_b, iq=iq, misc=misc, sza=sza, szb=szb)


def _sample_rows(a, db, ds):
    a = a.reshape(db, ds, KV_HEADS, GROUP, HEAD_DIM).transpose(0, 2, 3, 1, 4)
    a = jnp.pad(a, ((0, 0), (0, 0), (0, 0), (0, SAMPLE_Q - ds), (0, 0)))
    return a.reshape(db, KV_HEADS, QROWS, HEAD_DIM)


def _unsample_rows(o, db, ds):
    o = o.reshape(db, KV_HEADS, GROUP, SAMPLE_Q, HEAD_DIM)[:, :, :, :ds]
    return o.transpose(0, 3, 1, 2, 4).reshape(db * ds, WIDTH)


def _pad_new(a, db, ds):
    return jnp.pad(a.reshape(db, ds, a.shape[-1]), ((0, 0), (0, LANES - ds), (0, 0)))


def kernel(x_prompt, x_sample, cache_a_kv, cache_a_idx, cache_cmp_kv, cache_slc_kv, state_win_kv,
           page_table, norm_gain, w_in, q_norm_a, k_norm_a, q_norm_b, k_norm_cmp, k_norm_slc,
           k_norm_win, pe_cmp, w1_cmp, w2_cmp, w_proj_a, w_proj_b, w_out):
    b, t, _ = x_prompt.shape
    db, ds, _ = x_sample.shape
    n_pages = page_table.shape[1]
    past = n_pages * PAGE_SIZE
    assert ds <= SAMPLE_Q and ds <= CMP_STRIDE

    sizes = (WIDTH, KV_ROW, IDX_COLS, IDX_DIM, IDX_HEADS, WIDTH, WIDTH, 3 * KV_ROW, 3 * N_HEADS, WIDTH,
             2 * D_MODEL)
    offs = [0]
    for s in sizes:
        offs.append(offs[-1] + s)
    wb = w_in.astype(BF16)
    seg = lambda j: wb[:, offs[j]:offs[j + 1]]
    misc_pad = jnp.zeros((D_MODEL, LANES - (IDX_DIM + IDX_HEADS + 3 * N_HEADS)), BF16)
    wts = dict(
        norm_gain=norm_gain, q_norm_a=q_norm_a, q_norm_b=q_norm_b, k_norm_a=k_norm_a,
        k_norm_slc=k_norm_slc, k_norm_win=k_norm_win,
        w_qa=seg(0), w_qb=seg(6),
        w_kv=jnp.concatenate([seg(1), seg(7)], axis=1),
        w_idx=jnp.concatenate([seg(2), seg(3), seg(4), seg(8), misc_pad], axis=1),
        w_z=jnp.concatenate([seg(5), seg(9)], axis=1),
    )
    w_g0 = wb[:, offs[10]:offs[10] + D_MODEL]
    w_g1 = wb[:, offs[10] + D_MODEL:offs[11]]
    w_pa, w_pb, w_o = w_proj_a.astype(BF16), w_proj_b.astype(BF16), w_out.astype(BF16)
    w1_fs = jnp.concatenate([w1_cmp[:, :CMP_STRIDE].reshape(2, CMP_STRIDE * HEAD_DIM, CMP_HIDDEN),
                             w1_cmp[:, CMP_STRIDE:].reshape(2, CMP_STRIDE * HEAD_DIM, CMP_HIDDEN)],
                            axis=2).astype(BF16)
    w1_r = w1_cmp.reshape(2, CMP_LEN * HEAD_DIM, CMP_HIDDEN).astype(BF16)
    pe_r = pe_cmp.reshape(2, CMP_LEN * HEAD_DIM)
    w2_b = w2_cmp.astype(BF16)

    tr = min(512, t)
    tq = min(256, t)
    tk = min(512, t)
    xp = x_prompt.reshape(b * t, D_MODEL)
    pp = _project(xp, jnp.arange(t, dtype=jnp.int32), tr, wts)

    pages_p = t // PAGE_SIZE
    table_p = jnp.arange(b * pages_p, dtype=jnp.int32).reshape(b, pages_p)
    fs_p = _chunk_terms_paged(pp["cmp"], table_p, w1_fs, min(PAGES_PER_STEP, pages_p))
    kc_p, vc_p = _compress_mlp(fs_p, jnp.zeros((b, SUBLANES, FS_COLS), F32), pe_r, w1_r, w2_b, k_norm_cmp)

    ik_b = pp["misc"][:, :IDX_DIM].astype(BF16)
    ik2 = jnp.concatenate([ik_b, ik_b], axis=1).reshape(b, t, LANES)
    iw_t = pp["misc"][:, MISC_IK:MISC_IW].T
    a_p = _dsa_prompt(pp["iq"], iw_t, ik2, pp["qa"], pp["akv_b"].reshape(b, t, KV_ROW), pp["sza"],
                      b, t, tq, tk)
    win_front = jnp.pad(pp["win_b"].reshape(b, t, KV_ROW), ((0, 0), (WINDOW, 0), (0, 0)))
    b_p = _nsa_prompt(pp["qb"], pp["qb_rot"], kc_p, vc_p, pp["slc_b"].reshape(b, t, KV_ROW),
                      win_front, pp["misc"], pp["szb"], b, t, tq, tk)
    merged_p = _merge(a_p, b_p, pp["h"], w_pa, w_pb, w_g0, w_g1, tr, 1024)
    y_prompt = _out_proj(xp, merged_p, w_o, tr).reshape(b, t, D_MODEL)

    kv5 = lambda a, n0, n1: a.reshape(n0, n1, KV_HEADS, 2, HEAD_DIM)
    p_a_kv = kv5(pp["akv"], b, t)
    p_a_idx = pp["misc"][:, :IDX_DIM].reshape(b, t, IDX_DIM)
    p_cmp_kv = kv5(pp["cmp"], b, t)
    p_slc_kv = kv5(pp["slc"], b, t)
    p_win_kv = kv5(pp["win"], b, t)[:, t - min(WINDOW, t):]

    ns = db * ds
    xs = x_sample.reshape(ns, D_MODEL)
    pos_s = past + jnp.tile(jnp.arange(ds, dtype=jnp.int32), db)
    ps = _project(xs, pos_s, ns, wts)
    pg = min(PAGES_PER_STEP, n_pages)
    ng = n_pages // pg
    cw = pg * PAGE_SIZE

    iq_s = ps["iq"].reshape(db, ds, IDX_HEADS, IDX_DIM).transpose(0, 2, 1, 3)
    iqm = jnp.pad(iq_s, ((0, 0), (0, 0), (0, SAMPLE_Q - ds), (0, 0))).reshape(db, IDX_HEADS * SAMPLE_Q, IDX_DIM)
    iw_s = ps["misc"][:, MISC_IK:MISC_IW].reshape(db, ds, IDX_HEADS).transpose(0, 2, 1) * (IDX_DIM ** -0.5)
    iwm = jnp.pad(iw_s, ((0, 0), (0, 0), (0, SAMPLE_Q - ds))).reshape(db, IDX_HEADS * SAMPLE_Q, 1)
    iwm = jnp.broadcast_to(iwm, (db, IDX_HEADS * SAMPLE_Q, LANES))
    ik_new_t = jnp.swapaxes(_pad_new(ps["misc"][:, :IDX_DIM].astype(BF16), db, ds), 1, 2)
    sc_s, sc_new = _idx_scores(jnp.swapaxes(cache_a_idx, 1, 2), page_table, iqm, iwm, ik_new_t, pg, ds)
    mk_a, mk_a_new = _dsa_topk(sc_s, sc_new, past, ds)
    o_a_s = _paged_attn(cache_a_kv, page_table, _sample_rows(ps["qa"], db, ds), mk_a, mk_a_new,
                        _pad_new(ps["akv_b"], db, ds), pg)

    fs_past = _chunk_terms_paged(cache_cmp_kv, page_table, w1_fs, pg)
    new_chunk = jnp.pad(ps["cmp"].reshape(db, ds, KV_ROW), ((0, 0), (0, CMP_STRIDE - ds), (0, 0)))
    fs_new = _chunk_terms_dense(new_chunk.reshape(db, CHUNK_FLAT), w1_fs, db)
    fs_new = jnp.pad(fs_new.reshape(db, 1, FS_COLS), ((0, 0), (0, SUBLANES - 1), (0, 0)))
    kc_s, vc_s = _compress_mlp(fs_past, fs_new, pe_r, w1_r, w2_b, k_norm_cmp)
    o_cmp_s, sel_s, mk_s_new = _nsa_sample(_sample_rows(ps["qb"], db, ds), kc_s, vc_s, past, ds, ng, cw)
    q_rot_s = _sample_rows(ps["qb_rot"], db, ds)
    n_sel_s = min(N_SEL, -(-(past + ds) // SLC_BLOCK))
    sel_s = sel_s[:, :n_sel_s].reshape(db, KV_HEADS, SAMPLE_Q, n_sel_s)[:, :, :ds]
    sel_s = sel_s.transpose(0, 2, 1, 3).reshape(-1)
    o_slc_s = _slc_gather(cache_slc_kv, page_table, sel_s, q_rot_s, mk_s_new, _pad_new(ps["slc_b"], db, ds),
                          n_sel_s, ds)
    o_win_s = _win_sample(q_rot_s, state_win_kv, _pad_new(ps["win_b"], db, ds), ds)

    a_s, b_s = _gate_merge(_unsample_rows(o_a_s, db, ds), _unsample_rows(o_cmp_s, db, ds),
                           _unsample_rows(o_slc_s, db, ds), _unsample_rows(o_win_s, db, ds),
                           ps["misc"], ps["sza"], ps["szb"])
    merged_s = _merge(a_s, b_s, ps["h"], w_pa, w_pb, w_g0, w_g1, ns, 1024)
    y_sample = _out_proj(xs, merged_s, w_o, ns).reshape(db, ds, D_MODEL)

    s_win = kv5(ps["win"], db, ds)
    w_buf = state_win_kv.shape[1]
    s_win_kv = jnp.concatenate([state_win_kv, s_win], axis=1)[:, ds:ds + w_buf]

    return (y_prompt, y_sample, p_a_kv, p_a_idx, p_cmp_kv, p_slc_kv, p_win_kv,
            kv5(ps["akv"], db, ds), ps["misc"][:, :IDX_DIM].reshape(db, ds, IDX_DIM),
            kv5(ps["cmp"], db, ds), kv5(ps["slc"], db, ds), s_win_kv)
```

```python
import functools

import jax
import jax.numpy as jnp
from jax import lax
from jax.experimental import pallas as pl
from jax.experimental.pallas import tpu as pltpu

D_MODEL = 2048
HEAD_DIM = 128
ROT_DIM = HEAD_DIM // 4
N_HEADS = D_MODEL // (2 * HEAD_DIM)
KV_HEADS = 2
GROUP = N_HEADS // KV_HEADS
IDX_HEADS = 8
IDX_DIM = 64
IDX_ROT_DIM = IDX_DIM // 4
A_TOPK = 256
CMP_LEN = 32
CMP_STRIDE = 16
CMP_HIDDEN = 128
SLC_BLOCK = 64
N_SEL = 16
WINDOW = 512
PAGE_SIZE = 128
ROPE_THETA = 500000.0
EPS = 1e-6
NEG = -1e30
WIDTH = N_HEADS * HEAD_DIM
KV_ROW = KV_HEADS * 2 * HEAD_DIM
SCALE = HEAD_DIM ** -0.5

LANES = 128
SUBLANES = 8
VMEM_LIMIT = 48 * 1024 * 1024
PAGES_PER_STEP = 32
SAMPLE_Q = 8

F32 = jnp.float32
BF16 = jnp.bfloat16
NT_DIMS = (((1,), (1,)), ((), ()))


def _params(n_axes):
    return pltpu.CompilerParams(dimension_semantics=("arbitrary",) * n_axes,
                                vmem_limit_bytes=VMEM_LIMIT)


def _dot(a, b):
    return jnp.dot(a, b, preferred_element_type=F32)


def _dot_nt(a, b):
    return lax.dot_general(a, b, NT_DIMS, preferred_element_type=F32)


def _rope_tables(pos, rot, period, width=LANES):
    half = rot // 2
    inv = ROPE_THETA ** (-jnp.arange(half, dtype=F32) / half)
    ang = pos.astype(F32)[:, None] * inv
    cos, sin = jnp.cos(ang), jnp.sin(ang)
    n = pos.shape[0]
    zeros = lambda w: jnp.zeros((n, w), F32)
    ones = lambda w: jnp.ones((n, w), F32)
    c = jnp.concatenate([cos, cos, ones(period - rot)], axis=1)
    s1 = jnp.concatenate([-sin, zeros(period - half)], axis=1)
    s2 = jnp.concatenate([zeros(half), sin, zeros(period - rot)], axis=1)
    return c, s1, s2


def _apply_rope(y, c, s1, s2, half):
    return y * c + pltpu.roll(y, LANES - half, 1) * s1 + pltpu.roll(y, half, 1) * s2


def _head_norm(z, g):
    return z * lax.rsqrt(jnp.mean(z * z, axis=-1, keepdims=True) + EPS) * g


def _rmsnorm_kernel(x_ref, g_ref, o_ref):
    x = x_ref[...]
    y = x * lax.rsqrt(jnp.mean(x * x, axis=-1, keepdims=True) + EPS)
    o_ref[...] = (y * g_ref[...]).astype(o_ref.dtype)


def _rmsnorm(x, gain, tr):
    n, d = x.shape
    return pl.pallas_call(
        _rmsnorm_kernel,
        grid=(n // tr,),
        in_specs=[pl.BlockSpec((tr, d), lambda i: (i, 0)), pl.BlockSpec((1, d), lambda i: (0, 0))],
        out_specs=pl.BlockSpec((tr, d), lambda i: (i, 0)),
        out_shape=jax.ShapeDtypeStruct((n, d), BF16),
        compiler_params=_params(1),
        name="rmsnorm",
    )(x, gain.reshape(1, d))


PROJ_ROWS = 128


def _proj_q_kernel(h_ref, w_ref, g_ref, c_ref, s1_ref, s2_ref, rot_ref, *plain_ref):
    g = g_ref[...]
    n = h_ref.shape[0]
    nh = max(1, n // PROJ_ROWS)
    for r in range(nh):
        rs = slice(r * (n // nh), (r + 1) * (n // nh))
        z = _dot(h_ref[rs, :], w_ref[...])
        c, s1, s2 = c_ref[rs, :], s1_ref[rs, :], s2_ref[rs, :]
        for hd in range(N_HEADS):
            sl = slice(hd * HEAD_DIM, (hd + 1) * HEAD_DIM)
            y = _head_norm(z[:, sl], g)
            if plain_ref:
                plain_ref[0][rs, sl] = y.astype(BF16)
            rot_ref[rs, sl] = _apply_rope(y, c, s1, s2, ROT_DIM // 2).astype(BF16)


def _proj_q(h, w, gain, tabs, tr, n_tab_blocks, want_plain):
    n = h.shape[0]
    row = lambda i: (i, 0)
    fix = lambda i: (0, 0)
    tab = lambda i: (i % n_tab_blocks, 0)
    out_shape = [jax.ShapeDtypeStruct((n, WIDTH), BF16)] * (2 if want_plain else 1)
    out_specs = [pl.BlockSpec((tr, WIDTH), row)] * (2 if want_plain else 1)
    return pl.pallas_call(
        _proj_q_kernel,
        grid=(n // tr,),
        in_specs=[pl.BlockSpec((tr, D_MODEL), row), pl.BlockSpec((D_MODEL, WIDTH), fix),
                  pl.BlockSpec((1, HEAD_DIM), fix)] + [pl.BlockSpec((tr, LANES), tab)] * 3,
        out_specs=out_specs,
        out_shape=out_shape,
        compiler_params=_params(1),
        name="proj_q",
    )(h, w, gain.reshape(1, HEAD_DIM), *tabs)


def _proj_kv_kernel(h_ref, w_ref, ga_ref, gs_ref, gw_ref, c_ref, s1_ref, s2_ref,
                    akv_ref, cmp_ref, slc_ref, win_ref, akv_b_ref, slc_b_ref, win_b_ref):
    n = h_ref.shape[0]
    nh = max(1, n // PROJ_ROWS)
    nr = n // nh
    for r in range(nh):
        rs = slice(r * nr, (r + 1) * nr)
        z = _dot(h_ref[rs, :], w_ref[...])
        c, s1, s2 = c_ref[rs, :], s1_ref[rs, :], s2_ref[rs, :]

        def emit(base, gain_ref, f32_ref, bf_ref):
            for j in range(KV_HEADS * 2):
                sl = slice(j * HEAD_DIM, (j + 1) * HEAD_DIM)
                v = z[:, base + j * HEAD_DIM: base + (j + 1) * HEAD_DIM]
                if gain_ref is not None and j % 2 == 0:
                    v = _apply_rope(_head_norm(v, gain_ref[...]), c, s1, s2, ROT_DIM // 2)
                f32_ref[pl.ds(r * nr * KV_HEADS * 2 + j, nr, stride=KV_HEADS * 2), :] = v
                if bf_ref is not None:
                    bf_ref[rs, sl] = v.astype(BF16)

        emit(0 * KV_ROW, ga_ref, akv_ref, akv_b_ref)
        emit(1 * KV_ROW, None, cmp_ref, None)
        emit(2 * KV_ROW, gs_ref, slc_ref, slc_b_ref)
        emit(3 * KV_ROW, gw_ref, win_ref, win_b_ref)


def _proj_kv(h, w, g_a, g_slc, g_win, tabs, tr, n_tab_blocks):
    n = h.shape[0]
    row = lambda i: (i, 0)
    fix = lambda i: (0, 0)
    tab = lambda i: (i % n_tab_blocks, 0)
    f32_out = jax.ShapeDtypeStruct((n * KV_HEADS * 2, HEAD_DIM), F32)
    bf_out = jax.ShapeDtypeStruct((n, KV_ROW), BF16)
    return pl.pallas_call(
        _proj_kv_kernel,
        grid=(n // tr,),
        in_specs=[pl.BlockSpec((tr, D_MODEL), row), pl.BlockSpec((D_MODEL, 4 * KV_ROW), fix)]
                 + [pl.BlockSpec((1, HEAD_DIM), fix)] * 3 + [pl.BlockSpec((tr, LANES), tab)] * 3,
        out_specs=[pl.BlockSpec((tr * KV_HEADS * 2, HEAD_DIM), row)] * 4 + [pl.BlockSpec((tr, KV_ROW), row)] * 3,
        out_shape=[f32_out] * 4 + [bf_out] * 3,
        compiler_params=_params(1),
        name="proj_kv",
    )(h, w, g_a.reshape(1, HEAD_DIM), g_slc.reshape(1, HEAD_DIM), g_win.reshape(1, HEAD_DIM), *tabs)


IDX_COLS = IDX_HEADS * IDX_DIM
MISC_IK = IDX_DIM
MISC_IW = MISC_IK + IDX_HEADS
MISC_G = MISC_IW + 3 * N_HEADS


def _proj_idx_kernel(h_ref, w_ref, c_ref, s1_ref, s2_ref, cm_ref, s1m_ref, s2m_ref, iq_ref, misc_ref):
    z = _dot(h_ref[...], w_ref[...])
    half = IDX_ROT_DIM // 2
    c, s1, s2 = c_ref[...], s1_ref[...], s2_ref[...]
    for j in range(IDX_COLS // LANES):
        sl = slice(j * LANES, (j + 1) * LANES)
        iq_ref[:, sl] = _apply_rope(z[:, sl], c, s1, s2, half).astype(BF16)
    m = z[:, IDX_COLS:IDX_COLS + LANES]
    roped = _apply_rope(m, cm_ref[...], s1m_ref[...], s2m_ref[...], half)
    lane = lax.broadcasted_iota(jnp.int32, m.shape, 1)
    misc_ref[...] = jnp.where(lane < MISC_IK, roped,
                              jnp.where(lane < MISC_IW, m * (IDX_HEADS ** -0.5),
                                        jnp.where(lane < MISC_G, jax.nn.sigmoid(m), 0.0)))


def _proj_idx(h, w, tabs_iq, tabs_misc, tr, n_tab_blocks):
    n = h.shape[0]
    row = lambda i: (i, 0)
    fix = lambda i: (0, 0)
    tab = lambda i: (i % n_tab_blocks, 0)
    return pl.pallas_call(
        _proj_idx_kernel,
        grid=(n // tr,),
        in_specs=[pl.BlockSpec((tr, D_MODEL), row), pl.BlockSpec((D_MODEL, IDX_COLS + LANES), fix)]
                 + [pl.BlockSpec((tr, LANES), tab)] * 6,
        out_specs=[pl.BlockSpec((tr, IDX_COLS), row), pl.BlockSpec((tr, LANES), row)],
        out_shape=[jax.ShapeDtypeStruct((n, IDX_COLS), BF16), jax.ShapeDtypeStruct((n, LANES), F32)],
        compiler_params=_params(1),
        name="proj_idx",
    )(h, w, *tabs_iq, *tabs_misc)


def _proj_z_kernel(h_ref, w_ref, za_ref, zb_ref):
    n = h_ref.shape[0]
    nh = max(1, n // PROJ_ROWS)
    for r in range(nh):
        rs = slice(r * (n // nh), (r + 1) * (n // nh))
        z = _dot(h_ref[rs, :], w_ref[...])
        s = z * jax.nn.sigmoid(z)
        za_ref[rs, :] = s[:, :WIDTH]
        zb_ref[rs, :] = s[:, WIDTH:]


def _proj_z(h, w, tr):
    n = h.shape[0]
    row = lambda i: (i, 0)
    out = jax.ShapeDtypeStruct((n, WIDTH), F32)
    return pl.pallas_call(
        _proj_z_kernel,
        grid=(n // tr,),
        in_specs=[pl.BlockSpec((tr, D_MODEL), row), pl.BlockSpec((D_MODEL, 2 * WIDTH), lambda i: (0, 0))],
        out_specs=[pl.BlockSpec((tr, WIDTH), row)] * 2,
        out_shape=[out, out],
        compiler_params=_params(1),
        name="proj_z",
    )(h, w)


def _key_to_float(u):
    int_min = jnp.int32(-2 ** 31)
    bits = jnp.where(u < 0, u ^ int_min, ~u)
    return lax.bitcast_convert_type(bits, F32)


def _lane_fold(w):
    acc = w[:, :LANES]
    for j in range(1, w.shape[1] // LANES):
        acc = acc + w[:, j * LANES:(j + 1) * LANES]
    return acc


COUNT_ROWS = 1024


def _topk_threshold(load, nck, k, idx_bits, rows, cw, tail=None):
    kf = jnp.float32(k)
    lane = lax.broadcasted_iota(jnp.int32, (1, cw), 1)
    nr = min(rows, COUNT_ROWS)

    def count(pred_value):
        out = []
        for r0 in range(0, rows, nr):
            rs = slice(r0, r0 + nr)

            def body(c, acc, r0=r0, rs=rs):
                return acc + _lane_fold(pred_value(load(c, r0, nr), c * cw + lane, rs))
            acc = lax.fori_loop(0, nck, body, jnp.zeros((nr, LANES), F32))
            if tail is not None:
                acc = acc + pred_value(tail[0][rs], tail[1], rs)
            out.append(jnp.sum(acc, axis=1, keepdims=True))
        return out[0] if len(out) == 1 else jnp.concatenate(out, axis=0)

    def value_bit(it, u):
        cand = u | jnp.left_shift(jnp.int32(1), 31 - it)
        candf = _key_to_float(cand)
        cnt = count(lambda x, idx, rs: jnp.where(x >= candf[rs], 1.0, 0.0))
        return jnp.where(cnt >= kf, cand, u)

    u = lax.fori_loop(0, 32, value_bit, jnp.zeros((rows, 1), jnp.int32))
    thr = jnp.where((u >= 0) & (u < 2 ** 23), -jnp.inf, _key_to_float(u))
    need = kf - count(lambda x, idx, rs: jnp.where(x > thr[rs], 1.0, 0.0))

    def index_bit(it, v):
        cand = v | jnp.left_shift(jnp.int32(1), idx_bits - 1 - it)
        below = count(lambda x, idx, rs: jnp.where(x == thr[rs], jnp.where(idx < cand[rs], 1.0, 0.0), 0.0))
        return jnp.where(below < need, cand, v)

    tied = jnp.max(count(lambda x, idx, rs: jnp.where(x >= thr[rs], 1.0, 0.0))) > kf
    v = lax.cond(tied,
                 lambda: lax.fori_loop(0, idx_bits, index_bit, jnp.zeros((rows, 1), jnp.int32)),
                 lambda: jnp.full((rows, 1), (1 << idx_bits) - 1, jnp.int32))
    return thr, jnp.where(need >= 1.0, v, -1)


def _topk_mask(x, idx, thr, vmax):
    gt = jnp.where(x > thr, 1.0, 0.0)
    eq = jnp.where(x == thr, jnp.where(idx <= vmax, 1.0, 0.0), 0.0)
    return jnp.maximum(gt, eq)


def _sublane_fold(w):
    parts = [w[j * SUBLANES:(j + 1) * SUBLANES] for j in range(w.shape[0] // SUBLANES)]
    while len(parts) > 1:
        parts = [parts[j] + parts[j + 1] if j + 1 < len(parts) else parts[j] for j in range(0, len(parts), 2)]
    return parts[0]


def _topk_threshold_cols(load, nck, k, idx_bits, cols, ck):
    kf = jnp.float32(k)
    sub = lax.broadcasted_iota(jnp.int32, (ck, 1), 0)

    def count(pred_value):
        def body(c, acc):
            return acc + _sublane_fold(pred_value(load(c), c * ck + sub))
        acc = lax.fori_loop(0, nck, body, jnp.zeros((SUBLANES, cols), F32))
        return jnp.sum(acc, axis=0, keepdims=True)

    def value_bit(it, u):
        cand = u | jnp.left_shift(jnp.int32(1), 31 - it)
        candf = _key_to_float(cand)
        cnt = count(lambda x, idx: jnp.where(x >= candf, 1.0, 0.0))
        return jnp.where(cnt >= kf, cand, u)

    u = lax.fori_loop(0, 32, value_bit, jnp.zeros((1, cols), jnp.int32))
    thr = jnp.where((u >= 0) & (u < 2 ** 23), -jnp.inf, _key_to_float(u))
    need = kf - count(lambda x, idx: jnp.where(x > thr, 1.0, 0.0))

    def index_bit(it, v):
        cand = v | jnp.left_shift(jnp.int32(1), idx_bits - 1 - it)
        below = count(lambda x, idx: jnp.where(x == thr, jnp.where(idx < cand, 1.0, 0.0), 0.0))
        return jnp.where(below < need, cand, v)

    tied = jnp.max(count(lambda x, idx: jnp.where(x >= thr, 1.0, 0.0))) > kf
    v = lax.cond(tied,
                 lambda: lax.fori_loop(0, idx_bits, index_bit, jnp.zeros((1, cols), jnp.int32)),
                 lambda: jnp.full((1, cols), (1 << idx_bits) - 1, jnp.int32))
    return thr, jnp.where(need >= 1.0, v, -1)


LOG2E = 1.4426950408889634
SCORE_SCALE = SCALE * LOG2E


def _softmax_init(rows):
    return (jnp.full((rows, 1), NEG, F32), jnp.zeros((rows, 1), F32), jnp.zeros((rows, HEAD_DIM), F32))


def _softmax_step(carry, q, k, v, bias):
    m, l, acc = carry
    t = _dot_nt(q, k) * SCORE_SCALE + bias
    m_new = jnp.maximum(m, jnp.max(t, axis=1, keepdims=True))
    alpha = jnp.exp2(m - m_new)
    e = jnp.exp2(t - m_new)
    l = alpha * l + jnp.sum(e, axis=1, keepdims=True)
    acc = alpha * acc + _dot(e.astype(BF16), v)
    return m_new, l, acc


def _softmax_done(carry):
    _, l, acc = carry
    return acc / jnp.maximum(l, 1e-30)


def _mask_bias(maskf):
    return jnp.where(maskf > 0.0, 0.0, NEG)


def _stack_heads(ref, kvh):
    return jnp.concatenate([ref[:, (kvh * GROUP + g) * HEAD_DIM:(kvh * GROUP + g + 1) * HEAD_DIM]
                            for g in range(GROUP)], axis=0)


def _kv_cols(kvh):
    return (slice(kvh * 2 * HEAD_DIM, (kvh * 2 + 1) * HEAD_DIM),
            slice((kvh * 2 + 1) * HEAD_DIM, (kvh * 2 + 2) * HEAD_DIM))


def _dsa_prompt_kernel(iq_ref, iwt_ref, ik2_ref, qa_ref, akv_ref, sza_ref, o_ref, sc_ref, mk_ref,
                       *, tq, tk, k_top, idx_bits):
    i = pl.program_id(1)
    nck = ((i + 1) * tq + tk - 1) // tk
    lane = lax.broadcasted_iota(jnp.int32, (1, LANES), 1)
    qpos = i * tq + lax.broadcasted_iota(jnp.int32, (1, tq), 1)
    krow = lax.broadcasted_iota(jnp.int32, (tk, 1), 0)
    iwt = iwt_ref[...] * (IDX_DIM ** -0.5)

    iqs = []
    for hd in range(IDX_HEADS):
        pair = iq_ref[:, (hd // 2) * LANES:(hd // 2 + 1) * LANES]
        keep = (lane < IDX_DIM) if hd % 2 == 0 else (lane >= IDX_DIM)
        iqs.append(jnp.where(keep, pair, jnp.zeros_like(pair)))

    def score_chunk(c, carry):
        k0 = pl.multiple_of(c * tk, tk)
        ikc = ik2_ref[pl.ds(k0, tk), :]
        acc = jnp.zeros((tk, tq), F32)
        for hd in range(IDX_HEADS):
            acc = acc + jnp.maximum(_dot_nt(ikc, iqs[hd]), 0.0) * iwt[hd:hd + 1, :]
        sc_ref[c] = jnp.where(k0 + krow <= qpos, acc, NEG)
        return carry

    lax.fori_loop(0, nck, score_chunk, 0)

    thr, vmax = _topk_threshold_cols(lambda c: sc_ref[c], nck, k_top, idx_bits, tq, tk)

    def mask_chunk(c, carry):
        idx = c * tk + krow
        sel = _topk_mask(sc_ref[c], idx, thr, vmax) * jnp.where(idx <= qpos, 1.0, 0.0)
        mk_ref[c] = _mask_bias(sel).T
        return carry

    lax.fori_loop(0, nck, mask_chunk, 0)

    qs = [_stack_heads(qa_ref, kvh) for kvh in range(KV_HEADS)]

    def attend(c, carries):
        k0 = pl.multiple_of(c * tk, tk)
        bias = jnp.concatenate([mk_ref[c]] * GROUP, axis=0)
        out = []
        for kvh in range(KV_HEADS):
            kcol, vcol = _kv_cols(kvh)
            out.append(_softmax_step(carries[kvh], qs[kvh], akv_ref[pl.ds(k0, tk), kcol],
                                     akv_ref[pl.ds(k0, tk), vcol], bias))
        return tuple(out)

    carries = lax.fori_loop(0, nck, attend, tuple(_softmax_init(GROUP * tq) for _ in range(KV_HEADS)))
    for kvh in range(KV_HEADS):
        o = _softmax_done(carries[kvh])
        for g in range(GROUP):
            sl = slice((kvh * GROUP + g) * HEAD_DIM, (kvh * GROUP + g + 1) * HEAD_DIM)
            o_ref[:, sl] = (o[g * tq:(g + 1) * tq] * sza_ref[:, sl]).astype(BF16)


def _dsa_prompt(iq, iw_t, ik2, qa, akv_b, sza, b, t, tq, tk):
    nqb = t // tq
    k_top = min(A_TOPK, t // 4)
    idx_bits = max(1, (t - 1).bit_length())
    row = lambda bi, i: (bi * nqb + i, 0)
    whole = lambda bi, i: (bi, 0, 0)
    kern = functools.partial(_dsa_prompt_kernel, tq=tq, tk=tk, k_top=k_top, idx_bits=idx_bits)
    return pl.pallas_call(
        kern,
        grid=(b, nqb),
        in_specs=[pl.BlockSpec((tq, IDX_COLS), row),
                  pl.BlockSpec((IDX_HEADS, tq), lambda bi, i: (0, bi * nqb + i)),
                  pl.BlockSpec((None, t, LANES), whole), pl.BlockSpec((tq, WIDTH), row),
                  pl.BlockSpec((None, t, KV_ROW), whole), pl.BlockSpec((tq, WIDTH), row)],
        out_specs=pl.BlockSpec((tq, WIDTH), row),
        out_shape=jax.ShapeDtypeStruct((b * t, WIDTH), BF16),
        scratch_shapes=[pltpu.VMEM((t // tk, tk, tq), F32), pltpu.VMEM((t // tk, tq, tk), F32)],
        compiler_params=_params(2),
        name="dsa_prompt",
    )(iq, iw_t, ik2, qa, akv_b, sza)


SLC_SHIFT = SLC_BLOCK.bit_length() - 1
assert 1 << SLC_SHIFT == SLC_BLOCK


def _block_rules(imp, blk, qpos):
    cur = lax.shift_right_logical(qpos, SLC_SHIFT)
    return jnp.where(blk == cur, -NEG, jnp.where(blk * SLC_BLOCK > qpos, NEG, imp))


def _split_dot(p, m):
    hi = p.astype(BF16)
    lo = (p - hi.astype(F32)).astype(BF16)
    return _dot(hi, m) + _dot(lo, m)


def _nsa_prompt_kernel(qb_ref, qr_ref, kc_ref, vc_ref, slc_ref, win_ref, ovt_ref, misc_ref, szb_ref,
                       o_ref, sc_ref, *, tq, tk, n_cmp, n_blk, n_sel):
    i = pl.program_id(1)
    nck = ((i + 1) * tq + tk - 1) // tk
    qpos = i * tq + lax.broadcasted_iota(jnp.int32, (tq, 1), 0)
    kiota = lax.broadcasted_iota(jnp.int32, (1, tk), 1)
    misc = misc_ref[...]
    ncp = kc_ref.shape[0]
    cidx = lax.broadcasted_iota(jnp.int32, (1, ncp), 1)
    cmask = jnp.where((CMP_STRIDE * cidx + CMP_LEN - 1 <= qpos) & (cidx < n_cmp), 1.0, 0.0)
    cmask = jnp.concatenate([cmask] * GROUP, axis=0)
    blk = lax.broadcasted_iota(jnp.int32, (LANES, 1), 0)
    qlane = i * tq + lax.broadcasted_iota(jnp.int32, (1, tq), 1)

    o_cmp = []
    for kvh in range(KV_HEADS):
        kc = kc_ref[:, kvh * HEAD_DIM:(kvh + 1) * HEAD_DIM]
        vc = vc_ref[:, kvh * HEAD_DIM:(kvh + 1) * HEAD_DIM]
        s = _dot_nt(_stack_heads(qb_ref, kvh), kc) * SCALE
        sm = jnp.where(cmask > 0.0, s, NEG)
        e = jnp.exp(sm - jnp.max(sm, axis=1, keepdims=True)) * cmask
        p = e / jnp.maximum(jnp.sum(e, axis=1, keepdims=True), 1e-30)
        o_cmp.append(_dot(p.astype(BF16), vc))
        psum = p[0:tq]
        for g in range(1, GROUP):
            psum = psum + p[g * tq:(g + 1) * tq]
        hi = psum.astype(BF16)
        lo = (psum - hi.astype(F32)).astype(BF16)
        imp = _dot_nt(ovt_ref[...], hi) + _dot_nt(ovt_ref[...], lo)
        imp = _block_rules(imp, blk, qlane)
        sc_ref[:, kvh * tq:(kvh + 1) * tq] = jnp.where(blk < n_blk, imp, -jnp.inf)

    thr, vmax = _topk_threshold_cols(lambda c: sc_ref[...], 1, n_sel, 7, KV_HEADS * tq, LANES)
    bmask_t = _topk_mask(sc_ref[...], blk, thr, vmax)
    bmask = [bmask_t[:, kvh * tq:(kvh + 1) * tq].T.astype(BF16) for kvh in range(KV_HEADS)]

    riota = lax.broadcasted_iota(jnp.int32, (LANES, 1), 0)
    nwk = WINDOW + tq
    w0 = pl.multiple_of(i * tq, tq)
    kposw = i * tq - WINDOW + lax.broadcasted_iota(jnp.int32, (1, nwk), 1)
    dw = qpos - kposw
    bias_w = jnp.where((dw >= 0) & (dw < WINDOW) & (kposw >= 0), 0.0, NEG)
    bias_w = jnp.concatenate([bias_w] * GROUP, axis=0)

    qs = [_stack_heads(qr_ref, kvh) for kvh in range(KV_HEADS)]

    def attend(c, carries):
        k0 = pl.multiple_of(c * tk, tk)
        kpos = k0 + kiota
        expand = jnp.where(riota == lax.shift_right_logical(kpos, SLC_SHIFT), 1.0, 0.0).astype(BF16)
        causal = jnp.where(kpos <= qpos, 1.0, 0.0)
        out = []
        for kvh in range(KV_HEADS):
            kcol, vcol = _kv_cols(kvh)
            sel = _dot(bmask[kvh], expand) * causal
            bias = jnp.concatenate([_mask_bias(sel)] * GROUP, axis=0)
            out.append(_softmax_step(carries[kvh], qs[kvh], slc_ref[pl.ds(k0, tk), kcol],
                                     slc_ref[pl.ds(k0, tk), vcol], bias))
        return tuple(out)

    slc_carries = lax.fori_loop(0, nck, attend, tuple(_softmax_init(GROUP * tq) for _ in range(KV_HEADS)))

    for kvh in range(KV_HEADS):
        q = qs[kvh]
        kcol, vcol = _kv_cols(kvh)
        o_slc = _softmax_done(slc_carries[kvh])
        o_win = _softmax_done(_softmax_step(_softmax_init(GROUP * tq), q, win_ref[pl.ds(w0, nwk), kcol],
                                            win_ref[pl.ds(w0, nwk), vcol], bias_w))
        for g in range(GROUP):
            hd = kvh * GROUP + g
            sl = slice(hd * HEAD_DIM, (hd + 1) * HEAD_DIM)
            rs = slice(g * tq, (g + 1) * tq)
            gate = [misc[:, MISC_IW + j * N_HEADS + hd:MISC_IW + j * N_HEADS + hd + 1] for j in range(3)]
            o_b = gate[0] * o_cmp[kvh][rs] + gate[1] * o_slc[rs] + gate[2] * o_win[rs]
            o_ref[:, sl] = (o_b * szb_ref[:, sl]).astype(BF16)


def _overlap_matrix(n_cmp_pad, n_cols, col_block):
    i = jnp.arange(n_cmp_pad)[:, None]
    j = col_block[None, :]
    ov = (CMP_STRIDE * i < SLC_BLOCK * (j + 1)) & (CMP_STRIDE * i + CMP_LEN > SLC_BLOCK * j) & (j >= 0)
    return ov.astype(BF16)


def _nsa_prompt(qb, qr, kc, vc, slc_b, win_b, misc, szb, b, t, tq, tk):
    nqb = t // tq
    n_cmp = t // CMP_STRIDE - 1
    n_blk = t // SLC_BLOCK
    assert n_blk <= LANES
    ncp = kc.shape[1]
    ov = _overlap_matrix(ncp, LANES, jnp.where(jnp.arange(LANES) < n_blk, jnp.arange(LANES), -1))
    row = lambda bi, i: (bi * nqb + i, 0)
    whole = lambda bi, i: (bi, 0, 0)
    kern = functools.partial(_nsa_prompt_kernel, tq=tq, tk=tk, n_cmp=n_cmp, n_blk=n_blk,
                             n_sel=min(N_SEL, n_blk))
    return pl.pallas_call(
        kern,
        grid=(b, nqb),
        in_specs=[pl.BlockSpec((tq, WIDTH), row), pl.BlockSpec((tq, WIDTH), row),
                  pl.BlockSpec((None, ncp, KV_HEADS * HEAD_DIM), whole),
                  pl.BlockSpec((None, ncp, KV_HEADS * HEAD_DIM), whole),
                  pl.BlockSpec((None, t, KV_ROW), whole), pl.BlockSpec((None, t + WINDOW, KV_ROW), whole),
                  pl.BlockSpec((LANES, ncp), lambda bi, i: (0, 0)),
                  pl.BlockSpec((tq, LANES), row), pl.BlockSpec((tq, WIDTH), row)],
        out_specs=pl.BlockSpec((tq, WIDTH), row),
        out_shape=jax.ShapeDtypeStruct((b * t, WIDTH), BF16),
        scratch_shapes=[pltpu.VMEM((LANES, KV_HEADS * tq), F32)],
        compiler_params=_params(2),
        name="nsa_prompt",
    )(qb, qr, kc, vc, slc_b, win_b, ov.T, misc, szb)


CHUNK_FLAT = CMP_STRIDE * KV_ROW
FS_COLS = KV_HEADS * 2 * 2 * CMP_HIDDEN


def _chunk_terms_body(x, w_ref, o_ref):
    for kc in range(KV_HEADS * 2):
        xk = jnp.concatenate(
            [x[:, p * KV_ROW + kc * HEAD_DIM: p * KV_ROW + (kc + 1) * HEAD_DIM] for p in range(CMP_STRIDE)],
            axis=1).astype(BF16)
        o_ref[:, kc * 2 * CMP_HIDDEN:(kc + 1) * 2 * CMP_HIDDEN] = _dot(xk, w_ref[kc % 2])


def _chunk_terms_dense_kernel(x_ref, w_ref, o_ref):
    _chunk_terms_body(x_ref[...], w_ref, o_ref)


def _chunk_terms_dense(x, w, tr):
    n = x.shape[0]
    return pl.pallas_call(
        _chunk_terms_dense_kernel,
        grid=(n // tr,),
        in_specs=[pl.BlockSpec((tr, CHUNK_FLAT), lambda i: (i, 0)),
                  pl.BlockSpec((2, CMP_STRIDE * HEAD_DIM, 2 * CMP_HIDDEN), lambda i: (0, 0, 0))],
        out_specs=pl.BlockSpec((tr, FS_COLS), lambda i: (i, 0)),
        out_shape=jax.ShapeDtypeStruct((n, FS_COLS), F32),
        compiler_params=_params(1),
        name="chunk_terms_dense",
    )(x, w)


ROWS_PER_TOKEN = KV_HEADS * 2
PAGE_ROWS = PAGE_SIZE * ROWS_PER_TOKEN
CHUNKS_PER_PAGE = PAGE_SIZE // CMP_STRIDE


def _cache_rows(cache):
    return cache.reshape(-1, HEAD_DIM)


def _token_rows(ref, slot, n_tokens):
    return ref[pl.ds(slot, n_tokens, stride=ROWS_PER_TOKEN), :]


def _sublane_transpose(tiles):
    tiles = list(tiles)
    sub = lax.broadcasted_iota(jnp.int32, (SUBLANES, LANES), 0)
    for d in (4, 2, 1):
        keep = (sub & d) == 0
        for i in range(SUBLANES):
            if i & d:
                continue
            lo, hi = tiles[i], tiles[i + d]
            tiles[i] = jnp.where(keep, lo, pltpu.roll(hi, d, 0))
            tiles[i + d] = jnp.where(keep, pltpu.roll(lo, SUBLANES - d, 0), hi)
    return tiles


def _chunk_terms_paged_kernel(pt_ref, *refs, pg):
    pages, w_ref, o_ref = refs[:pg], refs[pg], refs[pg + 1]
    chunk_rows = CMP_STRIDE * ROWS_PER_TOKEN
    assert CHUNKS_PER_PAGE == SUBLANES and chunk_rows % SUBLANES == 0
    pieces = [[] for _ in range(chunk_rows)]
    for pr in pages:
        for t in range(chunk_rows // SUBLANES):
            tiles = [pr[n * chunk_rows + t * SUBLANES:n * chunk_rows + (t + 1) * SUBLANES, :]
                     for n in range(CHUNKS_PER_PAGE)]
            for s, tile in enumerate(_sublane_transpose(tiles)):
                pieces[t * SUBLANES + s].append(tile)
    for kc in range(ROWS_PER_TOKEN):
        xk = jnp.concatenate([jnp.concatenate(pieces[p * ROWS_PER_TOKEN + kc], axis=0)
                              for p in range(CMP_STRIDE)], axis=1).astype(BF16)
        o_ref[:, kc * 2 * CMP_HIDDEN:(kc + 1) * 2 * CMP_HIDDEN] = _dot(xk, w_ref[kc % 2])


def _page_specs(block, pg):
    tail = (0,) * (len(block) - 1)

    def spec(j):
        return pl.BlockSpec(block, lambda bi, g, pt: (pt[bi, g * pg + j],) + tail)
    return [spec(j) for j in range(pg)]


def _chunk_terms_paged(pool, page_table, w, pg):
    db, n_pages = page_table.shape
    cpp = CHUNKS_PER_PAGE
    pool_v = _cache_rows(pool)
    kern = functools.partial(_chunk_terms_paged_kernel, pg=pg)
    grid_spec = pltpu.PrefetchScalarGridSpec(
        num_scalar_prefetch=1,
        grid=(db, n_pages // pg),
        in_specs=_page_specs((PAGE_ROWS, HEAD_DIM), pg)
                 + [pl.BlockSpec((2, CMP_STRIDE * HEAD_DIM, 2 * CMP_HIDDEN), lambda bi, g, pt: (0, 0, 0))],
        out_specs=pl.BlockSpec((None, pg * cpp, FS_COLS), lambda bi, g, pt: (bi, g, 0)),
    )
    return pl.pallas_call(
        kern, grid_spec=grid_spec,
        out_shape=jax.ShapeDtypeStruct((db, n_pages * cpp, FS_COLS), F32),
        compiler_params=_params(2),
        name="chunk_terms_paged",
    )(page_table, *([pool_v] * pg), w)


def _compress_mlp_kernel(fs_ref, new_ref, pe_ref, w1_ref, w2_ref, g_ref, kc_ref, vc_ref):
    n = fs_ref.shape[0]
    last = lax.broadcasted_iota(jnp.int32, (n, 1), 0) == n - 1
    for kc in range(KV_HEADS * 2):
        kvh, c = kc // 2, kc % 2
        base = kc * 2 * CMP_HIDDEN
        first = fs_ref[:, base:base + CMP_HIDDEN]
        second = fs_ref[:, base + CMP_HIDDEN:base + 2 * CMP_HIDDEN]
        shifted = jnp.where(last, new_ref[0:1, base + CMP_HIDDEN:base + 2 * CMP_HIDDEN],
                            pltpu.roll(second, n - 1, 0))
        pe = jnp.broadcast_to(pe_ref[c:c + 1, :], (SUBLANES, pe_ref.shape[1])).astype(BF16)
        bias = _dot(pe, w1_ref[c])[0:1, :]
        pre = first + shifted + bias
        hid = pre * jax.nn.sigmoid(pre)
        out = _dot(hid.astype(BF16), w2_ref[c])
        sl = slice(kvh * HEAD_DIM, (kvh + 1) * HEAD_DIM)
        if c == 0:
            kc_ref[:, sl] = _head_norm(out, g_ref[...]).astype(BF16)
        else:
            vc_ref[:, sl] = out.astype(BF16)


def _compress_mlp(fs, fs_new, pe, w1r, w2, g_cmp):
    nb, n, _ = fs.shape
    whole = lambda bi: (bi, 0, 0)
    fix2 = lambda bi: (0, 0)
    fix3 = lambda bi: (0, 0, 0)
    out = jax.ShapeDtypeStruct((nb, n, KV_HEADS * HEAD_DIM), BF16)
    return pl.pallas_call(
        _compress_mlp_kernel,
        grid=(nb,),
        in_specs=[pl.BlockSpec((None, n, FS_COLS), whole), pl.BlockSpec((None, SUBLANES, FS_COLS), whole),
                  pl.BlockSpec((2, CMP_LEN * HEAD_DIM), fix2),
                  pl.BlockSpec((2, CMP_LEN * HEAD_DIM, CMP_HIDDEN), fix3),
                  pl.BlockSpec((2, CMP_HIDDEN, HEAD_DIM), fix3), pl.BlockSpec((1, HEAD_DIM), fix2)],
        out_specs=[pl.BlockSpec((None, n, KV_HEADS * HEAD_DIM), whole)] * 2,
        out_shape=[out, out],
        compiler_params=_params(1),
        name="compress_mlp",
    )(fs, fs_new, pe, w1r, w2, g_cmp.reshape(1, HEAD_DIM))


def _idx_scores_kernel(pt_ref, *refs, pg, n_new):
    pages = refs[:pg]
    iq_ref, iw_ref, new_ref, o_ref, onew_ref = refs[pg:pg + 5]
    g = pl.program_id(1)
    iq = iq_ref[...]
    iw = iw_ref[...]
    row = lax.broadcasted_iota(jnp.int32, (SAMPLE_Q, 1), 0)

    def scores(keys_t, first_idx):
        s = jnp.maximum(_dot(iq, keys_t), 0.0)
        n = s.shape[1]
        w = s * jnp.concatenate([iw] * (n // LANES), axis=1)
        acc = w[0:SAMPLE_Q]
        for hd in range(1, IDX_HEADS):
            acc = acc + w[hd * SAMPLE_Q:(hd + 1) * SAMPLE_Q]
        ramp = -(first_idx + lax.broadcasted_iota(jnp.int32, (1, n), 1)).astype(F32)
        return jnp.where(row < n_new, acc, ramp)

    cw = pg * PAGE_SIZE
    o_ref[...] = scores(jnp.concatenate([p[...] for p in pages], axis=1).astype(BF16), g * cw)

    @pl.when(g == pl.num_programs(1) - 1)
    def _():
        onew_ref[...] = scores(new_ref[...], pl.num_programs(1) * cw)


def _idx_scores(pool_idx_t, page_table, iqm, iwm, ik_new_t, pg, n_new):
    db, n_pages = page_table.shape
    ng = n_pages // pg
    kern = functools.partial(_idx_scores_kernel, pg=pg, n_new=n_new)
    per_b = lambda bi, g, pt: (bi, 0, 0)
    grid_spec = pltpu.PrefetchScalarGridSpec(
        num_scalar_prefetch=1,
        grid=(db, ng),
        in_specs=_page_specs((None, IDX_DIM, PAGE_SIZE), pg)
                 + [pl.BlockSpec((None, IDX_HEADS * SAMPLE_Q, IDX_DIM), per_b),
                    pl.BlockSpec((None, IDX_HEADS * SAMPLE_Q, LANES), per_b),
                    pl.BlockSpec((None, IDX_DIM, LANES), per_b)],
        out_specs=[pl.BlockSpec((None, SAMPLE_Q, pg * PAGE_SIZE), lambda bi, g, pt: (g, bi, 0)),
                   pl.BlockSpec((SAMPLE_Q, LANES), lambda bi, g, pt: (bi, 0))],
    )
    return pl.pallas_call(
        kern, grid_spec=grid_spec,
        out_shape=[jax.ShapeDtypeStruct((ng, db * SAMPLE_Q, pg * PAGE_SIZE), F32),
                   jax.ShapeDtypeStruct((db * SAMPLE_Q, LANES), F32)],
        compiler_params=_params(2),
        name="idx_scores",
    )(page_table, *([pool_idx_t] * pg), iqm, iwm, ik_new_t)


TOPK_ROWS = 64


def _dsa_topk_kernel(sc_ref, new_ref, o_ref, onew_ref, *, ng, cw, n_new, k_top, idx_bits, rb):
    row = lax.broadcasted_iota(jnp.int32, (rb, 1), 0)
    q = row & (SAMPLE_Q - 1)
    lane = lax.broadcasted_iota(jnp.int32, (1, LANES), 1)
    lane_cw = lax.broadcasted_iota(jnp.int32, (1, cw), 1)
    ok_new = (lane <= q) & (lane < n_new)
    x_new = jnp.where(ok_new, new_ref[...], NEG)
    idx_new = ng * cw + lane

    thr, vmax = _topk_threshold(lambda c, r0, nr: sc_ref[c, r0:r0 + nr, :], ng, k_top, idx_bits, rb, cw,
                                tail=(x_new, idx_new))
    for g in range(ng):
        o_ref[g] = _mask_bias(_topk_mask(sc_ref[g], g * cw + lane_cw, thr, vmax))
    onew_ref[...] = _mask_bias(_topk_mask(x_new, idx_new, thr, vmax) * jnp.where(ok_new, 1.0, 0.0))


def _dsa_topk(scores, scores_new, past, n_new):
    ng, rows, cw = scores.shape
    rb = min(TOPK_ROWS, rows)
    total = past + n_new
    kern = functools.partial(_dsa_topk_kernel, ng=ng, cw=cw, n_new=n_new, k_top=min(A_TOPK, total // 4),
                             idx_bits=(ng * cw + LANES - 1).bit_length(), rb=rb)
    big = pl.BlockSpec((ng, rb, cw), lambda r: (0, r, 0))
    small = pl.BlockSpec((rb, LANES), lambda r: (r, 0))
    return pl.pallas_call(
        kern,
        grid=(rows // rb,),
        in_specs=[big, small],
        out_specs=[big, small],
        out_shape=[jax.ShapeDtypeStruct((ng, rows, cw), F32), jax.ShapeDtypeStruct((rows, LANES), F32)],
        compiler_params=_params(1),
        name="dsa_topk",
    )(scores, scores_new)


QROWS = GROUP * SAMPLE_Q


def _paged_attn_kernel(pt_ref, *refs, pg, mask_rows):
    pages = refs[:pg]
    q_ref, mk_ref, mknew_ref, new_ref, o_ref, m_ref, l_ref, acc_ref = refs[pg:pg + 8]
    g = pl.program_id(1)

    @pl.when(g == 0)
    def _():
        m_ref[...] = jnp.full(m_ref.shape, NEG, F32)
        l_ref[...] = jnp.zeros(l_ref.shape, F32)
        acc_ref[...] = jnp.zeros(acc_ref.shape, F32)

    def update(keys, values, bias_all):
        carries = [(m_ref[kvh], l_ref[kvh], acc_ref[kvh]) for kvh in range(KV_HEADS)]
        for kvh in range(KV_HEADS):
            r0 = kvh * SAMPLE_Q if mask_rows == KV_HEADS * SAMPLE_Q else 0
            bias = jnp.concatenate([bias_all[r0:r0 + SAMPLE_Q]] * GROUP, axis=0)
            carries[kvh] = _softmax_step(carries[kvh], q_ref[kvh], keys(kvh), values(kvh), bias)
        for kvh in range(KV_HEADS):
            m_ref[kvh], l_ref[kvh], acc_ref[kvh] = carries[kvh]

    def paged(slot):
        return jnp.concatenate([_token_rows(p, slot, PAGE_SIZE) for p in pages], axis=0).astype(BF16)

    update(lambda kvh: paged(2 * kvh), lambda kvh: paged(2 * kvh + 1), mk_ref[...])

    @pl.when(g == pl.num_programs(1) - 1)
    def _():
        update(lambda kvh: new_ref[:, _kv_cols(kvh)[0]], lambda kvh: new_ref[:, _kv_cols(kvh)[1]], mknew_ref[...])
        for kvh in range(KV_HEADS):
            o_ref[kvh] = _softmax_done((m_ref[kvh], l_ref[kvh], acc_ref[kvh]))


def _paged_attn(pool, page_table, q, bias, bias_new, kv_new, pg):
    db, n_pages = page_table.shape
    ng = n_pages // pg
    mask_rows = bias.shape[1] // db
    pool_v = _cache_rows(pool)
    kern = functools.partial(_paged_attn_kernel, pg=pg, mask_rows=mask_rows)
    per_b3 = lambda bi, g, pt: (bi, 0, 0)
    per_b4 = lambda bi, g, pt: (bi, 0, 0, 0)
    grid_spec = pltpu.PrefetchScalarGridSpec(
        num_scalar_prefetch=1,
        grid=(db, ng),
        in_specs=_page_specs((PAGE_ROWS, HEAD_DIM), pg)
                 + [pl.BlockSpec((None, KV_HEADS, QROWS, HEAD_DIM), per_b4),
                    pl.BlockSpec((None, mask_rows, pg * PAGE_SIZE), lambda bi, g, pt: (g, bi, 0)),
                    pl.BlockSpec((mask_rows, LANES), lambda bi, g, pt: (bi, 0)),
                    pl.BlockSpec((None, LANES, KV_ROW), per_b3)],
        out_specs=pl.BlockSpec((None, KV_HEADS, QROWS, HEAD_DIM), per_b4),
        scratch_shapes=[pltpu.VMEM((KV_HEADS, QROWS, 1), F32), pltpu.VMEM((KV_HEADS, QROWS, 1), F32),
                        pltpu.VMEM((KV_HEADS, QROWS, HEAD_DIM), F32)],
    )
    return pl.pallas_call(
        kern, grid_spec=grid_spec,
        out_shape=jax.ShapeDtypeStruct((db, KV_HEADS, QROWS, HEAD_DIM), F32),
        compiler_params=_params(2),
        name="paged_attn",
    )(page_table, *([pool_v] * pg), q, bias, bias_new, kv_new)


BLOCKS_PER_PAGE = PAGE_SIZE // SLC_BLOCK
BLOCK_ROWS = SLC_BLOCK * ROWS_PER_TOKEN
assert BLOCKS_PER_PAGE == 2


def _slc_gather_kernel(pt_ref, sel_ref, *refs, nsel, ds, kvh):
    blocks = refs[:nsel]
    q_ref, bnew_ref, new_ref, o_ref = refs[nsel:nsel + 4]
    bi, qi = pl.program_id(0), pl.program_id(1)
    lane = lax.broadcasted_iota(jnp.int32, (1, LANES), 1)
    mine = (lax.broadcasted_iota(jnp.int32, (QROWS, 1), 0) & (SAMPLE_Q - 1)) == qi

    @pl.when(qi == 0)
    def _():
        o_ref[...] = jnp.zeros(o_ref.shape, F32)

    base = ((bi * ds + qi) * KV_HEADS + kvh) * nsel
    kcol, vcol = _kv_cols(kvh)
    keys = jnp.concatenate([_token_rows(blk, 2 * kvh, SLC_BLOCK) for blk in blocks], axis=0).astype(BF16)
    vals = jnp.concatenate([_token_rows(blk, 2 * kvh + 1, SLC_BLOCK) for blk in blocks], axis=0).astype(BF16)
    slot_bias = [jnp.where(sel_ref[base + j] >= 0, 0.0, NEG) for j in range(nsel)]
    pairs = [jnp.where(lane < SLC_BLOCK, slot_bias[j], slot_bias[min(j + 1, nsel - 1)])
             for j in range(0, nsel, 2)]
    bias = jnp.concatenate(pairs, axis=1)[:, :nsel * SLC_BLOCK]
    carry = _softmax_step(_softmax_init(QROWS), q_ref[...], keys, vals, bias)
    bnew = jnp.concatenate([bnew_ref[kvh * SAMPLE_Q:(kvh + 1) * SAMPLE_Q]] * GROUP, axis=0)
    carry = _softmax_step(carry, q_ref[...], new_ref[:, kcol], new_ref[:, vcol], bnew)
    o_ref[...] = jnp.where(mine, _softmax_done(carry), o_ref[...])


def _slc_gather(pool, page_table, sel, q, bias_new, kv_new, nsel, ds):
    db = page_table.shape[0]
    pool_v = _cache_rows(pool)
    outs = []
    for kvh in range(KV_HEADS):
        def spec(j, kvh=kvh):
            def imap(bi, qi, pt, sl):
                blk = jnp.maximum(sl[((bi * ds + qi) * KV_HEADS + kvh) * nsel + j], 0)
                return (pt[bi, lax.shift_right_logical(blk, 1)] * BLOCKS_PER_PAGE + (blk & 1), 0)
            return pl.BlockSpec((BLOCK_ROWS, HEAD_DIM), imap)

        grid_spec = pltpu.PrefetchScalarGridSpec(
            num_scalar_prefetch=2,
            grid=(db, ds),
            in_specs=[spec(j) for j in range(nsel)]
                     + [pl.BlockSpec((None, None, QROWS, HEAD_DIM), lambda bi, qi, pt, sl, kvh=kvh: (bi, kvh, 0, 0)),
                        pl.BlockSpec((KV_HEADS * SAMPLE_Q, LANES), lambda bi, qi, pt, sl: (bi, 0)),
                        pl.BlockSpec((None, LANES, KV_ROW), lambda bi, qi, pt, sl: (bi, 0, 0))],
            out_specs=pl.BlockSpec((None, QROWS, HEAD_DIM), lambda bi, qi, pt, sl: (bi, 0, 0)),
        )
        outs.append(pl.pallas_call(
            functools.partial(_slc_gather_kernel, nsel=nsel, ds=ds, kvh=kvh), grid_spec=grid_spec,
            out_shape=jax.ShapeDtypeStruct((db, QROWS, HEAD_DIM), F32),
            compiler_params=_params(2),
            name="slc_gather",
        )(page_table, sel, *([pool_v] * nsel), q, bias_new, kv_new))
    return jnp.stack(outs, axis=1)


def _nsa_sample_kernel(q_ref, kc_ref, vc_ref, ov_ref, o_ref, sel_ref, tmnew_ref, sc_ref,
                       *, past, n_new, n_cmp, n_blk, n_sel, ng, gb, nb):
    row = lax.broadcasted_iota(jnp.int32, (SAMPLE_Q, 1), 0)
    qpos = past + row
    ncp = kc_ref.shape[1]
    cidx = lax.broadcasted_iota(jnp.int32, (1, ncp), 1)
    cmask8 = jnp.where((CMP_STRIDE * cidx + CMP_LEN - 1 <= qpos) & (cidx < n_cmp), 1.0, 0.0)
    cmask = jnp.concatenate([cmask8] * GROUP, axis=0)
    lane = lax.broadcasted_iota(jnp.int32, (1, LANES), 1)

    for bb in range(nb):
        for kvh in range(KV_HEADS):
            kc = kc_ref[bb, :, kvh * HEAD_DIM:(kvh + 1) * HEAD_DIM]
            vc = vc_ref[bb, :, kvh * HEAD_DIM:(kvh + 1) * HEAD_DIM]
            s = _dot_nt(q_ref[bb, kvh], kc) * SCALE
            sm = jnp.where(cmask > 0.0, s, NEG)
            e = jnp.exp(sm - jnp.max(sm, axis=1, keepdims=True)) * cmask
            p = e / jnp.maximum(jnp.sum(e, axis=1, keepdims=True), 1e-30)
            o_ref[bb, kvh] = _dot(p.astype(BF16), vc)
            psum = p[0:SAMPLE_Q]
            for g in range(1, GROUP):
                psum = psum + p[g * SAMPLE_Q:(g + 1) * SAMPLE_Q]
            imp = _split_dot(psum, ov_ref[...])
            r0 = (bb * KV_HEADS + kvh) * SAMPLE_Q
            for g in range(ng + 1):
                blk = g * gb + lane
                v = _block_rules(imp[:, g * LANES:(g + 1) * LANES], blk, qpos)
                v = jnp.where(row < n_new, v, -blk.astype(F32))
                sc_ref[g, r0:r0 + SAMPLE_Q, :] = jnp.where((lane < gb) & (blk < n_blk), v, -jnp.inf)

    rows = nb * KV_HEADS * SAMPLE_Q
    idx_bits = ((ng + 1) * LANES - 1).bit_length()
    thr, vmax = _topk_threshold(lambda c, r0, nr: sc_ref[c, r0:r0 + nr, :], ng + 1, n_sel, idx_bits, rows, LANES)
    qpos2 = jnp.concatenate([qpos] * (nb * KV_HEADS), axis=0)
    n_past_blk = past // SLC_BLOCK
    tri = jnp.where(lax.broadcasted_iota(jnp.int32, (LANES, 1), 0) <= lane, 1.0, 0.0).astype(BF16)
    offs = jnp.zeros((rows, 1), F32)
    slots = [jnp.zeros((rows, 1), F32) for _ in range(n_sel)]
    for g in range(ng + 1):
        blk = g * gb + lane
        bm = _topk_mask(sc_ref[g], g * LANES + lane, thr, vmax)
        if g == ng:
            ok = jnp.where((past + lane <= qpos2) & (lane < n_new), 1.0, 0.0)
            tmnew_ref[...] = _mask_bias(bm[:, 0:1] * ok)
        bm = bm * jnp.where((lane < gb) & (blk < n_past_blk), 1.0, 0.0)
        rank = offs + _dot(bm.astype(BF16), tri)
        for j in range(n_sel):
            hit = jnp.where(bm > 0.0, jnp.where(rank == j + 1.0, blk.astype(F32), 0.0), 0.0)
            slots[j] = slots[j] + jnp.sum(hit, axis=1, keepdims=True)
        offs = offs + jnp.sum(bm, axis=1, keepdims=True)
    out = jnp.full((rows, LANES), -1.0, F32)
    for j in range(n_sel):
        out = jnp.where((lane == j) & (offs > j), slots[j], out)
    sel_ref[...] = out.astype(jnp.int32)


def _nsa_sample(q, kc, vc, past, n_new, ng, cw):
    db = q.shape[0]
    ncp = kc.shape[1]
    total = past + n_new
    n_blk = -(-total // SLC_BLOCK)
    n_cmp = total_chunks(past, n_new) - 1
    gb = cw // SLC_BLOCK
    assert gb <= LANES and past % cw == 0
    col = jnp.arange((ng + 1) * LANES)
    col_block = jnp.where(col % LANES < gb, (col // LANES) * gb + col % LANES, -1)
    col_block = jnp.where(col_block < n_blk, col_block, -1)
    ov = _overlap_matrix(ncp, (ng + 1) * LANES, col_block)
    nb = max(n for n in (4, 2, 1) if db % n == 0)
    kern = functools.partial(_nsa_sample_kernel, past=past, n_new=n_new, n_cmp=n_cmp, n_blk=n_blk,
                             n_sel=min(N_SEL, n_blk), ng=ng, gb=gb, nb=nb)
    rows = KV_HEADS * SAMPLE_Q
    b3 = lambda bi: (bi, 0, 0)
    b4 = lambda bi: (bi, 0, 0, 0)
    return pl.pallas_call(
        kern,
        grid=(db // nb,),
        in_specs=[pl.BlockSpec((nb, KV_HEADS, QROWS, HEAD_DIM), b4),
                  pl.BlockSpec((nb, ncp, KV_HEADS * HEAD_DIM), b3),
                  pl.BlockSpec((nb, ncp, KV_HEADS * HEAD_DIM), b3),
                  pl.BlockSpec((ncp, (ng + 1) * LANES), lambda bi: (0, 0))],
        out_specs=[pl.BlockSpec((nb, KV_HEADS, QROWS, HEAD_DIM), b4),
                   pl.BlockSpec((nb * rows, LANES), lambda bi: (bi, 0)),
                   pl.BlockSpec((nb * rows, LANES), lambda bi: (bi, 0))],
        out_shape=[jax.ShapeDtypeStruct((db, KV_HEADS, QROWS, HEAD_DIM), F32),
                   jax.ShapeDtypeStruct((db * rows, LANES), jnp.int32),
                   jax.ShapeDtypeStruct((db * rows, LANES), F32)],
        scratch_shapes=[pltpu.VMEM((ng + 1, nb * rows, LANES), F32)],
        compiler_params=_params(1),
        name="nsa_sample",
    )(q, kc, vc, ov)


def total_chunks(past, n_new):
    return past // CMP_STRIDE + -(-n_new // CMP_STRIDE)


def _win_sample_kernel(q_ref, st_ref, new_ref, o_ref, *, w_buf, n_new):
    row = lax.broadcasted_iota(jnp.int32, (SAMPLE_Q, 1), 0)
    row = jnp.concatenate([row] * GROUP, axis=0)
    j_old = lax.broadcasted_iota(jnp.int32, (1, w_buf), 1)
    j_new = w_buf + lax.broadcasted_iota(jnp.int32, (1, LANES), 1)
    def bias(j, valid):
        d = w_buf + row - j
        return jnp.where((d >= 0) & (d < WINDOW) & valid, 0.0, NEG)
    b_old, b_new = bias(j_old, True), bias(j_new, j_new < w_buf + n_new)
    for kvh in range(KV_HEADS):
        q = q_ref[kvh]
        kcol, vcol = _kv_cols(kvh)
        carry = _softmax_step(_softmax_init(QROWS), q, _token_rows(st_ref, 2 * kvh, w_buf).astype(BF16),
                              _token_rows(st_ref, 2 * kvh + 1, w_buf).astype(BF16), b_old)
        carry = _softmax_step(carry, q, new_ref[:, kcol], new_ref[:, vcol], b_new)
        o_ref[kvh] = _softmax_done(carry)


def _win_sample(q, state, kv_new, n_new):
    db, w_buf = state.shape[0], state.shape[1]
    kern = functools.partial(_win_sample_kernel, w_buf=w_buf, n_new=n_new)
    b3 = lambda bi: (bi, 0, 0)
    b4 = lambda bi: (bi, 0, 0, 0)
    return pl.pallas_call(
        kern,
        grid=(db,),
        in_specs=[pl.BlockSpec((None, KV_HEADS, QROWS, HEAD_DIM), b4),
                  pl.BlockSpec((w_buf * ROWS_PER_TOKEN, HEAD_DIM), lambda bi: (bi, 0)),
                  pl.BlockSpec((None, LANES, KV_ROW), b3)],
        out_specs=pl.BlockSpec((None, KV_HEADS, QROWS, HEAD_DIM), b4),
        out_shape=jax.ShapeDtypeStruct((db, KV_HEADS, QROWS, HEAD_DIM), F32),
        compiler_params=_params(1),
        name="win_sample",
    )(q, _cache_rows(state), kv_new)


def _gate_merge_kernel(oa_ref, oc_ref, os_ref, ow_ref, misc_ref, sza_ref, szb_ref, a_ref, b_ref):
    misc = misc_ref[...]
    a_ref[...] = (oa_ref[...] * sza_ref[...]).astype(BF16)
    for hd in range(N_HEADS):
        sl = slice(hd * HEAD_DIM, (hd + 1) * HEAD_DIM)
        g = [misc[:, MISC_IW + j * N_HEADS + hd:MISC_IW + j * N_HEADS + hd + 1] for j in range(3)]
        o_b = g[0] * oc_ref[:, sl] + g[1] * os_ref[:, sl] + g[2] * ow_ref[:, sl]
        b_ref[:, sl] = (o_b * szb_ref[:, sl]).astype(BF16)


def _gate_merge(o_a, o_cmp, o_slc, o_win, misc, sza, szb):
    n = o_a.shape[0]
    full = lambda w: pl.BlockSpec((n, w), lambda i: (0, 0))
    out = jax.ShapeDtypeStruct((n, WIDTH), BF16)
    return pl.pallas_call(
        _gate_merge_kernel,
        grid=(1,),
        in_specs=[full(WIDTH)] * 4 + [full(LANES), full(WIDTH), full(WIDTH)],
        out_specs=[full(WIDTH)] * 2,
        out_shape=[out, out],
        compiler_params=_params(1),
        name="gate_merge",
    )(o_a, o_cmp, o_slc, o_win, misc, sza, szb)


def _merge_kernel(a_ref, b_ref, h_ref, wa_ref, wb_ref, wg0_ref, wg1_ref, o_ref):
    h = h_ref[...]
    y_a = _dot(a_ref[...], wa_ref[...])
    y_b = _dot(b_ref[...], wb_ref[...])
    g0 = jax.nn.sigmoid(_dot(h, wg0_ref[...]))
    g1 = jax.nn.sigmoid(_dot(h, wg1_ref[...]))
    o_ref[...] = (g0 * y_a + g1 * y_b).astype(BF16)


def _merge(a, bm, h, w_a, w_b, w_g0, w_g1, tr, cb):
    n = a.shape[0]
    row = lambda j, i: (i, 0)
    col = lambda j, i: (0, j)
    return pl.pallas_call(
        _merge_kernel,
        grid=(D_MODEL // cb, n // tr),
        in_specs=[pl.BlockSpec((tr, WIDTH), row), pl.BlockSpec((tr, WIDTH), row),
                  pl.BlockSpec((tr, D_MODEL), row),
                  pl.BlockSpec((WIDTH, cb), col), pl.BlockSpec((WIDTH, cb), col),
                  pl.BlockSpec((D_MODEL, cb), col), pl.BlockSpec((D_MODEL, cb), col)],
        out_specs=pl.BlockSpec((tr, cb), lambda j, i: (i, j)),
        out_shape=jax.ShapeDtypeStruct((n, D_MODEL), BF16),
        compiler_params=_params(2),
        name="merge",
    )(a, bm, h, w_a, w_b, w_g0, w_g1)


def _out_proj_kernel(x_ref, m_ref, w_ref, o_ref):
    o_ref[...] = x_ref[...] + _dot(m_ref[...], w_ref[...])


def _out_proj(x, merged, w_out, tr):
    n = x.shape[0]
    row = lambda i: (i, 0)
    return pl.pallas_call(
        _out_proj_kernel,
        grid=(n // tr,),
        in_specs=[pl.BlockSpec((tr, D_MODEL), row), pl.BlockSpec((tr, D_MODEL), row),
                  pl.BlockSpec((D_MODEL, D_MODEL), lambda i: (0, 0))],
        out_specs=pl.BlockSpec((tr, D_MODEL), row),
        out_shape=jax.ShapeDtypeStruct((n, D_MODEL), F32),
        compiler_params=_params(1),
        name="out_proj",
    )(x, merged, w_out)


def _project(x2d, pos_period, tr, wts):
    n = x2d.shape[0]
    n_tab = pos_period.shape[0] // tr
    tabs_head = _rope_tables(pos_period, ROT_DIM, HEAD_DIM)
    tabs_iq = _rope_tables(pos_period, IDX_ROT_DIM, IDX_DIM)
    tabs_iq = tuple(jnp.concatenate([t, t], axis=1) for t in tabs_iq)
    ident = (jnp.ones_like(tabs_iq[0][:, :IDX_DIM]), jnp.zeros_like(tabs_iq[0][:, :IDX_DIM]),
             jnp.zeros_like(tabs_iq[0][:, :IDX_DIM]))
    tabs_misc = tuple(jnp.concatenate([t[:, :IDX_DIM], e], axis=1) for t, e in zip(tabs_iq, ident))

    h = _rmsnorm(x2d, wts["norm_gain"], tr)
    (qa,) = _proj_q(h, wts["w_qa"], wts["q_norm_a"], tabs_head, tr, n_tab, False)
    qb_rot, qb = _proj_q(h, wts["w_qb"], wts["q_norm_b"], tabs_head, tr, n_tab, True)
    akv, cmp_kv, slc, win, akv_b, slc_b, win_b = _proj_kv(
        h, wts["w_kv"], wts["k_norm_a"], wts["k_norm_slc"], wts["k_norm_win"], tabs_head, tr, n_tab)
    iq, misc = _proj_idx(h, wts["w_idx"], tabs_iq, tabs_misc, tr, n_tab)
    sza, szb = _proj_z(h, wts["w_z"], tr)
    return dict(h=h, qa=qa, qb=qb, qb_rot=qb_rot, akv=akv, cmp=cmp_kv, slc=slc, win=win,
                akv_b=akv_b, slc_b=slc_b, win_b=win_b, iq=iq, misc=misc, sza=sza, szb=szb)


def _sample_rows(a, db, ds):
    a = a.reshape(db, ds, KV_HEADS, GROUP, HEAD_DIM).transpose(0, 2, 3, 1, 4)
    a = jnp.pad(a, ((0, 0), (0, 0), (0, 0), (0, SAMPLE_Q - ds), (0, 0)))
    return a.reshape(db, KV_HEADS, QROWS, HEAD_DIM)


def _unsample_rows(o, db, ds):
    o = o.reshape(db, KV_HEADS, GROUP, SAMPLE_Q, HEAD_DIM)[:, :, :, :ds]
    return o.transpose(0, 3, 1, 2, 4).reshape(db * ds, WIDTH)


def _pad_new(a, db, ds):
    return jnp.pad(a.reshape(db, ds, a.shape[-1]), ((0, 0), (0, LANES - ds), (0, 0)))


def kernel(x_prompt, x_sample, cache_a_kv, cache_a_idx, cache_cmp_kv, cache_slc_kv, state_win_kv,
           page_table, norm_gain, w_in, q_norm_a, k_norm_a, q_norm_b, k_norm_cmp, k_norm_slc,
           k_norm_win, pe_cmp, w1_cmp, w2_cmp, w_proj_a, w_proj_b, w_out):
    b, t, _ = x_prompt.shape
    db, ds, _ = x_sample.shape
    n_pages = page_table.shape[1]
    past = n_pages * PAGE_SIZE
    assert ds <= SAMPLE_Q and ds <= CMP_STRIDE

    sizes = (WIDTH, KV_ROW, IDX_COLS, IDX_DIM, IDX_HEADS, WIDTH, WIDTH, 3 * KV_ROW, 3 * N_HEADS, WIDTH,
             2 * D_MODEL)
    offs = [0]
    for s in sizes:
        offs.append(offs[-1] + s)
    wb = w_in.astype(BF16)
    seg = lambda j: wb[:, offs[j]:offs[j + 1]]
    misc_pad = jnp.zeros((D_MODEL, LANES - (IDX_DIM + IDX_HEADS + 3 * N_HEADS)), BF16)
    wts = dict(
        norm_gain=norm_gain, q_norm_a=q_norm_a, q_norm_b=q_norm_b, k_norm_a=k_norm_a,
        k_norm_slc=k_norm_slc, k_norm_win=k_norm_win,
        w_qa=seg(0), w_qb=seg(6),
        w_kv=jnp.concatenate([seg(1), seg(7)], axis=1),
        w_idx=jnp.concatenate([seg(2), seg(3), seg(4), seg(8), misc_pad], axis=1),
        w_z=jnp.concatenate([seg(5), seg(9)], axis=1),
    )
    w_g0 = wb[:, offs[10]:offs[10] + D_MODEL]
    w_g1 = wb[:, offs[10] + D_MODEL:offs[11]]
    w_pa, w_pb, w_o = w_proj_a.astype(BF16), w_proj_b.astype(BF16), w_out.astype(BF16)
    w1_fs = jnp.concatenate([w1_cmp[:, :CMP_STRIDE].reshape(2, CMP_STRIDE * HEAD_DIM, CMP_HIDDEN),
                             w1_cmp[:, CMP_STRIDE:].reshape(2, CMP_STRIDE * HEAD_DIM, CMP_HIDDEN)],
                            axis=2).astype(BF16)
    w1_r = w1_cmp.reshape(2, CMP_LEN * HEAD_DIM, CMP_HIDDEN).astype(BF16)
    pe_r = pe_cmp.reshape(2, CMP_LEN * HEAD_DIM)
    w2_b = w2_cmp.astype(BF16)

    tr = min(512, t)
    tq = min(256, t)
    tk = min(512, t)
    xp = x_prompt.reshape(b * t, D_MODEL)
    pp = _project(xp, jnp.arange(t, dtype=jnp.int32), tr, wts)

    pages_p = t // PAGE_SIZE
    table_p = jnp.arange(b * pages_p, dtype=jnp.int32).reshape(b, pages_p)
    fs_p = _chunk_terms_paged(pp["cmp"], table_p, w1_fs, min(PAGES_PER_STEP, pages_p))
    kc_p, vc_p = _compress_mlp(fs_p, jnp.zeros((b, SUBLANES, FS_COLS), F32), pe_r, w1_r, w2_b, k_norm_cmp)

    ik_b = pp["misc"][:, :IDX_DIM].astype(BF16)
    ik2 = jnp.concatenate([ik_b, ik_b], axis=1).reshape(b, t, LANES)
    iw_t = pp["misc"][:, MISC_IK:MISC_IW].T
    a_p = _dsa_prompt(pp["iq"], iw_t, ik2, pp["qa"], pp["akv_b"].reshape(b, t, KV_ROW), pp["sza"],
                      b, t, tq, tk)
    win_front = jnp.pad(pp["win_b"].reshape(b, t, KV_ROW), ((0, 0), (WINDOW, 0), (0, 0)))
    b_p = _nsa_prompt(pp["qb"], pp["qb_rot"], kc_p, vc_p, pp["slc_b"].reshape(b, t, KV_ROW),
                      win_front, pp["misc"], pp["szb"], b, t, tq, tk)
    merged_p = _merge(a_p, b_p, pp["h"], w_pa, w_pb, w_g0, w_g1, tr, 1024)
    y_prompt = _out_proj(xp, merged_p, w_o, tr).reshape(b, t, D_MODEL)

    kv5 = lambda a, n0, n1: a.reshape(n0, n1, KV_HEADS, 2, HEAD_DIM)
    p_a_kv = kv5(pp["akv"], b, t)
    p_a_idx = pp["misc"][:, :IDX_DIM].reshape(b, t, IDX_DIM)
    p_cmp_kv = kv5(pp["cmp"], b, t)
    p_slc_kv = kv5(pp["slc"], b, t)
    p_win_kv = kv5(pp["win"], b, t)[:, t - min(WINDOW, t):]

    ns = db * ds
    xs = x_sample.reshape(ns, D_MODEL)
    pos_s = past + jnp.tile(jnp.arange(ds, dtype=jnp.int32), db)
    ps = _project(xs, pos_s, ns, wts)
    pg = min(PAGES_PER_STEP, n_pages)
    ng = n_pages // pg
    cw = pg * PAGE_SIZE

    iq_s = ps["iq"].reshape(db, ds, IDX_HEADS, IDX_DIM).transpose(0, 2, 1, 3)
    iqm = jnp.pad(iq_s, ((0, 0), (0, 0), (0, SAMPLE_Q - ds), (0, 0))).reshape(db, IDX_HEADS * SAMPLE_Q, IDX_DIM)
    iw_s = ps["misc"][:, MISC_IK:MISC_IW].reshape(db, ds, IDX_HEADS).transpose(0, 2, 1) * (IDX_DIM ** -0.5)
    iwm = jnp.pad(iw_s, ((0, 0), (0, 0), (0, SAMPLE_Q - ds))).reshape(db, IDX_HEADS * SAMPLE_Q, 1)
    iwm = jnp.broadcast_to(iwm, (db, IDX_HEADS * SAMPLE_Q, LANES))
    ik_new_t = jnp.swapaxes(_pad_new(ps["misc"][:, :IDX_DIM].astype(BF16), db, ds), 1, 2)
    sc_s, sc_new = _idx_scores(jnp.swapaxes(cache_a_idx, 1, 2), page_table, iqm, iwm, ik_new_t, pg, ds)
    mk_a, mk_a_new = _dsa_topk(sc_s, sc_new, past, ds)
    o_a_s = _paged_attn(cache_a_kv, page_table, _sample_rows(ps["qa"], db, ds), mk_a, mk_a_new,
                        _pad_new(ps["akv_b"], db, ds), pg)

    fs_past = _chunk_terms_paged(cache_cmp_kv, page_table, w1_fs, pg)
    new_chunk = jnp.pad(ps["cmp"].reshape(db, ds, KV_ROW), ((0, 0), (0, CMP_STRIDE - ds), (0, 0)))
    fs_new = _chunk_terms_dense(new_chunk.reshape(db, CHUNK_FLAT), w1_fs, db)
    fs_new = jnp.pad(fs_new.reshape(db, 1, FS_COLS), ((0, 0), (0, SUBLANES - 1), (0, 0)))
    kc_s, vc_s = _compress_mlp(fs_past, fs_new, pe_r, w1_r, w2_b, k_norm_cmp)
    o_cmp_s, sel_s, mk_s_new = _nsa_sample(_sample_rows(ps["qb"], db, ds), kc_s, vc_s, past, ds, ng, cw)
    q_rot_s = _sample_rows(ps["qb_rot"], db, ds)
    n_sel_s = min(N_SEL, -(-(past + ds) // SLC_BLOCK))
    sel_s = sel_s[:, :n_sel_s].reshape(db, KV_HEADS, SAMPLE_Q, n_sel_s)[:, :, :ds]
    sel_s = sel_s.transpose(0, 2, 1, 3).reshape(-1)
    o_slc_s = _slc_gather(cache_slc_kv, page_table, sel_s, q_rot_s, mk_s_new, _pad_new(ps["slc_b"], db, ds),
                          n_sel_s, ds)
    o_win_s = _win_sample(q_rot_s, state_win_kv, _pad_new(ps["win_b"], db, ds), ds)

    a_s, b_s = _gate_merge(_unsample_rows(o_a_s, db, ds), _unsample_rows(o_cmp_s, db, ds),
                           _unsample_rows(o_slc_s, db, ds), _unsample_rows(o_win_s, db, ds),
                           ps["misc"], ps["sza"], ps["szb"])
    merged_s = _merge(a_s, b_s, ps["h"], w_pa, w_pb, w_g0, w_g1, ns, 1024)
    y_sample = _out_proj(xs, merged_s, w_o, ns).reshape(db, ds, D_MODEL)

    s_win = kv5(ps["win"], db, ds)
    w_buf = state_win_kv.shape[1]
    s_win_kv = jnp.concatenate([state_win_kv, s_win], axis=1)[:, ds:ds + w_buf]

    return (y_prompt, y_sample, p_a_kv, p_a_idx, p_cmp_kv, p_slc_kv, p_win_kv,
            kv5(ps["akv"], db, ds), ps["misc"][:, :IDX_DIM].reshape(db, ds, IDX_DIM),
            kv5(ps["cmp"], db, ds), kv5(ps["slc"], db, ds), s_win_kv)
```

```python
import functools

import jax
import jax.numpy as jnp
from jax import lax
from jax.experimental import pallas as pl
from jax.experimental.pallas import tpu as pltpu

D_MODEL = 2048
HEAD_DIM = 128
ROT_DIM = HEAD_DIM // 4
N_HEADS = D_MODEL // (2 * HEAD_DIM)
KV_HEADS = 2
GROUP = N_HEADS // KV_HEADS
IDX_HEADS = 8
IDX_DIM = 64
IDX_ROT_DIM = IDX_DIM // 4
A_TOPK = 256
CMP_LEN = 32
CMP_STRIDE = 16
CMP_HIDDEN = 128
SLC_BLOCK = 64
N_SEL = 16
WINDOW = 512
PAGE_SIZE = 128
ROPE_THETA = 500000.0
EPS = 1e-6
NEG = -1e30
WIDTH = N_HEADS * HEAD_DIM
KV_ROW = KV_HEADS * 2 * HEAD_DIM
SCALE = HEAD_DIM ** -0.5

LANES = 128
SUBLANES = 8
VMEM_LIMIT = 48 * 1024 * 1024
PAGES_PER_STEP = 64
CHUNK_PAGES_PER_STEP = 32
SAMPLE_Q = 8

F32 = jnp.float32
BF16 = jnp.bfloat16
NT_DIMS = (((1,), (1,)), ((), ()))


def _params(n_axes):
    return pltpu.CompilerParams(dimension_semantics=("arbitrary",) * n_axes,
                                vmem_limit_bytes=VMEM_LIMIT)


def _dot(a, b):
    return jnp.dot(a, b, preferred_element_type=F32)


def _dot_nt(a, b):
    return lax.dot_general(a, b, NT_DIMS, preferred_element_type=F32)


def _rope_tables(pos, rot, period, width=LANES):
    half = rot // 2
    inv = ROPE_THETA ** (-jnp.arange(half, dtype=F32) / half)
    ang = pos.astype(F32)[:, None] * inv
    cos, sin = jnp.cos(ang), jnp.sin(ang)
    n = pos.shape[0]
    zeros = lambda w: jnp.zeros((n, w), F32)
    ones = lambda w: jnp.ones((n, w), F32)
    c = jnp.concatenate([cos, cos, ones(period - rot)], axis=1)
    s1 = jnp.concatenate([-sin, zeros(period - half)], axis=1)
    s2 = jnp.concatenate([zeros(half), sin, zeros(period - rot)], axis=1)
    return c, s1, s2


def _apply_rope(y, c, s1, s2, half):
    return y * c + pltpu.roll(y, LANES - half, 1) * s1 + pltpu.roll(y, half, 1) * s2


def _head_norm(z, g):
    return z * lax.rsqrt(jnp.mean(z * z, axis=-1, keepdims=True) + EPS) * g


def _rmsnorm_kernel(x_ref, g_ref, o_ref):
    x = x_ref[...]
    y = x * lax.rsqrt(jnp.mean(x * x, axis=-1, keepdims=True) + EPS)
    o_ref[...] = (y * g_ref[...]).astype(o_ref.dtype)


def _rmsnorm(x, gain, tr):
    n, d = x.shape
    return pl.pallas_call(
        _rmsnorm_kernel,
        grid=(n // tr,),
        in_specs=[pl.BlockSpec((tr, d), lambda i: (i, 0)), pl.BlockSpec((1, d), lambda i: (0, 0))],
        out_specs=pl.BlockSpec((tr, d), lambda i: (i, 0)),
        out_shape=jax.ShapeDtypeStruct((n, d), BF16),
        compiler_params=_params(1),
        name="rmsnorm",
    )(x, gain.reshape(1, d))


PROJ_ROWS = 128


def _proj_q_kernel(h_ref, w_ref, g_ref, c_ref, s1_ref, s2_ref, rot_ref, *plain_ref):
    g = g_ref[...]
    n = h_ref.shape[0]
    nh = max(1, n // PROJ_ROWS)
    for r in range(nh):
        rs = slice(r * (n // nh), (r + 1) * (n // nh))
        z = _dot(h_ref[rs, :], w_ref[...])
        c, s1, s2 = c_ref[rs, :], s1_ref[rs, :], s2_ref[rs, :]
        for hd in range(N_HEADS):
            sl = slice(hd * HEAD_DIM, (hd + 1) * HEAD_DIM)
            y = _head_norm(z[:, sl], g)
            if plain_ref:
                plain_ref[0][rs, sl] = y.astype(BF16)
            rot_ref[rs, sl] = _apply_rope(y, c, s1, s2, ROT_DIM // 2).astype(BF16)


def _proj_q(h, w, gain, tabs, tr, n_tab_blocks, want_plain):
    n = h.shape[0]
    row = lambda i: (i, 0)
    fix = lambda i: (0, 0)
    tab = lambda i: (i % n_tab_blocks, 0)
    out_shape = [jax.ShapeDtypeStruct((n, WIDTH), BF16)] * (2 if want_plain else 1)
    out_specs = [pl.BlockSpec((tr, WIDTH), row)] * (2 if want_plain else 1)
    return pl.pallas_call(
        _proj_q_kernel,
        grid=(n // tr,),
        in_specs=[pl.BlockSpec((tr, D_MODEL), row), pl.BlockSpec((D_MODEL, WIDTH), fix),
                  pl.BlockSpec((1, HEAD_DIM), fix)] + [pl.BlockSpec((tr, LANES), tab)] * 3,
        out_specs=out_specs,
        out_shape=out_shape,
        compiler_params=_params(1),
        name="proj_q",
    )(h, w, gain.reshape(1, HEAD_DIM), *tabs)


def _proj_kv_kernel(h_ref, w_ref, ga_ref, gs_ref, gw_ref, c_ref, s1_ref, s2_ref,
                    akv_ref, cmp_ref, slc_ref, win_ref, akv_b_ref, slc_b_ref, win_b_ref):
    n = h_ref.shape[0]
    nh = max(1, n // PROJ_ROWS)
    nr = n // nh
    for r in range(nh):
        rs = slice(r * nr, (r + 1) * nr)
        z = _dot(h_ref[rs, :], w_ref[...])
        c, s1, s2 = c_ref[rs, :], s1_ref[rs, :], s2_ref[rs, :]

        def emit(base, gain_ref, f32_ref, bf_ref):
            for j in range(KV_HEADS * 2):
                sl = slice(j * HEAD_DIM, (j + 1) * HEAD_DIM)
                v = z[:, base + j * HEAD_DIM: base + (j + 1) * HEAD_DIM]
                if gain_ref is not None and j % 2 == 0:
                    v = _apply_rope(_head_norm(v, gain_ref[...]), c, s1, s2, ROT_DIM // 2)
                f32_ref[pl.ds(r * nr * KV_HEADS * 2 + j, nr, stride=KV_HEADS * 2), :] = v
                if bf_ref is not None:
                    bf_ref[rs, sl] = v.astype(BF16)

        emit(0 * KV_ROW, ga_ref, akv_ref, akv_b_ref)
        emit(1 * KV_ROW, None, cmp_ref, None)
        emit(2 * KV_ROW, gs_ref, slc_ref, slc_b_ref)
        emit(3 * KV_ROW, gw_ref, win_ref, win_b_ref)


def _proj_kv(h, w, g_a, g_slc, g_win, tabs, tr, n_tab_blocks):
    n = h.shape[0]
    row = lambda i: (i, 0)
    fix = lambda i: (0, 0)
    tab = lambda i: (i % n_tab_blocks, 0)
    f32_out = jax.ShapeDtypeStruct((n * KV_HEADS * 2, HEAD_DIM), F32)
    bf_out = jax.ShapeDtypeStruct((n, KV_ROW), BF16)
    return pl.pallas_call(
        _proj_kv_kernel,
        grid=(n // tr,),
        in_specs=[pl.BlockSpec((tr, D_MODEL), row), pl.BlockSpec((D_MODEL, 4 * KV_ROW), fix)]
                 + [pl.BlockSpec((1, HEAD_DIM), fix)] * 3 + [pl.BlockSpec((tr, LANES), tab)] * 3,
        out_specs=[pl.BlockSpec((tr * KV_HEADS * 2, HEAD_DIM), row)] * 4 + [pl.BlockSpec((tr, KV_ROW), row)] * 3,
        out_shape=[f32_out] * 4 + [bf_out] * 3,
        compiler_params=_params(1),
        name="proj_kv",
    )(h, w, g_a.reshape(1, HEAD_DIM), g_slc.reshape(1, HEAD_DIM), g_win.reshape(1, HEAD_DIM), *tabs)


IDX_COLS = IDX_HEADS * IDX_DIM
MISC_IK = IDX_DIM
MISC_IW = MISC_IK + IDX_HEADS
MISC_G = MISC_IW + 3 * N_HEADS


def _proj_idx_kernel(h_ref, w_ref, c_ref, s1_ref, s2_ref, cm_ref, s1m_ref, s2m_ref, iq_ref, misc_ref):
    z = _dot(h_ref[...], w_ref[...])
    half = IDX_ROT_DIM // 2
    c, s1, s2 = c_ref[...], s1_ref[...], s2_ref[...]
    for j in range(IDX_COLS // LANES):
        sl = slice(j * LANES, (j + 1) * LANES)
        iq_ref[:, sl] = _apply_rope(z[:, sl], c, s1, s2, half).astype(BF16)
    m = z[:, IDX_COLS:IDX_COLS + LANES]
    roped = _apply_rope(m, cm_ref[...], s1m_ref[...], s2m_ref[...], half)
    lane = lax.broadcasted_iota(jnp.int32, m.shape, 1)
    misc_ref[...] = jnp.where(lane < MISC_IK, roped,
                              jnp.where(lane < MISC_IW, m * (IDX_HEADS ** -0.5),
                                        jnp.where(lane < MISC_G, jax.nn.sigmoid(m), 0.0)))


def _proj_idx(h, w, tabs_iq, tabs_misc, tr, n_tab_blocks):
    n = h.shape[0]
    row = lambda i: (i, 0)
    fix = lambda i: (0, 0)
    tab = lambda i: (i % n_tab_blocks, 0)
    return pl.pallas_call(
        _proj_idx_kernel,
        grid=(n // tr,),
        in_specs=[pl.BlockSpec((tr, D_MODEL), row), pl.BlockSpec((D_MODEL, IDX_COLS + LANES), fix)]
                 + [pl.BlockSpec((tr, LANES), tab)] * 6,
        out_specs=[pl.BlockSpec((tr, IDX_COLS), row), pl.BlockSpec((tr, LANES), row)],
        out_shape=[jax.ShapeDtypeStruct((n, IDX_COLS), BF16), jax.ShapeDtypeStruct((n, LANES), F32)],
        compiler_params=_params(1),
        name="proj_idx",
    )(h, w, *tabs_iq, *tabs_misc)


def _proj_z_kernel(h_ref, w_ref, za_ref, zb_ref):
    n = h_ref.shape[0]
    nh = max(1, n // PROJ_ROWS)
    for r in range(nh):
        rs = slice(r * (n // nh), (r + 1) * (n // nh))
        z = _dot(h_ref[rs, :], w_ref[...])
        s = z * jax.nn.sigmoid(z)
        za_ref[rs, :] = s[:, :WIDTH]
        zb_ref[rs, :] = s[:, WIDTH:]


def _proj_z(h, w, tr):
    n = h.shape[0]
    row = lambda i: (i, 0)
    out = jax.ShapeDtypeStruct((n, WIDTH), F32)
    return pl.pallas_call(
        _proj_z_kernel,
        grid=(n // tr,),
        in_specs=[pl.BlockSpec((tr, D_MODEL), row), pl.BlockSpec((D_MODEL, 2 * WIDTH), lambda i: (0, 0))],
        out_specs=[pl.BlockSpec((tr, WIDTH), row)] * 2,
        out_shape=[out, out],
        compiler_params=_params(1),
        name="proj_z",
    )(h, w)


def _key_to_float(u):
    int_min = jnp.int32(-2 ** 31)
    bits = jnp.where(u < 0, u ^ int_min, ~u)
    return lax.bitcast_convert_type(bits, F32)


def _lane_fold(w):
    acc = w[:, :LANES]
    for j in range(1, w.shape[1] // LANES):
        acc = acc + w[:, j * LANES:(j + 1) * LANES]
    return acc


COUNT_ROWS = 1024


def _topk_threshold(load, nck, k, idx_bits, rows, cw, tail=None):
    kf = jnp.float32(k)
    lane = lax.broadcasted_iota(jnp.int32, (1, cw), 1)
    nr = min(rows, COUNT_ROWS)

    def count(pred_value):
        out = []
        for r0 in range(0, rows, nr):
            rs = slice(r0, r0 + nr)

            def body(c, acc, r0=r0, rs=rs):
                return acc + _lane_fold(pred_value(load(c, r0, nr), c * cw + lane, rs))
            acc = lax.fori_loop(0, nck, body, jnp.zeros((nr, LANES), F32))
            if tail is not None:
                acc = acc + pred_value(tail[0][rs], tail[1], rs)
            out.append(jnp.sum(acc, axis=1, keepdims=True))
        return out[0] if len(out) == 1 else jnp.concatenate(out, axis=0)

    def value_bit(it, u):
        cand = u | jnp.left_shift(jnp.int32(1), 31 - it)
        candf = _key_to_float(cand)
        cnt = count(lambda x, idx, rs: jnp.where(x >= candf[rs], 1.0, 0.0))
        return jnp.where(cnt >= kf, cand, u)

    u = lax.fori_loop(0, 32, value_bit, jnp.zeros((rows, 1), jnp.int32))
    thr = jnp.where((u >= 0) & (u < 2 ** 23), -jnp.inf, _key_to_float(u))
    need = kf - count(lambda x, idx, rs: jnp.where(x > thr[rs], 1.0, 0.0))

    def index_bit(it, v):
        cand = v | jnp.left_shift(jnp.int32(1), idx_bits - 1 - it)
        below = count(lambda x, idx, rs: jnp.where(x == thr[rs], jnp.where(idx < cand[rs], 1.0, 0.0), 0.0))
        return jnp.where(below < need, cand, v)

    tied = jnp.max(count(lambda x, idx, rs: jnp.where(x >= thr[rs], 1.0, 0.0))) > kf
    v = lax.cond(tied,
                 lambda: lax.fori_loop(0, idx_bits, index_bit, jnp.zeros((rows, 1), jnp.int32)),
                 lambda: jnp.full((rows, 1), (1 << idx_bits) - 1, jnp.int32))
    return thr, jnp.where(need >= 1.0, v, -1)


def _topk_mask(x, idx, thr, vmax):
    gt = jnp.where(x > thr, 1.0, 0.0)
    eq = jnp.where(x == thr, jnp.where(idx <= vmax, 1.0, 0.0), 0.0)
    return jnp.maximum(gt, eq)


def _sublane_fold(w):
    parts = [w[j * SUBLANES:(j + 1) * SUBLANES] for j in range(w.shape[0] // SUBLANES)]
    while len(parts) > 1:
        parts = [parts[j] + parts[j + 1] if j + 1 < len(parts) else parts[j] for j in range(0, len(parts), 2)]
    return parts[0]


def _topk_threshold_cols(load, nck, k, idx_bits, cols, ck):
    kf = jnp.float32(k)
    sub = lax.broadcasted_iota(jnp.int32, (ck, 1), 0)

    def count(pred_value):
        def body(c, acc):
            return acc + _sublane_fold(pred_value(load(c), c * ck + sub))
        acc = lax.fori_loop(0, nck, body, jnp.zeros((SUBLANES, cols), F32))
        return jnp.sum(acc, axis=0, keepdims=True)

    def value_bit(it, u):
        cand = u | jnp.left_shift(jnp.int32(1), 31 - it)
        candf = _key_to_float(cand)
        cnt = count(lambda x, idx: jnp.where(x >= candf, 1.0, 0.0))
        return jnp.where(cnt >= kf, cand, u)

    u = lax.fori_loop(0, 32, value_bit, jnp.zeros((1, cols), jnp.int32))
    thr = jnp.where((u >= 0) & (u < 2 ** 23), -jnp.inf, _key_to_float(u))
    need = kf - count(lambda x, idx: jnp.where(x > thr, 1.0, 0.0))

    def index_bit(it, v):
        cand = v | jnp.left_shift(jnp.int32(1), idx_bits - 1 - it)
        below = count(lambda x, idx: jnp.where(x == thr, jnp.where(idx < cand, 1.0, 0.0), 0.0))
        return jnp.where(below < need, cand, v)

    tied = jnp.max(count(lambda x, idx: jnp.where(x >= thr, 1.0, 0.0))) > kf
    v = lax.cond(tied,
                 lambda: lax.fori_loop(0, idx_bits, index_bit, jnp.zeros((1, cols), jnp.int32)),
                 lambda: jnp.full((1, cols), (1 << idx_bits) - 1, jnp.int32))
    return thr, jnp.where(need >= 1.0, v, -1)


LOG2E = 1.4426950408889634
SCORE_SCALE = SCALE * LOG2E


def _softmax_init(rows):
    return (jnp.full((rows, 1), NEG, F32), jnp.zeros((rows, 1), F32), jnp.zeros((rows, HEAD_DIM), F32))


def _softmax_step(carry, q, k, v, bias):
    m, l, acc = carry
    t = _dot_nt(q, k) * SCORE_SCALE + bias
    m_new = jnp.maximum(m, jnp.max(t, axis=1, keepdims=True))
    alpha = jnp.exp2(m - m_new)
    e = jnp.exp2(t - m_new)
    l = alpha * l + jnp.sum(e, axis=1, keepdims=True)
    acc = alpha * acc + _dot(e.astype(BF16), v)
    return m_new, l, acc


def _softmax_done(carry):
    _, l, acc = carry
    return acc / jnp.maximum(l, 1e-30)


def _mask_bias(maskf):
    return jnp.where(maskf > 0.0, 0.0, NEG)


def _stack_heads(ref, kvh):
    return jnp.concatenate([ref[:, (kvh * GROUP + g) * HEAD_DIM:(kvh * GROUP + g + 1) * HEAD_DIM]
                            for g in range(GROUP)], axis=0)


def _kv_cols(kvh):
    return (slice(kvh * 2 * HEAD_DIM, (kvh * 2 + 1) * HEAD_DIM),
            slice((kvh * 2 + 1) * HEAD_DIM, (kvh * 2 + 2) * HEAD_DIM))


def _dsa_prompt_kernel(iq_ref, iwt_ref, ik2_ref, qa_ref, akv_ref, sza_ref, o_ref, sc_ref, mk_ref,
                       *, tq, tk, k_top, idx_bits):
    i = pl.program_id(1)
    nck = ((i + 1) * tq + tk - 1) // tk
    lane = lax.broadcasted_iota(jnp.int32, (1, LANES), 1)
    qpos = i * tq + lax.broadcasted_iota(jnp.int32, (1, tq), 1)
    krow = lax.broadcasted_iota(jnp.int32, (tk, 1), 0)
    iwt = iwt_ref[...] * (IDX_DIM ** -0.5)

    iqs = []
    for hd in range(IDX_HEADS):
        pair = iq_ref[:, (hd // 2) * LANES:(hd // 2 + 1) * LANES]
        keep = (lane < IDX_DIM) if hd % 2 == 0 else (lane >= IDX_DIM)
        iqs.append(jnp.where(keep, pair, jnp.zeros_like(pair)))

    def score_chunk(c, carry):
        k0 = pl.multiple_of(c * tk, tk)
        ikc = ik2_ref[pl.ds(k0, tk), :]
        acc = jnp.zeros((tk, tq), F32)
        for hd in range(IDX_HEADS):
            acc = acc + jnp.maximum(_dot_nt(ikc, iqs[hd]), 0.0) * iwt[hd:hd + 1, :]
        sc_ref[c] = jnp.where(k0 + krow <= qpos, acc, NEG)
        return carry

    lax.fori_loop(0, nck, score_chunk, 0)

    thr, vmax = _topk_threshold_cols(lambda c: sc_ref[c], nck, k_top, idx_bits, tq, tk)

    def mask_chunk(c, carry):
        idx = c * tk + krow
        sel = _topk_mask(sc_ref[c], idx, thr, vmax) * jnp.where(idx <= qpos, 1.0, 0.0)
        mk_ref[c] = _mask_bias(sel).T
        return carry

    lax.fori_loop(0, nck, mask_chunk, 0)

    qs = [_stack_heads(qa_ref, kvh) for kvh in range(KV_HEADS)]

    def attend(c, carries):
        k0 = pl.multiple_of(c * tk, tk)
        bias = jnp.concatenate([mk_ref[c]] * GROUP, axis=0)
        out = []
        for kvh in range(KV_HEADS):
            kcol, vcol = _kv_cols(kvh)
            out.append(_softmax_step(carries[kvh], qs[kvh], akv_ref[pl.ds(k0, tk), kcol],
                                     akv_ref[pl.ds(k0, tk), vcol], bias))
        return tuple(out)

    carries = lax.fori_loop(0, nck, attend, tuple(_softmax_init(GROUP * tq) for _ in range(KV_HEADS)))
    for kvh in range(KV_HEADS):
        o = _softmax_done(carries[kvh])
        for g in range(GROUP):
            sl = slice((kvh * GROUP + g) * HEAD_DIM, (kvh * GROUP + g + 1) * HEAD_DIM)
            o_ref[:, sl] = (o[g * tq:(g + 1) * tq] * sza_ref[:, sl]).astype(BF16)


def _dsa_prompt(iq, iw_t, ik2, qa, akv_b, sza, b, t, tq, tk):
    nqb = t // tq
    k_top = min(A_TOPK, t // 4)
    idx_bits = max(1, (t - 1).bit_length())
    row = lambda bi, i: (bi * nqb + i, 0)
    whole = lambda bi, i: (bi, 0, 0)
    kern = functools.partial(_dsa_prompt_kernel, tq=tq, tk=tk, k_top=k_top, idx_bits=idx_bits)
    return pl.pallas_call(
        kern,
        grid=(b, nqb),
        in_specs=[pl.BlockSpec((tq, IDX_COLS), row),
                  pl.BlockSpec((IDX_HEADS, tq), lambda bi, i: (0, bi * nqb + i)),
                  pl.BlockSpec((None, t, LANES), whole), pl.BlockSpec((tq, WIDTH), row),
                  pl.BlockSpec((None, t, KV_ROW), whole), pl.BlockSpec((tq, WIDTH), row)],
        out_specs=pl.BlockSpec((tq, WIDTH), row),
        out_shape=jax.ShapeDtypeStruct((b * t, WIDTH), BF16),
        scratch_shapes=[pltpu.VMEM((t // tk, tk, tq), F32), pltpu.VMEM((t // tk, tq, tk), F32)],
        compiler_params=_params(2),
        name="dsa_prompt",
    )(iq, iw_t, ik2, qa, akv_b, sza)


SLC_SHIFT = SLC_BLOCK.bit_length() - 1
assert 1 << SLC_SHIFT == SLC_BLOCK


def _block_rules(imp, blk, qpos):
    cur = lax.shift_right_logical(qpos, SLC_SHIFT)
    return jnp.where(blk == cur, -NEG, jnp.where(blk * SLC_BLOCK > qpos, NEG, imp))


def _split_dot(p, m):
    hi = p.astype(BF16)
    lo = (p - hi.astype(F32)).astype(BF16)
    return _dot(hi, m) + _dot(lo, m)


def _nsa_prompt_kernel(qb_ref, qr_ref, kc_ref, vc_ref, slc_ref, win_ref, ovt_ref, misc_ref, szb_ref,
                       o_ref, sc_ref, *, tq, tk, n_cmp, n_blk, n_sel):
    i = pl.program_id(1)
    nck = ((i + 1) * tq + tk - 1) // tk
    qpos = i * tq + lax.broadcasted_iota(jnp.int32, (tq, 1), 0)
    kiota = lax.broadcasted_iota(jnp.int32, (1, tk), 1)
    misc = misc_ref[...]
    ncp = kc_ref.shape[0]
    cidx = lax.broadcasted_iota(jnp.int32, (1, ncp), 1)
    cmask = jnp.where((CMP_STRIDE * cidx + CMP_LEN - 1 <= qpos) & (cidx < n_cmp), 1.0, 0.0)
    cmask = jnp.concatenate([cmask] * GROUP, axis=0)
    blk = lax.broadcasted_iota(jnp.int32, (LANES, 1), 0)
    qlane = i * tq + lax.broadcasted_iota(jnp.int32, (1, tq), 1)

    o_cmp = []
    for kvh in range(KV_HEADS):
        kc = kc_ref[:, kvh * HEAD_DIM:(kvh + 1) * HEAD_DIM]
        vc = vc_ref[:, kvh * HEAD_DIM:(kvh + 1) * HEAD_DIM]
        s = _dot_nt(_stack_heads(qb_ref, kvh), kc) * SCALE
        sm = jnp.where(cmask > 0.0, s, NEG)
        e = jnp.exp(sm - jnp.max(sm, axis=1, keepdims=True)) * cmask
        p = e / jnp.maximum(jnp.sum(e, axis=1, keepdims=True), 1e-30)
        o_cmp.append(_dot(p.astype(BF16), vc))
        psum = p[0:tq]
        for g in range(1, GROUP):
            psum = psum + p[g * tq:(g + 1) * tq]
        hi = psum.astype(BF16)
        lo = (psum - hi.astype(F32)).astype(BF16)
        imp = _dot_nt(ovt_ref[...], hi) + _dot_nt(ovt_ref[...], lo)
        imp = _block_rules(imp, blk, qlane)
        sc_ref[:, kvh * tq:(kvh + 1) * tq] = jnp.where(blk < n_blk, imp, -jnp.inf)

    thr, vmax = _topk_threshold_cols(lambda c: sc_ref[...], 1, n_sel, 7, KV_HEADS * tq, LANES)
    bmask_t = _topk_mask(sc_ref[...], blk, thr, vmax)
    bmask = [bmask_t[:, kvh * tq:(kvh + 1) * tq].T.astype(BF16) for kvh in range(KV_HEADS)]

    riota = lax.broadcasted_iota(jnp.int32, (LANES, 1), 0)
    nwk = WINDOW + tq
    w0 = pl.multiple_of(i * tq, tq)
    kposw = i * tq - WINDOW + lax.broadcasted_iota(jnp.int32, (1, nwk), 1)
    dw = qpos - kposw
    bias_w = jnp.where((dw >= 0) & (dw < WINDOW) & (kposw >= 0), 0.0, NEG)
    bias_w = jnp.concatenate([bias_w] * GROUP, axis=0)

    qs = [_stack_heads(qr_ref, kvh) for kvh in range(KV_HEADS)]

    def attend(c, carries):
        k0 = pl.multiple_of(c * tk, tk)
        kpos = k0 + kiota
        expand = jnp.where(riota == lax.shift_right_logical(kpos, SLC_SHIFT), 1.0, 0.0).astype(BF16)
        causal = jnp.where(kpos <= qpos, 1.0, 0.0)
        out = []
        for kvh in range(KV_HEADS):
            kcol, vcol = _kv_cols(kvh)
            sel = _dot(bmask[kvh], expand) * causal
            bias = jnp.concatenate([_mask_bias(sel)] * GROUP, axis=0)
            out.append(_softmax_step(carries[kvh], qs[kvh], slc_ref[pl.ds(k0, tk), kcol],
                                     slc_ref[pl.ds(k0, tk), vcol], bias))
        return tuple(out)

    slc_carries = lax.fori_loop(0, nck, attend, tuple(_softmax_init(GROUP * tq) for _ in range(KV_HEADS)))

    for kvh in range(KV_HEADS):
        q = qs[kvh]
        kcol, vcol = _kv_cols(kvh)
        o_slc = _softmax_done(slc_carries[kvh])
        o_win = _softmax_done(_softmax_step(_softmax_init(GROUP * tq), q, win_ref[pl.ds(w0, nwk), kcol],
                                            win_ref[pl.ds(w0, nwk), vcol], bias_w))
        for g in range(GROUP):
            hd = kvh * GROUP + g
            sl = slice(hd * HEAD_DIM, (hd + 1) * HEAD_DIM)
            rs = slice(g * tq, (g + 1) * tq)
            gate = [misc[:, MISC_IW + j * N_HEADS + hd:MISC_IW + j * N_HEADS + hd + 1] for j in range(3)]
            o_b = gate[0] * o_cmp[kvh][rs] + gate[1] * o_slc[rs] + gate[2] * o_win[rs]
            o_ref[:, sl] = (o_b * szb_ref[:, sl]).astype(BF16)


def _overlap_matrix(n_cmp_pad, n_cols, col_block):
    i = jnp.arange(n_cmp_pad)[:, None]
    j = col_block[None, :]
    ov = (CMP_STRIDE * i < SLC_BLOCK * (j + 1)) & (CMP_STRIDE * i + CMP_LEN > SLC_BLOCK * j) & (j >= 0)
    return ov.astype(BF16)


def _nsa_prompt(qb, qr, kc, vc, slc_b, win_b, misc, szb, b, t, tq, tk):
    nqb = t // tq
    n_cmp = t // CMP_STRIDE - 1
    n_blk = t // SLC_BLOCK
    assert n_blk <= LANES
    ncp = kc.shape[1]
    ov = _overlap_matrix(ncp, LANES, jnp.where(jnp.arange(LANES) < n_blk, jnp.arange(LANES), -1))
    row = lambda bi, i: (bi * nqb + i, 0)
    whole = lambda bi, i: (bi, 0, 0)
    kern = functools.partial(_nsa_prompt_kernel, tq=tq, tk=tk, n_cmp=n_cmp, n_blk=n_blk,
                             n_sel=min(N_SEL, n_blk))
    return pl.pallas_call(
        kern,
        grid=(b, nqb),
        in_specs=[pl.BlockSpec((tq, WIDTH), row), pl.BlockSpec((tq, WIDTH), row),
                  pl.BlockSpec((None, ncp, KV_HEADS * HEAD_DIM), whole),
                  pl.BlockSpec((None, ncp, KV_HEADS * HEAD_DIM), whole),
                  pl.BlockSpec((None, t, KV_ROW), whole), pl.BlockSpec((None, t + WINDOW, KV_ROW), whole),
                  pl.BlockSpec((LANES, ncp), lambda bi, i: (0, 0)),
                  pl.BlockSpec((tq, LANES), row), pl.BlockSpec((tq, WIDTH), row)],
        out_specs=pl.BlockSpec((tq, WIDTH), row),
        out_shape=jax.ShapeDtypeStruct((b * t, WIDTH), BF16),
        scratch_shapes=[pltpu.VMEM((LANES, KV_HEADS * tq), F32)],
        compiler_params=_params(2),
        name="nsa_prompt",
    )(qb, qr, kc, vc, slc_b, win_b, ov.T, misc, szb)


CHUNK_FLAT = CMP_STRIDE * KV_ROW
FS_COLS = KV_HEADS * 2 * 2 * CMP_HIDDEN


def _chunk_terms_body(x, w_ref, o_ref):
    for kc in range(KV_HEADS * 2):
        xk = jnp.concatenate(
            [x[:, p * KV_ROW + kc * HEAD_DIM: p * KV_ROW + (kc + 1) * HEAD_DIM] for p in range(CMP_STRIDE)],
            axis=1).astype(BF16)
        o_ref[:, kc * 2 * CMP_HIDDEN:(kc + 1) * 2 * CMP_HIDDEN] = _dot(xk, w_ref[kc % 2])


def _chunk_terms_dense_kernel(x_ref, w_ref, o_ref):
    _chunk_terms_body(x_ref[...], w_ref, o_ref)


def _chunk_terms_dense(x, w, tr):
    n = x.shape[0]
    return pl.pallas_call(
        _chunk_terms_dense_kernel,
        grid=(n // tr,),
        in_specs=[pl.BlockSpec((tr, CHUNK_FLAT), lambda i: (i, 0)),
                  pl.BlockSpec((2, CMP_STRIDE * HEAD_DIM, 2 * CMP_HIDDEN), lambda i: (0, 0, 0))],
        out_specs=pl.BlockSpec((tr, FS_COLS), lambda i: (i, 0)),
        out_shape=jax.ShapeDtypeStruct((n, FS_COLS), F32),
        compiler_params=_params(1),
        name="chunk_terms_dense",
    )(x, w)


ROWS_PER_TOKEN = KV_HEADS * 2
PAGE_ROWS = PAGE_SIZE * ROWS_PER_TOKEN
CHUNKS_PER_PAGE = PAGE_SIZE // CMP_STRIDE


def _cache_rows(cache):
    return cache.reshape(-1, HEAD_DIM)


def _token_rows(ref, slot, n_tokens):
    return ref[pl.ds(slot, n_tokens, stride=ROWS_PER_TOKEN), :]


def _sublane_transpose(tiles):
    tiles = list(tiles)
    sub = lax.broadcasted_iota(jnp.int32, (SUBLANES, LANES), 0)
    for d in (4, 2, 1):
        keep = (sub & d) == 0
        for i in range(SUBLANES):
            if i & d:
                continue
            lo, hi = tiles[i], tiles[i + d]
            tiles[i] = jnp.where(keep, lo, pltpu.roll(hi, d, 0))
            tiles[i + d] = jnp.where(keep, pltpu.roll(lo, SUBLANES - d, 0), hi)
    return tiles


def _chunk_terms_paged_kernel(pt_ref, *refs, pg):
    pages, w_ref, o_ref = refs[:pg], refs[pg], refs[pg + 1]
    chunk_rows = CMP_STRIDE * ROWS_PER_TOKEN
    assert CHUNKS_PER_PAGE == SUBLANES and chunk_rows % SUBLANES == 0
    pieces = [[] for _ in range(chunk_rows)]
    for pr in pages:
        for t in range(chunk_rows // SUBLANES):
            tiles = [pr[n * chunk_rows + t * SUBLANES:n * chunk_rows + (t + 1) * SUBLANES, :]
                     for n in range(CHUNKS_PER_PAGE)]
            for s, tile in enumerate(_sublane_transpose(tiles)):
                pieces[t * SUBLANES + s].append(tile)
    for kc in range(ROWS_PER_TOKEN):
        xk = jnp.concatenate([jnp.concatenate(pieces[p * ROWS_PER_TOKEN + kc], axis=0)
                              for p in range(CMP_STRIDE)], axis=1).astype(BF16)
        o_ref[:, kc * 2 * CMP_HIDDEN:(kc + 1) * 2 * CMP_HIDDEN] = _dot(xk, w_ref[kc % 2])


def _page_specs(block, pg):
    tail = (0,) * (len(block) - 1)

    def spec(j):
        return pl.BlockSpec(block, lambda bi, g, pt: (pt[bi, g * pg + j],) + tail)
    return [spec(j) for j in range(pg)]


def _chunk_terms_paged(pool, page_table, w, pg):
    db, n_pages = page_table.shape
    cpp = CHUNKS_PER_PAGE
    pool_v = _cache_rows(pool)
    kern = functools.partial(_chunk_terms_paged_kernel, pg=pg)
    grid_spec = pltpu.PrefetchScalarGridSpec(
        num_scalar_prefetch=1,
        grid=(db, n_pages // pg),
        in_specs=_page_specs((PAGE_ROWS, HEAD_DIM), pg)
                 + [pl.BlockSpec((2, CMP_STRIDE * HEAD_DIM, 2 * CMP_HIDDEN), lambda bi, g, pt: (0, 0, 0))],
        out_specs=pl.BlockSpec((None, pg * cpp, FS_COLS), lambda bi, g, pt: (bi, g, 0)),
    )
    return pl.pallas_call(
        kern, grid_spec=grid_spec,
        out_shape=jax.ShapeDtypeStruct((db, n_pages * cpp, FS_COLS), F32),
        compiler_params=_params(2),
        name="chunk_terms_paged",
    )(page_table, *([pool_v] * pg), w)


def _compress_mlp_kernel(fs_ref, new_ref, pe_ref, w1_ref, w2_ref, g_ref, kc_ref, vc_ref):
    n = fs_ref.shape[0]
    last = lax.broadcasted_iota(jnp.int32, (n, 1), 0) == n - 1
    for kc in range(KV_HEADS * 2):
        kvh, c = kc // 2, kc % 2
        base = kc * 2 * CMP_HIDDEN
        first = fs_ref[:, base:base + CMP_HIDDEN]
        second = fs_ref[:, base + CMP_HIDDEN:base + 2 * CMP_HIDDEN]
        shifted = jnp.where(last, new_ref[0:1, base + CMP_HIDDEN:base + 2 * CMP_HIDDEN],
                            pltpu.roll(second, n - 1, 0))
        pe = jnp.broadcast_to(pe_ref[c:c + 1, :], (SUBLANES, pe_ref.shape[1])).astype(BF16)
        bias = _dot(pe, w1_ref[c])[0:1, :]
        pre = first + shifted + bias
        hid = pre * jax.nn.sigmoid(pre)
        out = _dot(hid.astype(BF16), w2_ref[c])
        sl = slice(kvh * HEAD_DIM, (kvh + 1) * HEAD_DIM)
        if c == 0:
            kc_ref[:, sl] = _head_norm(out, g_ref[...]).astype(BF16)
        else:
            vc_ref[:, sl] = out.astype(BF16)


def _compress_mlp(fs, fs_new, pe, w1r, w2, g_cmp):
    nb, n, _ = fs.shape
    whole = lambda bi: (bi, 0, 0)
    fix2 = lambda bi: (0, 0)
    fix3 = lambda bi: (0, 0, 0)
    out = jax.ShapeDtypeStruct((nb, n, KV_HEADS * HEAD_DIM), BF16)
    return pl.pallas_call(
        _compress_mlp_kernel,
        grid=(nb,),
        in_specs=[pl.BlockSpec((None, n, FS_COLS), whole), pl.BlockSpec((None, SUBLANES, FS_COLS), whole),
                  pl.BlockSpec((2, CMP_LEN * HEAD_DIM), fix2),
                  pl.BlockSpec((2, CMP_LEN * HEAD_DIM, CMP_HIDDEN), fix3),
                  pl.BlockSpec((2, CMP_HIDDEN, HEAD_DIM), fix3), pl.BlockSpec((1, HEAD_DIM), fix2)],
        out_specs=[pl.BlockSpec((None, n, KV_HEADS * HEAD_DIM), whole)] * 2,
        out_shape=[out, out],
        compiler_params=_params(1),
        name="compress_mlp",
    )(fs, fs_new, pe, w1r, w2, g_cmp.reshape(1, HEAD_DIM))


def _idx_scores_kernel(pt_ref, *refs, pg, n_new):
    pages = refs[:pg]
    iq_ref, iw_ref, new_ref, o_ref, onew_ref = refs[pg:pg + 5]
    g = pl.program_id(1)
    iq = iq_ref[...]
    iw = iw_ref[...]
    row = lax.broadcasted_iota(jnp.int32, (SAMPLE_Q, 1), 0)

    def scores(keys_t, first_idx):
        s = jnp.maximum(_dot(iq, keys_t), 0.0)
        n = s.shape[1]
        w = s * jnp.concatenate([iw] * (n // LANES), axis=1)
        acc = w[0:SAMPLE_Q]
        for hd in range(1, IDX_HEADS):
            acc = acc + w[hd * SAMPLE_Q:(hd + 1) * SAMPLE_Q]
        ramp = -(first_idx + lax.broadcasted_iota(jnp.int32, (1, n), 1)).astype(F32)
        return jnp.where(row < n_new, acc, ramp)

    cw = pg * PAGE_SIZE
    o_ref[...] = scores(jnp.concatenate([p[...] for p in pages], axis=1).astype(BF16), g * cw)

    @pl.when(g == pl.num_programs(1) - 1)
    def _():
        onew_ref[...] = scores(new_ref[...], pl.num_programs(1) * cw)


def _idx_scores(pool_idx_t, page_table, iqm, iwm, ik_new_t, pg, n_new):
    db, n_pages = page_table.shape
    ng = n_pages // pg
    kern = functools.partial(_idx_scores_kernel, pg=pg, n_new=n_new)
    per_b = lambda bi, g, pt: (bi, 0, 0)
    grid_spec = pltpu.PrefetchScalarGridSpec(
        num_scalar_prefetch=1,
        grid=(db, ng),
        in_specs=_page_specs((None, IDX_DIM, PAGE_SIZE), pg)
                 + [pl.BlockSpec((None, IDX_HEADS * SAMPLE_Q, IDX_DIM), per_b),
                    pl.BlockSpec((None, IDX_HEADS * SAMPLE_Q, LANES), per_b),
                    pl.BlockSpec((None, IDX_DIM, LANES), per_b)],
        out_specs=[pl.BlockSpec((None, SAMPLE_Q, pg * PAGE_SIZE), lambda bi, g, pt: (g, bi, 0)),
                   pl.BlockSpec((SAMPLE_Q, LANES), lambda bi, g, pt: (bi, 0))],
    )
    return pl.pallas_call(
        kern, grid_spec=grid_spec,
        out_shape=[jax.ShapeDtypeStruct((ng, db * SAMPLE_Q, pg * PAGE_SIZE), F32),
                   jax.ShapeDtypeStruct((db * SAMPLE_Q, LANES), F32)],
        compiler_params=_params(2),
        name="idx_scores",
    )(page_table, *([pool_idx_t] * pg), iqm, iwm, ik_new_t)


TOPK_ROWS = 64


def _dsa_topk_kernel(sc_ref, new_ref, o_ref, onew_ref, *, ng, cw, n_new, k_top, idx_bits, rb):
    row = lax.broadcasted_iota(jnp.int32, (rb, 1), 0)
    q = row & (SAMPLE_Q - 1)
    lane = lax.broadcasted_iota(jnp.int32, (1, LANES), 1)
    lane_cw = lax.broadcasted_iota(jnp.int32, (1, cw), 1)
    ok_new = (lane <= q) & (lane < n_new)
    x_new = jnp.where(ok_new, new_ref[...], NEG)
    idx_new = ng * cw + lane

    thr, vmax = _topk_threshold(lambda c, r0, nr: sc_ref[c, r0:r0 + nr, :], ng, k_top, idx_bits, rb, cw,
                                tail=(x_new, idx_new))
    for g in range(ng):
        o_ref[g] = _mask_bias(_topk_mask(sc_ref[g], g * cw + lane_cw, thr, vmax))
    onew_ref[...] = _mask_bias(_topk_mask(x_new, idx_new, thr, vmax) * jnp.where(ok_new, 1.0, 0.0))


def _dsa_topk(scores, scores_new, past, n_new):
    ng, rows, cw = scores.shape
    rb = min(TOPK_ROWS, rows)
    total = past + n_new
    kern = functools.partial(_dsa_topk_kernel, ng=ng, cw=cw, n_new=n_new, k_top=min(A_TOPK, total // 4),
                             idx_bits=(ng * cw + LANES - 1).bit_length(), rb=rb)
    big = pl.BlockSpec((ng, rb, cw), lambda r: (0, r, 0))
    small = pl.BlockSpec((rb, LANES), lambda r: (r, 0))
    return pl.pallas_call(
        kern,
        grid=(rows // rb,),
        in_specs=[big, small],
        out_specs=[big, small],
        out_shape=[jax.ShapeDtypeStruct((ng, rows, cw), F32), jax.ShapeDtypeStruct((rows, LANES), F32)],
        compiler_params=_params(1),
        name="dsa_topk",
    )(scores, scores_new)


QROWS = GROUP * SAMPLE_Q


def _paged_attn_kernel(pt_ref, *refs, pg, mask_rows):
    pages = refs[:pg]
    q_ref, mk_ref, mknew_ref, new_ref, o_ref, m_ref, l_ref, acc_ref = refs[pg:pg + 8]
    g = pl.program_id(1)

    @pl.when(g == 0)
    def _():
        m_ref[...] = jnp.full(m_ref.shape, NEG, F32)
        l_ref[...] = jnp.zeros(l_ref.shape, F32)
        acc_ref[...] = jnp.zeros(acc_ref.shape, F32)

    def update(keys, values, bias_all):
        carries = [(m_ref[kvh], l_ref[kvh], acc_ref[kvh]) for kvh in range(KV_HEADS)]
        for kvh in range(KV_HEADS):
            r0 = kvh * SAMPLE_Q if mask_rows == KV_HEADS * SAMPLE_Q else 0
            bias = jnp.concatenate([bias_all[r0:r0 + SAMPLE_Q]] * GROUP, axis=0)
            carries[kvh] = _softmax_step(carries[kvh], q_ref[kvh], keys(kvh), values(kvh), bias)
        for kvh in range(KV_HEADS):
            m_ref[kvh], l_ref[kvh], acc_ref[kvh] = carries[kvh]

    def paged(slot):
        return jnp.concatenate([_token_rows(p, slot, PAGE_SIZE) for p in pages], axis=0).astype(BF16)

    update(lambda kvh: paged(2 * kvh), lambda kvh: paged(2 * kvh + 1), mk_ref[...])

    @pl.when(g == pl.num_programs(1) - 1)
    def _():
        update(lambda kvh: new_ref[:, _kv_cols(kvh)[0]], lambda kvh: new_ref[:, _kv_cols(kvh)[1]], mknew_ref[...])
        for kvh in range(KV_HEADS):
            o_ref[kvh] = _softmax_done((m_ref[kvh], l_ref[kvh], acc_ref[kvh]))


def _paged_attn(pool, page_table, q, bias, bias_new, kv_new, pg):
    db, n_pages = page_table.shape
    ng = n_pages // pg
    mask_rows = bias.shape[1] // db
    pool_v = _cache_rows(pool)
    kern = functools.partial(_paged_attn_kernel, pg=pg, mask_rows=mask_rows)
    per_b3 = lambda bi, g, pt: (bi, 0, 0)
    per_b4 = lambda bi, g, pt: (bi, 0, 0, 0)
    grid_spec = pltpu.PrefetchScalarGridSpec(
        num_scalar_prefetch=1,
        grid=(db, ng),
        in_specs=_page_specs((PAGE_ROWS, HEAD_DIM), pg)
                 + [pl.BlockSpec((None, KV_HEADS, QROWS, HEAD_DIM), per_b4),
                    pl.BlockSpec((None, mask_rows, pg * PAGE_SIZE), lambda bi, g, pt: (g, bi, 0)),
                    pl.BlockSpec((mask_rows, LANES), lambda bi, g, pt: (bi, 0)),
                    pl.BlockSpec((None, LANES, KV_ROW), per_b3)],
        out_specs=pl.BlockSpec((None, KV_HEADS, QROWS, HEAD_DIM), per_b4),
        scratch_shapes=[pltpu.VMEM((KV_HEADS, QROWS, 1), F32), pltpu.VMEM((KV_HEADS, QROWS, 1), F32),
                        pltpu.VMEM((KV_HEADS, QROWS, HEAD_DIM), F32)],
    )
    return pl.pallas_call(
        kern, grid_spec=grid_spec,
        out_shape=jax.ShapeDtypeStruct((db, KV_HEADS, QROWS, HEAD_DIM), F32),
        compiler_params=_params(2),
        name="paged_attn",
    )(page_table, *([pool_v] * pg), q, bias, bias_new, kv_new)


BLOCKS_PER_PAGE = PAGE_SIZE // SLC_BLOCK
BLOCK_ROWS = SLC_BLOCK * ROWS_PER_TOKEN
assert BLOCKS_PER_PAGE == 2


def _slc_gather_kernel(pt_ref, sel_ref, *refs, nsel, ds):
    nblk = KV_HEADS * nsel
    blocks = refs[:nblk]
    q_ref, bnew_ref, new_ref, o_ref = refs[nblk:nblk + 4]
    bi, qi = pl.program_id(0), pl.program_id(1)
    lane = lax.broadcasted_iota(jnp.int32, (1, LANES), 1)
    mine = (lax.broadcasted_iota(jnp.int32, (QROWS, 1), 0) & (SAMPLE_Q - 1)) == qi

    @pl.when(qi == 0)
    def _():
        o_ref[...] = jnp.zeros(o_ref.shape, F32)

    for kvh in range(KV_HEADS):
        base = ((bi * ds + qi) * KV_HEADS + kvh) * nsel
        kcol, vcol = _kv_cols(kvh)
        keys = jnp.concatenate([_token_rows(blocks[kvh * nsel + j], 2 * kvh, SLC_BLOCK)
                                for j in range(nsel)], axis=0).astype(BF16)
        vals = jnp.concatenate([_token_rows(blocks[kvh * nsel + j], 2 * kvh + 1, SLC_BLOCK)
                                for j in range(nsel)], axis=0).astype(BF16)
        slot_bias = [jnp.where(sel_ref[base + j] >= 0, 0.0, NEG) for j in range(nsel)]
        pairs = [jnp.where(lane < SLC_BLOCK, slot_bias[j], slot_bias[min(j + 1, nsel - 1)])
                 for j in range(0, nsel, 2)]
        bias = jnp.concatenate(pairs, axis=1)[:, :nsel * SLC_BLOCK]
        carry = _softmax_step(_softmax_init(QROWS), q_ref[kvh], keys, vals, bias)
        bnew = jnp.concatenate([bnew_ref[kvh * SAMPLE_Q:(kvh + 1) * SAMPLE_Q]] * GROUP, axis=0)
        carry = _softmax_step(carry, q_ref[kvh], new_ref[:, kcol], new_ref[:, vcol], bnew)
        o_ref[kvh] = jnp.where(mine, _softmax_done(carry), o_ref[kvh])


def _slc_gather(pool, page_table, sel, q, bias_new, kv_new, nsel, ds):
    db = page_table.shape[0]
    pool_v = _cache_rows(pool)

    def spec(kvh, j):
        def imap(bi, qi, pt, sl):
            blk = jnp.maximum(sl[((bi * ds + qi) * KV_HEADS + kvh) * nsel + j], 0)
            return (pt[bi, lax.shift_right_logical(blk, 1)] * BLOCKS_PER_PAGE + (blk & 1), 0)
        return pl.BlockSpec((BLOCK_ROWS, HEAD_DIM), imap)

    kern = functools.partial(_slc_gather_kernel, nsel=nsel, ds=ds)
    grid_spec = pltpu.PrefetchScalarGridSpec(
        num_scalar_prefetch=2,
        grid=(db, ds),
        in_specs=[spec(kvh, j) for kvh in range(KV_HEADS) for j in range(nsel)]
                 + [pl.BlockSpec((None, KV_HEADS, QROWS, HEAD_DIM), lambda bi, qi, pt, sl: (bi, 0, 0, 0)),
                    pl.BlockSpec((KV_HEADS * SAMPLE_Q, LANES), lambda bi, qi, pt, sl: (bi, 0)),
                    pl.BlockSpec((None, LANES, KV_ROW), lambda bi, qi, pt, sl: (bi, 0, 0))],
        out_specs=pl.BlockSpec((None, KV_HEADS, QROWS, HEAD_DIM), lambda bi, qi, pt, sl: (bi, 0, 0, 0)),
    )
    return pl.pallas_call(
        kern, grid_spec=grid_spec,
        out_shape=jax.ShapeDtypeStruct((db, KV_HEADS, QROWS, HEAD_DIM), F32),
        compiler_params=_params(2),
        name="slc_gather",
    )(page_table, sel, *([pool_v] * (KV_HEADS * nsel)), q, bias_new, kv_new)


def _nsa_sample_kernel(q_ref, kc_ref, vc_ref, ov_ref, o_ref, sel_ref, tmnew_ref, sc_ref,
                       *, past, n_new, n_cmp, n_blk, n_sel, ng, gb, nb):
    row = lax.broadcasted_iota(jnp.int32, (SAMPLE_Q, 1), 0)
    qpos = past + row
    ncp = kc_ref.shape[1]
    cidx = lax.broadcasted_iota(jnp.int32, (1, ncp), 1)
    cmask8 = jnp.where((CMP_STRIDE * cidx + CMP_LEN - 1 <= qpos) & (cidx < n_cmp), 1.0, 0.0)
    cmask = jnp.concatenate([cmask8] * GROUP, axis=0)
    lane = lax.broadcasted_iota(jnp.int32, (1, LANES), 1)

    for bb in range(nb):
        for kvh in range(KV_HEADS):
            kc = kc_ref[bb, :, kvh * HEAD_DIM:(kvh + 1) * HEAD_DIM]
            vc = vc_ref[bb, :, kvh * HEAD_DIM:(kvh + 1) * HEAD_DIM]
            s = _dot_nt(q_ref[bb, kvh], kc) * SCALE
            sm = jnp.where(cmask > 0.0, s, NEG)
            e = jnp.exp(sm - jnp.max(sm, axis=1, keepdims=True)) * cmask
            p = e / jnp.maximum(jnp.sum(e, axis=1, keepdims=True), 1e-30)
            o_ref[bb, kvh] = _dot(p.astype(BF16), vc)
            psum = p[0:SAMPLE_Q]
            for g in range(1, GROUP):
                psum = psum + p[g * SAMPLE_Q:(g + 1) * SAMPLE_Q]
            imp = _split_dot(psum, ov_ref[...])
            r0 = (bb * KV_HEADS + kvh) * SAMPLE_Q
            for g in range(ng + 1):
                blk = g * gb + lane
                v = _block_rules(imp[:, g * LANES:(g + 1) * LANES], blk, qpos)
                v = jnp.where(row < n_new, v, -blk.astype(F32))
                sc_ref[g, r0:r0 + SAMPLE_Q, :] = jnp.where((lane < gb) & (blk < n_blk), v, -jnp.inf)

    rows = nb * KV_HEADS * SAMPLE_Q
    idx_bits = ((ng + 1) * LANES - 1).bit_length()
    thr, vmax = _topk_threshold(lambda c, r0, nr: sc_ref[c, r0:r0 + nr, :], ng + 1, n_sel, idx_bits, rows, LANES)
    qpos2 = jnp.concatenate([qpos] * (nb * KV_HEADS), axis=0)
    n_past_blk = past // SLC_BLOCK
    tri = jnp.where(lax.broadcasted_iota(jnp.int32, (LANES, 1), 0) <= lane, 1.0, 0.0).astype(BF16)
    offs = jnp.zeros((rows, 1), F32)
    slots = [jnp.zeros((rows, 1), F32) for _ in range(n_sel)]
    for g in range(ng + 1):
        blk = g * gb + lane
        bm = _topk_mask(sc_ref[g], g * LANES + lane, thr, vmax)
        if g == ng:
            ok = jnp.where((past + lane <= qpos2) & (lane < n_new), 1.0, 0.0)
            tmnew_ref[...] = _mask_bias(bm[:, 0:1] * ok)
        bm = bm * jnp.where((lane < gb) & (blk < n_past_blk), 1.0, 0.0)
        rank = offs + _dot(bm.astype(BF16), tri)
        for j in range(n_sel):
            hit = jnp.where(bm > 0.0, jnp.where(rank == j + 1.0, blk.astype(F32), 0.0), 0.0)
            slots[j] = slots[j] + jnp.sum(hit, axis=1, keepdims=True)
        offs = offs + jnp.sum(bm, axis=1, keepdims=True)
    out = jnp.full((rows, LANES), -1.0, F32)
    for j in range(n_sel):
        out = jnp.where((lane == j) & (offs > j), slots[j], out)
    sel_ref[...] = out.astype(jnp.int32)


def _nsa_sample(q, kc, vc, past, n_new, ng, cw):
    db = q.shape[0]
    ncp = kc.shape[1]
    total = past + n_new
    n_blk = -(-total // SLC_BLOCK)
    n_cmp = total_chunks(past, n_new) - 1
    gb = cw // SLC_BLOCK
    assert gb <= LANES and past % cw == 0
    col = jnp.arange((ng + 1) * LANES)
    col_block = jnp.where(col % LANES < gb, (col // LANES) * gb + col % LANES, -1)
    col_block = jnp.where(col_block < n_blk, col_block, -1)
    ov = _overlap_matrix(ncp, (ng + 1) * LANES, col_block)
    nb = max(n for n in (4, 2, 1) if db % n == 0)
    kern = functools.partial(_nsa_sample_kernel, past=past, n_new=n_new, n_cmp=n_cmp, n_blk=n_blk,
                             n_sel=min(N_SEL, n_blk), ng=ng, gb=gb, nb=nb)
    rows = KV_HEADS * SAMPLE_Q
    b3 = lambda bi: (bi, 0, 0)
    b4 = lambda bi: (bi, 0, 0, 0)
    return pl.pallas_call(
        kern,
        grid=(db // nb,),
        in_specs=[pl.BlockSpec((nb, KV_HEADS, QROWS, HEAD_DIM), b4),
                  pl.BlockSpec((nb, ncp, KV_HEADS * HEAD_DIM), b3),
                  pl.BlockSpec((nb, ncp, KV_HEADS * HEAD_DIM), b3),
                  pl.BlockSpec((ncp, (ng + 1) * LANES), lambda bi: (0, 0))],
        out_specs=[pl.BlockSpec((nb, KV_HEADS, QROWS, HEAD_DIM), b4),
                   pl.BlockSpec((nb * rows, LANES), lambda bi: (bi, 0)),
                   pl.BlockSpec((nb * rows, LANES), lambda bi: (bi, 0))],
        out_shape=[jax.ShapeDtypeStruct((db, KV_HEADS, QROWS, HEAD_DIM), F32),
                   jax.ShapeDtypeStruct((db * rows, LANES), jnp.int32),
                   jax.ShapeDtypeStruct((db * rows, LANES), F32)],
        scratch_shapes=[pltpu.VMEM((ng + 1, nb * rows, LANES), F32)],
        compiler_params=_params(1),
        name="nsa_sample",
    )(q, kc, vc, ov)


def total_chunks(past, n_new):
    return past // CMP_STRIDE + -(-n_new // CMP_STRIDE)


def _win_sample_kernel(q_ref, st_ref, new_ref, o_ref, *, w_buf, n_new):
    row = lax.broadcasted_iota(jnp.int32, (SAMPLE_Q, 1), 0)
    row = jnp.concatenate([row] * GROUP, axis=0)
    j_old = lax.broadcasted_iota(jnp.int32, (1, w_buf), 1)
    j_new = w_buf + lax.broadcasted_iota(jnp.int32, (1, LANES), 1)
    def bias(j, valid):
        d = w_buf + row - j
        return jnp.where((d >= 0) & (d < WINDOW) & valid, 0.0, NEG)
    b_old, b_new = bias(j_old, True), bias(j_new, j_new < w_buf + n_new)
    for kvh in range(KV_HEADS):
        q = q_ref[kvh]
        kcol, vcol = _kv_cols(kvh)
        carry = _softmax_step(_softmax_init(QROWS), q, _token_rows(st_ref, 2 * kvh, w_buf).astype(BF16),
                              _token_rows(st_ref, 2 * kvh + 1, w_buf).astype(BF16), b_old)
        carry = _softmax_step(carry, q, new_ref[:, kcol], new_ref[:, vcol], b_new)
        o_ref[kvh] = _softmax_done(carry)


def _win_sample(q, state, kv_new, n_new):
    db, w_buf = state.shape[0], state.shape[1]
    kern = functools.partial(_win_sample_kernel, w_buf=w_buf, n_new=n_new)
    b3 = lambda bi: (bi, 0, 0)
    b4 = lambda bi: (bi, 0, 0, 0)
    return pl.pallas_call(
        kern,
        grid=(db,),
        in_specs=[pl.BlockSpec((None, KV_HEADS, QROWS, HEAD_DIM), b4),
                  pl.BlockSpec((w_buf * ROWS_PER_TOKEN, HEAD_DIM), lambda bi: (bi, 0)),
                  pl.BlockSpec((None, LANES, KV_ROW), b3)],
        out_specs=pl.BlockSpec((None, KV_HEADS, QROWS, HEAD_DIM), b4),
        out_shape=jax.ShapeDtypeStruct((db, KV_HEADS, QROWS, HEAD_DIM), F32),
        compiler_params=_params(1),
        name="win_sample",
    )(q, _cache_rows(state), kv_new)


def _gate_merge_kernel(oa_ref, oc_ref, os_ref, ow_ref, misc_ref, sza_ref, szb_ref, a_ref, b_ref):
    misc = misc_ref[...]
    a_ref[...] = (oa_ref[...] * sza_ref[...]).astype(BF16)
    for hd in range(N_HEADS):
        sl = slice(hd * HEAD_DIM, (hd + 1) * HEAD_DIM)
        g = [misc[:, MISC_IW + j * N_HEADS + hd:MISC_IW + j * N_HEADS + hd + 1] for j in range(3)]
        o_b = g[0] * oc_ref[:, sl] + g[1] * os_ref[:, sl] + g[2] * ow_ref[:, sl]
        b_ref[:, sl] = (o_b * szb_ref[:, sl]).astype(BF16)


def _gate_merge(o_a, o_cmp, o_slc, o_win, misc, sza, szb):
    n = o_a.shape[0]
    full = lambda w: pl.BlockSpec((n, w), lambda i: (0, 0))
    out = jax.ShapeDtypeStruct((n, WIDTH), BF16)
    return pl.pallas_call(
        _gate_merge_kernel,
        grid=(1,),
        in_specs=[full(WIDTH)] * 4 + [full(LANES), full(WIDTH), full(WIDTH)],
        out_specs=[full(WIDTH)] * 2,
        out_shape=[out, out],
        compiler_params=_params(1),
        name="gate_merge",
    )(o_a, o_cmp, o_slc, o_win, misc, sza, szb)


def _merge_kernel(a_ref, b_ref, h_ref, wa_ref, wb_ref, wg0_ref, wg1_ref, o_ref):
    h = h_ref[...]
    y_a = _dot(a_ref[...], wa_ref[...])
    y_b = _dot(b_ref[...], wb_ref[...])
    g0 = jax.nn.sigmoid(_dot(h, wg0_ref[...]))
    g1 = jax.nn.sigmoid(_dot(h, wg1_ref[...]))
    o_ref[...] = (g0 * y_a + g1 * y_b).astype(BF16)


def _merge(a, bm, h, w_a, w_b, w_g0, w_g1, tr, cb):
    n = a.shape[0]
    row = lambda j, i: (i, 0)
    col = lambda j, i: (0, j)
    return pl.pallas_call(
        _merge_kernel,
        grid=(D_MODEL // cb, n // tr),
        in_specs=[pl.BlockSpec((tr, WIDTH), row), pl.BlockSpec((tr, WIDTH), row),
                  pl.BlockSpec((tr, D_MODEL), row),
                  pl.BlockSpec((WIDTH, cb), col), pl.BlockSpec((WIDTH, cb), col),
                  pl.BlockSpec((D_MODEL, cb), col), pl.BlockSpec((D_MODEL, cb), col)],
        out_specs=pl.BlockSpec((tr, cb), lambda j, i: (i, j)),
        out_shape=jax.ShapeDtypeStruct((n, D_MODEL), BF16),
        compiler_params=_params(2),
        name="merge",
    )(a, bm, h, w_a, w_b, w_g0, w_g1)


def _out_proj_kernel(x_ref, m_ref, w_ref, o_ref):
    o_ref[...] = x_ref[...] + _dot(m_ref[...], w_ref[...])


def _out_proj(x, merged, w_out, tr):
    n = x.shape[0]
    row = lambda i: (i, 0)
    return pl.pallas_call(
        _out_proj_kernel,
        grid=(n // tr,),
        in_specs=[pl.BlockSpec((tr, D_MODEL), row), pl.BlockSpec((tr, D_MODEL), row),
                  pl.BlockSpec((D_MODEL, D_MODEL), lambda i: (0, 0))],
        out_specs=pl.BlockSpec((tr, D_MODEL), row),
        out_shape=jax.ShapeDtypeStruct((n, D_MODEL), F32),
        compiler_params=_params(1),
        name="out_proj",
    )(x, merged, w_out)


def _project(x2d, pos_period, tr, wts):
    n = x2d.shape[0]
    n_tab = pos_period.shape[0] // tr
    tabs_head = _rope_tables(pos_period, ROT_DIM, HEAD_DIM)
    tabs_iq = _rope_tables(pos_period, IDX_ROT_DIM, IDX_DIM)
    tabs_iq = tuple(jnp.concatenate([t, t], axis=1) for t in tabs_iq)
    ident = (jnp.ones_like(tabs_iq[0][:, :IDX_DIM]), jnp.zeros_like(tabs_iq[0][:, :IDX_DIM]),
             jnp.zeros_like(tabs_iq[0][:, :IDX_DIM]))
    tabs_misc = tuple(jnp.concatenate([t[:, :IDX_DIM], e], axis=1) for t, e in zip(tabs_iq, ident))

    h = _rmsnorm(x2d, wts["norm_gain"], tr)
    (qa,) = _proj_q(h, wts["w_qa"], wts["q_norm_a"], tabs_head, tr, n_tab, False)
    qb_rot, qb = _proj_q(h, wts["w_qb"], wts["q_norm_b"], tabs_head, tr, n_tab, True)
    akv, cmp_kv, slc, win, akv_b, slc_b, win_b = _proj_kv(
        h, wts["w_kv"], wts["k_norm_a"], wts["k_norm_slc"], wts["k_norm_win"], tabs_head, tr, n_tab)
    iq, misc = _proj_idx(h, wts["w_idx"], tabs_iq, tabs_misc, tr, n_tab)
    sza, szb = _proj_z(h, wts["w_z"], tr)
    return dict(h=h, qa=qa, qb=qb, qb_rot=qb_rot, akv=akv, cmp=cmp_kv, slc=slc, win=win,
                akv_b=akv_b, slc_b=slc_b, win_b=win_b, iq=iq, misc=misc, sza=sza, szb=szb)


def _sample_rows(a, db, ds):
    a = a.reshape(db, ds, KV_HEADS, GROUP, HEAD_DIM).transpose(0, 2, 3, 1, 4)
    a = jnp.pad(a, ((0, 0), (0, 0), (0, 0), (0, SAMPLE_Q - ds), (0, 0)))
    return a.reshape(db, KV_HEADS, QROWS, HEAD_DIM)


def _unsample_rows(o, db, ds):
    o = o.reshape(db, KV_HEADS, GROUP, SAMPLE_Q, HEAD_DIM)[:, :, :, :ds]
    return o.transpose(0, 3, 1, 2, 4).reshape(db * ds, WIDTH)


def _pad_new(a, db, ds):
    return jnp.pad(a.reshape(db, ds, a.shape[-1]), ((0, 0), (0, LANES - ds), (0, 0)))


def kernel(x_prompt, x_sample, cache_a_kv, cache_a_idx, cache_cmp_kv, cache_slc_kv, state_win_kv,
           page_table, norm_gain, w_in, q_norm_a, k_norm_a, q_norm_b, k_norm_cmp, k_norm_slc,
           k_norm_win, pe_cmp, w1_cmp, w2_cmp, w_proj_a, w_proj_b, w_out):
    b, t, _ = x_prompt.shape
    db, ds, _ = x_sample.shape
    n_pages = page_table.shape[1]
    past = n_pages * PAGE_SIZE
    assert ds <= SAMPLE_Q and ds <= CMP_STRIDE

    sizes = (WIDTH, KV_ROW, IDX_COLS, IDX_DIM, IDX_HEADS, WIDTH, WIDTH, 3 * KV_ROW, 3 * N_HEADS, WIDTH,
             2 * D_MODEL)
    offs = [0]
    for s in sizes:
        offs.append(offs[-1] + s)
    wb = w_in.astype(BF16)
    seg = lambda j: wb[:, offs[j]:offs[j + 1]]
    misc_pad = jnp.zeros((D_MODEL, LANES - (IDX_DIM + IDX_HEADS + 3 * N_HEADS)), BF16)
    wts = dict(
        norm_gain=norm_gain, q_norm_a=q_norm_a, q_norm_b=q_norm_b, k_norm_a=k_norm_a,
        k_norm_slc=k_norm_slc, k_norm_win=k_norm_win,
        w_qa=seg(0), w_qb=seg(6),
        w_kv=jnp.concatenate([seg(1), seg(7)], axis=1),
        w_idx=jnp.concatenate([seg(2), seg(3), seg(4), seg(8), misc_pad], axis=1),
        w_z=jnp.concatenate([seg(5), seg(9)], axis=1),
    )
    w_g0 = wb[:, offs[10]:offs[10] + D_MODEL]
    w_g1 = wb[:, offs[10] + D_MODEL:offs[11]]
    w_pa, w_pb, w_o = w_proj_a.astype(BF16), w_proj_b.astype(BF16), w_out.astype(BF16)
    w1_fs = jnp.concatenate([w1_cmp[:, :CMP_STRIDE].reshape(2, CMP_STRIDE * HEAD_DIM, CMP_HIDDEN),
                             w1_cmp[:, CMP_STRIDE:].reshape(2, CMP_STRIDE * HEAD_DIM, CMP_HIDDEN)],
                            axis=2).astype(BF16)
    w1_r = w1_cmp.reshape(2, CMP_LEN * HEAD_DIM, CMP_HIDDEN).astype(BF16)
    pe_r = pe_cmp.reshape(2, CMP_LEN * HEAD_DIM)
    w2_b = w2_cmp.astype(BF16)

    tr = min(512, t)
    tq = min(256, t)
    tk = min(512, t)
    xp = x_prompt.reshape(b * t, D_MODEL)
    pp = _project(xp, jnp.arange(t, dtype=jnp.int32), tr, wts)

    pages_p = t // PAGE_SIZE
    table_p = jnp.arange(b * pages_p, dtype=jnp.int32).reshape(b, pages_p)
    fs_p = _chunk_terms_paged(pp["cmp"], table_p, w1_fs, min(CHUNK_PAGES_PER_STEP, pages_p))
    kc_p, vc_p = _compress_mlp(fs_p, jnp.zeros((b, SUBLANES, FS_COLS), F32), pe_r, w1_r, w2_b, k_norm_cmp)

    ik_b = pp["misc"][:, :IDX_DIM].astype(BF16)
    ik2 = jnp.concatenate([ik_b, ik_b], axis=1).reshape(b, t, LANES)
    iw_t = pp["misc"][:, MISC_IK:MISC_IW].T
    a_p = _dsa_prompt(pp["iq"], iw_t, ik2, pp["qa"], pp["akv_b"].reshape(b, t, KV_ROW), pp["sza"],
                      b, t, tq, tk)
    win_front = jnp.pad(pp["win_b"].reshape(b, t, KV_ROW), ((0, 0), (WINDOW, 0), (0, 0)))
    b_p = _nsa_prompt(pp["qb"], pp["qb_rot"], kc_p, vc_p, pp["slc_b"].reshape(b, t, KV_ROW),
                      win_front, pp["misc"], pp["szb"], b, t, tq, tk)
    merged_p = _merge(a_p, b_p, pp["h"], w_pa, w_pb, w_g0, w_g1, tr, 1024)
    y_prompt = _out_proj(xp, merged_p, w_o, tr).reshape(b, t, D_MODEL)

    kv5 = lambda a, n0, n1: a.reshape(n0, n1, KV_HEADS, 2, HEAD_DIM)
    p_a_kv = kv5(pp["akv"], b, t)
    p_a_idx = pp["misc"][:, :IDX_DIM].reshape(b, t, IDX_DIM)
    p_cmp_kv = kv5(pp["cmp"], b, t)
    p_slc_kv = kv5(pp["slc"], b, t)
    p_win_kv = kv5(pp["win"], b, t)[:, t - min(WINDOW, t):]

    ns = db * ds
    xs = x_sample.reshape(ns, D_MODEL)
    pos_s = past + jnp.tile(jnp.arange(ds, dtype=jnp.int32), db)
    ps = _project(xs, pos_s, ns, wts)
    pg = min(PAGES_PER_STEP, n_pages)
    ng = n_pages // pg
    cw = pg * PAGE_SIZE

    iq_s = ps["iq"].reshape(db, ds, IDX_HEADS, IDX_DIM).transpose(0, 2, 1, 3)
    iqm = jnp.pad(iq_s, ((0, 0), (0, 0), (0, SAMPLE_Q - ds), (0, 0))).reshape(db, IDX_HEADS * SAMPLE_Q, IDX_DIM)
    iw_s = ps["misc"][:, MISC_IK:MISC_IW].reshape(db, ds, IDX_HEADS).transpose(0, 2, 1) * (IDX_DIM ** -0.5)
    iwm = jnp.pad(iw_s, ((0, 0), (0, 0), (0, SAMPLE_Q - ds))).reshape(db, IDX_HEADS * SAMPLE_Q, 1)
    iwm = jnp.broadcast_to(iwm, (db, IDX_HEADS * SAMPLE_Q, LANES))
    ik_new_t = jnp.swapaxes(_pad_new(ps["misc"][:, :IDX_DIM].astype(BF16), db, ds), 1, 2)
    sc_s, sc_new = _idx_scores(jnp.swapaxes(cache_a_idx, 1, 2), page_table, iqm, iwm, ik_new_t, pg, ds)
    mk_a, mk_a_new = _dsa_topk(sc_s, sc_new, past, ds)
    o_a_s = _paged_attn(cache_a_kv, page_table, _sample_rows(ps["qa"], db, ds), mk_a, mk_a_new,
                        _pad_new(ps["akv_b"], db, ds), pg)

    fs_past = _chunk_terms_paged(cache_cmp_kv, page_table, w1_fs, min(CHUNK_PAGES_PER_STEP, n_pages))
    new_chunk = jnp.pad(ps["cmp"].reshape(db, ds, KV_ROW), ((0, 0), (0, CMP_STRIDE - ds), (0, 0)))
    fs_new = _chunk_terms_dense(new_chunk.reshape(db, CHUNK_FLAT), w1_fs, db)
    fs_new = jnp.pad(fs_new.reshape(db, 1, FS_COLS), ((0, 0), (0, SUBLANES - 1), (0, 0)))
    kc_s, vc_s = _compress_mlp(fs_past, fs_new, pe_r, w1_r, w2_b, k_norm_cmp)
    o_cmp_s, sel_s, mk_s_new = _nsa_sample(_sample_rows(ps["qb"], db, ds), kc_s, vc_s, past, ds, ng, cw)
    q_rot_s = _sample_rows(ps["qb_rot"], db, ds)
    n_sel_s = min(N_SEL, -(-(past + ds) // SLC_BLOCK))
    sel_s = sel_s[:, :n_sel_s].reshape(db, KV_HEADS, SAMPLE_Q, n_sel_s)[:, :, :ds]
    sel_s = sel_s.transpose(0, 2, 1, 3).reshape(-1)
    o_slc_s = _slc_gather(cache_slc_kv, page_table, sel_s, q_rot_s, mk_s_new, _pad_new(ps["slc_b"], db, ds),
                          n_sel_s, ds)
    o_win_s = _win_sample(q_rot_s, state_win_kv, _pad_new(ps["win_b"], db, ds), ds)

    a_s, b_s = _gate_merge(_unsample_rows(o_a_s, db, ds), _unsample_rows(o_cmp_s, db, ds),
                           _unsample_rows(o_slc_s, db, ds), _unsample_rows(o_win_s, db, ds),
                           ps["misc"], ps["sza"], ps["szb"])
    merged_s = _merge(a_s, b_s, ps["h"], w_pa, w_pb, w_g0, w_g1, ns, 1024)
    y_sample = _out_proj(xs, merged_s, w_o, ns).reshape(db, ds, D_MODEL)

    s_win = kv5(ps["win"], db, ds)
    w_buf = state_win_kv.shape[1]
    s_win_kv = jnp.concatenate([state_win_kv, s_win], axis=1)[:, ds:ds + w_buf]

    return (y_prompt, y_sample, p_a_kv, p_a_idx, p_cmp_kv, p_slc_kv, p_win_kv,
            kv5(ps["akv"], db, ds), ps["misc"][:, :IDX_DIM].reshape(db, ds, IDX_DIM),
            kv5(ps["cmp"], db, ds), kv5(ps["slc"], db, ds), s_win_kv)
```

```python
import functools

import jax
import jax.numpy as jnp
from jax import lax
from jax.experimental import pallas as pl
from jax.experimental.pallas import tpu as pltpu

D_MODEL = 2048
HEAD_DIM = 128
ROT_DIM = HEAD_DIM // 4
N_HEADS = D_MODEL // (2 * HEAD_DIM)
KV_HEADS = 2
GROUP = N_HEADS // KV_HEADS
IDX_HEADS = 8
IDX_DIM = 64
IDX_ROT_DIM = IDX_DIM // 4
A_TOPK = 256
CMP_LEN = 32
CMP_STRIDE = 16
CMP_HIDDEN = 128
SLC_BLOCK = 64
N_SEL = 16
WINDOW = 512
PAGE_SIZE = 128
ROPE_THETA = 500000.0
EPS = 1e-6
NEG = -1e30
WIDTH = N_HEADS * HEAD_DIM
KV_ROW = KV_HEADS * 2 * HEAD_DIM
SCALE = HEAD_DIM ** -0.5

LANES = 128
SUBLANES = 8
VMEM_LIMIT = 48 * 1024 * 1024
CHUNK_VMEM_LIMIT = 56 * 1024 * 1024
PAGES_PER_STEP = 64
CHUNK_PAGES_PER_STEP = 64
SAMPLE_Q = 8

F32 = jnp.float32
BF16 = jnp.bfloat16
NT_DIMS = (((1,), (1,)), ((), ()))


def _params(n_axes, vmem_limit=VMEM_LIMIT):
    return pltpu.CompilerParams(dimension_semantics=("arbitrary",) * n_axes,
                                vmem_limit_bytes=vmem_limit)


def _dot(a, b):
    return jnp.dot(a, b, preferred_element_type=F32)


def _dot_nt(a, b):
    return lax.dot_general(a, b, NT_DIMS, preferred_element_type=F32)


def _rope_tables(pos, rot, period, width=LANES):
    half = rot // 2
    inv = ROPE_THETA ** (-jnp.arange(half, dtype=F32) / half)
    ang = pos.astype(F32)[:, None] * inv
    cos, sin = jnp.cos(ang), jnp.sin(ang)
    n = pos.shape[0]
    zeros = lambda w: jnp.zeros((n, w), F32)
    ones = lambda w: jnp.ones((n, w), F32)
    c = jnp.concatenate([cos, cos, ones(period - rot)], axis=1)
    s1 = jnp.concatenate([-sin, zeros(period - half)], axis=1)
    s2 = jnp.concatenate([zeros(half), sin, zeros(period - rot)], axis=1)
    return c, s1, s2


def _apply_rope(y, c, s1, s2, half):
    return y * c + pltpu.roll(y, LANES - half, 1) * s1 + pltpu.roll(y, half, 1) * s2


def _head_norm(z, g):
    return z * lax.rsqrt(jnp.mean(z * z, axis=-1, keepdims=True) + EPS) * g


def _rmsnorm_kernel(x_ref, g_ref, o_ref):
    x = x_ref[...]
    y = x * lax.rsqrt(jnp.mean(x * x, axis=-1, keepdims=True) + EPS)
    o_ref[...] = (y * g_ref[...]).astype(o_ref.dtype)


def _rmsnorm(x, gain, tr):
    n, d = x.shape
    return pl.pallas_call(
        _rmsnorm_kernel,
        grid=(n // tr,),
        in_specs=[pl.BlockSpec((tr, d), lambda i: (i, 0)), pl.BlockSpec((1, d), lambda i: (0, 0))],
        out_specs=pl.BlockSpec((tr, d), lambda i: (i, 0)),
        out_shape=jax.ShapeDtypeStruct((n, d), BF16),
        compiler_params=_params(1),
        name="rmsnorm",
    )(x, gain.reshape(1, d))


PROJ_ROWS = 128


def _proj_q_kernel(h_ref, w_ref, g_ref, c_ref, s1_ref, s2_ref, rot_ref, *plain_ref):
    g = g_ref[...]
    n = h_ref.shape[0]
    nh = max(1, n // PROJ_ROWS)
    for r in range(nh):
        rs = slice(r * (n // nh), (r + 1) * (n // nh))
        z = _dot(h_ref[rs, :], w_ref[...])
        c, s1, s2 = c_ref[rs, :], s1_ref[rs, :], s2_ref[rs, :]
        for hd in range(N_HEADS):
            sl = slice(hd * HEAD_DIM, (hd + 1) * HEAD_DIM)
            y = _head_norm(z[:, sl], g)
            if plain_ref:
                plain_ref[0][rs, sl] = y.astype(BF16)
            rot_ref[rs, sl] = _apply_rope(y, c, s1, s2, ROT_DIM // 2).astype(BF16)


def _proj_q(h, w, gain, tabs, tr, n_tab_blocks, want_plain):
    n = h.shape[0]
    row = lambda i: (i, 0)
    fix = lambda i: (0, 0)
    tab = lambda i: (i % n_tab_blocks, 0)
    out_shape = [jax.ShapeDtypeStruct((n, WIDTH), BF16)] * (2 if want_plain else 1)
    out_specs = [pl.BlockSpec((tr, WIDTH), row)] * (2 if want_plain else 1)
    return pl.pallas_call(
        _proj_q_kernel,
        grid=(n // tr,),
        in_specs=[pl.BlockSpec((tr, D_MODEL), row), pl.BlockSpec((D_MODEL, WIDTH), fix),
                  pl.BlockSpec((1, HEAD_DIM), fix)] + [pl.BlockSpec((tr, LANES), tab)] * 3,
        out_specs=out_specs,
        out_shape=out_shape,
        compiler_params=_params(1),
        name="proj_q",
    )(h, w, gain.reshape(1, HEAD_DIM), *tabs)


def _proj_kv_kernel(h_ref, w_ref, ga_ref, gs_ref, gw_ref, c_ref, s1_ref, s2_ref,
                    akv_ref, cmp_ref, slc_ref, win_ref, akv_b_ref, slc_b_ref, win_b_ref):
    n = h_ref.shape[0]
    nh = max(1, n // PROJ_ROWS)
    nr = n // nh
    for r in range(nh):
        rs = slice(r * nr, (r + 1) * nr)
        z = _dot(h_ref[rs, :], w_ref[...])
        c, s1, s2 = c_ref[rs, :], s1_ref[rs, :], s2_ref[rs, :]

        def emit(base, gain_ref, f32_ref, bf_ref):
            for j in range(KV_HEADS * 2):
                sl = slice(j * HEAD_DIM, (j + 1) * HEAD_DIM)
                v = z[:, base + j * HEAD_DIM: base + (j + 1) * HEAD_DIM]
                if gain_ref is not None and j % 2 == 0:
                    v = _apply_rope(_head_norm(v, gain_ref[...]), c, s1, s2, ROT_DIM // 2)
                f32_ref[pl.ds(r * nr * KV_HEADS * 2 + j, nr, stride=KV_HEADS * 2), :] = v
                if bf_ref is not None:
                    bf_ref[rs, sl] = v.astype(BF16)

        emit(0 * KV_ROW, ga_ref, akv_ref, akv_b_ref)
        emit(1 * KV_ROW, None, cmp_ref, None)
        emit(2 * KV_ROW, gs_ref, slc_ref, slc_b_ref)
        emit(3 * KV_ROW, gw_ref, win_ref, win_b_ref)


def _proj_kv(h, w, g_a, g_slc, g_win, tabs, tr, n_tab_blocks):
    n = h.shape[0]
    row = lambda i: (i, 0)
    fix = lambda i: (0, 0)
    tab = lambda i: (i % n_tab_blocks, 0)
    f32_out = jax.ShapeDtypeStruct((n * KV_HEADS * 2, HEAD_DIM), F32)
    bf_out = jax.ShapeDtypeStruct((n, KV_ROW), BF16)
    return pl.pallas_call(
        _proj_kv_kernel,
        grid=(n // tr,),
        in_specs=[pl.BlockSpec((tr, D_MODEL), row), pl.BlockSpec((D_MODEL, 4 * KV_ROW), fix)]
                 + [pl.BlockSpec((1, HEAD_DIM), fix)] * 3 + [pl.BlockSpec((tr, LANES), tab)] * 3,
        out_specs=[pl.BlockSpec((tr * KV_HEADS * 2, HEAD_DIM), row)] * 4 + [pl.BlockSpec((tr, KV_ROW), row)] * 3,
        out_shape=[f32_out] * 4 + [bf_out] * 3,
        compiler_params=_params(1),
        name="proj_kv",
    )(h, w, g_a.reshape(1, HEAD_DIM), g_slc.reshape(1, HEAD_DIM), g_win.reshape(1, HEAD_DIM), *tabs)


IDX_COLS = IDX_HEADS * IDX_DIM
MISC_IK = IDX_DIM
MISC_IW = MISC_IK + IDX_HEADS
MISC_G = MISC_IW + 3 * N_HEADS


def _proj_idx_kernel(h_ref, w_ref, c_ref, s1_ref, s2_ref, cm_ref, s1m_ref, s2m_ref, iq_ref, misc_ref):
    z = _dot(h_ref[...], w_ref[...])
    half = IDX_ROT_DIM // 2
    c, s1, s2 = c_ref[...], s1_ref[...], s2_ref[...]
    for j in range(IDX_COLS // LANES):
        sl = slice(j * LANES, (j + 1) * LANES)
        iq_ref[:, sl] = _apply_rope(z[:, sl], c, s1, s2, half).astype(BF16)
    m = z[:, IDX_COLS:IDX_COLS + LANES]
    roped = _apply_rope(m, cm_ref[...], s1m_ref[...], s2m_ref[...], half)
    lane = lax.broadcasted_iota(jnp.int32, m.shape, 1)
    misc_ref[...] = jnp.where(lane < MISC_IK, roped,
                              jnp.where(lane < MISC_IW, m * (IDX_HEADS ** -0.5),
                                        jnp.where(lane < MISC_G, jax.nn.sigmoid(m), 0.0)))


def _proj_idx(h, w, tabs_iq, tabs_misc, tr, n_tab_blocks):
    n = h.shape[0]
    row = lambda i: (i, 0)
    fix = lambda i: (0, 0)
    tab = lambda i: (i % n_tab_blocks, 0)
    return pl.pallas_call(
        _proj_idx_kernel,
        grid=(n // tr,),
        in_specs=[pl.BlockSpec((tr, D_MODEL), row), pl.BlockSpec((D_MODEL, IDX_COLS + LANES), fix)]
                 + [pl.BlockSpec((tr, LANES), tab)] * 6,
        out_specs=[pl.BlockSpec((tr, IDX_COLS), row), pl.BlockSpec((tr, LANES), row)],
        out_shape=[jax.ShapeDtypeStruct((n, IDX_COLS), BF16), jax.ShapeDtypeStruct((n, LANES), F32)],
        compiler_params=_params(1),
        name="proj_idx",
    )(h, w, *tabs_iq, *tabs_misc)


def _proj_z_kernel(h_ref, w_ref, za_ref, zb_ref):
    n = h_ref.shape[0]
    nh = max(1, n // PROJ_ROWS)
    for r in range(nh):
        rs = slice(r * (n // nh), (r + 1) * (n // nh))
        z = _dot(h_ref[rs, :], w_ref[...])
        s = z * jax.nn.sigmoid(z)
        za_ref[rs, :] = s[:, :WIDTH]
        zb_ref[rs, :] = s[:, WIDTH:]


def _proj_z(h, w, tr):
    n = h.shape[0]
    row = lambda i: (i, 0)
    out = jax.ShapeDtypeStruct((n, WIDTH), F32)
    return pl.pallas_call(
        _proj_z_kernel,
        grid=(n // tr,),
        in_specs=[pl.BlockSpec((tr, D_MODEL), row), pl.BlockSpec((D_MODEL, 2 * WIDTH), lambda i: (0, 0))],
        out_specs=[pl.BlockSpec((tr, WIDTH), row)] * 2,
        out_shape=[out, out],
        compiler_params=_params(1),
        name="proj_z",
    )(h, w)


def _key_to_float(u):
    int_min = jnp.int32(-2 ** 31)
    bits = jnp.where(u < 0, u ^ int_min, ~u)
    return lax.bitcast_convert_type(bits, F32)


def _lane_fold(w):
    acc = w[:, :LANES]
    for j in range(1, w.shape[1] // LANES):
        acc = acc + w[:, j * LANES:(j + 1) * LANES]
    return acc


COUNT_ROWS = 1024


def _topk_threshold(load, nck, k, idx_bits, rows, cw, tail=None):
    kf = jnp.float32(k)
    lane = lax.broadcasted_iota(jnp.int32, (1, cw), 1)
    nr = min(rows, COUNT_ROWS)

    def count(pred_value):
        out = []
        for r0 in range(0, rows, nr):
            rs = slice(r0, r0 + nr)

            def body(c, acc, r0=r0, rs=rs):
                return acc + _lane_fold(pred_value(load(c, r0, nr), c * cw + lane, rs))
            acc = lax.fori_loop(0, nck, body, jnp.zeros((nr, LANES), F32))
            if tail is not None:
                acc = acc + pred_value(tail[0][rs], tail[1], rs)
            out.append(jnp.sum(acc, axis=1, keepdims=True))
        return out[0] if len(out) == 1 else jnp.concatenate(out, axis=0)

    def value_bit(it, u):
        cand = u | jnp.left_shift(jnp.int32(1), 31 - it)
        candf = _key_to_float(cand)
        cnt = count(lambda x, idx, rs: jnp.where(x >= candf[rs], 1.0, 0.0))
        return jnp.where(cnt >= kf, cand, u)

    u = lax.fori_loop(0, 32, value_bit, jnp.zeros((rows, 1), jnp.int32))
    thr = jnp.where((u >= 0) & (u < 2 ** 23), -jnp.inf, _key_to_float(u))
    need = kf - count(lambda x, idx, rs: jnp.where(x > thr[rs], 1.0, 0.0))

    def index_bit(it, v):
        cand = v | jnp.left_shift(jnp.int32(1), idx_bits - 1 - it)
        below = count(lambda x, idx, rs: jnp.where(x == thr[rs], jnp.where(idx < cand[rs], 1.0, 0.0), 0.0))
        return jnp.where(below < need, cand, v)

    tied = jnp.max(count(lambda x, idx, rs: jnp.where(x >= thr[rs], 1.0, 0.0))) > kf
    v = lax.cond(tied,
                 lambda: lax.fori_loop(0, idx_bits, index_bit, jnp.zeros((rows, 1), jnp.int32)),
                 lambda: jnp.full((rows, 1), (1 << idx_bits) - 1, jnp.int32))
    return thr, jnp.where(need >= 1.0, v, -1)


def _topk_mask(x, idx, thr, vmax):
    gt = jnp.where(x > thr, 1.0, 0.0)
    eq = jnp.where(x == thr, jnp.where(idx <= vmax, 1.0, 0.0), 0.0)
    return jnp.maximum(gt, eq)


def _sublane_fold(w):
    parts = [w[j * SUBLANES:(j + 1) * SUBLANES] for j in range(w.shape[0] // SUBLANES)]
    while len(parts) > 1:
        parts = [parts[j] + parts[j + 1] if j + 1 < len(parts) else parts[j] for j in range(0, len(parts), 2)]
    return parts[0]


def _topk_threshold_cols(load, nck, k, idx_bits, cols, ck):
    kf = jnp.float32(k)
    sub = lax.broadcasted_iota(jnp.int32, (ck, 1), 0)

    def count(pred_value):
        def body(c, acc):
            return acc + _sublane_fold(pred_value(load(c), c * ck + sub))
        acc = lax.fori_loop(0, nck, body, jnp.zeros((SUBLANES, cols), F32))
        return jnp.sum(acc, axis=0, keepdims=True)

    def value_bit(it, u):
        cand = u | jnp.left_shift(jnp.int32(1), 31 - it)
        candf = _key_to_float(cand)
        cnt = count(lambda x, idx: jnp.where(x >= candf, 1.0, 0.0))
        return jnp.where(cnt >= kf, cand, u)

    u = lax.fori_loop(0, 32, value_bit, jnp.zeros((1, cols), jnp.int32))
    thr = jnp.where((u >= 0) & (u < 2 ** 23), -jnp.inf, _key_to_float(u))
    need = kf - count(lambda x, idx: jnp.where(x > thr, 1.0, 0.0))

    def index_bit(it, v):
        cand = v | jnp.left_shift(jnp.int32(1), idx_bits - 1 - it)
        below = count(lambda x, idx: jnp.where(x == thr, jnp.where(idx < cand, 1.0, 0.0), 0.0))
        return jnp.where(below < need, cand, v)

    tied = jnp.max(count(lambda x, idx: jnp.where(x >= thr, 1.0, 0.0))) > kf
    v = lax.cond(tied,
                 lambda: lax.fori_loop(0, idx_bits, index_bit, jnp.zeros((1, cols), jnp.int32)),
                 lambda: jnp.full((1, cols), (1 << idx_bits) - 1, jnp.int32))
    return thr, jnp.where(need >= 1.0, v, -1)


LOG2E = 1.4426950408889634
SCORE_SCALE = SCALE * LOG2E


def _softmax_init(rows):
    return (jnp.full((rows, 1), NEG, F32), jnp.zeros((rows, 1), F32), jnp.zeros((rows, HEAD_DIM), F32))


def _softmax_step(carry, q, k, v, bias):
    m, l, acc = carry
    t = _dot_nt(q, k) * SCORE_SCALE + bias
    m_new = jnp.maximum(m, jnp.max(t, axis=1, keepdims=True))
    alpha = jnp.exp2(m - m_new)
    e = jnp.exp2(t - m_new)
    l = alpha * l + jnp.sum(e, axis=1, keepdims=True)
    acc = alpha * acc + _dot(e.astype(BF16), v)
    return m_new, l, acc


def _softmax_done(carry):
    _, l, acc = carry
    return acc / jnp.maximum(l, 1e-30)


def _mask_bias(maskf):
    return jnp.where(maskf > 0.0, 0.0, NEG)


def _stack_heads(ref, kvh):
    return jnp.concatenate([ref[:, (kvh * GROUP + g) * HEAD_DIM:(kvh * GROUP + g + 1) * HEAD_DIM]
                            for g in range(GROUP)], axis=0)


def _kv_cols(kvh):
    return (slice(kvh * 2 * HEAD_DIM, (kvh * 2 + 1) * HEAD_DIM),
            slice((kvh * 2 + 1) * HEAD_DIM, (kvh * 2 + 2) * HEAD_DIM))


def _dsa_prompt_kernel(iq_ref, iwt_ref, ik2_ref, qa_ref, akv_ref, sza_ref, o_ref, sc_ref, mk_ref,
                       *, tq, tk, k_top, idx_bits):
    i = pl.program_id(1)
    nck = ((i + 1) * tq + tk - 1) // tk
    lane = lax.broadcasted_iota(jnp.int32, (1, LANES), 1)
    qpos = i * tq + lax.broadcasted_iota(jnp.int32, (1, tq), 1)
    krow = lax.broadcasted_iota(jnp.int32, (tk, 1), 0)
    iwt = iwt_ref[...] * (IDX_DIM ** -0.5)

    iqs = []
    for hd in range(IDX_HEADS):
        pair = iq_ref[:, (hd // 2) * LANES:(hd // 2 + 1) * LANES]
        keep = (lane < IDX_DIM) if hd % 2 == 0 else (lane >= IDX_DIM)
        iqs.append(jnp.where(keep, pair, jnp.zeros_like(pair)))

    def score_chunk(c, carry):
        k0 = pl.multiple_of(c * tk, tk)
        ikc = ik2_ref[pl.ds(k0, tk), :]
        acc = jnp.zeros((tk, tq), F32)
        for hd in range(IDX_HEADS):
            acc = acc + jnp.maximum(_dot_nt(ikc, iqs[hd]), 0.0) * iwt[hd:hd + 1, :]
        sc_ref[c] = jnp.where(k0 + krow <= qpos, acc, NEG)
        return carry

    lax.fori_loop(0, nck, score_chunk, 0)

    thr, vmax = _topk_threshold_cols(lambda c: sc_ref[c], nck, k_top, idx_bits, tq, tk)

    def mask_chunk(c, carry):
        idx = c * tk + krow
        sel = _topk_mask(sc_ref[c], idx, thr, vmax) * jnp.where(idx <= qpos, 1.0, 0.0)
        mk_ref[c] = _mask_bias(sel).T
        return carry

    lax.fori_loop(0, nck, mask_chunk, 0)

    qs = [_stack_heads(qa_ref, kvh) for kvh in range(KV_HEADS)]

    def attend(c, carries):
        k0 = pl.multiple_of(c * tk, tk)
        bias = jnp.concatenate([mk_ref[c]] * GROUP, axis=0)
        out = []
        for kvh in range(KV_HEADS):
            kcol, vcol = _kv_cols(kvh)
            out.append(_softmax_step(carries[kvh], qs[kvh], akv_ref[pl.ds(k0, tk), kcol],
                                     akv_ref[pl.ds(k0, tk), vcol], bias))
        return tuple(out)

    carries = lax.fori_loop(0, nck, attend, tuple(_softmax_init(GROUP * tq) for _ in range(KV_HEADS)))
    for kvh in range(KV_HEADS):
        o = _softmax_done(carries[kvh])
        for g in range(GROUP):
            sl = slice((kvh * GROUP + g) * HEAD_DIM, (kvh * GROUP + g + 1) * HEAD_DIM)
            o_ref[:, sl] = (o[g * tq:(g + 1) * tq] * sza_ref[:, sl]).astype(BF16)


def _dsa_prompt(iq, iw_t, ik2, qa, akv_b, sza, b, t, tq, tk):
    nqb = t // tq
    k_top = min(A_TOPK, t // 4)
    idx_bits = max(1, (t - 1).bit_length())
    row = lambda bi, i: (bi * nqb + i, 0)
    whole = lambda bi, i: (bi, 0, 0)
    kern = functools.partial(_dsa_prompt_kernel, tq=tq, tk=tk, k_top=k_top, idx_bits=idx_bits)
    return pl.pallas_call(
        kern,
        grid=(b, nqb),
        in_specs=[pl.BlockSpec((tq, IDX_COLS), row),
                  pl.BlockSpec((IDX_HEADS, tq), lambda bi, i: (0, bi * nqb + i)),
                  pl.BlockSpec((None, t, LANES), whole), pl.BlockSpec((tq, WIDTH), row),
                  pl.BlockSpec((None, t, KV_ROW), whole), pl.BlockSpec((tq, WIDTH), row)],
        out_specs=pl.BlockSpec((tq, WIDTH), row),
        out_shape=jax.ShapeDtypeStruct((b * t, WIDTH), BF16),
        scratch_shapes=[pltpu.VMEM((t // tk, tk, tq), F32), pltpu.VMEM((t // tk, tq, tk), F32)],
        compiler_params=_params(2),
        name="dsa_prompt",
    )(iq, iw_t, ik2, qa, akv_b, sza)


SLC_SHIFT = SLC_BLOCK.bit_length() - 1
assert 1 << SLC_SHIFT == SLC_BLOCK


def _block_rules(imp, blk, qpos):
    cur = lax.shift_right_logical(qpos, SLC_SHIFT)
    return jnp.where(blk == cur, -NEG, jnp.where(blk * SLC_BLOCK > qpos, NEG, imp))


def _split_dot(p, m):
    hi = p.astype(BF16)
    lo = (p - hi.astype(F32)).astype(BF16)
    return _dot(hi, m) + _dot(lo, m)


def _nsa_prompt_kernel(qb_ref, qr_ref, kc_ref, vc_ref, slc_ref, win_ref, ovt_ref, misc_ref, szb_ref,
                       o_ref, sc_ref, *, tq, tk, n_cmp, n_blk, n_sel):
    i = pl.program_id(1)
    nck = ((i + 1) * tq + tk - 1) // tk
    qpos = i * tq + lax.broadcasted_iota(jnp.int32, (tq, 1), 0)
    kiota = lax.broadcasted_iota(jnp.int32, (1, tk), 1)
    misc = misc_ref[...]
    ncp = kc_ref.shape[0]
    cidx = lax.broadcasted_iota(jnp.int32, (1, ncp), 1)
    cmask = jnp.where((CMP_STRIDE * cidx + CMP_LEN - 1 <= qpos) & (cidx < n_cmp), 1.0, 0.0)
    cmask = jnp.concatenate([cmask] * GROUP, axis=0)
    blk = lax.broadcasted_iota(jnp.int32, (LANES, 1), 0)
    qlane = i * tq + lax.broadcasted_iota(jnp.int32, (1, tq), 1)

    o_cmp = []
    for kvh in range(KV_HEADS):
        kc = kc_ref[:, kvh * HEAD_DIM:(kvh + 1) * HEAD_DIM]
        vc = vc_ref[:, kvh * HEAD_DIM:(kvh + 1) * HEAD_DIM]
        s = _dot_nt(_stack_heads(qb_ref, kvh), kc) * SCALE
        sm = jnp.where(cmask > 0.0, s, NEG)
        e = jnp.exp(sm - jnp.max(sm, axis=1, keepdims=True)) * cmask
        p = e / jnp.maximum(jnp.sum(e, axis=1, keepdims=True), 1e-30)
        o_cmp.append(_dot(p.astype(BF16), vc))
        psum = p[0:tq]
        for g in range(1, GROUP):
            psum = psum + p[g * tq:(g + 1) * tq]
        hi = psum.astype(BF16)
        lo = (psum - hi.astype(F32)).astype(BF16)
        imp = _dot_nt(ovt_ref[...], hi) + _dot_nt(ovt_ref[...], lo)
        imp = _block_rules(imp, blk, qlane)
        sc_ref[:, kvh * tq:(kvh + 1) * tq] = jnp.where(blk < n_blk, imp, -jnp.inf)

    thr, vmax = _topk_threshold_cols(lambda c: sc_ref[...], 1, n_sel, 7, KV_HEADS * tq, LANES)
    bmask_t = _topk_mask(sc_ref[...], blk, thr, vmax)
    bmask = [bmask_t[:, kvh * tq:(kvh + 1) * tq].T.astype(BF16) for kvh in range(KV_HEADS)]

    riota = lax.broadcasted_iota(jnp.int32, (LANES, 1), 0)
    nwk = WINDOW + tq
    w0 = pl.multiple_of(i * tq, tq)
    kposw = i * tq - WINDOW + lax.broadcasted_iota(jnp.int32, (1, nwk), 1)
    dw = qpos - kposw
    bias_w = jnp.where((dw >= 0) & (dw < WINDOW) & (kposw >= 0), 0.0, NEG)
    bias_w = jnp.concatenate([bias_w] * GROUP, axis=0)

    qs = [_stack_heads(qr_ref, kvh) for kvh in range(KV_HEADS)]

    def attend(c, carries):
        k0 = pl.multiple_of(c * tk, tk)
        kpos = k0 + kiota
        expand = jnp.where(riota == lax.shift_right_logical(kpos, SLC_SHIFT), 1.0, 0.0).astype(BF16)
        causal = jnp.where(kpos <= qpos, 1.0, 0.0)
        out = []
        for kvh in range(KV_HEADS):
            kcol, vcol = _kv_cols(kvh)
            sel = _dot(bmask[kvh], expand) * causal
            bias = jnp.concatenate([_mask_bias(sel)] * GROUP, axis=0)
            out.append(_softmax_step(carries[kvh], qs[kvh], slc_ref[pl.ds(k0, tk), kcol],
                                     slc_ref[pl.ds(k0, tk), vcol], bias))
        return tuple(out)

    slc_carries = lax.fori_loop(0, nck, attend, tuple(_softmax_init(GROUP * tq) for _ in range(KV_HEADS)))

    for kvh in range(KV_HEADS):
        q = qs[kvh]
        kcol, vcol = _kv_cols(kvh)
        o_slc = _softmax_done(slc_carries[kvh])
        o_win = _softmax_done(_softmax_step(_softmax_init(GROUP * tq), q, win_ref[pl.ds(w0, nwk), kcol],
                                            win_ref[pl.ds(w0, nwk), vcol], bias_w))
        for g in range(GROUP):
            hd = kvh * GROUP + g
            sl = slice(hd * HEAD_DIM, (hd + 1) * HEAD_DIM)
            rs = slice(g * tq, (g + 1) * tq)
            gate = [misc[:, MISC_IW + j * N_HEADS + hd:MISC_IW + j * N_HEADS + hd + 1] for j in range(3)]
            o_b = gate[0] * o_cmp[kvh][rs] + gate[1] * o_slc[rs] + gate[2] * o_win[rs]
            o_ref[:, sl] = (o_b * szb_ref[:, sl]).astype(BF16)


def _overlap_matrix(n_cmp_pad, n_cols, col_block):
    i = jnp.arange(n_cmp_pad)[:, None]
    j = col_block[None, :]
    ov = (CMP_STRIDE * i < SLC_BLOCK * (j + 1)) & (CMP_STRIDE * i + CMP_LEN > SLC_BLOCK * j) & (j >= 0)
    return ov.astype(BF16)


def _nsa_prompt(qb, qr, kc, vc, slc_b, win_b, misc, szb, b, t, tq, tk):
    nqb = t // tq
    n_cmp = t // CMP_STRIDE - 1
    n_blk = t // SLC_BLOCK
    assert n_blk <= LANES
    ncp = kc.shape[1]
    ov = _overlap_matrix(ncp, LANES, jnp.where(jnp.arange(LANES) < n_blk, jnp.arange(LANES), -1))
    row = lambda bi, i: (bi * nqb + i, 0)
    whole = lambda bi, i: (bi, 0, 0)
    kern = functools.partial(_nsa_prompt_kernel, tq=tq, tk=tk, n_cmp=n_cmp, n_blk=n_blk,
                             n_sel=min(N_SEL, n_blk))
    return pl.pallas_call(
        kern,
        grid=(b, nqb),
        in_specs=[pl.BlockSpec((tq, WIDTH), row), pl.BlockSpec((tq, WIDTH), row),
                  pl.BlockSpec((None, ncp, KV_HEADS * HEAD_DIM), whole),
                  pl.BlockSpec((None, ncp, KV_HEADS * HEAD_DIM), whole),
                  pl.BlockSpec((None, t, KV_ROW), whole), pl.BlockSpec((None, t + WINDOW, KV_ROW), whole),
                  pl.BlockSpec((LANES, ncp), lambda bi, i: (0, 0)),
                  pl.BlockSpec((tq, LANES), row), pl.BlockSpec((tq, WIDTH), row)],
        out_specs=pl.BlockSpec((tq, WIDTH), row),
        out_shape=jax.ShapeDtypeStruct((b * t, WIDTH), BF16),
        scratch_shapes=[pltpu.VMEM((LANES, KV_HEADS * tq), F32)],
        compiler_params=_params(2),
        name="nsa_prompt",
    )(qb, qr, kc, vc, slc_b, win_b, ov.T, misc, szb)


CHUNK_FLAT = CMP_STRIDE * KV_ROW
FS_COLS = KV_HEADS * 2 * 2 * CMP_HIDDEN


def _chunk_terms_body(x, w_ref, o_ref):
    for kc in range(KV_HEADS * 2):
        xk = jnp.concatenate(
            [x[:, p * KV_ROW + kc * HEAD_DIM: p * KV_ROW + (kc + 1) * HEAD_DIM] for p in range(CMP_STRIDE)],
            axis=1).astype(BF16)
        o_ref[:, kc * 2 * CMP_HIDDEN:(kc + 1) * 2 * CMP_HIDDEN] = _dot(xk, w_ref[kc % 2])


def _chunk_terms_dense_kernel(x_ref, w_ref, o_ref):
    _chunk_terms_body(x_ref[...], w_ref, o_ref)


def _chunk_terms_dense(x, w, tr):
    n = x.shape[0]
    return pl.pallas_call(
        _chunk_terms_dense_kernel,
        grid=(n // tr,),
        in_specs=[pl.BlockSpec((tr, CHUNK_FLAT), lambda i: (i, 0)),
                  pl.BlockSpec((2, CMP_STRIDE * HEAD_DIM, 2 * CMP_HIDDEN), lambda i: (0, 0, 0))],
        out_specs=pl.BlockSpec((tr, FS_COLS), lambda i: (i, 0)),
        out_shape=jax.ShapeDtypeStruct((n, FS_COLS), F32),
        compiler_params=_params(1),
        name="chunk_terms_dense",
    )(x, w)


ROWS_PER_TOKEN = KV_HEADS * 2
PAGE_ROWS = PAGE_SIZE * ROWS_PER_TOKEN
CHUNKS_PER_PAGE = PAGE_SIZE // CMP_STRIDE


def _cache_rows(cache):
    return cache.reshape(-1, HEAD_DIM)


def _token_rows(ref, slot, n_tokens):
    return ref[pl.ds(slot, n_tokens, stride=ROWS_PER_TOKEN), :]


def _sublane_transpose(tiles):
    tiles = list(tiles)
    sub = lax.broadcasted_iota(jnp.int32, (SUBLANES, LANES), 0)
    for d in (4, 2, 1):
        keep = (sub & d) == 0
        for i in range(SUBLANES):
            if i & d:
                continue
            lo, hi = tiles[i], tiles[i + d]
            tiles[i] = jnp.where(keep, lo, pltpu.roll(hi, d, 0))
            tiles[i + d] = jnp.where(keep, pltpu.roll(lo, SUBLANES - d, 0), hi)
    return tiles


def _chunk_terms_paged_kernel(pt_ref, *refs, pg):
    pages, w_ref, o_ref = refs[:pg], refs[pg], refs[pg + 1]
    chunk_rows = CMP_STRIDE * ROWS_PER_TOKEN
    assert CHUNKS_PER_PAGE == SUBLANES and chunk_rows % SUBLANES == 0
    pieces = [[] for _ in range(chunk_rows)]
    for pr in pages:
        for t in range(chunk_rows // SUBLANES):
            tiles = [pr[n * chunk_rows + t * SUBLANES:n * chunk_rows + (t + 1) * SUBLANES, :]
                     for n in range(CHUNKS_PER_PAGE)]
            for s, tile in enumerate(_sublane_transpose(tiles)):
                pieces[t * SUBLANES + s].append(tile)
    for kc in range(ROWS_PER_TOKEN):
        xk = jnp.concatenate([jnp.concatenate(pieces[p * ROWS_PER_TOKEN + kc], axis=0)
                              for p in range(CMP_STRIDE)], axis=1).astype(BF16)
        o_ref[:, kc * 2 * CMP_HIDDEN:(kc + 1) * 2 * CMP_HIDDEN] = _dot(xk, w_ref[kc % 2])


def _page_specs(block, pg):
    tail = (0,) * (len(block) - 1)

    def spec(j):
        return pl.BlockSpec(block, lambda bi, g, pt: (pt[bi, g * pg + j],) + tail)
    return [spec(j) for j in range(pg)]


def _chunk_terms_paged(pool, page_table, w, pg):
    db, n_pages = page_table.shape
    cpp = CHUNKS_PER_PAGE
    pool_v = _cache_rows(pool)
    kern = functools.partial(_chunk_terms_paged_kernel, pg=pg)
    grid_spec = pltpu.PrefetchScalarGridSpec(
        num_scalar_prefetch=1,
        grid=(db, n_pages // pg),
        in_specs=_page_specs((PAGE_ROWS, HEAD_DIM), pg)
                 + [pl.BlockSpec((2, CMP_STRIDE * HEAD_DIM, 2 * CMP_HIDDEN), lambda bi, g, pt: (0, 0, 0))],
        out_specs=pl.BlockSpec((None, pg * cpp, FS_COLS), lambda bi, g, pt: (bi, g, 0)),
    )
    return pl.pallas_call(
        kern, grid_spec=grid_spec,
        out_shape=jax.ShapeDtypeStruct((db, n_pages * cpp, FS_COLS), F32),
        compiler_params=_params(2, CHUNK_VMEM_LIMIT),
        name="chunk_terms_paged",
    )(page_table, *([pool_v] * pg), w)


def _compress_mlp_kernel(fs_ref, new_ref, pe_ref, w1_ref, w2_ref, g_ref, kc_ref, vc_ref):
    n = fs_ref.shape[0]
    last = lax.broadcasted_iota(jnp.int32, (n, 1), 0) == n - 1
    for kc in range(KV_HEADS * 2):
        kvh, c = kc // 2, kc % 2
        base = kc * 2 * CMP_HIDDEN
        first = fs_ref[:, base:base + CMP_HIDDEN]
        second = fs_ref[:, base + CMP_HIDDEN:base + 2 * CMP_HIDDEN]
        shifted = jnp.where(last, new_ref[0:1, base + CMP_HIDDEN:base + 2 * CMP_HIDDEN],
                            pltpu.roll(second, n - 1, 0))
        pe = jnp.broadcast_to(pe_ref[c:c + 1, :], (SUBLANES, pe_ref.shape[1])).astype(BF16)
        bias = _dot(pe, w1_ref[c])[0:1, :]
        pre = first + shifted + bias
        hid = pre * jax.nn.sigmoid(pre)
        out = _dot(hid.astype(BF16), w2_ref[c])
        sl = slice(kvh * HEAD_DIM, (kvh + 1) * HEAD_DIM)
        if c == 0:
            kc_ref[:, sl] = _head_norm(out, g_ref[...]).astype(BF16)
        else:
            vc_ref[:, sl] = out.astype(BF16)


def _compress_mlp(fs, fs_new, pe, w1r, w2, g_cmp):
    nb, n, _ = fs.shape
    whole = lambda bi: (bi, 0, 0)
    fix2 = lambda bi: (0, 0)
    fix3 = lambda bi: (0, 0, 0)
    out = jax.ShapeDtypeStruct((nb, n, KV_HEADS * HEAD_DIM), BF16)
    return pl.pallas_call(
        _compress_mlp_kernel,
        grid=(nb,),
        in_specs=[pl.BlockSpec((None, n, FS_COLS), whole), pl.BlockSpec((None, SUBLANES, FS_COLS), whole),
                  pl.BlockSpec((2, CMP_LEN * HEAD_DIM), fix2),
                  pl.BlockSpec((2, CMP_LEN * HEAD_DIM, CMP_HIDDEN), fix3),
                  pl.BlockSpec((2, CMP_HIDDEN, HEAD_DIM), fix3), pl.BlockSpec((1, HEAD_DIM), fix2)],
        out_specs=[pl.BlockSpec((None, n, KV_HEADS * HEAD_DIM), whole)] * 2,
        out_shape=[out, out],
        compiler_params=_params(1),
        name="compress_mlp",
    )(fs, fs_new, pe, w1r, w2, g_cmp.reshape(1, HEAD_DIM))


def _idx_scores_kernel(pt_ref, *refs, pg, n_new):
    pages = refs[:pg]
    iq_ref, iw_ref, new_ref, o_ref, onew_ref = refs[pg:pg + 5]
    g = pl.program_id(1)
    iq = iq_ref[...]
    iw = iw_ref[...]
    row = lax.broadcasted_iota(jnp.int32, (SAMPLE_Q, 1), 0)

    def scores(keys_t, first_idx):
        s = jnp.maximum(_dot(iq, keys_t), 0.0)
        n = s.shape[1]
        w = s * jnp.concatenate([iw] * (n // LANES), axis=1)
        acc = w[0:SAMPLE_Q]
        for hd in range(1, IDX_HEADS):
            acc = acc + w[hd * SAMPLE_Q:(hd + 1) * SAMPLE_Q]
        ramp = -(first_idx + lax.broadcasted_iota(jnp.int32, (1, n), 1)).astype(F32)
        return jnp.where(row < n_new, acc, ramp)

    cw = pg * PAGE_SIZE
    o_ref[...] = scores(jnp.concatenate([p[...] for p in pages], axis=1).astype(BF16), g * cw)

    @pl.when(g == pl.num_programs(1) - 1)
    def _():
        onew_ref[...] = scores(new_ref[...], pl.num_programs(1) * cw)


def _idx_scores(pool_idx_t, page_table, iqm, iwm, ik_new_t, pg, n_new):
    db, n_pages = page_table.shape
    ng = n_pages // pg
    kern = functools.partial(_idx_scores_kernel, pg=pg, n_new=n_new)
    per_b = lambda bi, g, pt: (bi, 0, 0)
    grid_spec = pltpu.PrefetchScalarGridSpec(
        num_scalar_prefetch=1,
        grid=(db, ng),
        in_specs=_page_specs((None, IDX_DIM, PAGE_SIZE), pg)
                 + [pl.BlockSpec((None, IDX_HEADS * SAMPLE_Q, IDX_DIM), per_b),
                    pl.BlockSpec((None, IDX_HEADS * SAMPLE_Q, LANES), per_b),
                    pl.BlockSpec((None, IDX_DIM, LANES), per_b)],
        out_specs=[pl.BlockSpec((None, SAMPLE_Q, pg * PAGE_SIZE), lambda bi, g, pt: (g, bi, 0)),
                   pl.BlockSpec((SAMPLE_Q, LANES), lambda bi, g, pt: (bi, 0))],
    )
    return pl.pallas_call(
        kern, grid_spec=grid_spec,
        out_shape=[jax.ShapeDtypeStruct((ng, db * SAMPLE_Q, pg * PAGE_SIZE), F32),
                   jax.ShapeDtypeStruct((db * SAMPLE_Q, LANES), F32)],
        compiler_params=_params(2),
        name="idx_scores",
    )(page_table, *([pool_idx_t] * pg), iqm, iwm, ik_new_t)


TOPK_ROWS = 64


def _dsa_topk_kernel(sc_ref, new_ref, o_ref, onew_ref, *, ng, cw, n_new, k_top, idx_bits, rb):
    row = lax.broadcasted_iota(jnp.int32, (rb, 1), 0)
    q = row & (SAMPLE_Q - 1)
    lane = lax.broadcasted_iota(jnp.int32, (1, LANES), 1)
    lane_cw = lax.broadcasted_iota(jnp.int32, (1, cw), 1)
    ok_new = (lane <= q) & (lane < n_new)
    x_new = jnp.where(ok_new, new_ref[...], NEG)
    idx_new = ng * cw + lane

    thr, vmax = _topk_threshold(lambda c, r0, nr: sc_ref[c, r0:r0 + nr, :], ng, k_top, idx_bits, rb, cw,
                                tail=(x_new, idx_new))
    for g in range(ng):
        o_ref[g] = _mask_bias(_topk_mask(sc_ref[g], g * cw + lane_cw, thr, vmax))
    onew_ref[...] = _mask_bias(_topk_mask(x_new, idx_new, thr, vmax) * jnp.where(ok_new, 1.0, 0.0))


def _dsa_topk(scores, scores_new, past, n_new):
    ng, rows, cw = scores.shape
    rb = min(TOPK_ROWS, rows)
    total = past + n_new
    kern = functools.partial(_dsa_topk_kernel, ng=ng, cw=cw, n_new=n_new, k_top=min(A_TOPK, total // 4),
                             idx_bits=(ng * cw + LANES - 1).bit_length(), rb=rb)
    big = pl.BlockSpec((ng, rb, cw), lambda r: (0, r, 0))
    small = pl.BlockSpec((rb, LANES), lambda r: (r, 0))
    return pl.pallas_call(
        kern,
        grid=(rows // rb,),
        in_specs=[big, small],
        out_specs=[big, small],
        out_shape=[jax.ShapeDtypeStruct((ng, rows, cw), F32), jax.ShapeDtypeStruct((rows, LANES), F32)],
        compiler_params=_params(1),
        name="dsa_topk",
    )(scores, scores_new)


QROWS = GROUP * SAMPLE_Q


def _paged_attn_kernel(pt_ref, *refs, pg, mask_rows):
    pages = refs[:pg]
    q_ref, mk_ref, mknew_ref, new_ref, o_ref, m_ref, l_ref, acc_ref = refs[pg:pg + 8]
    g = pl.program_id(1)

    @pl.when(g == 0)
    def _():
        m_ref[...] = jnp.full(m_ref.shape, NEG, F32)
        l_ref[...] = jnp.zeros(l_ref.shape, F32)
        acc_ref[...] = jnp.zeros(acc_ref.shape, F32)

    def update(keys, values, bias_all):
        carries = [(m_ref[kvh], l_ref[kvh], acc_ref[kvh]) for kvh in range(KV_HEADS)]
        for kvh in range(KV_HEADS):
            r0 = kvh * SAMPLE_Q if mask_rows == KV_HEADS * SAMPLE_Q else 0
            bias = jnp.concatenate([bias_all[r0:r0 + SAMPLE_Q]] * GROUP, axis=0)
            carries[kvh] = _softmax_step(carries[kvh], q_ref[kvh], keys(kvh), values(kvh), bias)
        for kvh in range(KV_HEADS):
            m_ref[kvh], l_ref[kvh], acc_ref[kvh] = carries[kvh]

    def paged(slot):
        return jnp.concatenate([_token_rows(p, slot, PAGE_SIZE) for p in pages], axis=0).astype(BF16)

    update(lambda kvh: paged(2 * kvh), lambda kvh: paged(2 * kvh + 1), mk_ref[...])

    @pl.when(g == pl.num_programs(1) - 1)
    def _():
        update(lambda kvh: new_ref[:, _kv_cols(kvh)[0]], lambda kvh: new_ref[:, _kv_cols(kvh)[1]], mknew_ref[...])
        for kvh in range(KV_HEADS):
            o_ref[kvh] = _softmax_done((m_ref[kvh], l_ref[kvh], acc_ref[kvh]))


def _paged_attn(pool, page_table, q, bias, bias_new, kv_new, pg):
    db, n_pages = page_table.shape
    ng = n_pages // pg
    mask_rows = bias.shape[1] // db
    pool_v = _cache_rows(pool)
    kern = functools.partial(_paged_attn_kernel, pg=pg, mask_rows=mask_rows)
    per_b3 = lambda bi, g, pt: (bi, 0, 0)
    per_b4 = lambda bi, g, pt: (bi, 0, 0, 0)
    grid_spec = pltpu.PrefetchScalarGridSpec(
        num_scalar_prefetch=1,
        grid=(db, ng),
        in_specs=_page_specs((PAGE_ROWS, HEAD_DIM), pg)
                 + [pl.BlockSpec((None, KV_HEADS, QROWS, HEAD_DIM), per_b4),
                    pl.BlockSpec((None, mask_rows, pg * PAGE_SIZE), lambda bi, g, pt: (g, bi, 0)),
                    pl.BlockSpec((mask_rows, LANES), lambda bi, g, pt: (bi, 0)),
                    pl.BlockSpec((None, LANES, KV_ROW), per_b3)],
        out_specs=pl.BlockSpec((None, KV_HEADS, QROWS, HEAD_DIM), per_b4),
        scratch_shapes=[pltpu.VMEM((KV_HEADS, QROWS, 1), F32), pltpu.VMEM((KV_HEADS, QROWS, 1), F32),
                        pltpu.VMEM((KV_HEADS, QROWS, HEAD_DIM), F32)],
    )
    return pl.pallas_call(
        kern, grid_spec=grid_spec,
        out_shape=jax.ShapeDtypeStruct((db, KV_HEADS, QROWS, HEAD_DIM), F32),
        compiler_params=_params(2),
        name="paged_attn",
    )(page_table, *([pool_v] * pg), q, bias, bias_new, kv_new)


BLOCKS_PER_PAGE = PAGE_SIZE // SLC_BLOCK
BLOCK_ROWS = SLC_BLOCK * ROWS_PER_TOKEN
assert BLOCKS_PER_PAGE == 2


def _slc_gather_kernel(pt_ref, sel_ref, *refs, nsel, ds):
    nblk = KV_HEADS * nsel
    blocks = refs[:nblk]
    q_ref, bnew_ref, new_ref, o_ref = refs[nblk:nblk + 4]
    bi, qi = pl.program_id(0), pl.program_id(1)
    lane = lax.broadcasted_iota(jnp.int32, (1, LANES), 1)
    mine = (lax.broadcasted_iota(jnp.int32, (QROWS, 1), 0) & (SAMPLE_Q - 1)) == qi

    @pl.when(qi == 0)
    def _():
        o_ref[...] = jnp.zeros(o_ref.shape, F32)

    for kvh in range(KV_HEADS):
        base = ((bi * ds + qi) * KV_HEADS + kvh) * nsel
        kcol, vcol = _kv_cols(kvh)
        keys = jnp.concatenate([_token_rows(blocks[kvh * nsel + j], 2 * kvh, SLC_BLOCK)
                                for j in range(nsel)], axis=0).astype(BF16)
        vals = jnp.concatenate([_token_rows(blocks[kvh * nsel + j], 2 * kvh + 1, SLC_BLOCK)
                                for j in range(nsel)], axis=0).astype(BF16)
        slot_bias = [jnp.where(sel_ref[base + j] >= 0, 0.0, NEG) for j in range(nsel)]
        pairs = [jnp.where(lane < SLC_BLOCK, slot_bias[j], slot_bias[min(j + 1, nsel - 1)])
                 for j in range(0, nsel, 2)]
        bias = jnp.concatenate(pairs, axis=1)[:, :nsel * SLC_BLOCK]
        carry = _softmax_step(_softmax_init(QROWS), q_ref[kvh], keys, vals, bias)
        bnew = jnp.concatenate([bnew_ref[kvh * SAMPLE_Q:(kvh + 1) * SAMPLE_Q]] * GROUP, axis=0)
        carry = _softmax_step(carry, q_ref[kvh], new_ref[:, kcol], new_ref[:, vcol], bnew)
        o_ref[kvh] = jnp.where(mine, _softmax_done(carry), o_ref[kvh])


def _slc_gather(pool, page_table, sel, q, bias_new, kv_new, nsel, ds):
    db = page_table.shape[0]
    pool_v = _cache_rows(pool)

    def spec(kvh, j):
        def imap(bi, qi, pt, sl):
            blk = jnp.maximum(sl[((bi * ds + qi) * KV_HEADS + kvh) * nsel + j], 0)
            return (pt[bi, lax.shift_right_logical(blk, 1)] * BLOCKS_PER_PAGE + (blk & 1), 0)
        return pl.BlockSpec((BLOCK_ROWS, HEAD_DIM), imap)

    kern = functools.partial(_slc_gather_kernel, nsel=nsel, ds=ds)
    grid_spec = pltpu.PrefetchScalarGridSpec(
        num_scalar_prefetch=2,
        grid=(db, ds),
        in_specs=[spec(kvh, j) for kvh in range(KV_HEADS) for j in range(nsel)]
                 + [pl.BlockSpec((None, KV_HEADS, QROWS, HEAD_DIM), lambda bi, qi, pt, sl: (bi, 0, 0, 0)),
                    pl.BlockSpec((KV_HEADS * SAMPLE_Q, LANES), lambda bi, qi, pt, sl: (bi, 0)),
                    pl.BlockSpec((None, LANES, KV_ROW), lambda bi, qi, pt, sl: (bi, 0, 0))],
        out_specs=pl.BlockSpec((None, KV_HEADS, QROWS, HEAD_DIM), lambda bi, qi, pt, sl: (bi, 0, 0, 0)),
    )
    return pl.pallas_call(
        kern, grid_spec=grid_spec,
        out_shape=jax.ShapeDtypeStruct((db, KV_HEADS, QROWS, HEAD_DIM), F32),
        compiler_params=_params(2),
        name="slc_gather",
    )(page_table, sel, *([pool_v] * (KV_HEADS * nsel)), q, bias_new, kv_new)


def _nsa_sample_kernel(q_ref, kc_ref, vc_ref, ov_ref, o_ref, sel_ref, tmnew_ref, sc_ref,
                       *, past, n_new, n_cmp, n_blk, n_sel, ng, gb, nb):
    row = lax.broadcasted_iota(jnp.int32, (SAMPLE_Q, 1), 0)
    qpos = past + row
    ncp = kc_ref.shape[1]
    cidx = lax.broadcasted_iota(jnp.int32, (1, ncp), 1)
    cmask8 = jnp.where((CMP_STRIDE * cidx + CMP_LEN - 1 <= qpos) & (cidx < n_cmp), 1.0, 0.0)
    cmask = jnp.concatenate([cmask8] * GROUP, axis=0)
    lane = lax.broadcasted_iota(jnp.int32, (1, LANES), 1)

    for bb in range(nb):
        for kvh in range(KV_HEADS):
            kc = kc_ref[bb, :, kvh * HEAD_DIM:(kvh + 1) * HEAD_DIM]
            vc = vc_ref[bb, :, kvh * HEAD_DIM:(kvh + 1) * HEAD_DIM]
            s = _dot_nt(q_ref[bb, kvh], kc) * SCALE
            sm = jnp.where(cmask > 0.0, s, NEG)
            e = jnp.exp(sm - jnp.max(sm, axis=1, keepdims=True)) * cmask
            p = e / jnp.maximum(jnp.sum(e, axis=1, keepdims=True), 1e-30)
            o_ref[bb, kvh] = _dot(p.astype(BF16), vc)
            psum = p[0:SAMPLE_Q]
            for g in range(1, GROUP):
                psum = psum + p[g * SAMPLE_Q:(g + 1) * SAMPLE_Q]
            imp = _split_dot(psum, ov_ref[...])
            r0 = (bb * KV_HEADS + kvh) * SAMPLE_Q
            for g in range(ng + 1):
                blk = g * gb + lane
                v = _block_rules(imp[:, g * LANES:(g + 1) * LANES], blk, qpos)
                v = jnp.where(row < n_new, v, -blk.astype(F32))
                sc_ref[g, r0:r0 + SAMPLE_Q, :] = jnp.where((lane < gb) & (blk < n_blk), v, -jnp.inf)

    rows = nb * KV_HEADS * SAMPLE_Q
    idx_bits = ((ng + 1) * LANES - 1).bit_length()
    thr, vmax = _topk_threshold(lambda c, r0, nr: sc_ref[c, r0:r0 + nr, :], ng + 1, n_sel, idx_bits, rows, LANES)
    qpos2 = jnp.concatenate([qpos] * (nb * KV_HEADS), axis=0)
    n_past_blk = past // SLC_BLOCK
    tri = jnp.where(lax.broadcasted_iota(jnp.int32, (LANES, 1), 0) <= lane, 1.0, 0.0).astype(BF16)
    offs = jnp.zeros((rows, 1), F32)
    slots = [jnp.zeros((rows, 1), F32) for _ in range(n_sel)]
    for g in range(ng + 1):
        blk = g * gb + lane
        bm = _topk_mask(sc_ref[g], g * LANES + lane, thr, vmax)
        if g == ng:
            ok = jnp.where((past + lane <= qpos2) & (lane < n_new), 1.0, 0.0)
            tmnew_ref[...] = _mask_bias(bm[:, 0:1] * ok)
        bm = bm * jnp.where((lane < gb) & (blk < n_past_blk), 1.0, 0.0)
        rank = offs + _dot(bm.astype(BF16), tri)
        for j in range(n_sel):
            hit = jnp.where(bm > 0.0, jnp.where(rank == j + 1.0, blk.astype(F32), 0.0), 0.0)
            slots[j] = slots[j] + jnp.sum(hit, axis=1, keepdims=True)
        offs = offs + jnp.sum(bm, axis=1, keepdims=True)
    out = jnp.full((rows, LANES), -1.0, F32)
    for j in range(n_sel):
        out = jnp.where((lane == j) & (offs > j), slots[j], out)
    sel_ref[...] = out.astype(jnp.int32)


def _nsa_sample(q, kc, vc, past, n_new, ng, cw):
    db = q.shape[0]
    ncp = kc.shape[1]
    total = past + n_new
    n_blk = -(-total // SLC_BLOCK)
    n_cmp = total_chunks(past, n_new) - 1
    gb = cw // SLC_BLOCK
    assert gb <= LANES and past % cw == 0
    col = jnp.arange((ng + 1) * LANES)
    col_block = jnp.where(col % LANES < gb, (col // LANES) * gb + col % LANES, -1)
    col_block = jnp.where(col_block < n_blk, col_block, -1)
    ov = _overlap_matrix(ncp, (ng + 1) * LANES, col_block)
    nb = max(n for n in (4, 2, 1) if db % n == 0)
    kern = functools.partial(_nsa_sample_kernel, past=past, n_new=n_new, n_cmp=n_cmp, n_blk=n_blk,
                             n_sel=min(N_SEL, n_blk), ng=ng, gb=gb, nb=nb)
    rows = KV_HEADS * SAMPLE_Q
    b3 = lambda bi: (bi, 0, 0)
    b4 = lambda bi: (bi, 0, 0, 0)
    return pl.pallas_call(
        kern,
        grid=(db // nb,),
        in_specs=[pl.BlockSpec((nb, KV_HEADS, QROWS, HEAD_DIM), b4),
                  pl.BlockSpec((nb, ncp, KV_HEADS * HEAD_DIM), b3),
                  pl.BlockSpec((nb, ncp, KV_HEADS * HEAD_DIM), b3),
                  pl.BlockSpec((ncp, (ng + 1) * LANES), lambda bi: (0, 0))],
        out_specs=[pl.BlockSpec((nb, KV_HEADS, QROWS, HEAD_DIM), b4),
                   pl.BlockSpec((nb * rows, LANES), lambda bi: (bi, 0)),
                   pl.BlockSpec((nb * rows, LANES), lambda bi: (bi, 0))],
        out_shape=[jax.ShapeDtypeStruct((db, KV_HEADS, QROWS, HEAD_DIM), F32),
                   jax.ShapeDtypeStruct((db * rows, LANES), jnp.int32),
                   jax.ShapeDtypeStruct((db * rows, LANES), F32)],
        scratch_shapes=[pltpu.VMEM((ng + 1, nb * rows, LANES), F32)],
        compiler_params=_params(1),
        name="nsa_sample",
    )(q, kc, vc, ov)


def total_chunks(past, n_new):
    return past // CMP_STRIDE + -(-n_new // CMP_STRIDE)


def _win_sample_kernel(q_ref, st_ref, new_ref, o_ref, *, w_buf, n_new):
    row = lax.broadcasted_iota(jnp.int32, (SAMPLE_Q, 1), 0)
    row = jnp.concatenate([row] * GROUP, axis=0)
    j_old = lax.broadcasted_iota(jnp.int32, (1, w_buf), 1)
    j_new = w_buf + lax.broadcasted_iota(jnp.int32, (1, LANES), 1)
    def bias(j, valid):
        d = w_buf + row - j
        return jnp.where((d >= 0) & (d < WINDOW) & valid, 0.0, NEG)
    b_old, b_new = bias(j_old, True), bias(j_new, j_new < w_buf + n_new)
    for kvh in range(KV_HEADS):
        q = q_ref[kvh]
        kcol, vcol = _kv_cols(kvh)
        carry = _softmax_step(_softmax_init(QROWS), q, _token_rows(st_ref, 2 * kvh, w_buf).astype(BF16),
                              _token_rows(st_ref, 2 * kvh + 1, w_buf).astype(BF16), b_old)
        carry = _softmax_step(carry, q, new_ref[:, kcol], new_ref[:, vcol], b_new)
        o_ref[kvh] = _softmax_done(carry)


def _win_sample(q, state, kv_new, n_new):
    db, w_buf = state.shape[0], state.shape[1]
    kern = functools.partial(_win_sample_kernel, w_buf=w_buf, n_new=n_new)
    b3 = lambda bi: (bi, 0, 0)
    b4 = lambda bi: (bi, 0, 0, 0)
    return pl.pallas_call(
        kern,
        grid=(db,),
        in_specs=[pl.BlockSpec((None, KV_HEADS, QROWS, HEAD_DIM), b4),
                  pl.BlockSpec((w_buf * ROWS_PER_TOKEN, HEAD_DIM), lambda bi: (bi, 0)),
                  pl.BlockSpec((None, LANES, KV_ROW), b3)],
        out_specs=pl.BlockSpec((None, KV_HEADS, QROWS, HEAD_DIM), b4),
        out_shape=jax.ShapeDtypeStruct((db, KV_HEADS, QROWS, HEAD_DIM), F32),
        compiler_params=_params(1),
        name="win_sample",
    )(q, _cache_rows(state), kv_new)


def _gate_merge_kernel(oa_ref, oc_ref, os_ref, ow_ref, misc_ref, sza_ref, szb_ref, a_ref, b_ref):
    misc = misc_ref[...]
    a_ref[...] = (oa_ref[...] * sza_ref[...]).astype(BF16)
    for hd in range(N_HEADS):
        sl = slice(hd * HEAD_DIM, (hd + 1) * HEAD_DIM)
        g = [misc[:, MISC_IW + j * N_HEADS + hd:MISC_IW + j * N_HEADS + hd + 1] for j in range(3)]
        o_b = g[0] * oc_ref[:, sl] + g[1] * os_ref[:, sl] + g[2] * ow_ref[:, sl]
        b_ref[:, sl] = (o_b * szb_ref[:, sl]).astype(BF16)


def _gate_merge(o_a, o_cmp, o_slc, o_win, misc, sza, szb):
    n = o_a.shape[0]
    full = lambda w: pl.BlockSpec((n, w), lambda i: (0, 0))
    out = jax.ShapeDtypeStruct((n, WIDTH), BF16)
    return pl.pallas_call(
        _gate_merge_kernel,
        grid=(1,),
        in_specs=[full(WIDTH)] * 4 + [full(LANES), full(WIDTH), full(WIDTH)],
        out_specs=[full(WIDTH)] * 2,
        out_shape=[out, out],
        compiler_params=_params(1),
        name="gate_merge",
    )(o_a, o_cmp, o_slc, o_win, misc, sza, szb)


def _merge_kernel(a_ref, b_ref, h_ref, wa_ref, wb_ref, wg0_ref, wg1_ref, o_ref):
    h = h_ref[...]
    y_a = _dot(a_ref[...], wa_ref[...])
    y_b = _dot(b_ref[...], wb_ref[...])
    g0 = jax.nn.sigmoid(_dot(h, wg0_ref[...]))
    g1 = jax.nn.sigmoid(_dot(h, wg1_ref[...]))
    o_ref[...] = (g0 * y_a + g1 * y_b).astype(BF16)


def _merge(a, bm, h, w_a, w_b, w_g0, w_g1, tr, cb):
    n = a.shape[0]
    row = lambda j, i: (i, 0)
    col = lambda j, i: (0, j)
    return pl.pallas_call(
        _merge_kernel,
        grid=(D_MODEL // cb, n // tr),
        in_specs=[pl.BlockSpec((tr, WIDTH), row), pl.BlockSpec((tr, WIDTH), row),
                  pl.BlockSpec((tr, D_MODEL), row),
                  pl.BlockSpec((WIDTH, cb), col), pl.BlockSpec((WIDTH, cb), col),
                  pl.BlockSpec((D_MODEL, cb), col), pl.BlockSpec((D_MODEL, cb), col)],
        out_specs=pl.BlockSpec((tr, cb), lambda j, i: (i, j)),
        out_shape=jax.ShapeDtypeStruct((n, D_MODEL), BF16),
        compiler_params=_params(2),
        name="merge",
    )(a, bm, h, w_a, w_b, w_g0, w_g1)


def _out_proj_kernel(x_ref, m_ref, w_ref, o_ref):
    o_ref[...] = x_ref[...] + _dot(m_ref[...], w_ref[...])


def _out_proj(x, merged, w_out, tr):
    n = x.shape[0]
    row = lambda i: (i, 0)
    return pl.pallas_call(
        _out_proj_kernel,
        grid=(n // tr,),
        in_specs=[pl.BlockSpec((tr, D_MODEL), row), pl.BlockSpec((tr, D_MODEL), row),
                  pl.BlockSpec((D_MODEL, D_MODEL), lambda i: (0, 0))],
        out_specs=pl.BlockSpec((tr, D_MODEL), row),
        out_shape=jax.ShapeDtypeStruct((n, D_MODEL), F32),
        compiler_params=_params(1),
        name="out_proj",
    )(x, merged, w_out)


def _project(x2d, pos_period, tr, wts):
    n = x2d.shape[0]
    n_tab = pos_period.shape[0] // tr
    tabs_head = _rope_tables(pos_period, ROT_DIM, HEAD_DIM)
    tabs_iq = _rope_tables(pos_period, IDX_ROT_DIM, IDX_DIM)
    tabs_iq = tuple(jnp.concatenate([t, t], axis=1) for t in tabs_iq)
    ident = (jnp.ones_like(tabs_iq[0][:, :IDX_DIM]), jnp.zeros_like(tabs_iq[0][:, :IDX_DIM]),
             jnp.zeros_like(tabs_iq[0][:, :IDX_DIM]))
    tabs_misc = tuple(jnp.concatenate([t[:, :IDX_DIM], e], axis=1) for t, e in zip(tabs_iq, ident))

    h = _rmsnorm(x2d, wts["norm_gain"], tr)
    (qa,) = _proj_q(h, wts["w_qa"], wts["q_norm_a"], tabs_head, tr, n_tab, False)
    qb_rot, qb = _proj_q(h, wts["w_qb"], wts["q_norm_b"], tabs_head, tr, n_tab, True)
    akv, cmp_kv, slc, win, akv_b, slc_b, win_b = _proj_kv(
        h, wts["w_kv"], wts["k_norm_a"], wts["k_norm_slc"], wts["k_norm_win"], tabs_head, tr, n_tab)
    iq, misc = _proj_idx(h, wts["w_idx"], tabs_iq, tabs_misc, tr, n_tab)
    sza, szb = _proj_z(h, wts["w_z"], tr)
    return dict(h=h, qa=qa, qb=qb, qb_rot=qb_rot, akv=akv, cmp=cmp_kv, slc=slc, win=win,
                akv_b=akv_b, slc_b=slc_b, win_b=win_b, iq=iq, misc=misc, sza=sza, szb=szb)


def _sample_rows(a, db, ds):
    a = a.reshape(db, ds, KV_HEADS, GROUP, HEAD_DIM).transpose(0, 2, 3, 1, 4)
    a = jnp.pad(a, ((0, 0), (0, 0), (0, 0), (0, SAMPLE_Q - ds), (0, 0)))
    return a.reshape(db, KV_HEADS, QROWS, HEAD_DIM)


def _unsample_rows(o, db, ds):
    o = o.reshape(db, KV_HEADS, GROUP, SAMPLE_Q, HEAD_DIM)[:, :, :, :ds]
    return o.transpose(0, 3, 1, 2, 4).reshape(db * ds, WIDTH)


def _pad_new(a, db, ds):
    return jnp.pad(a.reshape(db, ds, a.shape[-1]), ((0, 0), (0, LANES - ds), (0, 0)))


def kernel(x_prompt, x_sample, cache_a_kv, cache_a_idx, cache_cmp_kv, cache_slc_kv, state_win_kv,
           page_table, norm_gain, w_in, q_norm_a, k_norm_a, q_norm_b, k_norm_cmp, k_norm_slc,
           k_norm_win, pe_cmp, w1_cmp, w2_cmp, w_proj_a, w_proj_b, w_out):
    b, t, _ = x_prompt.shape
    db, ds, _ = x_sample.shape
    n_pages = page_table.shape[1]
    past = n_pages * PAGE_SIZE
    assert ds <= SAMPLE_Q and ds <= CMP_STRIDE

    sizes = (WIDTH, KV_ROW, IDX_COLS, IDX_DIM, IDX_HEADS, WIDTH, WIDTH, 3 * KV_ROW, 3 * N_HEADS, WIDTH,
             2 * D_MODEL)
    offs = [0]
    for s in sizes:
        offs.append(offs[-1] + s)
    wb = w_in.astype(BF16)
    seg = lambda j: wb[:, offs[j]:offs[j + 1]]
    misc_pad = jnp.zeros((D_MODEL, LANES - (IDX_DIM + IDX_HEADS + 3 * N_HEADS)), BF16)
    wts = dict(
        norm_gain=norm_gain, q_norm_a=q_norm_a, q_norm_b=q_norm_b, k_norm_a=k_norm_a,
        k_norm_slc=k_norm_slc, k_norm_win=k_norm_win,
        w_qa=seg(0), w_qb=seg(6),
        w_kv=jnp.concatenate([seg(1), seg(7)], axis=1),
        w_idx=jnp.concatenate([seg(2), seg(3), seg(4), seg(8), misc_pad], axis=1),
        w_z=jnp.concatenate([seg(5), seg(9)], axis=1),
    )
    w_g0 = wb[:, offs[10]:offs[10] + D_MODEL]
    w_g1 = wb[:, offs[10] + D_MODEL:offs[11]]
    w_pa, w_pb, w_o = w_proj_a.astype(BF16), w_proj_b.astype(BF16), w_out.astype(BF16)
    w1_fs = jnp.concatenate([w1_cmp[:, :CMP_STRIDE].reshape(2, CMP_STRIDE * HEAD_DIM, CMP_HIDDEN),
                             w1_cmp[:, CMP_STRIDE:].reshape(2, CMP_STRIDE * HEAD_DIM, CMP_HIDDEN)],
                            axis=2).astype(BF16)
    w1_r = w1_cmp.reshape(2, CMP_LEN * HEAD_DIM, CMP_HIDDEN).astype(BF16)
    pe_r = pe_cmp.reshape(2, CMP_LEN * HEAD_DIM)
    w2_b = w2_cmp.astype(BF16)

    tr = min(512, t)
    tq = min(256, t)
    tk = min(512, t)
    xp = x_prompt.reshape(b * t, D_MODEL)
    pp = _project(xp, jnp.arange(t, dtype=jnp.int32), tr, wts)

    pages_p = t // PAGE_SIZE
    table_p = jnp.arange(b * pages_p, dtype=jnp.int32).reshape(b, pages_p)
    fs_p = _chunk_terms_paged(pp["cmp"], table_p, w1_fs, min(CHUNK_PAGES_PER_STEP, pages_p))
    kc_p, vc_p = _compress_mlp(fs_p, jnp.zeros((b, SUBLANES, FS_COLS), F32), pe_r, w1_r, w2_b, k_norm_cmp)

    ik_b = pp["misc"][:, :IDX_DIM].astype(BF16)
    ik2 = jnp.concatenate([ik_b, ik_b], axis=1).reshape(b, t, LANES)
    iw_t = pp["misc"][:, MISC_IK:MISC_IW].T
    a_p = _dsa_prompt(pp["iq"], iw_t, ik2, pp["qa"], pp["akv_b"].reshape(b, t, KV_ROW), pp["sza"],
                      b, t, tq, tk)
    win_front = jnp.pad(pp["win_b"].reshape(b, t, KV_ROW), ((0, 0), (WINDOW, 0), (0, 0)))
    b_p = _nsa_prompt(pp["qb"], pp["qb_rot"], kc_p, vc_p, pp["slc_b"].reshape(b, t, KV_ROW),
                      win_front, pp["misc"], pp["szb"], b, t, tq, tk)
    merged_p = _merge(a_p, b_p, pp["h"], w_pa, w_pb, w_g0, w_g1, tr, 1024)
    y_prompt = _out_proj(xp, merged_p, w_o, tr).reshape(b, t, D_MODEL)

    kv5 = lambda a, n0, n1: a.reshape(n0, n1, KV_HEADS, 2, HEAD_DIM)
    p_a_kv = kv5(pp["akv"], b, t)
    p_a_idx = pp["misc"][:, :IDX_DIM].reshape(b, t, IDX_DIM)
    p_cmp_kv = kv5(pp["cmp"], b, t)
    p_slc_kv = kv5(pp["slc"], b, t)
    p_win_kv = kv5(pp["win"], b, t)[:, t - min(WINDOW, t):]

    ns = db * ds
    xs = x_sample.reshape(ns, D_MODEL)
    pos_s = past + jnp.tile(jnp.arange(ds, dtype=jnp.int32), db)
    ps = _project(xs, pos_s, ns, wts)
    pg = min(PAGES_PER_STEP, n_pages)
    ng = n_pages // pg
    cw = pg * PAGE_SIZE

    iq_s = ps["iq"].reshape(db, ds, IDX_HEADS, IDX_DIM).transpose(0, 2, 1, 3)
    iqm = jnp.pad(iq_s, ((0, 0), (0, 0), (0, SAMPLE_Q - ds), (0, 0))).reshape(db, IDX_HEADS * SAMPLE_Q, IDX_DIM)
    iw_s = ps["misc"][:, MISC_IK:MISC_IW].reshape(db, ds, IDX_HEADS).transpose(0, 2, 1) * (IDX_DIM ** -0.5)
    iwm = jnp.pad(iw_s, ((0, 0), (0, 0), (0, SAMPLE_Q - ds))).reshape(db, IDX_HEADS * SAMPLE_Q, 1)
    iwm = jnp.broadcast_to(iwm, (db, IDX_HEADS * SAMPLE_Q, LANES))
    ik_new_t = jnp.swapaxes(_pad_new(ps["misc"][:, :IDX_DIM].astype(BF16), db, ds), 1, 2)
    sc_s, sc_new = _idx_scores(jnp.swapaxes(cache_a_idx, 1, 2), page_table, iqm, iwm, ik_new_t, pg, ds)
    mk_a, mk_a_new = _dsa_topk(sc_s, sc_new, past, ds)
    o_a_s = _paged_attn(cache_a_kv, page_table, _sample_rows(ps["qa"], db, ds), mk_a, mk_a_new,
                        _pad_new(ps["akv_b"], db, ds), pg)

    fs_past = _chunk_terms_paged(cache_cmp_kv, page_table, w1_fs, min(CHUNK_PAGES_PER_STEP, n_pages))
    new_chunk = jnp.pad(ps["cmp"].reshape(db, ds, KV_ROW), ((0, 0), (0, CMP_STRIDE - ds), (0, 0)))
    fs_new = _chunk_terms_dense(new_chunk.reshape(db, CHUNK_FLAT), w1_fs, db)
    fs_new = jnp.pad(fs_new.reshape(db, 1, FS_COLS), ((0, 0), (0, SUBLANES - 1), (0, 0)))
    kc_s, vc_s = _compress_mlp(fs_past, fs_new, pe_r, w1_r, w2_b, k_norm_cmp)
    o_cmp_s, sel_s, mk_s_new = _nsa_sample(_sample_rows(ps["qb"], db, ds), kc_s, vc_s, past, ds, ng, cw)
    q_rot_s = _sample_rows(ps["qb_rot"], db, ds)
    n_sel_s = min(N_SEL, -(-(past + ds) // SLC_BLOCK))
    sel_s = sel_s[:, :n_sel_s].reshape(db, KV_HEADS, SAMPLE_Q, n_sel_s)[:, :, :ds]
    sel_s = sel_s.transpose(0, 2, 1, 3).reshape(-1)
    o_slc_s = _slc_gather(cache_slc_kv, page_table, sel_s, q_rot_s, mk_s_new, _pad_new(ps["slc_b"], db, ds),
                          n_sel_s, ds)
    o_win_s = _win_sample(q_rot_s, state_win_kv, _pad_new(ps["win_b"], db, ds), ds)

    a_s, b_s = _gate_merge(_unsample_rows(o_a_s, db, ds), _unsample_rows(o_cmp_s, db, ds),
                           _unsample_rows(o_slc_s, db, ds), _unsample_rows(o_win_s, db, ds),
                           ps["misc"], ps["sza"], ps["szb"])
    merged_s = _merge(a_s, b_s, ps["h"], w_pa, w_pb, w_g0, w_g1, ns, 1024)
    y_sample = _out_proj(xs, merged_s, w_o, ns).reshape(db, ds, D_MODEL)

    s_win = kv5(ps["win"], db, ds)
    w_buf = state_win_kv.shape[1]
    s_win_kv = jnp.concatenate([state_win_kv, s_win], axis=1)[:, ds:ds + w_buf]

    return (y_prompt, y_sample, p_a_kv, p_a_idx, p_cmp_kv, p_slc_kv, p_win_kv,
            kv5(ps["akv"], db, ds), ps["misc"][:, :IDX_DIM].reshape(db, ds, IDX_DIM),
            kv5(ps["cmp"], db, ds), kv5(ps["slc"], db, ds), s_win_kv)
```
